```python
import jax, jax.numpy as jnp
from jax import lax
import numpy as np

D_MODEL = 1024
BATCH = 4
SEQ = 4096
DEPTH = 1

GRID_W = 64
NA_HEADS = 8
NA_HEAD_DIM = 64
NA_WIDTH = NA_HEADS * NA_HEAD_DIM
NA_WIN_ROWS = 8
NA_WIN_COLS = 16
F_GROUPS = 4
F_GROUP_DIM = 128
F_WIDTH = F_GROUPS * F_GROUP_DIM
MEM_TOKENS = 256
MEM_HEADS = 4
MEM_HEAD_DIM = 128
MEM_WIDTH = MEM_HEADS * MEM_HEAD_DIM
N_BRANCHES = 3
D_FF = 4 * D_MODEL
IN_WIDTH = 3 * NA_WIDTH + F_WIDTH + MEM_WIDTH + N_BRANCHES * D_MODEL
EPS = 1e-6
NEG_INF = -1e30

kernel_name = "hybrid_na_fourier_memory_gated_encoder"


def rmsnorm(x, g):
    xf = x.astype(jnp.float32)
    y = xf * lax.rsqrt(jnp.mean(xf * xf, axis=-1, keepdims=True) + EPS)
    return (y * g.astype(jnp.float32)).astype(x.dtype)


def neighbourhood_attention_2d(q, k, v, rpb):
    B, S, H, d = q.shape
    rows = S // GRID_W
    wr = min(NA_WIN_ROWS, rows)
    r = jnp.arange(rows)
    rs = jnp.clip(r - wr // 2, 0, rows - wr)
    row_idx = rs[:, None] + jnp.arange(wr)[None, :]
    c = jnp.arange(GRID_W)
    cs = jnp.clip(c - NA_WIN_COLS // 2, 0, GRID_W - NA_WIN_COLS)
    kc = jnp.arange(GRID_W)
    col_valid = (kc[None, :] >= cs[:, None]) & (kc[None, :] < cs[:, None] + NA_WIN_COLS)
    dr_idx = row_idx - r[:, None] + (NA_WIN_ROWS - 1)
    dc_idx = jnp.clip(kc[None, :] - c[:, None], -(NA_WIN_COLS - 1), NA_WIN_COLS - 1) + (NA_WIN_COLS - 1)
    bias = rpb[:, dr_idx[:, None, :, None], dc_idx[None, :, None, :]]

    qg = q.reshape(B, rows, GRID_W, H, d)
    kb = k.reshape(B, rows, GRID_W, H, d)[:, row_idx]
    vb = v.reshape(B, rows, GRID_W, H, d)[:, row_idx]
    scale = 1.0 / np.sqrt(d).astype(np.float32)
    s = jnp.einsum('brchd,brwkhd->bhrcwk', qg, kb).astype(jnp.float32) * scale
    s = s + bias[None].astype(jnp.float32)
    s = jnp.where(col_valid[:, None, :], s, NEG_INF)
    p = jax.nn.softmax(s, axis=(-2, -1)).astype(v.dtype)
    o = jnp.einsum('bhrcwk,brwkhd->brchd', p, vb)
    return o.reshape(B, S, H * d)


def fourier_mix(u):
    B, S, _ = u.shape
    ug = u.reshape(B, S, F_GROUPS, F_GROUP_DIM).astype(jnp.float32)
    yf = jnp.fft.fft2(ug, axes=(1, 3), norm='ortho').real
    return yf.reshape(B, S, F_WIDTH).astype(u.dtype)


def memory_cross_attention(q, k, v):
    B, S, H, d = q.shape
    scale = 1.0 / np.sqrt(d).astype(np.float32)
    s = jnp.einsum('bshd,bmhd->bhsm', q, k).astype(jnp.float32) * scale
    p = jax.nn.softmax(s, axis=-1).astype(v.dtype)
    o = jnp.einsum('bhsm,bmhd->bshd', p, v)
    return o.reshape(B, S, H * d)


def setup_inputs(seed: int = 0) -> dict:
    key = jax.random.key(seed)
    ks = jax.random.split(key, 20)
    f32 = jnp.float32

    def w(k, shape, fan_in):
        return jax.random.normal(k, shape, f32) * (fan_in ** -0.5)

    def gain(k, n):
        return 1.0 + 0.02 * jax.random.normal(k, (n,), f32)

    return {
        "x": jax.random.normal(ks[0], (BATCH, SEQ, D_MODEL), f32),
        "mem": jax.random.normal(ks[1], (BATCH, MEM_TOKENS, D_MODEL), f32),
        "norm1_g": gain(ks[2], D_MODEL),
        "w_in": w(ks[3], (D_MODEL, IN_WIDTH), D_MODEL),
        "b_gate": 0.01 * jax.random.normal(ks[4], (N_BRANCHES * D_MODEL,), f32),
        "na_q_g": gain(ks[5], NA_HEAD_DIM),
        "na_k_g": gain(ks[6], NA_HEAD_DIM),
        "na_rpb": 0.1 * jax.random.normal(ks[7], (NA_HEADS, 2 * NA_WIN_ROWS - 1, 2 * NA_WIN_COLS - 1), f32),
        "w_na_o": w(ks[8], (NA_WIDTH, D_MODEL), NA_WIDTH),
        "w_f": w(ks[9], (F_WIDTH, D_MODEL), F_WIDTH),
        "mem_norm_g": gain(ks[10], D_MODEL),
        "w_mem_kv": w(ks[11], (D_MODEL, 2 * MEM_WIDTH), D_MODEL),
        "mem_q_g": gain(ks[12], MEM_HEAD_DIM),
        "mem_k_g": gain(ks[13], MEM_HEAD_DIM),
        "w_mem_o": w(ks[14], (MEM_WIDTH, D_MODEL), MEM_WIDTH),
        "w_out": w(ks[15], (D_MODEL, D_MODEL), D_MODEL),
        "norm2_g": gain(ks[16], D_MODEL),
        "w_ff1": w(ks[17], (D_MODEL, D_FF), D_MODEL),
        "w_ff2": w(ks[18], (D_FF, D_MODEL), D_FF),
    }


def reference(x, mem, norm1_g, w_in, b_gate, na_q_g, na_k_g, na_rpb, w_na_o, w_f,
              mem_norm_g, w_mem_kv, mem_q_g, mem_k_g, w_mem_o, w_out,
              norm2_g, w_ff1, w_ff2):
    B, S, _ = x.shape
    M = mem.shape[1]
    mem_n = rmsnorm(mem, mem_norm_g)
    mem_kv = mem_n @ w_mem_kv
    for _ in range(DEPTH):
        h = rmsnorm(x, norm1_g)
        z = h @ w_in
        o0 = 0
        na_q = z[..., o0:o0 + NA_WIDTH]; o0 += NA_WIDTH
        na_k = z[..., o0:o0 + NA_WIDTH]; o0 += NA_WIDTH
        na_v = z[..., o0:o0 + NA_WIDTH]; o0 += NA_WIDTH
        f_u = z[..., o0:o0 + F_WIDTH]; o0 += F_WIDTH
        m_q = z[..., o0:o0 + MEM_WIDTH]; o0 += MEM_WIDTH
        gates = jax.nn.sigmoid(z[..., o0:] + b_gate).reshape(B, S, N_BRANCHES, D_MODEL)

        qa = rmsnorm(na_q.reshape(B, S, NA_HEADS, NA_HEAD_DIM), na_q_g)
        ka = rmsnorm(na_k.reshape(B, S, NA_HEADS, NA_HEAD_DIM), na_k_g)
        va = na_v.reshape(B, S, NA_HEADS, NA_HEAD_DIM)
        y_na = neighbourhood_attention_2d(qa, ka, va, na_rpb) @ w_na_o

        y_f = fourier_mix(f_u) @ w_f

        qm = rmsnorm(m_q.reshape(B, S, MEM_HEADS, MEM_HEAD_DIM), mem_q_g)
        km = rmsnorm(mem_kv[..., :MEM_WIDTH].reshape(B, M, MEM_HEADS, MEM_HEAD_DIM), mem_k_g)
        vm = mem_kv[..., MEM_WIDTH:].reshape(B, M, MEM_HEADS, MEM_HEAD_DIM)
        y_mem = memory_cross_attention(qm, km, vm) @ w_mem_o

        merged = gates[:, :, 0] * y_na + gates[:, :, 1] * y_f + gates[:, :, 2] * y_mem
        x = x + merged @ w_out

        h2 = rmsnorm(x, norm2_g)
        x = x + jnp.square(jax.nn.relu(h2 @ w_ff1)) @ w_ff2
    return x
```

```python
import functools

import numpy as np
import jax
import jax.numpy as jnp
from jax import lax
from jax.experimental import pallas as pl
from jax.experimental.pallas import tpu as pltpu

D_MODEL = 1024
GRID_W = 64
NA_HEADS = 8
NA_HEAD_DIM = 64
NA_WIDTH = NA_HEADS * NA_HEAD_DIM
NA_WIN_ROWS = 8
NA_WIN_COLS = 16
F_GROUPS = 4
F_GROUP_DIM = 128
F_WIDTH = F_GROUPS * F_GROUP_DIM
MEM_HEADS = 4
MEM_HEAD_DIM = 128
MEM_WIDTH = MEM_HEADS * MEM_HEAD_DIM
N_BRANCHES = 3
D_FF = 4 * D_MODEL
EPS = 1e-6
NEG_INF = -1e30

V7X_LANES = 128
V7X_VMEM_LIMIT_BYTES = 60 * 1024 * 1024

NA_PAIR_TOKENS = 2 * GRID_W
NA_KEY_PAIRS = 5
NA_KEY_TOKENS = NA_KEY_PAIRS * NA_PAIR_TOKENS
NA_BIAS_CASES = 5

BF16 = jnp.bfloat16
F32 = jnp.float32


def _dot(a, b):
    return jnp.dot(a, b, preferred_element_type=F32)


def _dot_nt(a, b):
    return lax.dot_general(a, b, (((1,), (1,)), ((), ())), preferred_element_type=F32)


def _const_spec(shape):
    return pl.BlockSpec(shape, lambda *_: (0,) * len(shape), pipeline_mode=pl.Buffered(1))


def _params(n_axes):
    return pltpu.CompilerParams(
        dimension_semantics=("arbitrary",) * n_axes,
        vmem_limit_bytes=V7X_VMEM_LIMIT_BYTES,
    )


def _group_mean_matrix(width, group):
    idx = np.arange(width) // group
    return jnp.asarray((idx[:, None] == idx[None, :]).astype(np.float32) / group, dtype=BF16)


def _mem_kv_kernel(mem_ref, g_ref, w_ref, gm_ref, kg_ref, kT_ref, v_ref):
    m = mem_ref[0]
    ms = jnp.mean(m * m, axis=-1, keepdims=True)
    mn = (m * lax.rsqrt(ms + EPS) * g_ref[...]).astype(BF16)
    kv = _dot(mn, w_ref[...])
    k = kv[:, :MEM_WIDTH]
    msk = _dot((k * k).astype(BF16), gm_ref[...])
    kn = k * lax.rsqrt(msk + EPS) * kg_ref[...]
    kT_ref[0] = kn.T.astype(BF16)
    v_ref[0] = kv[:, MEM_WIDTH:].astype(BF16)


def _mem_kv(mem, mem_norm_g, w_mem_kv, gmean_mem, mem_k_g):
    B, M, _ = mem.shape
    return pl.pallas_call(
        _mem_kv_kernel,
        grid=(B,),
        in_specs=[
            pl.BlockSpec((1, M, D_MODEL), lambda b: (b, 0, 0)),
            _const_spec((1, D_MODEL)),
            _const_spec((D_MODEL, 2 * MEM_WIDTH)),
            _const_spec((MEM_WIDTH, MEM_WIDTH)),
            _const_spec((1, MEM_WIDTH)),
        ],
        out_specs=[
            pl.BlockSpec((1, MEM_WIDTH, M), lambda b: (b, 0, 0)),
            pl.BlockSpec((1, M, MEM_WIDTH), lambda b: (b, 0, 0)),
        ],
        out_shape=[
            jax.ShapeDtypeStruct((B, MEM_WIDTH, M), BF16),
            jax.ShapeDtypeStruct((B, M, MEM_WIDTH), BF16),
        ],
        compiler_params=_params(1),
        name="mem_kv",
    )(mem, mem_norm_g.reshape(1, D_MODEL), w_mem_kv.astype(BF16), gmean_mem,
      jnp.tile(mem_k_g, MEM_HEADS).reshape(1, MEM_WIDTH))


def _in_proj_kernel(x_ref, g1_ref, wq_ref, wkT_ref, wv_ref, wf_ref, wm_ref, wg_ref, bg_ref,
                    gna_ref, gmem_ref, qg_ref, kg_ref, mqg_ref, memkT_ref, memv_ref,
                    q_ref, kT_ref, v_ref, fu_ref, om_ref, gate_ref):
    x = x_ref[...]
    ms = jnp.mean(x * x, axis=-1, keepdims=True)
    h = (x * lax.rsqrt(ms + EPS) * g1_ref[...]).astype(BF16)

    zq = _dot(h, wq_ref[...])
    msq = _dot((zq * zq).astype(BF16), gna_ref[...])
    q_ref[...] = (zq * lax.rsqrt(msq + EPS) * qg_ref[...]).astype(BF16)

    zkT = _dot_nt(wkT_ref[...], h)
    mskT = _dot(gna_ref[...], (zkT * zkT).astype(BF16))
    kT = (zkT * lax.rsqrt(mskT + EPS) * kg_ref[...]).astype(BF16)
    for j in range(kT_ref.shape[1]):
        kT_ref[0, j] = kT[:, j * V7X_LANES:(j + 1) * V7X_LANES]

    v_ref[...] = _dot(h, wv_ref[...]).astype(BF16)
    fu_ref[...] = _dot(h, wf_ref[...]).astype(BF16)

    zm = _dot(h, wm_ref[...])
    msm = _dot((zm * zm).astype(BF16), gmem_ref[...])
    qm = (zm * lax.rsqrt(msm + EPS) * mqg_ref[...]).astype(BF16)
    outs = []
    for hd in range(MEM_HEADS):
        sl = slice(hd * MEM_HEAD_DIM, (hd + 1) * MEM_HEAD_DIM)
        s = _dot(qm[:, sl], memkT_ref[0, sl, :])
        e = jnp.exp(s - jnp.max(s, axis=-1, keepdims=True))
        l = jnp.sum(e, axis=-1, keepdims=True)
        outs.append(_dot(e.astype(BF16), memv_ref[0, :, sl]) / l)
    om_ref[...] = jnp.concatenate(outs, axis=1).astype(BF16)

    for br in range(N_BRANCHES):
        sl = slice(br * D_MODEL, (br + 1) * D_MODEL)
        zg = _dot(h, wg_ref[:, sl]) + bg_ref[:, sl]
        gate_ref[:, sl] = jax.nn.sigmoid(zg).astype(BF16)


def _in_proj(x2, norm1_g, w_in, b_gate, na_q_g, na_k_g, mem_q_g, gmean_na, gmean_mem,
             memkT, memv, tokens_per_batch, tm):
    T = x2.shape[0]
    M = memv.shape[1]
    tiles_per_batch = tokens_per_batch // tm
    slabs = tm // V7X_LANES
    B = T // tokens_per_batch
    wb = w_in.astype(BF16)
    o = 0
    wq = wb[:, o:o + NA_WIDTH]; o += NA_WIDTH
    wkT = wb[:, o:o + NA_WIDTH].T; o += NA_WIDTH
    wv = wb[:, o:o + NA_WIDTH]; o += NA_WIDTH
    wf = wb[:, o:o + F_WIDTH]; o += F_WIDTH
    wm = wb[:, o:o + MEM_WIDTH]; o += MEM_WIDTH
    wg = wb[:, o:]
    na_scale = np.float32(1.0 / np.sqrt(NA_HEAD_DIM))
    mem_scale = np.float32(1.0 / np.sqrt(MEM_HEAD_DIM))
    qg = (jnp.tile(na_q_g, NA_HEADS) * na_scale).reshape(1, NA_WIDTH)
    kg = jnp.tile(na_k_g, NA_HEADS).reshape(NA_WIDTH, 1)
    mqg = (jnp.tile(mem_q_g, MEM_HEADS) * mem_scale).reshape(1, MEM_WIDTH)
    tok = lambda w: pl.BlockSpec((tm, w), lambda i: (i, 0))
    batch_of = lambda i: i // tiles_per_batch
    return pl.pallas_call(
        _in_proj_kernel,
        grid=(T // tm,),
        in_specs=[
            tok(D_MODEL),
            _const_spec((1, D_MODEL)),
            _const_spec((D_MODEL, NA_WIDTH)),
            _const_spec((NA_WIDTH, D_MODEL)),
            _const_spec((D_MODEL, NA_WIDTH)),
            _const_spec((D_MODEL, F_WIDTH)),
            _const_spec((D_MODEL, MEM_WIDTH)),
            _const_spec((D_MODEL, N_BRANCHES * D_MODEL)),
            _const_spec((1, N_BRANCHES * D_MODEL)),
            _const_spec((NA_WIDTH, NA_WIDTH)),
            _const_spec((MEM_WIDTH, MEM_WIDTH)),
            _const_spec((1, NA_WIDTH)),
            _const_spec((NA_WIDTH, 1)),
            _const_spec((1, MEM_WIDTH)),
            pl.BlockSpec((1, MEM_WIDTH, M), lambda i: (batch_of(i), 0, 0)),
            pl.BlockSpec((1, M, MEM_WIDTH), lambda i: (batch_of(i), 0, 0)),
        ],
        out_specs=[
            tok(NA_WIDTH),
            pl.BlockSpec((1, slabs, NA_WIDTH, V7X_LANES),
                         lambda i: (batch_of(i), i % tiles_per_batch, 0, 0)),
            tok(NA_WIDTH),
            tok(F_WIDTH),
            tok(MEM_WIDTH),
            tok(N_BRANCHES * D_MODEL),
        ],
        out_shape=[
            jax.ShapeDtypeStruct((T, NA_WIDTH), BF16),
            jax.ShapeDtypeStruct((B, tokens_per_batch // V7X_LANES, NA_WIDTH, V7X_LANES), BF16),
            jax.ShapeDtypeStruct((T, NA_WIDTH), BF16),
            jax.ShapeDtypeStruct((T, F_WIDTH), BF16),
            jax.ShapeDtypeStruct((T, MEM_WIDTH), BF16),
            jax.ShapeDtypeStruct((T, N_BRANCHES * D_MODEL), BF16),
        ],
        compiler_params=_params(1),
        name="in_proj",
    )(x2, norm1_g.reshape(1, D_MODEL), wq, wkT, wv, wf, wm, wg,
      b_gate.reshape(1, N_BRANCHES * D_MODEL), gmean_na, gmean_mem, qg, kg, mqg, memkT, memv)


def _na_bias_tables(rpb, rows):
    n_pairs = rows // 2
    wr = min(NA_WIN_ROWS, rows)
    reps = np.array([0, 1, 2, n_pairs - 2, n_pairs - 1])
    starts = np.clip(reps - 2, 0, n_pairs - NA_KEY_PAIRS)
    r = (2 * reps[:, None] + np.arange(2)[None, :])[:, :, None, None, None]
    c = np.arange(GRID_W)[None, None, :, None, None]
    kr = (2 * starts[:, None] + np.arange(2 * NA_KEY_PAIRS)[None, :])[:, None, None, :, None]
    kc = np.arange(GRID_W)[None, None, None, None, :]
    rs = np.clip(r - wr // 2, 0, rows - wr)
    cs = np.clip(c - NA_WIN_COLS // 2, 0, GRID_W - NA_WIN_COLS)
    valid = (kr >= rs) & (kr < rs + wr) & (kc >= cs) & (kc < cs + NA_WIN_COLS)
    dr = np.clip(kr - r + (NA_WIN_ROWS - 1), 0, 2 * NA_WIN_ROWS - 2)
    dc = np.clip(kc - c, -(NA_WIN_COLS - 1), NA_WIN_COLS - 1) + (NA_WIN_COLS - 1)
    shape = (len(reps), 2, GRID_W, 2 * NA_KEY_PAIRS, GRID_W)
    dr = np.broadcast_to(dr, shape).reshape(len(reps), NA_PAIR_TOKENS, NA_KEY_TOKENS)
    dc = np.broadcast_to(dc, shape).reshape(len(reps), NA_PAIR_TOKENS, NA_KEY_TOKENS)
    valid = np.broadcast_to(valid, shape).reshape(len(reps), NA_PAIR_TOKENS, NA_KEY_TOKENS)
    bias = rpb.astype(F32)[:, dr, dc]
    bias = jnp.where(valid[None], bias, NEG_INF)
    return jnp.transpose(bias, (1, 0, 2, 3))


def _na_kernel(q_ref, kT_ref, v_ref, bias_ref, o_ref, *, pairs_per_step, n_pairs):
    lane = lax.broadcasted_iota(jnp.int32, (NA_PAIR_TOKENS, 2 * NA_HEAD_DIM), 1)
    first_head = lane < NA_HEAD_DIM

    def pair_body(i, carry):
        p = pl.program_id(1) * pairs_per_step + i
        start = jnp.clip(p - 2, 0, n_pairs - NA_KEY_PAIRS)
        case = jnp.where(p < 2, p, jnp.where(p >= n_pairs - 2, p - (n_pairs - NA_BIAS_CASES), 2))
        qrow = pl.multiple_of(i * NA_PAIR_TOKENS, NA_PAIR_TOKENS)
        krow = pl.multiple_of(start * NA_PAIR_TOKENS, NA_PAIR_TOKENS)
        outs = []
        for hp in range(NA_HEADS // 2):
            vpair = v_ref[0, pl.ds(krow, NA_KEY_TOKENS), hp * 2 * NA_HEAD_DIM:(hp + 1) * 2 * NA_HEAD_DIM]
            halves = []
            for sub in range(2):
                hd = 2 * hp + sub
                hsl = slice(hd * NA_HEAD_DIM, (hd + 1) * NA_HEAD_DIM)
                qh = q_ref[0, pl.ds(qrow, NA_PAIR_TOKENS), hsl]
                kslabs = kT_ref[0, pl.ds(start, NA_KEY_PAIRS), hsl, :]
                kT = jnp.concatenate([kslabs[j] for j in range(NA_KEY_PAIRS)], axis=1)
                s = _dot(qh, kT) + bias_ref[case, hd]
                e = jnp.exp(s - jnp.max(s, axis=-1, keepdims=True))
                l = jnp.sum(e, axis=-1, keepdims=True)
                halves.append(_dot(e.astype(BF16), vpair) / l)
            outs.append(jnp.where(first_head, halves[0], halves[1]))
        o_ref[0, pl.ds(qrow, NA_PAIR_TOKENS), :] = jnp.concatenate(outs, axis=1).astype(BF16)
        return carry

    lax.fori_loop(0, pairs_per_step, pair_body, 0)


def _na_attention(q, kT4, v, bias, pairs_per_step):
    B, S, _ = q.shape
    n_pairs = S // NA_PAIR_TOKENS
    steps = n_pairs // pairs_per_step
    tq = pairs_per_step * NA_PAIR_TOKENS
    return pl.pallas_call(
        functools.partial(_na_kernel, pairs_per_step=pairs_per_step, n_pairs=n_pairs),
        grid=(B, steps),
        in_specs=[
            pl.BlockSpec((1, tq, NA_WIDTH), lambda b, j: (b, j, 0)),
            pl.BlockSpec((1, n_pairs, NA_WIDTH, V7X_LANES), lambda b, j: (b, 0, 0, 0)),
            pl.BlockSpec((1, S, NA_WIDTH), lambda b, j: (b, 0, 0)),
            _const_spec((NA_BIAS_CASES, NA_HEADS, NA_PAIR_TOKENS, NA_KEY_TOKENS)),
        ],
        out_specs=pl.BlockSpec((1, tq, NA_WIDTH), lambda b, j: (b, j, 0)),
        out_shape=jax.ShapeDtypeStruct((B, S, NA_WIDTH), BF16),
        compiler_params=_params(2),
        name="na_attn",
    )(q, kT4, v, bias)


def _dft_tables():
    n = GRID_W
    a = np.arange(n)
    ang1 = 2.0 * np.pi * np.outer(a, a) / n
    w_rows = np.concatenate([np.cos(ang1), -np.sin(ang1)], axis=0) / np.sqrt(n)
    c = np.arange(n)[:, None, None]
    d = np.arange(n)[None, :, None]
    b = np.arange(n)[None, None, :]
    ang2 = 2.0 * np.pi * ((c + n * d) * b % (n * n)) / (n * n)
    gr, gi = np.cos(ang2) / np.sqrt(n), -np.sin(ang2) / np.sqrt(n)
    w_cols = np.concatenate([np.concatenate([gr, -gi], axis=2),
                             np.concatenate([gi, gr], axis=2)], axis=1)
    ch = np.arange(F_GROUP_DIM)
    ang3 = 2.0 * np.pi * np.outer(ch, ch) / F_GROUP_DIM
    w_chan = np.concatenate([np.cos(ang3), np.sin(ang3)], axis=0) / np.sqrt(F_GROUP_DIM)
    return tuple(jnp.asarray(t, dtype=F32).astype(BF16) for t in (w_rows, w_cols, w_chan))


def _dft_rows_kernel(w_ref, u_ref, z_ref):
    z_ref[0] = _dot(w_ref[...], u_ref[0]).astype(BF16)


def _dft_rows(u2, w_rows, chunk):
    B, n, width = u2.shape
    return pl.pallas_call(
        _dft_rows_kernel,
        grid=(B, width // chunk),
        in_specs=[
            _const_spec((2 * n, n)),
            pl.BlockSpec((1, n, chunk), lambda b, j: (b, 0, j)),
        ],
        out_specs=pl.BlockSpec((1, 2 * n, chunk), lambda b, j: (b, 0, j)),
        out_shape=jax.ShapeDtypeStruct((B, 2 * n, width), BF16),
        compiler_params=_params(2),
        name="dft_rows",
    )(w_rows, u2)


def _dft_cols_kernel(w_ref, z_ref, y_ref, *, cols_per_step):
    n = GRID_W
    for j in range(cols_per_step):
        c = pl.program_id(1) * cols_per_step + j
        data = jnp.concatenate([z_ref[0, 0, j], z_ref[0, 1, j]], axis=0)
        y = _dot(w_ref[c], data)
        y_ref[0, 0, :, j * F_WIDTH:(j + 1) * F_WIDTH] = y[:n].astype(BF16)
        y_ref[1, 0, :, j * F_WIDTH:(j + 1) * F_WIDTH] = y[n:].astype(BF16)


def _dft_cols(z5, w_cols, cols_per_step):
    B = z5.shape[0]
    n = GRID_W
    return pl.pallas_call(
        functools.partial(_dft_cols_kernel, cols_per_step=cols_per_step),
        grid=(B, n // cols_per_step),
        in_specs=[
            _const_spec((n, 2 * n, 2 * n)),
            pl.BlockSpec((1, 2, cols_per_step, n, F_WIDTH), lambda b, j: (b, 0, j, 0, 0)),
        ],
        out_specs=pl.BlockSpec((2, 1, n, cols_per_step * F_WIDTH), lambda b, j: (0, b, 0, j)),
        out_shape=jax.ShapeDtypeStruct((2, B, n, n * F_WIDTH), BF16),
        compiler_params=_params(2),
        name="dft_cols",
    )(w_cols, z5)


def _out_ffn_kernel(x_ref, ona_ref, y_ref, om_ref, gate_ref, wch_ref, wna_ref, wf_ref, wmo_ref,
                    wout_ref, g2_ref, w1_ref, w2_ref, o_ref, *, ff_chunk):
    yr, yi = y_ref[0], y_ref[1]
    groups = []
    for g in range(F_GROUPS):
        sl = slice(g * F_GROUP_DIM, (g + 1) * F_GROUP_DIM)
        groups.append(_dot(jnp.concatenate([yr[:, sl], yi[:, sl]], axis=1), wch_ref[...]))
    yf = jnp.concatenate(groups, axis=1).astype(BF16)

    gate = lambda br: gate_ref[:, br * D_MODEL:(br + 1) * D_MODEL].astype(F32)
    merged = gate(0) * _dot(ona_ref[...], wna_ref[...])
    merged = merged + gate(1) * _dot(yf, wf_ref[...])
    merged = merged + gate(2) * _dot(om_ref[...], wmo_ref[...])
    x1 = x_ref[...] + _dot(merged.astype(BF16), wout_ref[...])

    ms = jnp.mean(x1 * x1, axis=-1, keepdims=True)
    h2 = (x1 * lax.rsqrt(ms + EPS) * g2_ref[...]).astype(BF16)
    acc = x1
    for j in range(D_FF // ff_chunk):
        sl = slice(j * ff_chunk, (j + 1) * ff_chunk)
        a = jnp.maximum(_dot(h2, w1_ref[:, sl]), 0.0)
        acc = acc + _dot((a * a).astype(BF16), w2_ref[sl, :])
    o_ref[...] = acc


def _out_ffn(x2, ona, y3, om, gates, w_chan, w_na_o, w_f, w_mem_o, w_out, norm2_g, w_ff1, w_ff2,
             tm, ff_chunk):
    T = x2.shape[0]
    tok = lambda w: pl.BlockSpec((tm, w), lambda i: (i, 0))
    return pl.pallas_call(
        functools.partial(_out_ffn_kernel, ff_chunk=ff_chunk),
        grid=(T // tm,),
        in_specs=[
            tok(D_MODEL),
            tok(NA_WIDTH),
            pl.BlockSpec((2, tm, F_WIDTH), lambda i: (0, i, 0)),
            tok(MEM_WIDTH),
            tok(N_BRANCHES * D_MODEL),
            _const_spec((2 * F_GROUP_DIM, F_GROUP_DIM)),
            _const_spec((NA_WIDTH, D_MODEL)),
            _const_spec((F_WIDTH, D_MODEL)),
            _const_spec((MEM_WIDTH, D_MODEL)),
            _const_spec((D_MODEL, D_MODEL)),
            _const_spec((1, D_MODEL)),
            _const_spec((D_MODEL, D_FF)),
            _const_spec((D_FF, D_MODEL)),
        ],
        out_specs=tok(D_MODEL),
        out_shape=jax.ShapeDtypeStruct((T, D_MODEL), F32),
        compiler_params=_params(1),
        name="out_ffn",
    )(x2, ona, y3, om, gates, w_chan, w_na_o.astype(BF16), w_f.astype(BF16), w_mem_o.astype(BF16),
      w_out.astype(BF16), norm2_g.reshape(1, D_MODEL), w_ff1.astype(BF16), w_ff2.astype(BF16))


def kernel(x, mem, norm1_g, w_in, b_gate, na_q_g, na_k_g, na_rpb, w_na_o, w_f, mem_norm_g,
           w_mem_kv, mem_q_g, mem_k_g, w_mem_o, w_out, norm2_g, w_ff1, w_ff2):
    B, S, _ = x.shape
    T = B * S
    rows = S // GRID_W
    assert rows == GRID_W and S % NA_PAIR_TOKENS == 0
    x2 = x.reshape(T, D_MODEL)
    gmean_na = _group_mean_matrix(NA_WIDTH, NA_HEAD_DIM)
    gmean_mem = _group_mean_matrix(MEM_WIDTH, MEM_HEAD_DIM)
    w_rows, w_cols, w_chan = _dft_tables()

    memkT, memv = _mem_kv(mem, mem_norm_g, w_mem_kv, gmean_mem, mem_k_g)
    q, kT4, v, fu, om, gates = _in_proj(x2, norm1_g, w_in, b_gate, na_q_g, na_k_g, mem_q_g,
                                        gmean_na, gmean_mem, memkT, memv, S, tm=512)

    bias = _na_bias_tables(na_rpb, rows)
    ona = _na_attention(q.reshape(B, S, NA_WIDTH), kT4, v.reshape(B, S, NA_WIDTH), bias,
                        pairs_per_step=8)

    z = _dft_rows(fu.reshape(B, GRID_W, GRID_W * F_WIDTH), w_rows, chunk=4096)
    y = _dft_cols(z.reshape(B, 2, GRID_W, GRID_W, F_WIDTH), w_cols, cols_per_step=8)
    y3 = y.reshape(2, T, F_WIDTH)

    out = _out_ffn(x2, ona.reshape(T, NA_WIDTH), y3, om, gates, w_chan, w_na_o, w_f, w_mem_o,
                   w_out, norm2_g, w_ff1, w_ff2, tm=256, ff_chunk=1024)
    return out.reshape(B, S, D_MODEL)
```

```python
import functools

import numpy as np
import jax
import jax.numpy as jnp
from jax import lax
from jax.experimental import pallas as pl
from jax.experimental.pallas import tpu as pltpu

D_MODEL = 1024
GRID_W = 64
NA_HEADS = 8
NA_HEAD_DIM = 64
NA_WIDTH = NA_HEADS * NA_HEAD_DIM
NA_WIN_ROWS = 8
NA_WIN_COLS = 16
F_GROUPS = 4
F_GROUP_DIM = 128
F_WIDTH = F_GROUPS * F_GROUP_DIM
MEM_HEADS = 4
MEM_HEAD_DIM = 128
MEM_WIDTH = MEM_HEADS * MEM_HEAD_DIM
N_BRANCHES = 3
D_FF = 4 * D_MODEL
EPS = 1e-6
NEG_INF = -1e30

V7X_LANES = 128
V7X_VMEM_LIMIT_BYTES = 60 * 1024 * 1024

NA_PAIR_TOKENS = 2 * GRID_W
NA_KEY_PAIRS = 5
NA_KEY_TOKENS = NA_KEY_PAIRS * NA_PAIR_TOKENS
NA_BIAS_CASES = 5

BF16 = jnp.bfloat16
F32 = jnp.float32


def _dot(a, b):
    return jnp.dot(a, b, preferred_element_type=F32)


def _dot_nt(a, b):
    return lax.dot_general(a, b, (((1,), (1,)), ((), ())), preferred_element_type=F32)


def _const_spec(shape):
    return pl.BlockSpec(shape, lambda *_: (0,) * len(shape), pipeline_mode=pl.Buffered(1))


def _params(n_axes):
    return pltpu.CompilerParams(
        dimension_semantics=("arbitrary",) * n_axes,
        vmem_limit_bytes=V7X_VMEM_LIMIT_BYTES,
    )


def _group_mean_matrix(width, group):
    idx = np.arange(width) // group
    return jnp.asarray((idx[:, None] == idx[None, :]).astype(np.float32) / group, dtype=BF16)


def _mem_kv_kernel(mem_ref, g_ref, w_ref, gm_ref, kg_ref, kT_ref, v_ref):
    m = mem_ref[0]
    ms = jnp.mean(m * m, axis=-1, keepdims=True)
    mn = (m * lax.rsqrt(ms + EPS) * g_ref[...]).astype(BF16)
    kv = _dot(mn, w_ref[...])
    k = kv[:, :MEM_WIDTH]
    msk = _dot((k * k).astype(BF16), gm_ref[...])
    kn = k * lax.rsqrt(msk + EPS) * kg_ref[...]
    kT_ref[0] = kn.T.astype(BF16)
    v_ref[0] = kv[:, MEM_WIDTH:].astype(BF16)


def _mem_kv(mem, mem_norm_g, w_mem_kv, gmean_mem, mem_k_g):
    B, M, _ = mem.shape
    return pl.pallas_call(
        _mem_kv_kernel,
        grid=(B,),
        in_specs=[
            pl.BlockSpec((1, M, D_MODEL), lambda b: (b, 0, 0)),
            _const_spec((1, D_MODEL)),
            _const_spec((D_MODEL, 2 * MEM_WIDTH)),
            _const_spec((MEM_WIDTH, MEM_WIDTH)),
            _const_spec((1, MEM_WIDTH)),
        ],
        out_specs=[
            pl.BlockSpec((1, MEM_WIDTH, M), lambda b: (b, 0, 0)),
            pl.BlockSpec((1, M, MEM_WIDTH), lambda b: (b, 0, 0)),
        ],
        out_shape=[
            jax.ShapeDtypeStruct((B, MEM_WIDTH, M), BF16),
            jax.ShapeDtypeStruct((B, M, MEM_WIDTH), BF16),
        ],
        compiler_params=_params(1),
        name="mem_kv",
    )(mem, mem_norm_g.reshape(1, D_MODEL), w_mem_kv.astype(BF16), gmean_mem,
      jnp.tile(mem_k_g, MEM_HEADS).reshape(1, MEM_WIDTH))


def _in_proj_kernel(x_ref, g1_ref, wq_ref, wkT_ref, wv_ref, wf_ref, wm_ref, wg_ref, bg_ref,
                    gna_ref, gmem_ref, qg_ref, kg_ref, mqg_ref, memkT_ref, memv_ref,
                    q_ref, kT_ref, v_ref, fu_ref, om_ref, gate_ref):
    x = x_ref[...]
    ms = jnp.mean(x * x, axis=-1, keepdims=True)
    h = (x * lax.rsqrt(ms + EPS) * g1_ref[...]).astype(BF16)

    zq = _dot(h, wq_ref[...])
    msq = _dot((zq * zq).astype(BF16), gna_ref[...])
    q_ref[...] = (zq * lax.rsqrt(msq + EPS) * qg_ref[...]).astype(BF16)

    zkT = _dot_nt(wkT_ref[...], h)
    mskT = _dot(gna_ref[...], (zkT * zkT).astype(BF16))
    kT = (zkT * lax.rsqrt(mskT + EPS) * kg_ref[...]).astype(BF16)
    for j in range(kT_ref.shape[1]):
        kT_ref[0, j] = kT[:, j * V7X_LANES:(j + 1) * V7X_LANES]

    v_ref[...] = _dot(h, wv_ref[...]).astype(BF16)
    fu_ref[...] = _dot(h, wf_ref[...]).astype(BF16)

    zm = _dot(h, wm_ref[...])
    msm = _dot((zm * zm).astype(BF16), gmem_ref[...])
    qm = (zm * lax.rsqrt(msm + EPS) * mqg_ref[...]).astype(BF16)
    outs = []
    for hd in range(MEM_HEADS):
        sl = slice(hd * MEM_HEAD_DIM, (hd + 1) * MEM_HEAD_DIM)
        s = _dot(qm[:, sl], memkT_ref[0, sl, :])
        e = jnp.exp(s - jnp.max(s, axis=-1, keepdims=True))
        l = jnp.sum(e, axis=-1, keepdims=True)
        outs.append(_dot(e.astype(BF16), memv_ref[0, :, sl]) / l)
    om_ref[...] = jnp.concatenate(outs, axis=1).astype(BF16)

    for br in range(N_BRANCHES):
        sl = slice(br * D_MODEL, (br + 1) * D_MODEL)
        zg = _dot(h, wg_ref[:, sl]) + bg_ref[:, sl]
        gate_ref[:, sl] = jax.nn.sigmoid(zg).astype(BF16)


def _in_proj(x2, norm1_g, w_in, b_gate, na_q_g, na_k_g, mem_q_g, gmean_na, gmean_mem,
             memkT, memv, tokens_per_batch, tm):
    T = x2.shape[0]
    M = memv.shape[1]
    tiles_per_batch = tokens_per_batch // tm
    slabs = tm // V7X_LANES
    B = T // tokens_per_batch
    wb = w_in.astype(BF16)
    o = 0
    wq = wb[:, o:o + NA_WIDTH]; o += NA_WIDTH
    wkT = wb[:, o:o + NA_WIDTH].T; o += NA_WIDTH
    wv = wb[:, o:o + NA_WIDTH]; o += NA_WIDTH
    wf = wb[:, o:o + F_WIDTH]; o += F_WIDTH
    wm = wb[:, o:o + MEM_WIDTH]; o += MEM_WIDTH
    wg = wb[:, o:]
    na_scale = np.float32(1.0 / np.sqrt(NA_HEAD_DIM))
    mem_scale = np.float32(1.0 / np.sqrt(MEM_HEAD_DIM))
    qg = (jnp.tile(na_q_g, NA_HEADS) * na_scale).reshape(1, NA_WIDTH)
    kg = jnp.tile(na_k_g, NA_HEADS).reshape(NA_WIDTH, 1)
    mqg = (jnp.tile(mem_q_g, MEM_HEADS) * mem_scale).reshape(1, MEM_WIDTH)
    tok = lambda w: pl.BlockSpec((tm, w), lambda i: (i, 0))
    batch_of = lambda i: i // tiles_per_batch
    return pl.pallas_call(
        _in_proj_kernel,
        grid=(T // tm,),
        in_specs=[
            tok(D_MODEL),
            _const_spec((1, D_MODEL)),
            _const_spec((D_MODEL, NA_WIDTH)),
            _const_spec((NA_WIDTH, D_MODEL)),
            _const_spec((D_MODEL, NA_WIDTH)),
            _const_spec((D_MODEL, F_WIDTH)),
            _const_spec((D_MODEL, MEM_WIDTH)),
            _const_spec((D_MODEL, N_BRANCHES * D_MODEL)),
            _const_spec((1, N_BRANCHES * D_MODEL)),
            _const_spec((NA_WIDTH, NA_WIDTH)),
            _const_spec((MEM_WIDTH, MEM_WIDTH)),
            _const_spec((1, NA_WIDTH)),
            _const_spec((NA_WIDTH, 1)),
            _const_spec((1, MEM_WIDTH)),
            pl.BlockSpec((1, MEM_WIDTH, M), lambda i: (batch_of(i), 0, 0)),
            pl.BlockSpec((1, M, MEM_WIDTH), lambda i: (batch_of(i), 0, 0)),
        ],
        out_specs=[
            tok(NA_WIDTH),
            pl.BlockSpec((1, slabs, NA_WIDTH, V7X_LANES),
                         lambda i: (batch_of(i), i % tiles_per_batch, 0, 0)),
            tok(NA_WIDTH),
            tok(F_WIDTH),
            tok(MEM_WIDTH),
            tok(N_BRANCHES * D_MODEL),
        ],
        out_shape=[
            jax.ShapeDtypeStruct((T, NA_WIDTH), BF16),
            jax.ShapeDtypeStruct((B, tokens_per_batch // V7X_LANES, NA_WIDTH, V7X_LANES), BF16),
            jax.ShapeDtypeStruct((T, NA_WIDTH), BF16),
            jax.ShapeDtypeStruct((T, F_WIDTH), BF16),
            jax.ShapeDtypeStruct((T, MEM_WIDTH), BF16),
            jax.ShapeDtypeStruct((T, N_BRANCHES * D_MODEL), BF16),
        ],
        compiler_params=_params(1),
        name="in_proj",
    )(x2, norm1_g.reshape(1, D_MODEL), wq, wkT, wv, wf, wm, wg,
      b_gate.reshape(1, N_BRANCHES * D_MODEL), gmean_na, gmean_mem, qg, kg, mqg, memkT, memv)


def _na_bias_tables(rpb, rows):
    n_pairs = rows // 2
    wr = min(NA_WIN_ROWS, rows)
    reps = np.array([0, 1, 2, n_pairs - 2, n_pairs - 1])
    starts = np.clip(reps - 2, 0, n_pairs - NA_KEY_PAIRS)
    n_dr, n_dc = 2 * NA_WIN_ROWS - 1, 2 * NA_WIN_COLS - 1
    c = np.arange(GRID_W)[:, None]
    kc = np.arange(GRID_W)[None, :]
    cs = np.clip(c - NA_WIN_COLS // 2, 0, GRID_W - NA_WIN_COLS)
    col_valid = (kc >= cs) & (kc < cs + NA_WIN_COLS)
    dc = np.clip(kc - c, -(NA_WIN_COLS - 1), NA_WIN_COLS - 1) + (NA_WIN_COLS - 1)
    onehot = (np.arange(n_dc)[:, None] == dc.reshape(1, -1)).astype(np.float32)
    cols = jnp.dot(rpb.astype(F32).reshape(NA_HEADS * n_dr, n_dc), onehot,
                   precision=lax.Precision.HIGHEST)
    cols = jnp.where(col_valid.reshape(1, -1), cols, NEG_INF).reshape(NA_HEADS, n_dr, GRID_W, GRID_W)
    r = (2 * reps[:, None] + np.arange(2)[None, :])[:, :, None]
    kr = (2 * starts[:, None] + np.arange(2 * NA_KEY_PAIRS)[None, :])[:, None, :]
    rs = np.clip(r - wr // 2, 0, rows - wr)
    row_valid = (kr >= rs) & (kr < rs + wr)
    dr = np.clip(kr - r + (NA_WIN_ROWS - 1), 0, n_dr - 1)
    blocks = cols[:, dr]
    blocks = jnp.where(row_valid[None, :, :, :, None, None], blocks, NEG_INF)
    return jnp.transpose(blocks, (1, 0, 2, 4, 3, 5)).reshape(
        len(reps), NA_HEADS, NA_PAIR_TOKENS, NA_KEY_TOKENS)


def _na_kernel(q_ref, kT_ref, v_ref, bias_ref, o_ref, *, pairs_per_step, n_pairs):
    lane = lax.broadcasted_iota(jnp.int32, (NA_PAIR_TOKENS, 2 * NA_HEAD_DIM), 1)
    first_head = lane < NA_HEAD_DIM

    def pair_body(i, carry):
        p = pl.program_id(1) * pairs_per_step + i
        start = jnp.clip(p - 2, 0, n_pairs - NA_KEY_PAIRS)
        case = jnp.where(p < 2, p, jnp.where(p >= n_pairs - 2, p - (n_pairs - NA_BIAS_CASES), 2))
        qrow = pl.multiple_of(i * NA_PAIR_TOKENS, NA_PAIR_TOKENS)
        krow = pl.multiple_of(start * NA_PAIR_TOKENS, NA_PAIR_TOKENS)
        outs = []
        for hp in range(NA_HEADS // 2):
            vpair = v_ref[0, pl.ds(krow, NA_KEY_TOKENS), hp * 2 * NA_HEAD_DIM:(hp + 1) * 2 * NA_HEAD_DIM]
            halves = []
            for sub in range(2):
                hd = 2 * hp + sub
                hsl = slice(hd * NA_HEAD_DIM, (hd + 1) * NA_HEAD_DIM)
                qh = q_ref[0, pl.ds(qrow, NA_PAIR_TOKENS), hsl]
                kslabs = kT_ref[0, pl.ds(start, NA_KEY_PAIRS), hsl, :]
                kT = jnp.concatenate([kslabs[j] for j in range(NA_KEY_PAIRS)], axis=1)
                s = _dot(qh, kT) + bias_ref[case, hd]
                e = jnp.exp(s - jnp.max(s, axis=-1, keepdims=True))
                l = jnp.sum(e, axis=-1, keepdims=True)
                halves.append(_dot(e.astype(BF16), vpair) / l)
            outs.append(jnp.where(first_head, halves[0], halves[1]))
        o_ref[0, pl.ds(qrow, NA_PAIR_TOKENS), :] = jnp.concatenate(outs, axis=1).astype(BF16)
        return carry

    lax.fori_loop(0, pairs_per_step, pair_body, 0)


def _na_attention(q, kT4, v, bias, pairs_per_step):
    B, S, _ = q.shape
    n_pairs = S // NA_PAIR_TOKENS
    steps = n_pairs // pairs_per_step
    tq = pairs_per_step * NA_PAIR_TOKENS
    return pl.pallas_call(
        functools.partial(_na_kernel, pairs_per_step=pairs_per_step, n_pairs=n_pairs),
        grid=(B, steps),
        in_specs=[
            pl.BlockSpec((1, tq, NA_WIDTH), lambda b, j: (b, j, 0)),
            pl.BlockSpec((1, n_pairs, NA_WIDTH, V7X_LANES), lambda b, j: (b, 0, 0, 0)),
            pl.BlockSpec((1, S, NA_WIDTH), lambda b, j: (b, 0, 0)),
            _const_spec((NA_BIAS_CASES, NA_HEADS, NA_PAIR_TOKENS, NA_KEY_TOKENS)),
        ],
        out_specs=pl.BlockSpec((1, tq, NA_WIDTH), lambda b, j: (b, j, 0)),
        out_shape=jax.ShapeDtypeStruct((B, S, NA_WIDTH), BF16),
        compiler_params=_params(2),
        name="na_attn",
    )(q, kT4, v, bias)


def _dft_tables():
    n = GRID_W
    a = np.arange(n)
    ang1 = 2.0 * np.pi * np.outer(a, a) / n
    w_rows = np.concatenate([np.cos(ang1), -np.sin(ang1)], axis=0) / np.sqrt(n)
    c = np.arange(n)[:, None, None]
    d = np.arange(n)[None, :, None]
    b = np.arange(n)[None, None, :]
    ang2 = 2.0 * np.pi * ((c + n * d) * b % (n * n)) / (n * n)
    gr, gi = np.cos(ang2) / np.sqrt(n), -np.sin(ang2) / np.sqrt(n)
    w_cols = np.concatenate([np.concatenate([gr, -gi], axis=2),
                             np.concatenate([gi, gr], axis=2)], axis=1)
    ch = np.arange(F_GROUP_DIM)
    ang3 = 2.0 * np.pi * np.outer(ch, ch) / F_GROUP_DIM
    w_chan = np.concatenate([np.cos(ang3), np.sin(ang3)], axis=0) / np.sqrt(F_GROUP_DIM)
    return tuple(jnp.asarray(t, dtype=F32).astype(BF16) for t in (w_rows, w_cols, w_chan))


def _dft_rows_kernel(w_ref, u_ref, z_ref):
    z_ref[0] = _dot(w_ref[...], u_ref[0]).astype(BF16)


def _dft_rows(u2, w_rows, chunk):
    B, n, width = u2.shape
    return pl.pallas_call(
        _dft_rows_kernel,
        grid=(B, width // chunk),
        in_specs=[
            _const_spec((2 * n, n)),
            pl.BlockSpec((1, n, chunk), lambda b, j: (b, 0, j)),
        ],
        out_specs=pl.BlockSpec((1, 2 * n, chunk), lambda b, j: (b, 0, j)),
        out_shape=jax.ShapeDtypeStruct((B, 2 * n, width), BF16),
        compiler_params=_params(2),
        name="dft_rows",
    )(w_rows, u2)


def _dft_cols_kernel(w_ref, z_ref, y_ref, *, cols_per_step):
    n = GRID_W
    for j in range(cols_per_step):
        c = pl.program_id(1) * cols_per_step + j
        data = jnp.concatenate([z_ref[0, 0, j], z_ref[0, 1, j]], axis=0)
        y = _dot(w_ref[c], data)
        y_ref[0, 0, :, j * F_WIDTH:(j + 1) * F_WIDTH] = y[:n].astype(BF16)
        y_ref[1, 0, :, j * F_WIDTH:(j + 1) * F_WIDTH] = y[n:].astype(BF16)


def _dft_cols(z5, w_cols, cols_per_step):
    B = z5.shape[0]
    n = GRID_W
    return pl.pallas_call(
        functools.partial(_dft_cols_kernel, cols_per_step=cols_per_step),
        grid=(B, n // cols_per_step),
        in_specs=[
            _const_spec((n, 2 * n, 2 * n)),
            pl.BlockSpec((1, 2, cols_per_step, n, F_WIDTH), lambda b, j: (b, 0, j, 0, 0)),
        ],
        out_specs=pl.BlockSpec((2, 1, n, cols_per_step * F_WIDTH), lambda b, j: (0, b, 0, j)),
        out_shape=jax.ShapeDtypeStruct((2, B, n, n * F_WIDTH), BF16),
        compiler_params=_params(2),
        name="dft_cols",
    )(w_cols, z5)


def _out_ffn_kernel(x_ref, ona_ref, y_ref, om_ref, gate_ref, wch_ref, wna_ref, wf_ref, wmo_ref,
                    wout_ref, g2_ref, w1_ref, w2_ref, o_ref, *, ff_chunk):
    yr, yi = y_ref[0], y_ref[1]
    groups = []
    for g in range(F_GROUPS):
        sl = slice(g * F_GROUP_DIM, (g + 1) * F_GROUP_DIM)
        groups.append(_dot(jnp.concatenate([yr[:, sl], yi[:, sl]], axis=1), wch_ref[...]))
    yf = jnp.concatenate(groups, axis=1).astype(BF16)

    gate = lambda br: gate_ref[:, br * D_MODEL:(br + 1) * D_MODEL].astype(F32)
    merged = gate(0) * _dot(ona_ref[...], wna_ref[...])
    merged = merged + gate(1) * _dot(yf, wf_ref[...])
    merged = merged + gate(2) * _dot(om_ref[...], wmo_ref[...])
    x1 = x_ref[...] + _dot(merged.astype(BF16), wout_ref[...])

    ms = jnp.mean(x1 * x1, axis=-1, keepdims=True)
    h2 = (x1 * lax.rsqrt(ms + EPS) * g2_ref[...]).astype(BF16)
    acc = x1
    for j in range(D_FF // ff_chunk):
        sl = slice(j * ff_chunk, (j + 1) * ff_chunk)
        a = jnp.maximum(_dot(h2, w1_ref[:, sl]), 0.0)
        acc = acc + _dot((a * a).astype(BF16), w2_ref[sl, :])
    o_ref[...] = acc


def _out_ffn(x2, ona, y3, om, gates, w_chan, w_na_o, w_f, w_mem_o, w_out, norm2_g, w_ff1, w_ff2,
             tm, ff_chunk):
    T = x2.shape[0]
    tok = lambda w: pl.BlockSpec((tm, w), lambda i: (i, 0))
    return pl.pallas_call(
        functools.partial(_out_ffn_kernel, ff_chunk=ff_chunk),
        grid=(T // tm,),
        in_specs=[
            tok(D_MODEL),
            tok(NA_WIDTH),
            pl.BlockSpec((2, tm, F_WIDTH), lambda i: (0, i, 0)),
            tok(MEM_WIDTH),
            tok(N_BRANCHES * D_MODEL),
            _const_spec((2 * F_GROUP_DIM, F_GROUP_DIM)),
            _const_spec((NA_WIDTH, D_MODEL)),
            _const_spec((F_WIDTH, D_MODEL)),
            _const_spec((MEM_WIDTH, D_MODEL)),
            _const_spec((D_MODEL, D_MODEL)),
            _const_spec((1, D_MODEL)),
            _const_spec((D_MODEL, D_FF)),
            _const_spec((D_FF, D_MODEL)),
        ],
        out_specs=tok(D_MODEL),
        out_shape=jax.ShapeDtypeStruct((T, D_MODEL), F32),
        compiler_params=_params(1),
        name="out_ffn",
    )(x2, ona, y3, om, gates, w_chan, w_na_o.astype(BF16), w_f.astype(BF16), w_mem_o.astype(BF16),
      w_out.astype(BF16), norm2_g.reshape(1, D_MODEL), w_ff1.astype(BF16), w_ff2.astype(BF16))


def kernel(x, mem, norm1_g, w_in, b_gate, na_q_g, na_k_g, na_rpb, w_na_o, w_f, mem_norm_g,
           w_mem_kv, mem_q_g, mem_k_g, w_mem_o, w_out, norm2_g, w_ff1, w_ff2):
    B, S, _ = x.shape
    T = B * S
    rows = S // GRID_W
    assert rows == GRID_W and S % NA_PAIR_TOKENS == 0
    x2 = x.reshape(T, D_MODEL)
    gmean_na = _group_mean_matrix(NA_WIDTH, NA_HEAD_DIM)
    gmean_mem = _group_mean_matrix(MEM_WIDTH, MEM_HEAD_DIM)
    w_rows, w_cols, w_chan = _dft_tables()

    memkT, memv = _mem_kv(mem, mem_norm_g, w_mem_kv, gmean_mem, mem_k_g)
    q, kT4, v, fu, om, gates = _in_proj(x2, norm1_g, w_in, b_gate, na_q_g, na_k_g, mem_q_g,
                                        gmean_na, gmean_mem, memkT, memv, S, tm=512)

    bias = _na_bias_tables(na_rpb, rows)
    ona = _na_attention(q.reshape(B, S, NA_WIDTH), kT4, v.reshape(B, S, NA_WIDTH), bias,
                        pairs_per_step=8)

    z = _dft_rows(fu.reshape(B, GRID_W, GRID_W * F_WIDTH), w_rows, chunk=4096)
    y = _dft_cols(z.reshape(B, 2, GRID_W, GRID_W, F_WIDTH), w_cols, cols_per_step=8)
    y3 = y.reshape(2, T, F_WIDTH)

    out = _out_ffn(x2, ona.reshape(T, NA_WIDTH), y3, om, gates, w_chan, w_na_o, w_f, w_mem_o,
                   w_out, norm2_g, w_ff1, w_ff2, tm=256, ff_chunk=1024)
    return out.reshape(B, S, D_MODEL)
```

```python
import functools

import numpy as np
import jax
import jax.numpy as jnp
from jax import lax
from jax.experimental import pallas as pl
from jax.experimental.pallas import tpu as pltpu

D_MODEL = 1024
GRID_W = 64
NA_HEADS = 8
NA_HEAD_DIM = 64
NA_WIDTH = NA_HEADS * NA_HEAD_DIM
NA_WIN_ROWS = 8
NA_WIN_COLS = 16
F_GROUPS = 4
F_GROUP_DIM = 128
F_WIDTH = F_GROUPS * F_GROUP_DIM
MEM_HEADS = 4
MEM_HEAD_DIM = 128
MEM_WIDTH = MEM_HEADS * MEM_HEAD_DIM
N_BRANCHES = 3
D_FF = 4 * D_MODEL
EPS = 1e-6
NEG_INF = -1e30

V7X_LANES = 128
V7X_VMEM_LIMIT_BYTES = 60 * 1024 * 1024

NA_PAIR_TOKENS = 2 * GRID_W
NA_KEY_PAIRS = 5
NA_KEY_TOKENS = NA_KEY_PAIRS * NA_PAIR_TOKENS
NA_BIAS_CASES = 5
NA_PAIRS_PER_ITER = 2
NA_SCORES_AHEAD = 2

BF16 = jnp.bfloat16
F32 = jnp.float32


def _dot(a, b):
    return jnp.dot(a, b, preferred_element_type=F32)


def _dot_nt(a, b):
    return lax.dot_general(a, b, (((1,), (1,)), ((), ())), preferred_element_type=F32)


def _const_spec(shape):
    return pl.BlockSpec(shape, lambda *_: (0,) * len(shape), pipeline_mode=pl.Buffered(1))


def _params(n_axes):
    return pltpu.CompilerParams(
        dimension_semantics=("arbitrary",) * n_axes,
        vmem_limit_bytes=V7X_VMEM_LIMIT_BYTES,
    )


def _group_mean_matrix(width, group):
    idx = np.arange(width) // group
    return jnp.asarray((idx[:, None] == idx[None, :]).astype(np.float32) / group, dtype=BF16)


def _mem_kv_kernel(mem_ref, g_ref, w_ref, gm_ref, kg_ref, kT_ref, v_ref):
    m = mem_ref[0]
    ms = jnp.mean(m * m, axis=-1, keepdims=True)
    mn = (m * lax.rsqrt(ms + EPS) * g_ref[...]).astype(BF16)
    kv = _dot(mn, w_ref[...])
    k = kv[:, :MEM_WIDTH]
    msk = _dot((k * k).astype(BF16), gm_ref[...])
    kn = k * lax.rsqrt(msk + EPS) * kg_ref[...]
    kT_ref[0] = kn.T.astype(BF16)
    v_ref[0] = kv[:, MEM_WIDTH:].astype(BF16)


def _mem_kv(mem, mem_norm_g, w_mem_kv, gmean_mem, mem_k_g):
    B, M, _ = mem.shape
    return pl.pallas_call(
        _mem_kv_kernel,
        grid=(B,),
        in_specs=[
            pl.BlockSpec((1, M, D_MODEL), lambda b: (b, 0, 0)),
            _const_spec((1, D_MODEL)),
            _const_spec((D_MODEL, 2 * MEM_WIDTH)),
            _const_spec((MEM_WIDTH, MEM_WIDTH)),
            _const_spec((1, MEM_WIDTH)),
        ],
        out_specs=[
            pl.BlockSpec((1, MEM_WIDTH, M), lambda b: (b, 0, 0)),
            pl.BlockSpec((1, M, MEM_WIDTH), lambda b: (b, 0, 0)),
        ],
        out_shape=[
            jax.ShapeDtypeStruct((B, MEM_WIDTH, M), BF16),
            jax.ShapeDtypeStruct((B, M, MEM_WIDTH), BF16),
        ],
        compiler_params=_params(1),
        name="mem_kv",
    )(mem, mem_norm_g.reshape(1, D_MODEL), w_mem_kv.astype(BF16), gmean_mem,
      jnp.tile(mem_k_g, MEM_HEADS).reshape(1, MEM_WIDTH))


def _in_proj_kernel(x_ref, g1_ref, wq_ref, wkT_ref, wv_ref, wf_ref, wm_ref, wg_ref, bg_ref,
                    gna_ref, gmem_ref, qg_ref, kg_ref, mqg_ref, memkT_ref, memv_ref,
                    q_ref, kT_ref, v_ref, fu_ref, om_ref, gate_ref):
    x = x_ref[...]
    ms = jnp.mean(x * x, axis=-1, keepdims=True)
    h = (x * lax.rsqrt(ms + EPS) * g1_ref[...]).astype(BF16)

    zq = _dot(h, wq_ref[...])
    msq = _dot((zq * zq).astype(BF16), gna_ref[...])
    q_ref[...] = (zq * lax.rsqrt(msq + EPS) * qg_ref[...]).astype(BF16)

    zkT = _dot_nt(wkT_ref[...], h)
    mskT = _dot(gna_ref[...], (zkT * zkT).astype(BF16))
    kT = (zkT * lax.rsqrt(mskT + EPS) * kg_ref[...]).astype(BF16)
    for j in range(kT_ref.shape[1]):
        kT_ref[0, j] = kT[:, j * V7X_LANES:(j + 1) * V7X_LANES]

    zv = _dot(h, wv_ref[...])
    low_half = lax.broadcasted_iota(jnp.int32, (zv.shape[0], V7X_LANES), 1) < NA_HEAD_DIM
    for hp in range(NA_HEADS // 2):
        pair = zv[:, hp * V7X_LANES:(hp + 1) * V7X_LANES]
        v_ref[:, (2 * hp) * V7X_LANES:(2 * hp + 1) * V7X_LANES] = jnp.where(low_half, pair, 1.0).astype(BF16)
        v_ref[:, (2 * hp + 1) * V7X_LANES:(2 * hp + 2) * V7X_LANES] = jnp.where(low_half, 1.0, pair).astype(BF16)
    fu_ref[...] = _dot(h, wf_ref[...]).astype(BF16)

    zm = _dot(h, wm_ref[...])
    msm = _dot((zm * zm).astype(BF16), gmem_ref[...])
    qm = (zm * lax.rsqrt(msm + EPS) * mqg_ref[...]).astype(BF16)
    outs = []
    for hd in range(MEM_HEADS):
        sl = slice(hd * MEM_HEAD_DIM, (hd + 1) * MEM_HEAD_DIM)
        s = _dot(qm[:, sl], memkT_ref[0, sl, :])
        e = jnp.exp(s - jnp.max(s, axis=-1, keepdims=True))
        l = jnp.sum(e, axis=-1, keepdims=True)
        outs.append(_dot(e.astype(BF16), memv_ref[0, :, sl]) / l)
    om_ref[...] = jnp.concatenate(outs, axis=1).astype(BF16)

    for br in range(N_BRANCHES):
        sl = slice(br * D_MODEL, (br + 1) * D_MODEL)
        zg = _dot(h, wg_ref[:, sl]) + bg_ref[:, sl]
        gate_ref[:, sl] = jax.nn.sigmoid(zg).astype(BF16)


def _in_proj(x2, norm1_g, w_in, b_gate, na_q_g, na_k_g, mem_q_g, gmean_na, gmean_mem,
             memkT, memv, tokens_per_batch, tm):
    T = x2.shape[0]
    M = memv.shape[1]
    tiles_per_batch = tokens_per_batch // tm
    slabs = tm // V7X_LANES
    B = T // tokens_per_batch
    wb = w_in.astype(BF16)
    o = 0
    wq = wb[:, o:o + NA_WIDTH]; o += NA_WIDTH
    wkT = wb[:, o:o + NA_WIDTH].T; o += NA_WIDTH
    wv = wb[:, o:o + NA_WIDTH]; o += NA_WIDTH
    wf = wb[:, o:o + F_WIDTH]; o += F_WIDTH
    wm = wb[:, o:o + MEM_WIDTH]; o += MEM_WIDTH
    wg = wb[:, o:]
    na_scale = np.float32(1.0 / np.sqrt(NA_HEAD_DIM))
    mem_scale = np.float32(1.0 / np.sqrt(MEM_HEAD_DIM))
    qg = (jnp.tile(na_q_g, NA_HEADS) * na_scale).reshape(1, NA_WIDTH)
    kg = jnp.tile(na_k_g, NA_HEADS).reshape(NA_WIDTH, 1)
    mqg = (jnp.tile(mem_q_g, MEM_HEADS) * mem_scale).reshape(1, MEM_WIDTH)
    tok = lambda w: pl.BlockSpec((tm, w), lambda i: (i, 0))
    batch_of = lambda i: i // tiles_per_batch
    return pl.pallas_call(
        _in_proj_kernel,
        grid=(T // tm,),
        in_specs=[
            tok(D_MODEL),
            _const_spec((1, D_MODEL)),
            _const_spec((D_MODEL, NA_WIDTH)),
            _const_spec((NA_WIDTH, D_MODEL)),
            _const_spec((D_MODEL, NA_WIDTH)),
            _const_spec((D_MODEL, F_WIDTH)),
            _const_spec((D_MODEL, MEM_WIDTH)),
            _const_spec((D_MODEL, N_BRANCHES * D_MODEL)),
            _const_spec((1, N_BRANCHES * D_MODEL)),
            _const_spec((NA_WIDTH, NA_WIDTH)),
            _const_spec((MEM_WIDTH, MEM_WIDTH)),
            _const_spec((1, NA_WIDTH)),
            _const_spec((NA_WIDTH, 1)),
            _const_spec((1, MEM_WIDTH)),
            pl.BlockSpec((1, MEM_WIDTH, M), lambda i: (batch_of(i), 0, 0)),
            pl.BlockSpec((1, M, MEM_WIDTH), lambda i: (batch_of(i), 0, 0)),
        ],
        out_specs=[
            tok(NA_WIDTH),
            pl.BlockSpec((1, slabs, NA_WIDTH, V7X_LANES),
                         lambda i: (batch_of(i), i % tiles_per_batch, 0, 0)),
            tok(NA_HEADS * V7X_LANES),
            tok(F_WIDTH),
            tok(MEM_WIDTH),
            tok(N_BRANCHES * D_MODEL),
        ],
        out_shape=[
            jax.ShapeDtypeStruct((T, NA_WIDTH), BF16),
            jax.ShapeDtypeStruct((B, tokens_per_batch // V7X_LANES, NA_WIDTH, V7X_LANES), BF16),
            jax.ShapeDtypeStruct((T, NA_HEADS * V7X_LANES), BF16),
            jax.ShapeDtypeStruct((T, F_WIDTH), BF16),
            jax.ShapeDtypeStruct((T, MEM_WIDTH), BF16),
            jax.ShapeDtypeStruct((T, N_BRANCHES * D_MODEL), BF16),
        ],
        compiler_params=_params(1),
        name="in_proj",
    )(x2, norm1_g.reshape(1, D_MODEL), wq, wkT, wv, wf, wm, wg,
      b_gate.reshape(1, N_BRANCHES * D_MODEL), gmean_na, gmean_mem, qg, kg, mqg, memkT, memv)


def _na_bias_tables(rpb, rows):
    n_pairs = rows // 2
    wr = min(NA_WIN_ROWS, rows)
    reps = np.array([0, 1, 2, n_pairs - 2, n_pairs - 1])
    starts = np.clip(reps - 2, 0, n_pairs - NA_KEY_PAIRS)
    n_dr, n_dc = 2 * NA_WIN_ROWS - 1, 2 * NA_WIN_COLS - 1
    c = np.arange(GRID_W)[:, None]
    kc = np.arange(GRID_W)[None, :]
    cs = np.clip(c - NA_WIN_COLS // 2, 0, GRID_W - NA_WIN_COLS)
    col_valid = (kc >= cs) & (kc < cs + NA_WIN_COLS)
    dc = np.clip(kc - c, -(NA_WIN_COLS - 1), NA_WIN_COLS - 1) + (NA_WIN_COLS - 1)
    onehot = (np.arange(n_dc)[:, None] == dc.reshape(1, -1)).astype(np.float32)
    cols = jnp.dot(rpb.astype(F32).reshape(NA_HEADS * n_dr, n_dc), onehot,
                   precision=lax.Precision.HIGHEST)
    cols = jnp.where(col_valid.reshape(1, -1), cols, NEG_INF).reshape(NA_HEADS, n_dr, GRID_W, GRID_W)
    r = (2 * reps[:, None] + np.arange(2)[None, :])[:, :, None]
    kr = (2 * starts[:, None] + np.arange(2 * NA_KEY_PAIRS)[None, :])[:, None, :]
    rs = np.clip(r - wr // 2, 0, rows - wr)
    row_valid = (kr >= rs) & (kr < rs + wr)
    dr = np.clip(kr - r + (NA_WIN_ROWS - 1), 0, n_dr - 1)
    blocks = cols[:, dr]
    blocks = jnp.where(row_valid[None, :, :, :, None, None], blocks, NEG_INF)
    return jnp.transpose(blocks, (1, 0, 2, 4, 3, 5)).reshape(
        len(reps), NA_HEADS, NA_PAIR_TOKENS, NA_KEY_TOKENS)


def _na_kernel(q_ref, kT_ref, v_ref, bias_ref, o_ref, *, pairs_per_step, n_pairs):
    lane = lax.broadcasted_iota(jnp.int32, (NA_PAIR_TOKENS, 2 * NA_HEAD_DIM), 1)
    first_head = lane < NA_HEAD_DIM

    def pair_coords(i):
        p = pl.program_id(1) * pairs_per_step + i
        start = jnp.clip(p - 2, 0, n_pairs - NA_KEY_PAIRS)
        case = jnp.where(p < 2, p, jnp.where(p >= n_pairs - 2, p - (n_pairs - NA_BIAS_CASES), 2))
        qrow = pl.multiple_of(i * NA_PAIR_TOKENS, NA_PAIR_TOKENS)
        krow = pl.multiple_of(start * NA_PAIR_TOKENS, NA_PAIR_TOKENS)
        return start, case, qrow, krow

    def scores(coords, hd):
        start, case, qrow, _ = coords
        hsl = slice(hd * NA_HEAD_DIM, (hd + 1) * NA_HEAD_DIM)
        qh = q_ref[0, pl.ds(qrow, NA_PAIR_TOKENS), hsl]
        kslabs = kT_ref[0, pl.ds(start, NA_KEY_PAIRS), hsl, :]
        kT = jnp.concatenate([kslabs[j] for j in range(NA_KEY_PAIRS)], axis=1)
        return _dot(qh, kT) + bias_ref[case, hd]

    def body(it, carry):
        coords = [pair_coords(it * NA_PAIRS_PER_ITER + j) for j in range(NA_PAIRS_PER_ITER)]
        items = [(j, hd) for j in range(NA_PAIRS_PER_ITER) for hd in range(NA_HEADS)]
        pending = [scores(coords[j], hd) for j, hd in items[:NA_SCORES_AHEAD]]
        halves = []
        for n, (j, hd) in enumerate(items):
            s = pending.pop(0)
            if n + NA_SCORES_AHEAD < len(items):
                jn, hn = items[n + NA_SCORES_AHEAD]
                pending.append(scores(coords[jn], hn))
            _, _, qrow, krow = coords[j]
            vext = v_ref[0, pl.ds(krow, NA_KEY_TOKENS), hd * 2 * NA_HEAD_DIM:(hd + 1) * 2 * NA_HEAD_DIM]
            e = jnp.exp((s - jnp.max(s, axis=-1, keepdims=True)).astype(BF16))
            r = _dot(e, vext)
            halves.append(r / pltpu.roll(r, NA_HEAD_DIM, axis=1))
            if hd == NA_HEADS - 1:
                outs = [jnp.where(first_head, halves[2 * k], halves[2 * k + 1])
                        for k in range(NA_HEADS // 2)]
                o_ref[0, pl.ds(qrow, NA_PAIR_TOKENS), :] = jnp.concatenate(outs, axis=1).astype(BF16)
                halves = []
        return carry

    lax.fori_loop(0, pairs_per_step // NA_PAIRS_PER_ITER, body, 0)


def _na_attention(q, kT4, v, bias, pairs_per_step):
    B, S, _ = q.shape
    n_pairs = S // NA_PAIR_TOKENS
    steps = n_pairs // pairs_per_step
    tq = pairs_per_step * NA_PAIR_TOKENS
    return pl.pallas_call(
        functools.partial(_na_kernel, pairs_per_step=pairs_per_step, n_pairs=n_pairs),
        grid=(B, steps),
        in_specs=[
            pl.BlockSpec((1, tq, NA_WIDTH), lambda b, j: (b, j, 0)),
            pl.BlockSpec((1, n_pairs, NA_WIDTH, V7X_LANES), lambda b, j: (b, 0, 0, 0)),
            pl.BlockSpec((1, S, NA_HEADS * V7X_LANES), lambda b, j: (b, 0, 0)),
            _const_spec((NA_BIAS_CASES, NA_HEADS, NA_PAIR_TOKENS, NA_KEY_TOKENS)),
        ],
        out_specs=pl.BlockSpec((1, tq, NA_WIDTH), lambda b, j: (b, j, 0)),
        out_shape=jax.ShapeDtypeStruct((B, S, NA_WIDTH), BF16),
        compiler_params=_params(2),
        name="na_attn",
    )(q, kT4, v, bias)


def _dft_tables():
    n = GRID_W
    a = np.arange(n)
    ang1 = 2.0 * np.pi * np.outer(a, a) / n
    w_rows = np.concatenate([np.cos(ang1), -np.sin(ang1)], axis=0) / np.sqrt(n)
    c = np.arange(n)[:, None, None]
    d = np.arange(n)[None, :, None]
    b = np.arange(n)[None, None, :]
    ang2 = 2.0 * np.pi * ((c + n * d) * b % (n * n)) / (n * n)
    gr, gi = np.cos(ang2) / np.sqrt(n), -np.sin(ang2) / np.sqrt(n)
    w_cols = np.concatenate([np.concatenate([gr, -gi], axis=2),
                             np.concatenate([gi, gr], axis=2)], axis=1)
    ch = np.arange(F_GROUP_DIM)
    ang3 = 2.0 * np.pi * np.outer(ch, ch) / F_GROUP_DIM
    w_chan = np.concatenate([np.cos(ang3), np.sin(ang3)], axis=0) / np.sqrt(F_GROUP_DIM)
    return tuple(jnp.asarray(t, dtype=F32).astype(BF16) for t in (w_rows, w_cols, w_chan))


def _dft_rows_kernel(w_ref, u_ref, z_ref):
    z_ref[0] = _dot(w_ref[...], u_ref[0]).astype(BF16)


def _dft_rows(u2, w_rows, chunk):
    B, n, width = u2.shape
    return pl.pallas_call(
        _dft_rows_kernel,
        grid=(B, width // chunk),
        in_specs=[
            _const_spec((2 * n, n)),
            pl.BlockSpec((1, n, chunk), lambda b, j: (b, 0, j)),
        ],
        out_specs=pl.BlockSpec((1, 2 * n, chunk), lambda b, j: (b, 0, j)),
        out_shape=jax.ShapeDtypeStruct((B, 2 * n, width), BF16),
        compiler_params=_params(2),
        name="dft_rows",
    )(w_rows, u2)


def _dft_cols_kernel(w_ref, z_ref, y_ref, *, cols_per_step):
    n = GRID_W
    for j in range(cols_per_step):
        c = pl.program_id(1) * cols_per_step + j
        data = jnp.concatenate([z_ref[0, 0, j], z_ref[0, 1, j]], axis=0)
        y = _dot(w_ref[c], data)
        y_ref[0, 0, :, j * F_WIDTH:(j + 1) * F_WIDTH] = y[:n].astype(BF16)
        y_ref[1, 0, :, j * F_WIDTH:(j + 1) * F_WIDTH] = y[n:].astype(BF16)


def _dft_cols(z5, w_cols, cols_per_step):
    B = z5.shape[0]
    n = GRID_W
    return pl.pallas_call(
        functools.partial(_dft_cols_kernel, cols_per_step=cols_per_step),
        grid=(B, n // cols_per_step),
        in_specs=[
            _const_spec((n, 2 * n, 2 * n)),
            pl.BlockSpec((1, 2, cols_per_step, n, F_WIDTH), lambda b, j: (b, 0, j, 0, 0)),
        ],
        out_specs=pl.BlockSpec((2, 1, n, cols_per_step * F_WIDTH), lambda b, j: (0, b, 0, j)),
        out_shape=jax.ShapeDtypeStruct((2, B, n, n * F_WIDTH), BF16),
        compiler_params=_params(2),
        name="dft_cols",
    )(w_cols, z5)


def _out_ffn_kernel(x_ref, ona_ref, y_ref, om_ref, gate_ref, wch_ref, wna_ref, wf_ref, wmo_ref,
                    wout_ref, g2_ref, w1_ref, w2_ref, o_ref, *, ff_chunk):
    yr, yi = y_ref[0], y_ref[1]
    groups = []
    for g in range(F_GROUPS):
        sl = slice(g * F_GROUP_DIM, (g + 1) * F_GROUP_DIM)
        groups.append(_dot(jnp.concatenate([yr[:, sl], yi[:, sl]], axis=1), wch_ref[...]))
    yf = jnp.concatenate(groups, axis=1).astype(BF16)

    gate = lambda br: gate_ref[:, br * D_MODEL:(br + 1) * D_MODEL].astype(F32)
    merged = gate(0) * _dot(ona_ref[...], wna_ref[...])
    merged = merged + gate(1) * _dot(yf, wf_ref[...])
    merged = merged + gate(2) * _dot(om_ref[...], wmo_ref[...])
    x1 = x_ref[...] + _dot(merged.astype(BF16), wout_ref[...])

    ms = jnp.mean(x1 * x1, axis=-1, keepdims=True)
    h2 = (x1 * lax.rsqrt(ms + EPS) * g2_ref[...]).astype(BF16)
    acc = x1
    for j in range(D_FF // ff_chunk):
        sl = slice(j * ff_chunk, (j + 1) * ff_chunk)
        a = jnp.maximum(_dot(h2, w1_ref[:, sl]), 0.0)
        acc = acc + _dot((a * a).astype(BF16), w2_ref[sl, :])
    o_ref[...] = acc


def _out_ffn(x2, ona, y3, om, gates, w_chan, w_na_o, w_f, w_mem_o, w_out, norm2_g, w_ff1, w_ff2,
             tm, ff_chunk):
    T = x2.shape[0]
    tok = lambda w: pl.BlockSpec((tm, w), lambda i: (i, 0))
    return pl.pallas_call(
        functools.partial(_out_ffn_kernel, ff_chunk=ff_chunk),
        grid=(T // tm,),
        in_specs=[
            tok(D_MODEL),
            tok(NA_WIDTH),
            pl.BlockSpec((2, tm, F_WIDTH), lambda i: (0, i, 0)),
            tok(MEM_WIDTH),
            tok(N_BRANCHES * D_MODEL),
            _const_spec((2 * F_GROUP_DIM, F_GROUP_DIM)),
            _const_spec((NA_WIDTH, D_MODEL)),
            _const_spec((F_WIDTH, D_MODEL)),
            _const_spec((MEM_WIDTH, D_MODEL)),
            _const_spec((D_MODEL, D_MODEL)),
            _const_spec((1, D_MODEL)),
            _const_spec((D_MODEL, D_FF)),
            _const_spec((D_FF, D_MODEL)),
        ],
        out_specs=tok(D_MODEL),
        out_shape=jax.ShapeDtypeStruct((T, D_MODEL), F32),
        compiler_params=_params(1),
        name="out_ffn",
    )(x2, ona, y3, om, gates, w_chan, w_na_o.astype(BF16), w_f.astype(BF16), w_mem_o.astype(BF16),
      w_out.astype(BF16), norm2_g.reshape(1, D_MODEL), w_ff1.astype(BF16), w_ff2.astype(BF16))


def kernel(x, mem, norm1_g, w_in, b_gate, na_q_g, na_k_g, na_rpb, w_na_o, w_f, mem_norm_g,
           w_mem_kv, mem_q_g, mem_k_g, w_mem_o, w_out, norm2_g, w_ff1, w_ff2):
    B, S, _ = x.shape
    T = B * S
    rows = S // GRID_W
    assert rows == GRID_W and S % NA_PAIR_TOKENS == 0
    x2 = x.reshape(T, D_MODEL)
    gmean_na = _group_mean_matrix(NA_WIDTH, NA_HEAD_DIM)
    gmean_mem = _group_mean_matrix(MEM_WIDTH, MEM_HEAD_DIM)
    w_rows, w_cols, w_chan = _dft_tables()

    memkT, memv = _mem_kv(mem, mem_norm_g, w_mem_kv, gmean_mem, mem_k_g)
    q, kT4, v, fu, om, gates = _in_proj(x2, norm1_g, w_in, b_gate, na_q_g, na_k_g, mem_q_g,
                                        gmean_na, gmean_mem, memkT, memv, S, tm=512)

    bias = _na_bias_tables(na_rpb, rows)
    ona = _na_attention(q.reshape(B, S, NA_WIDTH), kT4, v.reshape(B, S, NA_HEADS * V7X_LANES), bias,
                        pairs_per_step=8)

    z = _dft_rows(fu.reshape(B, GRID_W, GRID_W * F_WIDTH), w_rows, chunk=4096)
    y = _dft_cols(z.reshape(B, 2, GRID_W, GRID_W, F_WIDTH), w_cols, cols_per_step=8)
    y3 = y.reshape(2, T, F_WIDTH)

    out = _out_ffn(x2, ona.reshape(T, NA_WIDTH), y3, om, gates, w_chan, w_na_o, w_f, w_mem_o,
                   w_out, norm2_g, w_ff1, w_ff2, tm=256, ff_chunk=1024)
    return out.reshape(B, S, D_MODEL)
```

```python
import functools

import numpy as np
import jax
import jax.numpy as jnp
from jax import lax
from jax.experimental import pallas as pl
from jax.experimental.pallas import tpu as pltpu

D_MODEL = 1024
GRID_W = 64
NA_HEADS = 8
NA_HEAD_DIM = 64
NA_WIDTH = NA_HEADS * NA_HEAD_DIM
NA_WIN_ROWS = 8
NA_WIN_COLS = 16
F_GROUPS = 4
F_GROUP_DIM = 128
F_WIDTH = F_GROUPS * F_GROUP_DIM
MEM_HEADS = 4
MEM_HEAD_DIM = 128
MEM_WIDTH = MEM_HEADS * MEM_HEAD_DIM
N_BRANCHES = 3
D_FF = 4 * D_MODEL
EPS = 1e-6
NEG_INF = -1e30

V7X_LANES = 128
V7X_VMEM_LIMIT_BYTES = 60 * 1024 * 1024

NA_PAIR_TOKENS = 2 * GRID_W
NA_KEY_PAIRS = 5
NA_KEY_TOKENS = NA_KEY_PAIRS * NA_PAIR_TOKENS
NA_BIAS_CASES = 5
NA_PAIRS_PER_ITER = 2
NA_SCORES_AHEAD = 2

BF16 = jnp.bfloat16
F32 = jnp.float32


def _dot(a, b):
    return jnp.dot(a, b, preferred_element_type=F32)


def _dot_nt(a, b):
    return lax.dot_general(a, b, (((1,), (1,)), ((), ())), preferred_element_type=F32)


def _const_spec(shape):
    return pl.BlockSpec(shape, lambda *_: (0,) * len(shape), pipeline_mode=pl.Buffered(1))


def _params(n_axes):
    return pltpu.CompilerParams(
        dimension_semantics=("arbitrary",) * n_axes,
        vmem_limit_bytes=V7X_VMEM_LIMIT_BYTES,
    )


def _group_mean_matrix(width, group):
    idx = np.arange(width) // group
    return jnp.asarray((idx[:, None] == idx[None, :]).astype(np.float32) / group, dtype=BF16)


def _mem_kv_kernel(mem_ref, g_ref, w_ref, gm_ref, kg_ref, kT_ref, v_ref):
    m = mem_ref[0]
    ms = jnp.mean(m * m, axis=-1, keepdims=True)
    mn = (m * lax.rsqrt(ms + EPS) * g_ref[...]).astype(BF16)
    kv = _dot(mn, w_ref[...])
    k = kv[:, :MEM_WIDTH]
    msk = _dot((k * k).astype(BF16), gm_ref[...])
    kn = k * lax.rsqrt(msk + EPS) * kg_ref[...]
    kT_ref[0] = kn.T.astype(BF16)
    v_ref[0] = kv[:, MEM_WIDTH:].astype(BF16)


def _mem_kv(mem, mem_norm_g, w_mem_kv, gmean_mem, mem_k_g):
    B, M, _ = mem.shape
    return pl.pallas_call(
        _mem_kv_kernel,
        grid=(B,),
        in_specs=[
            pl.BlockSpec((1, M, D_MODEL), lambda b: (b, 0, 0)),
            _const_spec((1, D_MODEL)),
            _const_spec((D_MODEL, 2 * MEM_WIDTH)),
            _const_spec((MEM_WIDTH, MEM_WIDTH)),
            _const_spec((1, MEM_WIDTH)),
        ],
        out_specs=[
            pl.BlockSpec((1, MEM_WIDTH, M), lambda b: (b, 0, 0)),
            pl.BlockSpec((1, M, MEM_WIDTH), lambda b: (b, 0, 0)),
        ],
        out_shape=[
            jax.ShapeDtypeStruct((B, MEM_WIDTH, M), BF16),
            jax.ShapeDtypeStruct((B, M, MEM_WIDTH), BF16),
        ],
        compiler_params=_params(1),
        name="mem_kv",
    )(mem, mem_norm_g.reshape(1, D_MODEL), w_mem_kv.astype(BF16), gmean_mem,
      jnp.tile(mem_k_g, MEM_HEADS).reshape(1, MEM_WIDTH))


def _in_proj_kernel(x_ref, g1_ref, wq_ref, wkT_ref, wv_ref, wf_ref, wm_ref, wg_ref, bg_ref,
                    gna_ref, gmem_ref, qg_ref, kg_ref, mqg_ref, memkT_ref, memv_ref,
                    q_ref, kT_ref, v_ref, fu_ref, om_ref, gate_ref):
    x = x_ref[...]
    ms = jnp.mean(x * x, axis=-1, keepdims=True)
    h = (x * lax.rsqrt(ms + EPS) * g1_ref[...]).astype(BF16)

    zq = _dot(h, wq_ref[...])
    msq = _dot((zq * zq).astype(BF16), gna_ref[...])
    q_ref[...] = (zq * lax.rsqrt(msq + EPS) * qg_ref[...]).astype(BF16)

    zkT = _dot_nt(wkT_ref[...], h)
    mskT = _dot(gna_ref[...], (zkT * zkT).astype(BF16))
    kT = (zkT * lax.rsqrt(mskT + EPS) * kg_ref[...]).astype(BF16)
    for j in range(kT_ref.shape[1]):
        kT_ref[0, j] = kT[:, j * V7X_LANES:(j + 1) * V7X_LANES]

    zv = _dot(h, wv_ref[...])
    low_half = lax.broadcasted_iota(jnp.int32, (zv.shape[0], V7X_LANES), 1) < NA_HEAD_DIM
    for hp in range(NA_HEADS // 2):
        pair = zv[:, hp * V7X_LANES:(hp + 1) * V7X_LANES]
        v_ref[:, (2 * hp) * V7X_LANES:(2 * hp + 1) * V7X_LANES] = jnp.where(low_half, pair, 1.0).astype(BF16)
        v_ref[:, (2 * hp + 1) * V7X_LANES:(2 * hp + 2) * V7X_LANES] = jnp.where(low_half, 1.0, pair).astype(BF16)
    fu_ref[...] = _dot(h, wf_ref[...]).astype(BF16)

    zm = _dot(h, wm_ref[...])
    msm = _dot((zm * zm).astype(BF16), gmem_ref[...])
    qm = (zm * lax.rsqrt(msm + EPS) * mqg_ref[...]).astype(BF16)
    outs = []
    for hd in range(MEM_HEADS):
        sl = slice(hd * MEM_HEAD_DIM, (hd + 1) * MEM_HEAD_DIM)
        s = _dot(qm[:, sl], memkT_ref[0, sl, :])
        e = jnp.exp(s - jnp.max(s, axis=-1, keepdims=True))
        l = jnp.sum(e, axis=-1, keepdims=True)
        outs.append(_dot(e.astype(BF16), memv_ref[0, :, sl]) / l)
    om_ref[...] = jnp.concatenate(outs, axis=1).astype(BF16)

    for br in range(N_BRANCHES):
        sl = slice(br * D_MODEL, (br + 1) * D_MODEL)
        zg = _dot(h, wg_ref[:, sl]) + bg_ref[:, sl]
        gate_ref[:, sl] = jax.nn.sigmoid(zg).astype(BF16)


def _in_proj(x2, norm1_g, w_in, b_gate, na_q_g, na_k_g, mem_q_g, gmean_na, gmean_mem,
             memkT, memv, tokens_per_batch, tm):
    T = x2.shape[0]
    M = memv.shape[1]
    tiles_per_batch = tokens_per_batch // tm
    slabs = tm // V7X_LANES
    B = T // tokens_per_batch
    wb = w_in.astype(BF16)
    o = 0
    wq = wb[:, o:o + NA_WIDTH]; o += NA_WIDTH
    wkT = wb[:, o:o + NA_WIDTH].T; o += NA_WIDTH
    wv = wb[:, o:o + NA_WIDTH]; o += NA_WIDTH
    wf = wb[:, o:o + F_WIDTH]; o += F_WIDTH
    wm = wb[:, o:o + MEM_WIDTH]; o += MEM_WIDTH
    wg = wb[:, o:]
    na_scale = np.float32(1.0 / np.sqrt(NA_HEAD_DIM))
    mem_scale = np.float32(1.0 / np.sqrt(MEM_HEAD_DIM))
    qg = (jnp.tile(na_q_g, NA_HEADS) * na_scale).reshape(1, NA_WIDTH)
    kg = jnp.tile(na_k_g, NA_HEADS).reshape(NA_WIDTH, 1)
    mqg = (jnp.tile(mem_q_g, MEM_HEADS) * mem_scale).reshape(1, MEM_WIDTH)
    tok = lambda w: pl.BlockSpec((tm, w), lambda i: (i, 0))
    batch_of = lambda i: i // tiles_per_batch
    return pl.pallas_call(
        _in_proj_kernel,
        grid=(T // tm,),
        in_specs=[
            tok(D_MODEL),
            _const_spec((1, D_MODEL)),
            _const_spec((D_MODEL, NA_WIDTH)),
            _const_spec((NA_WIDTH, D_MODEL)),
            _const_spec((D_MODEL, NA_WIDTH)),
            _const_spec((D_MODEL, F_WIDTH)),
            _const_spec((D_MODEL, MEM_WIDTH)),
            _const_spec((D_MODEL, N_BRANCHES * D_MODEL)),
            _const_spec((1, N_BRANCHES * D_MODEL)),
            _const_spec((NA_WIDTH, NA_WIDTH)),
            _const_spec((MEM_WIDTH, MEM_WIDTH)),
            _const_spec((1, NA_WIDTH)),
            _const_spec((NA_WIDTH, 1)),
            _const_spec((1, MEM_WIDTH)),
            pl.BlockSpec((1, MEM_WIDTH, M), lambda i: (batch_of(i), 0, 0)),
            pl.BlockSpec((1, M, MEM_WIDTH), lambda i: (batch_of(i), 0, 0)),
        ],
        out_specs=[
            tok(NA_WIDTH),
            pl.BlockSpec((1, slabs, NA_WIDTH, V7X_LANES),
                         lambda i: (batch_of(i), i % tiles_per_batch, 0, 0)),
            tok(NA_HEADS * V7X_LANES),
            tok(F_WIDTH),
            tok(MEM_WIDTH),
            tok(N_BRANCHES * D_MODEL),
        ],
        out_shape=[
            jax.ShapeDtypeStruct((T, NA_WIDTH), BF16),
            jax.ShapeDtypeStruct((B, tokens_per_batch // V7X_LANES, NA_WIDTH, V7X_LANES), BF16),
            jax.ShapeDtypeStruct((T, NA_HEADS * V7X_LANES), BF16),
            jax.ShapeDtypeStruct((T, F_WIDTH), BF16),
            jax.ShapeDtypeStruct((T, MEM_WIDTH), BF16),
            jax.ShapeDtypeStruct((T, N_BRANCHES * D_MODEL), BF16),
        ],
        compiler_params=_params(1),
        name="in_proj",
    )(x2, norm1_g.reshape(1, D_MODEL), wq, wkT, wv, wf, wm, wg,
      b_gate.reshape(1, N_BRANCHES * D_MODEL), gmean_na, gmean_mem, qg, kg, mqg, memkT, memv)


def _na_bias_tables(rpb, rows):
    n_pairs = rows // 2
    wr = min(NA_WIN_ROWS, rows)
    reps = np.array([0, 1, 2, n_pairs - 2, n_pairs - 1])
    starts = np.clip(reps - 2, 0, n_pairs - NA_KEY_PAIRS)
    n_dr, n_dc = 2 * NA_WIN_ROWS - 1, 2 * NA_WIN_COLS - 1
    c = np.arange(GRID_W)[:, None]
    kc = np.arange(GRID_W)[None, :]
    cs = np.clip(c - NA_WIN_COLS // 2, 0, GRID_W - NA_WIN_COLS)
    col_valid = (kc >= cs) & (kc < cs + NA_WIN_COLS)
    dc = np.clip(kc - c, -(NA_WIN_COLS - 1), NA_WIN_COLS - 1) + (NA_WIN_COLS - 1)
    onehot = (np.arange(n_dc)[:, None] == dc.reshape(1, -1)).astype(np.float32)
    cols = jnp.dot(rpb.astype(F32).reshape(NA_HEADS * n_dr, n_dc), onehot,
                   precision=lax.Precision.HIGHEST)
    cols = jnp.where(col_valid.reshape(1, -1), cols, NEG_INF).reshape(NA_HEADS, n_dr, GRID_W, GRID_W)
    r = (2 * reps[:, None] + np.arange(2)[None, :])[:, :, None]
    kr = (2 * starts[:, None] + np.arange(2 * NA_KEY_PAIRS)[None, :])[:, None, :]
    rs = np.clip(r - wr // 2, 0, rows - wr)
    row_valid = (kr >= rs) & (kr < rs + wr)
    dr = np.clip(kr - r + (NA_WIN_ROWS - 1), 0, n_dr - 1)
    blocks = cols[:, dr]
    blocks = jnp.where(row_valid[None, :, :, :, None, None], blocks, NEG_INF)
    return jnp.transpose(blocks, (1, 0, 2, 4, 3, 5)).reshape(
        len(reps), NA_HEADS, NA_PAIR_TOKENS, NA_KEY_TOKENS)


def _na_kernel(q_ref, kT_ref, v_ref, bias_ref, o_ref, *, pairs_per_step, n_pairs):
    lane = lax.broadcasted_iota(jnp.int32, (NA_PAIR_TOKENS, 2 * NA_HEAD_DIM), 1)
    first_head = lane < NA_HEAD_DIM

    def pair_coords(i):
        p = pl.program_id(1) * pairs_per_step + i
        start = jnp.clip(p - 2, 0, n_pairs - NA_KEY_PAIRS)
        case = jnp.where(p < 2, p, jnp.where(p >= n_pairs - 2, p - (n_pairs - NA_BIAS_CASES), 2))
        qrow = pl.multiple_of(i * NA_PAIR_TOKENS, NA_PAIR_TOKENS)
        krow = pl.multiple_of(start * NA_PAIR_TOKENS, NA_PAIR_TOKENS)
        return start, case, qrow, krow

    def scores(coords, hd):
        start, case, qrow, _ = coords
        hsl = slice(hd * NA_HEAD_DIM, (hd + 1) * NA_HEAD_DIM)
        qh = q_ref[0, pl.ds(qrow, NA_PAIR_TOKENS), hsl]
        kslabs = kT_ref[0, pl.ds(start, NA_KEY_PAIRS), hsl, :]
        kT = jnp.concatenate([kslabs[j] for j in range(NA_KEY_PAIRS)], axis=1)
        return _dot(qh, kT) + bias_ref[case, hd]

    def body(it, carry):
        coords = [pair_coords(it * NA_PAIRS_PER_ITER + j) for j in range(NA_PAIRS_PER_ITER)]
        items = [(j, hd) for j in range(NA_PAIRS_PER_ITER) for hd in range(NA_HEADS)]
        pending = [scores(coords[j], hd) for j, hd in items[:NA_SCORES_AHEAD]]
        halves = []
        for n, (j, hd) in enumerate(items):
            s = pending.pop(0)
            if n + NA_SCORES_AHEAD < len(items):
                jn, hn = items[n + NA_SCORES_AHEAD]
                pending.append(scores(coords[jn], hn))
            _, _, qrow, krow = coords[j]
            vext = v_ref[0, pl.ds(krow, NA_KEY_TOKENS), hd * 2 * NA_HEAD_DIM:(hd + 1) * 2 * NA_HEAD_DIM]
            e = jnp.exp((s - jnp.max(s, axis=-1, keepdims=True)).astype(BF16))
            r = _dot(e, vext)
            halves.append(r / pltpu.roll(r, NA_HEAD_DIM, axis=1))
            if hd == NA_HEADS - 1:
                outs = [jnp.where(first_head, halves[2 * k], halves[2 * k + 1])
                        for k in range(NA_HEADS // 2)]
                o_ref[0, pl.ds(qrow, NA_PAIR_TOKENS), :] = jnp.concatenate(outs, axis=1).astype(BF16)
                halves = []
        return carry

    lax.fori_loop(0, pairs_per_step // NA_PAIRS_PER_ITER, body, 0)


def _na_attention(q, kT4, v, bias, pairs_per_step):
    B, S, _ = q.shape
    n_pairs = S // NA_PAIR_TOKENS
    steps = n_pairs // pairs_per_step
    tq = pairs_per_step * NA_PAIR_TOKENS
    return pl.pallas_call(
        functools.partial(_na_kernel, pairs_per_step=pairs_per_step, n_pairs=n_pairs),
        grid=(B, steps),
        in_specs=[
            pl.BlockSpec((1, tq, NA_WIDTH), lambda b, j: (b, j, 0)),
            pl.BlockSpec((1, n_pairs, NA_WIDTH, V7X_LANES), lambda b, j: (b, 0, 0, 0)),
            pl.BlockSpec((1, S, NA_HEADS * V7X_LANES), lambda b, j: (b, 0, 0)),
            _const_spec((NA_BIAS_CASES, NA_HEADS, NA_PAIR_TOKENS, NA_KEY_TOKENS)),
        ],
        out_specs=pl.BlockSpec((1, tq, NA_WIDTH), lambda b, j: (b, j, 0)),
        out_shape=jax.ShapeDtypeStruct((B, S, NA_WIDTH), BF16),
        compiler_params=_params(2),
        name="na_attn",
    )(q, kT4, v, bias)


DFT_CHUNK = 16


def _dft_tables():
    n, m = GRID_W, DFT_CHUNK
    a = np.arange(n)
    ang1 = 2.0 * np.pi * np.outer(a, a) / n
    w_ri = np.stack([np.cos(ang1), -np.sin(ang1)], axis=1) / np.sqrt(n)
    eye = np.eye(m)
    w_rows = np.einsum('cra,xy->crxay', w_ri, eye).reshape(n * 2 * m, n * m)
    perm = np.einsum('dx,cy->dcyx', eye, eye).reshape(m * m, m * m)
    c = np.arange(n)[:, None, None]
    d = np.arange(n)[None, :, None]
    b = np.arange(n)[None, None, :]
    ang2 = 2.0 * np.pi * ((c + n * d) * b % (n * n)) / (n * n)
    gr, gi = np.cos(ang2) / np.sqrt(n), -np.sin(ang2) / np.sqrt(n)
    w_cols = np.concatenate([np.concatenate([gr, -gi], axis=2),
                             np.concatenate([gi, gr], axis=2)], axis=1)
    ch = np.arange(F_GROUP_DIM)
    ang3 = 2.0 * np.pi * np.outer(ch, ch) / F_GROUP_DIM
    w_chan = np.concatenate([np.cos(ang3), np.sin(ang3)], axis=0) / np.sqrt(F_GROUP_DIM)
    return tuple(jnp.asarray(t, dtype=F32).astype(BF16) for t in (w_rows, w_cols, perm, w_chan))


def _dft_rows_kernel(w_ref, u_ref, z_ref):
    n, m = GRID_W, DFT_CHUNK
    x = u_ref[0].reshape(n * m, F_WIDTH)
    rows_per_dot = 2 * m * 8
    for i in range(n // 8):
        z = _dot(w_ref[i * rows_per_dot:(i + 1) * rows_per_dot, :], x)
        z_ref[0, i * 8:(i + 1) * 8] = z.astype(BF16).reshape(8, 2, m, F_WIDTH)


def _dft_rows(u4, w_rows):
    B, n = u4.shape[0], GRID_W
    m = DFT_CHUNK
    return pl.pallas_call(
        _dft_rows_kernel,
        grid=(B, n // m),
        in_specs=[
            _const_spec((n * 2 * m, n * m)),
            pl.BlockSpec((1, n, m, F_WIDTH), lambda b, j: (b, 0, j, 0)),
        ],
        out_specs=pl.BlockSpec((1, n, 2, m, F_WIDTH), lambda b, j: (b, 0, 0, j, 0)),
        out_shape=jax.ShapeDtypeStruct((B, n, 2, n, F_WIDTH), BF16),
        compiler_params=_params(2),
        name="dft_rows",
    )(w_rows, u4)


def _dft_cols_kernel(w_ref, perm_ref, z_ref, y_ref):
    n, m = GRID_W, DFT_CHUNK
    res = []
    for j in range(m):
        c = pl.program_id(1) * m + j
        data = z_ref[0, j].reshape(2 * n, F_WIDTH)
        res.append(_dot(w_ref[c], data).astype(BF16))
    for ri in range(2):
        for g in range(n // m):
            lo = ri * n + g * m
            stacked = jnp.concatenate([r[lo:lo + m] for r in res], axis=0)
            y = _dot(perm_ref[...], stacked).astype(BF16)
            y_ref[ri, 0, g * m:(g + 1) * m] = y.reshape(m, m, F_WIDTH)


def _dft_cols(z5, w_cols, perm):
    B, n, m = z5.shape[0], GRID_W, DFT_CHUNK
    return pl.pallas_call(
        _dft_cols_kernel,
        grid=(B, n // m),
        in_specs=[
            _const_spec((n, 2 * n, 2 * n)),
            _const_spec((m * m, m * m)),
            pl.BlockSpec((1, m, 2, n, F_WIDTH), lambda b, j: (b, j, 0, 0, 0)),
        ],
        out_specs=pl.BlockSpec((2, 1, n, m, F_WIDTH), lambda b, j: (0, b, 0, j, 0)),
        out_shape=jax.ShapeDtypeStruct((2, B, n, n, F_WIDTH), BF16),
        compiler_params=_params(2),
        name="dft_cols",
    )(w_cols, perm, z5)


def _out_ffn_kernel(x_ref, ona_ref, y_ref, om_ref, gate_ref, wch_ref, wna_ref, wf_ref, wmo_ref,
                    wout_ref, g2_ref, w1_ref, w2_ref, o_ref, *, ff_chunk):
    yr, yi = y_ref[0], y_ref[1]
    groups = []
    for g in range(F_GROUPS):
        sl = slice(g * F_GROUP_DIM, (g + 1) * F_GROUP_DIM)
        groups.append(_dot(jnp.concatenate([yr[:, sl], yi[:, sl]], axis=1), wch_ref[...]))
    yf = jnp.concatenate(groups, axis=1).astype(BF16)

    gate = lambda br: gate_ref[:, br * D_MODEL:(br + 1) * D_MODEL].astype(F32)
    merged = gate(0) * _dot(ona_ref[...], wna_ref[...])
    merged = merged + gate(1) * _dot(yf, wf_ref[...])
    merged = merged + gate(2) * _dot(om_ref[...], wmo_ref[...])
    x1 = x_ref[...] + _dot(merged.astype(BF16), wout_ref[...])

    ms = jnp.mean(x1 * x1, axis=-1, keepdims=True)
    h2 = (x1 * lax.rsqrt(ms + EPS) * g2_ref[...]).astype(BF16)
    acc = x1
    for j in range(D_FF // ff_chunk):
        sl = slice(j * ff_chunk, (j + 1) * ff_chunk)
        a = jnp.maximum(_dot(h2, w1_ref[:, sl]), 0.0)
        acc = acc + _dot((a * a).astype(BF16), w2_ref[sl, :])
    o_ref[...] = acc


def _out_ffn(x2, ona, y3, om, gates, w_chan, w_na_o, w_f, w_mem_o, w_out, norm2_g, w_ff1, w_ff2,
             tm, ff_chunk):
    T = x2.shape[0]
    tok = lambda w: pl.BlockSpec((tm, w), lambda i: (i, 0))
    return pl.pallas_call(
        functools.partial(_out_ffn_kernel, ff_chunk=ff_chunk),
        grid=(T // tm,),
        in_specs=[
            tok(D_MODEL),
            tok(NA_WIDTH),
            pl.BlockSpec((2, tm, F_WIDTH), lambda i: (0, i, 0)),
            tok(MEM_WIDTH),
            tok(N_BRANCHES * D_MODEL),
            _const_spec((2 * F_GROUP_DIM, F_GROUP_DIM)),
            _const_spec((NA_WIDTH, D_MODEL)),
            _const_spec((F_WIDTH, D_MODEL)),
            _const_spec((MEM_WIDTH, D_MODEL)),
            _const_spec((D_MODEL, D_MODEL)),
            _const_spec((1, D_MODEL)),
            _const_spec((D_MODEL, D_FF)),
            _const_spec((D_FF, D_MODEL)),
        ],
        out_specs=tok(D_MODEL),
        out_shape=jax.ShapeDtypeStruct((T, D_MODEL), F32),
        compiler_params=_params(1),
        name="out_ffn",
    )(x2, ona, y3, om, gates, w_chan, w_na_o.astype(BF16), w_f.astype(BF16), w_mem_o.astype(BF16),
      w_out.astype(BF16), norm2_g.reshape(1, D_MODEL), w_ff1.astype(BF16), w_ff2.astype(BF16))


def kernel(x, mem, norm1_g, w_in, b_gate, na_q_g, na_k_g, na_rpb, w_na_o, w_f, mem_norm_g,
           w_mem_kv, mem_q_g, mem_k_g, w_mem_o, w_out, norm2_g, w_ff1, w_ff2):
    B, S, _ = x.shape
    T = B * S
    rows = S // GRID_W
    assert rows == GRID_W and S % NA_PAIR_TOKENS == 0
    x2 = x.reshape(T, D_MODEL)
    gmean_na = _group_mean_matrix(NA_WIDTH, NA_HEAD_DIM)
    gmean_mem = _group_mean_matrix(MEM_WIDTH, MEM_HEAD_DIM)
    w_rows, w_cols, perm, w_chan = _dft_tables()

    memkT, memv = _mem_kv(mem, mem_norm_g, w_mem_kv, gmean_mem, mem_k_g)
    q, kT4, v, fu, om, gates = _in_proj(x2, norm1_g, w_in, b_gate, na_q_g, na_k_g, mem_q_g,
                                        gmean_na, gmean_mem, memkT, memv, S, tm=512)

    bias = _na_bias_tables(na_rpb, rows)
    ona = _na_attention(q.reshape(B, S, NA_WIDTH), kT4, v.reshape(B, S, NA_HEADS * V7X_LANES), bias,
                        pairs_per_step=8)

    z = _dft_rows(fu.reshape(B, GRID_W, GRID_W, F_WIDTH), w_rows)
    y = _dft_cols(z, w_cols, perm)
    y3 = y.reshape(2, T, F_WIDTH)

    out = _out_ffn(x2, ona.reshape(T, NA_WIDTH), y3, om, gates, w_chan, w_na_o, w_f, w_mem_o,
                   w_out, norm2_g, w_ff1, w_ff2, tm=256, ff_chunk=1024)
    return out.reshape(B, S, D_MODEL)
```

```python
import functools

import numpy as np
import jax
import jax.numpy as jnp
from jax import lax
from jax.experimental import pallas as pl
from jax.experimental.pallas import tpu as pltpu

D_MODEL = 1024
GRID_W = 64
NA_HEADS = 8
NA_HEAD_DIM = 64
NA_WIDTH = NA_HEADS * NA_HEAD_DIM
NA_WIN_ROWS = 8
NA_WIN_COLS = 16
F_GROUPS = 4
F_GROUP_DIM = 128
F_WIDTH = F_GROUPS * F_GROUP_DIM
MEM_HEADS = 4
MEM_HEAD_DIM = 128
MEM_WIDTH = MEM_HEADS * MEM_HEAD_DIM
N_BRANCHES = 3
D_FF = 4 * D_MODEL
EPS = 1e-6
NEG_INF = -1e30

V7X_LANES = 128
V7X_VMEM_LIMIT_BYTES = 60 * 1024 * 1024

NA_PAIR_TOKENS = 2 * GRID_W
NA_KEY_PAIRS = 5
NA_KEY_TOKENS = NA_KEY_PAIRS * NA_PAIR_TOKENS
NA_BIAS_CASES = 5
NA_PAIRS_PER_ITER = 2
NA_SCORES_AHEAD = 2

BF16 = jnp.bfloat16
F32 = jnp.float32


def _dot(a, b):
    return jnp.dot(a, b, preferred_element_type=F32)


def _dot_nt(a, b):
    return lax.dot_general(a, b, (((1,), (1,)), ((), ())), preferred_element_type=F32)


def _const_spec(shape):
    return pl.BlockSpec(shape, lambda *_: (0,) * len(shape), pipeline_mode=pl.Buffered(1))


def _params(n_axes):
    return pltpu.CompilerParams(
        dimension_semantics=("arbitrary",) * n_axes,
        vmem_limit_bytes=V7X_VMEM_LIMIT_BYTES,
    )


def _group_mean_matrix(width, group):
    idx = np.arange(width) // group
    return jnp.asarray((idx[:, None] == idx[None, :]).astype(np.float32) / group, dtype=BF16)


def _mem_kv_kernel(mem_ref, g_ref, w_ref, gm_ref, kg_ref, kT_ref, v_ref):
    m = mem_ref[0]
    ms = jnp.mean(m * m, axis=-1, keepdims=True)
    mn = (m * lax.rsqrt(ms + EPS) * g_ref[...]).astype(BF16)
    kv = _dot(mn, w_ref[...])
    k = kv[:, :MEM_WIDTH]
    msk = _dot((k * k).astype(BF16), gm_ref[...])
    kn = k * lax.rsqrt(msk + EPS) * kg_ref[...]
    kT_ref[0] = kn.T.astype(BF16)
    v_ref[0] = kv[:, MEM_WIDTH:].astype(BF16)


def _mem_kv(mem, mem_norm_g, w_mem_kv, gmean_mem, mem_k_g):
    B, M, _ = mem.shape
    return pl.pallas_call(
        _mem_kv_kernel,
        grid=(B,),
        in_specs=[
            pl.BlockSpec((1, M, D_MODEL), lambda b: (b, 0, 0)),
            _const_spec((1, D_MODEL)),
            _const_spec((D_MODEL, 2 * MEM_WIDTH)),
            _const_spec((MEM_WIDTH, MEM_WIDTH)),
            _const_spec((1, MEM_WIDTH)),
        ],
        out_specs=[
            pl.BlockSpec((1, MEM_WIDTH, M), lambda b: (b, 0, 0)),
            pl.BlockSpec((1, M, MEM_WIDTH), lambda b: (b, 0, 0)),
        ],
        out_shape=[
            jax.ShapeDtypeStruct((B, MEM_WIDTH, M), BF16),
            jax.ShapeDtypeStruct((B, M, MEM_WIDTH), BF16),
        ],
        compiler_params=_params(1),
        name="mem_kv",
    )(mem, mem_norm_g.reshape(1, D_MODEL), w_mem_kv.astype(BF16), gmean_mem,
      jnp.tile(mem_k_g, MEM_HEADS).reshape(1, MEM_WIDTH))


def _in_proj_kernel(x_ref, g1_ref, wq_ref, wkT_ref, wv_ref, wf_ref, wm_ref, wg_ref, bg_ref,
                    qg_ref, kg_ref, mqg_ref, memkT_ref, memv_ref,
                    q_ref, kT_ref, v_ref, fu_ref, om_ref, gate_ref):
    x = x_ref[...]
    ms = jnp.mean(x * x, axis=-1, keepdims=True)
    h = (x * lax.rsqrt(ms + EPS) * g1_ref[...]).astype(BF16)

    zq = _dot(h, wq_ref[...])
    low_half = lax.broadcasted_iota(jnp.int32, (zq.shape[0], V7X_LANES), 1) < NA_HEAD_DIM
    for hp in range(NA_HEADS // 2):
        sl = slice(hp * V7X_LANES, (hp + 1) * V7X_LANES)
        zz = zq[:, sl] * zq[:, sl]
        ms_lo = jnp.sum(jnp.where(low_half, zz, 0.0), axis=-1, keepdims=True)
        ms_hi = jnp.sum(jnp.where(low_half, 0.0, zz), axis=-1, keepdims=True)
        ms = jnp.where(low_half, ms_lo, ms_hi) * (1.0 / NA_HEAD_DIM)
        q_ref[:, sl] = (zq[:, sl] * lax.rsqrt(ms + EPS) * qg_ref[:, sl]).astype(BF16)

    zkT = _dot_nt(wkT_ref[...], h)
    zk3 = zkT.reshape(NA_HEADS, NA_HEAD_DIM, zkT.shape[1])
    msk = jnp.mean(zk3 * zk3, axis=1, keepdims=True)
    kT = ((zk3 * lax.rsqrt(msk + EPS)).reshape(zkT.shape) * kg_ref[...]).astype(BF16)
    for j in range(kT_ref.shape[1]):
        kT_ref[0, j] = kT[:, j * V7X_LANES:(j + 1) * V7X_LANES]

    zv = _dot(h, wv_ref[...])
    low_half = lax.broadcasted_iota(jnp.int32, (zv.shape[0], V7X_LANES), 1) < NA_HEAD_DIM
    for hp in range(NA_HEADS // 2):
        pair = zv[:, hp * V7X_LANES:(hp + 1) * V7X_LANES]
        v_ref[:, (2 * hp) * V7X_LANES:(2 * hp + 1) * V7X_LANES] = jnp.where(low_half, pair, 1.0).astype(BF16)
        v_ref[:, (2 * hp + 1) * V7X_LANES:(2 * hp + 2) * V7X_LANES] = jnp.where(low_half, 1.0, pair).astype(BF16)
    fu_ref[...] = _dot(h, wf_ref[...]).astype(BF16)

    zm = _dot(h, wm_ref[...])
    outs = []
    for hd in range(MEM_HEADS):
        sl = slice(hd * MEM_HEAD_DIM, (hd + 1) * MEM_HEAD_DIM)
        zh = zm[:, sl]
        msm = jnp.mean(zh * zh, axis=-1, keepdims=True)
        qm = (zh * lax.rsqrt(msm + EPS) * mqg_ref[:, sl]).astype(BF16)
        s = _dot(qm, memkT_ref[0, sl, :])
        e = jnp.exp(s - jnp.max(s, axis=-1, keepdims=True))
        l = jnp.sum(e, axis=-1, keepdims=True)
        outs.append(_dot(e.astype(BF16), memv_ref[0, :, sl]) / l)
    om_ref[...] = jnp.concatenate(outs, axis=1).astype(BF16)

    for br in range(N_BRANCHES):
        sl = slice(br * D_MODEL, (br + 1) * D_MODEL)
        zg = _dot(h, wg_ref[:, sl]) + bg_ref[:, sl]
        gate_ref[:, sl] = jax.nn.sigmoid(zg).astype(BF16)


def _in_proj(x2, norm1_g, w_in, b_gate, na_q_g, na_k_g, mem_q_g,
             memkT, memv, tokens_per_batch, tm):
    T = x2.shape[0]
    M = memv.shape[1]
    tiles_per_batch = tokens_per_batch // tm
    slabs = tm // V7X_LANES
    B = T // tokens_per_batch
    wb = w_in.astype(BF16)
    o = 0
    wq = wb[:, o:o + NA_WIDTH]; o += NA_WIDTH
    wkT = wb[:, o:o + NA_WIDTH].T; o += NA_WIDTH
    wv = wb[:, o:o + NA_WIDTH]; o += NA_WIDTH
    wf = wb[:, o:o + F_WIDTH]; o += F_WIDTH
    wm = wb[:, o:o + MEM_WIDTH]; o += MEM_WIDTH
    wg = wb[:, o:]
    na_scale = np.float32(1.0 / np.sqrt(NA_HEAD_DIM))
    mem_scale = np.float32(1.0 / np.sqrt(MEM_HEAD_DIM))
    qg = (jnp.tile(na_q_g, NA_HEADS) * na_scale).reshape(1, NA_WIDTH)
    kg = jnp.tile(na_k_g, NA_HEADS).reshape(NA_WIDTH, 1)
    mqg = (jnp.tile(mem_q_g, MEM_HEADS) * mem_scale).reshape(1, MEM_WIDTH)
    tok = lambda w: pl.BlockSpec((tm, w), lambda i: (i, 0))
    batch_of = lambda i: i // tiles_per_batch
    return pl.pallas_call(
        _in_proj_kernel,
        grid=(T // tm,),
        in_specs=[
            tok(D_MODEL),
            _const_spec((1, D_MODEL)),
            _const_spec((D_MODEL, NA_WIDTH)),
            _const_spec((NA_WIDTH, D_MODEL)),
            _const_spec((D_MODEL, NA_WIDTH)),
            _const_spec((D_MODEL, F_WIDTH)),
            _const_spec((D_MODEL, MEM_WIDTH)),
            _const_spec((D_MODEL, N_BRANCHES * D_MODEL)),
            _const_spec((1, N_BRANCHES * D_MODEL)),
            _const_spec((1, NA_WIDTH)),
            _const_spec((NA_WIDTH, 1)),
            _const_spec((1, MEM_WIDTH)),
            pl.BlockSpec((1, MEM_WIDTH, M), lambda i: (batch_of(i), 0, 0)),
            pl.BlockSpec((1, M, MEM_WIDTH), lambda i: (batch_of(i), 0, 0)),
        ],
        out_specs=[
            tok(NA_WIDTH),
            pl.BlockSpec((1, slabs, NA_WIDTH, V7X_LANES),
                         lambda i: (batch_of(i), i % tiles_per_batch, 0, 0)),
            tok(NA_HEADS * V7X_LANES),
            tok(F_WIDTH),
            tok(MEM_WIDTH),
            tok(N_BRANCHES * D_MODEL),
        ],
        out_shape=[
            jax.ShapeDtypeStruct((T, NA_WIDTH), BF16),
            jax.ShapeDtypeStruct((B, tokens_per_batch // V7X_LANES, NA_WIDTH, V7X_LANES), BF16),
            jax.ShapeDtypeStruct((T, NA_HEADS * V7X_LANES), BF16),
            jax.ShapeDtypeStruct((T, F_WIDTH), BF16),
            jax.ShapeDtypeStruct((T, MEM_WIDTH), BF16),
            jax.ShapeDtypeStruct((T, N_BRANCHES * D_MODEL), BF16),
        ],
        compiler_params=_params(1),
        name="in_proj",
    )(x2, norm1_g.reshape(1, D_MODEL), wq, wkT, wv, wf, wm, wg,
      b_gate.reshape(1, N_BRANCHES * D_MODEL), qg, kg, mqg, memkT, memv)


def _na_bias_tables(rpb, rows):
    n_pairs = rows // 2
    wr = min(NA_WIN_ROWS, rows)
    reps = np.array([0, 1, 2, n_pairs - 2, n_pairs - 1])
    starts = np.clip(reps - 2, 0, n_pairs - NA_KEY_PAIRS)
    n_dr, n_dc = 2 * NA_WIN_ROWS - 1, 2 * NA_WIN_COLS - 1
    c = np.arange(GRID_W)[:, None]
    kc = np.arange(2 * GRID_W)[None, :] % GRID_W
    cs = np.clip(c - NA_WIN_COLS // 2, 0, GRID_W - NA_WIN_COLS)
    col_valid = (kc >= cs) & (kc < cs + NA_WIN_COLS)
    dc = np.clip(kc - c, -(NA_WIN_COLS - 1), NA_WIN_COLS - 1) + (NA_WIN_COLS - 1)
    onehot = (np.arange(n_dc)[:, None] == dc.reshape(1, -1)).astype(np.float32)
    cols = jnp.dot(rpb.astype(F32).reshape(NA_HEADS * n_dr, n_dc), onehot,
                   precision=lax.Precision.HIGHEST)
    cols = jnp.where(col_valid.reshape(1, -1), cols, NEG_INF)
    cols = cols.reshape(NA_HEADS, n_dr, GRID_W, 2 * GRID_W)
    cols = jnp.concatenate([cols, jnp.full((NA_HEADS, 1, GRID_W, 2 * GRID_W), NEG_INF, F32)], axis=1)
    low = (np.arange(2 * GRID_W) < GRID_W)[None, None, None, :]
    lo = jnp.where(low, cols, 0.0)
    hi = jnp.where(low, 0.0, cols)
    r = (2 * reps[:, None] + np.arange(2)[None, :])[:, :, None]
    kr = (2 * starts[:, None] + np.arange(2 * NA_KEY_PAIRS)[None, :])[:, None, :]
    rs = np.clip(r - wr // 2, 0, rows - wr)
    row_valid = (kr >= rs) & (kr < rs + wr)
    dr = np.where(row_valid, kr - r + (NA_WIN_ROWS - 1), n_dr)
    assert dr.min() >= 0 and dr.max() <= n_dr

    def assemble(lo_ref, hi_ref, o_ref):
        for case in range(len(reps)):
            for rr in range(2):
                for j in range(NA_KEY_PAIRS):
                    tile = lo_ref[0, int(dr[case, rr, 2 * j])] + hi_ref[0, int(dr[case, rr, 2 * j + 1])]
                    o_ref[case, 0, rr * GRID_W:(rr + 1) * GRID_W, j * V7X_LANES:(j + 1) * V7X_LANES] = tile

    half_spec = pl.BlockSpec((1, n_dr + 1, GRID_W, 2 * GRID_W), lambda h: (h, 0, 0, 0))
    return pl.pallas_call(
        assemble,
        grid=(NA_HEADS,),
        in_specs=[half_spec, half_spec],
        out_specs=pl.BlockSpec((len(reps), 1, NA_PAIR_TOKENS, NA_KEY_TOKENS), lambda h: (0, h, 0, 0)),
        out_shape=jax.ShapeDtypeStruct((len(reps), NA_HEADS, NA_PAIR_TOKENS, NA_KEY_TOKENS), F32),
        compiler_params=_params(1),
        name="na_bias",
    )(lo, hi)


def _na_kernel(q_ref, kT_ref, v_ref, bias_ref, o_ref, *, pairs_per_step, n_pairs):
    lane = lax.broadcasted_iota(jnp.int32, (NA_PAIR_TOKENS, 2 * NA_HEAD_DIM), 1)
    first_head = lane < NA_HEAD_DIM

    def pair_coords(i):
        p = pl.program_id(1) * pairs_per_step + i
        start = jnp.clip(p - 2, 0, n_pairs - NA_KEY_PAIRS)
        case = jnp.where(p < 2, p, jnp.where(p >= n_pairs - 2, p - (n_pairs - NA_BIAS_CASES), 2))
        qrow = pl.multiple_of(i * NA_PAIR_TOKENS, NA_PAIR_TOKENS)
        krow = pl.multiple_of(start * NA_PAIR_TOKENS, NA_PAIR_TOKENS)
        return start, case, qrow, krow

    def scores(coords, hd):
        start, case, qrow, _ = coords
        hsl = slice(hd * NA_HEAD_DIM, (hd + 1) * NA_HEAD_DIM)
        qh = q_ref[0, pl.ds(qrow, NA_PAIR_TOKENS), hsl]
        kslabs = kT_ref[0, pl.ds(start, NA_KEY_PAIRS), hsl, :]
        kT = jnp.concatenate([kslabs[j] for j in range(NA_KEY_PAIRS)], axis=1)
        return _dot(qh, kT) + bias_ref[case, hd]

    def body(it, carry):
        coords = [pair_coords(it * NA_PAIRS_PER_ITER + j) for j in range(NA_PAIRS_PER_ITER)]
        items = [(j, hd) for j in range(NA_PAIRS_PER_ITER) for hd in range(NA_HEADS)]
        pending = [scores(coords[j], hd) for j, hd in items[:NA_SCORES_AHEAD]]
        halves = []
        for n, (j, hd) in enumerate(items):
            s = pending.pop(0)
            if n + NA_SCORES_AHEAD < len(items):
                jn, hn = items[n + NA_SCORES_AHEAD]
                pending.append(scores(coords[jn], hn))
            _, _, qrow, krow = coords[j]
            vext = v_ref[0, pl.ds(krow, NA_KEY_TOKENS), hd * 2 * NA_HEAD_DIM:(hd + 1) * 2 * NA_HEAD_DIM]
            e = jnp.exp((s - jnp.max(s, axis=-1, keepdims=True)).astype(BF16))
            r = _dot(e, vext)
            halves.append(r / pltpu.roll(r, NA_HEAD_DIM, axis=1))
            if hd == NA_HEADS - 1:
                outs = [jnp.where(first_head, halves[2 * k], halves[2 * k + 1])
                        for k in range(NA_HEADS // 2)]
                o_ref[0, pl.ds(qrow, NA_PAIR_TOKENS), :] = jnp.concatenate(outs, axis=1).astype(BF16)
                halves = []
        return carry

    lax.fori_loop(0, pairs_per_step // NA_PAIRS_PER_ITER, body, 0)


def _na_attention(q, kT4, v, bias, pairs_per_step):
    B, S, _ = q.shape
    n_pairs = S // NA_PAIR_TOKENS
    steps = n_pairs // pairs_per_step
    tq = pairs_per_step * NA_PAIR_TOKENS
    return pl.pallas_call(
        functools.partial(_na_kernel, pairs_per_step=pairs_per_step, n_pairs=n_pairs),
        grid=(B, steps),
        in_specs=[
            pl.BlockSpec((1, tq, NA_WIDTH), lambda b, j: (b, j, 0)),
            pl.BlockSpec((1, n_pairs, NA_WIDTH, V7X_LANES), lambda b, j: (b, 0, 0, 0)),
            pl.BlockSpec((1, S, NA_HEADS * V7X_LANES), lambda b, j: (b, 0, 0)),
            _const_spec((NA_BIAS_CASES, NA_HEADS, NA_PAIR_TOKENS, NA_KEY_TOKENS)),
        ],
        out_specs=pl.BlockSpec((1, tq, NA_WIDTH), lambda b, j: (b, j, 0)),
        out_shape=jax.ShapeDtypeStruct((B, S, NA_WIDTH), BF16),
        compiler_params=_params(2),
        name="na_attn",
    )(q, kT4, v, bias)


DFT_CHUNK = 16


def _dft_tables():
    n, m = GRID_W, DFT_CHUNK
    a = np.arange(n)
    ang1 = 2.0 * np.pi * np.outer(a, a) / n
    w_ri = np.stack([np.cos(ang1), -np.sin(ang1)], axis=1) / np.sqrt(n)
    eye = np.eye(m)
    w_rows = np.einsum('cra,xy->crxay', w_ri, eye).reshape(n * 2 * m, n * m)
    perm = np.einsum('dx,cy->dcyx', eye, eye).reshape(m * m, m * m)
    c = np.arange(n)[:, None, None]
    d = np.arange(n)[None, :, None]
    b = np.arange(n)[None, None, :]
    ang2 = 2.0 * np.pi * ((c + n * d) * b % (n * n)) / (n * n)
    gr, gi = np.cos(ang2) / np.sqrt(n), -np.sin(ang2) / np.sqrt(n)
    w_cols = np.concatenate([np.concatenate([gr, -gi], axis=2),
                             np.concatenate([gi, gr], axis=2)], axis=1)
    ch = np.arange(F_GROUP_DIM)
    ang3 = 2.0 * np.pi * np.outer(ch, ch) / F_GROUP_DIM
    w_chan = np.concatenate([np.cos(ang3), np.sin(ang3)], axis=0) / np.sqrt(F_GROUP_DIM)
    return tuple(jnp.asarray(t, dtype=F32).astype(BF16) for t in (w_rows, w_cols, perm, w_chan))


def _dft_rows_kernel(w_ref, u_ref, z_ref):
    n, m = GRID_W, DFT_CHUNK
    x = u_ref[0].reshape(n * m, F_WIDTH)
    rows_per_dot = 2 * m * 8
    for i in range(n // 8):
        z = _dot(w_ref[i * rows_per_dot:(i + 1) * rows_per_dot, :], x)
        z_ref[0, i * 8:(i + 1) * 8] = z.astype(BF16).reshape(8, 2, m, F_WIDTH)


def _dft_rows(u4, w_rows):
    B, n = u4.shape[0], GRID_W
    m = DFT_CHUNK
    return pl.pallas_call(
        _dft_rows_kernel,
        grid=(B, n // m),
        in_specs=[
            _const_spec((n * 2 * m, n * m)),
            pl.BlockSpec((1, n, m, F_WIDTH), lambda b, j: (b, 0, j, 0)),
        ],
        out_specs=pl.BlockSpec((1, n, 2, m, F_WIDTH), lambda b, j: (b, 0, 0, j, 0)),
        out_shape=jax.ShapeDtypeStruct((B, n, 2, n, F_WIDTH), BF16),
        compiler_params=_params(2),
        name="dft_rows",
    )(w_rows, u4)


def _dft_cols_kernel(w_ref, perm_ref, z_ref, y_ref):
    n, m = GRID_W, DFT_CHUNK
    res = []
    for j in range(m):
        c = pl.program_id(1) * m + j
        data = z_ref[0, j].reshape(2 * n, F_WIDTH)
        res.append(_dot(w_ref[c], data).astype(BF16))
    for ri in range(2):
        for g in range(n // m):
            lo = ri * n + g * m
            stacked = jnp.concatenate([r[lo:lo + m] for r in res], axis=0)
            y = _dot(perm_ref[...], stacked).astype(BF16)
            y_ref[ri, 0, g * m:(g + 1) * m] = y.reshape(m, m, F_WIDTH)


def _dft_cols(z5, w_cols, perm):
    B, n, m = z5.shape[0], GRID_W, DFT_CHUNK
    return pl.pallas_call(
        _dft_cols_kernel,
        grid=(B, n // m),
        in_specs=[
            _const_spec((n, 2 * n, 2 * n)),
            _const_spec((m * m, m * m)),
            pl.BlockSpec((1, m, 2, n, F_WIDTH), lambda b, j: (b, j, 0, 0, 0)),
        ],
        out_specs=pl.BlockSpec((2, 1, n, m, F_WIDTH), lambda b, j: (0, b, 0, j, 0)),
        out_shape=jax.ShapeDtypeStruct((2, B, n, n, F_WIDTH), BF16),
        compiler_params=_params(2),
        name="dft_cols",
    )(w_cols, perm, z5)


def _out_ffn_kernel(x_ref, ona_ref, y_ref, om_ref, gate_ref, wch_ref, wna_ref, wf_ref, wmo_ref,
                    wout_ref, g2_ref, w1_ref, w2_ref, o_ref, *, ff_chunk):
    yr, yi = y_ref[0], y_ref[1]
    groups = []
    for g in range(F_GROUPS):
        sl = slice(g * F_GROUP_DIM, (g + 1) * F_GROUP_DIM)
        groups.append(_dot(jnp.concatenate([yr[:, sl], yi[:, sl]], axis=1), wch_ref[...]))
    yf = jnp.concatenate(groups, axis=1).astype(BF16)

    gate = lambda br: gate_ref[:, br * D_MODEL:(br + 1) * D_MODEL].astype(F32)
    merged = gate(0) * _dot(ona_ref[...], wna_ref[...])
    merged = merged + gate(1) * _dot(yf, wf_ref[...])
    merged = merged + gate(2) * _dot(om_ref[...], wmo_ref[...])
    x1 = x_ref[...] + _dot(merged.astype(BF16), wout_ref[...])

    ms = jnp.mean(x1 * x1, axis=-1, keepdims=True)
    h2 = (x1 * lax.rsqrt(ms + EPS) * g2_ref[...]).astype(BF16)
    acc = x1
    for j in range(D_FF // ff_chunk):
        sl = slice(j * ff_chunk, (j + 1) * ff_chunk)
        a = jnp.maximum(_dot(h2, w1_ref[:, sl]), 0.0)
        acc = acc + _dot((a * a).astype(BF16), w2_ref[sl, :])
    o_ref[...] = acc


def _out_ffn(x2, ona, y3, om, gates, w_chan, w_na_o, w_f, w_mem_o, w_out, norm2_g, w_ff1, w_ff2,
             tm, ff_chunk):
    T = x2.shape[0]
    tok = lambda w: pl.BlockSpec((tm, w), lambda i: (i, 0))
    return pl.pallas_call(
        functools.partial(_out_ffn_kernel, ff_chunk=ff_chunk),
        grid=(T // tm,),
        in_specs=[
            tok(D_MODEL),
            tok(NA_WIDTH),
            pl.BlockSpec((2, tm, F_WIDTH), lambda i: (0, i, 0)),
            tok(MEM_WIDTH),
            tok(N_BRANCHES * D_MODEL),
            _const_spec((2 * F_GROUP_DIM, F_GROUP_DIM)),
            _const_spec((NA_WIDTH, D_MODEL)),
            _const_spec((F_WIDTH, D_MODEL)),
            _const_spec((MEM_WIDTH, D_MODEL)),
            _const_spec((D_MODEL, D_MODEL)),
            _const_spec((1, D_MODEL)),
            _const_spec((D_MODEL, D_FF)),
            _const_spec((D_FF, D_MODEL)),
        ],
        out_specs=tok(D_MODEL),
        out_shape=jax.ShapeDtypeStruct((T, D_MODEL), F32),
        compiler_params=_params(1),
        name="out_ffn",
    )(x2, ona, y3, om, gates, w_chan, w_na_o.astype(BF16), w_f.astype(BF16), w_mem_o.astype(BF16),
      w_out.astype(BF16), norm2_g.reshape(1, D_MODEL), w_ff1.astype(BF16), w_ff2.astype(BF16))


def kernel(x, mem, norm1_g, w_in, b_gate, na_q_g, na_k_g, na_rpb, w_na_o, w_f, mem_norm_g,
           w_mem_kv, mem_q_g, mem_k_g, w_mem_o, w_out, norm2_g, w_ff1, w_ff2):
    B, S, _ = x.shape
    T = B * S
    rows = S // GRID_W
    assert rows == GRID_W and S % NA_PAIR_TOKENS == 0
    x2 = x.reshape(T, D_MODEL)
    gmean_mem = _group_mean_matrix(MEM_WIDTH, MEM_HEAD_DIM)
    w_rows, w_cols, perm, w_chan = _dft_tables()

    memkT, memv = _mem_kv(mem, mem_norm_g, w_mem_kv, gmean_mem, mem_k_g)
    q, kT4, v, fu, om, gates = _in_proj(x2, norm1_g, w_in, b_gate, na_q_g, na_k_g, mem_q_g,
                                        memkT, memv, S, tm=512)

    bias = _na_bias_tables(na_rpb, rows)
    ona = _na_attention(q.reshape(B, S, NA_WIDTH), kT4, v.reshape(B, S, NA_HEADS * V7X_LANES), bias,
                        pairs_per_step=8)

    z = _dft_rows(fu.reshape(B, GRID_W, GRID_W, F_WIDTH), w_rows)
    y = _dft_cols(z, w_cols, perm)
    y3 = y.reshape(2, T, F_WIDTH)

    out = _out_ffn(x2, ona.reshape(T, NA_WIDTH), y3, om, gates, w_chan, w_na_o, w_f, w_mem_o,
                   w_out, norm2_g, w_ff1, w_ff2, tm=256, ff_chunk=1024)
    return out.reshape(B, S, D_MODEL)
```

```python
import functools

import numpy as np
import jax
import jax.numpy as jnp
from jax import lax
from jax.experimental import pallas as pl
from jax.experimental.pallas import tpu as pltpu

D_MODEL = 1024
GRID_W = 64
NA_HEADS = 8
NA_HEAD_DIM = 64
NA_WIDTH = NA_HEADS * NA_HEAD_DIM
NA_WIN_ROWS = 8
NA_WIN_COLS = 16
F_GROUPS = 4
F_GROUP_DIM = 128
F_WIDTH = F_GROUPS * F_GROUP_DIM
MEM_HEADS = 4
MEM_HEAD_DIM = 128
MEM_WIDTH = MEM_HEADS * MEM_HEAD_DIM
N_BRANCHES = 3
D_FF = 4 * D_MODEL
EPS = 1e-6
NEG_INF = -1e30

V7X_LANES = 128
V7X_VMEM_LIMIT_BYTES = 60 * 1024 * 1024

NA_PAIR_TOKENS = 2 * GRID_W
NA_KEY_PAIRS = 5
NA_KEY_TOKENS = NA_KEY_PAIRS * NA_PAIR_TOKENS
NA_BIAS_CASES = 5
NA_PAIRS_PER_ITER = 2
NA_SCORES_AHEAD = 2

BF16 = jnp.bfloat16
F32 = jnp.float32


def _dot(a, b):
    return jnp.dot(a, b, preferred_element_type=F32)


def _dot_nt(a, b):
    return lax.dot_general(a, b, (((1,), (1,)), ((), ())), preferred_element_type=F32)


def _const_spec(shape):
    return pl.BlockSpec(shape, lambda *_: (0,) * len(shape), pipeline_mode=pl.Buffered(1))


def _params(n_axes):
    return pltpu.CompilerParams(
        dimension_semantics=("arbitrary",) * n_axes,
        vmem_limit_bytes=V7X_VMEM_LIMIT_BYTES,
    )


def _group_mean_matrix(width, group):
    idx = np.arange(width) // group
    return jnp.asarray((idx[:, None] == idx[None, :]).astype(np.float32) / group, dtype=BF16)


def _mem_kv_kernel(mem_ref, g_ref, w_ref, gm_ref, kg_ref, kT_ref, v_ref):
    m = mem_ref[0]
    ms = jnp.mean(m * m, axis=-1, keepdims=True)
    mn = (m * lax.rsqrt(ms + EPS) * g_ref[...]).astype(BF16)
    kv = _dot(mn, w_ref[...])
    k = kv[:, :MEM_WIDTH]
    msk = _dot((k * k).astype(BF16), gm_ref[...])
    kn = k * lax.rsqrt(msk + EPS) * kg_ref[...]
    kT_ref[0] = kn.T.astype(BF16)
    v_ref[0] = kv[:, MEM_WIDTH:].astype(BF16)


def _mem_kv(mem, mem_norm_g, w_mem_kv, gmean_mem, mem_k_g):
    B, M, _ = mem.shape
    return pl.pallas_call(
        _mem_kv_kernel,
        grid=(B,),
        in_specs=[
            pl.BlockSpec((1, M, D_MODEL), lambda b: (b, 0, 0)),
            _const_spec((1, D_MODEL)),
            _const_spec((D_MODEL, 2 * MEM_WIDTH)),
            _const_spec((MEM_WIDTH, MEM_WIDTH)),
            _const_spec((1, MEM_WIDTH)),
        ],
        out_specs=[
            pl.BlockSpec((1, MEM_WIDTH, M), lambda b: (b, 0, 0)),
            pl.BlockSpec((1, M, MEM_WIDTH), lambda b: (b, 0, 0)),
        ],
        out_shape=[
            jax.ShapeDtypeStruct((B, MEM_WIDTH, M), BF16),
            jax.ShapeDtypeStruct((B, M, MEM_WIDTH), BF16),
        ],
        compiler_params=_params(1),
        name="mem_kv",
    )(mem, mem_norm_g.reshape(1, D_MODEL), w_mem_kv.astype(BF16), gmean_mem,
      jnp.tile(mem_k_g, MEM_HEADS).reshape(1, MEM_WIDTH))


def _in_proj_kernel(x_ref, g1_ref, wq_ref, wkT_ref, wv_ref, wf_ref, wm_ref, wg_ref, bg_ref,
                    qg_ref, kg_ref, mqg_ref, memkT_ref, memv_ref,
                    q_ref, kT_ref, v_ref, fu_ref, om_ref, gate_ref):
    x = x_ref[...]
    ms = jnp.mean(x * x, axis=-1, keepdims=True)
    h = (x * lax.rsqrt(ms + EPS) * g1_ref[...]).astype(BF16)

    def gate_dot(br):
        sl = slice(br * D_MODEL, (br + 1) * D_MODEL)
        zg = _dot(h, wg_ref[:, sl]) + bg_ref[:, sl]
        gate_ref[:, sl] = jax.nn.sigmoid(zg).astype(BF16)

    zm = _dot(h, wm_ref[...])
    mem_scores = []
    for hd in range(MEM_HEADS):
        sl = slice(hd * MEM_HEAD_DIM, (hd + 1) * MEM_HEAD_DIM)
        zh = zm[:, sl]
        msm = jnp.mean(zh * zh, axis=-1, keepdims=True)
        qm = (zh * lax.rsqrt(msm + EPS) * mqg_ref[:, sl]).astype(BF16)
        if hd == 0:
            gate_dot(0)
        mem_scores.append(_dot(qm, memkT_ref[0, sl, :]))
    gate_dot(1)

    zq = _dot(h, wq_ref[...])
    low_half = lax.broadcasted_iota(jnp.int32, (zq.shape[0], V7X_LANES), 1) < NA_HEAD_DIM
    for hp in range(NA_HEADS // 2):
        sl = slice(hp * V7X_LANES, (hp + 1) * V7X_LANES)
        zz = zq[:, sl] * zq[:, sl]
        ms_lo = jnp.sum(jnp.where(low_half, zz, 0.0), axis=-1, keepdims=True)
        ms_hi = jnp.sum(jnp.where(low_half, 0.0, zz), axis=-1, keepdims=True)
        ms = jnp.where(low_half, ms_lo, ms_hi) * (1.0 / NA_HEAD_DIM)
        q_ref[:, sl] = (zq[:, sl] * lax.rsqrt(ms + EPS) * qg_ref[:, sl]).astype(BF16)

    zkT = _dot_nt(wkT_ref[...], h)
    zk3 = zkT.reshape(NA_HEADS, NA_HEAD_DIM, zkT.shape[1])
    msk = jnp.mean(zk3 * zk3, axis=1, keepdims=True)
    kT = ((zk3 * lax.rsqrt(msk + EPS)).reshape(zkT.shape) * kg_ref[...]).astype(BF16)
    for j in range(kT_ref.shape[1]):
        kT_ref[0, j] = kT[:, j * V7X_LANES:(j + 1) * V7X_LANES]

    zv = _dot(h, wv_ref[...])
    low_half = lax.broadcasted_iota(jnp.int32, (zv.shape[0], V7X_LANES), 1) < NA_HEAD_DIM
    for hp in range(NA_HEADS // 2):
        pair = zv[:, hp * V7X_LANES:(hp + 1) * V7X_LANES]
        v_ref[:, (2 * hp) * V7X_LANES:(2 * hp + 1) * V7X_LANES] = jnp.where(low_half, pair, 1.0).astype(BF16)
        v_ref[:, (2 * hp + 1) * V7X_LANES:(2 * hp + 2) * V7X_LANES] = jnp.where(low_half, 1.0, pair).astype(BF16)
    fu_ref[...] = _dot(h, wf_ref[...]).astype(BF16)
    gate_dot(2)

    outs = []
    for hd in range(MEM_HEADS):
        sl = slice(hd * MEM_HEAD_DIM, (hd + 1) * MEM_HEAD_DIM)
        s = mem_scores[hd]
        e = jnp.exp(s - jnp.max(s, axis=-1, keepdims=True))
        l = jnp.sum(e, axis=-1, keepdims=True)
        outs.append(_dot(e.astype(BF16), memv_ref[0, :, sl]) / l)
    om_ref[...] = jnp.concatenate(outs, axis=1).astype(BF16)


def _in_proj(x2, norm1_g, w_in, b_gate, na_q_g, na_k_g, mem_q_g,
             memkT, memv, tokens_per_batch, tm):
    T = x2.shape[0]
    M = memv.shape[1]
    tiles_per_batch = tokens_per_batch // tm
    slabs = tm // V7X_LANES
    B = T // tokens_per_batch
    wb = w_in.astype(BF16)
    o = 0
    wq = wb[:, o:o + NA_WIDTH]; o += NA_WIDTH
    wkT = wb[:, o:o + NA_WIDTH].T; o += NA_WIDTH
    wv = wb[:, o:o + NA_WIDTH]; o += NA_WIDTH
    wf = wb[:, o:o + F_WIDTH]; o += F_WIDTH
    wm = wb[:, o:o + MEM_WIDTH]; o += MEM_WIDTH
    wg = wb[:, o:]
    na_scale = np.float32(1.0 / np.sqrt(NA_HEAD_DIM))
    mem_scale = np.float32(1.0 / np.sqrt(MEM_HEAD_DIM))
    qg = (jnp.tile(na_q_g, NA_HEADS) * na_scale).reshape(1, NA_WIDTH)
    kg = jnp.tile(na_k_g, NA_HEADS).reshape(NA_WIDTH, 1)
    mqg = (jnp.tile(mem_q_g, MEM_HEADS) * mem_scale).reshape(1, MEM_WIDTH)
    tok = lambda w: pl.BlockSpec((tm, w), lambda i: (i, 0))
    batch_of = lambda i: i // tiles_per_batch
    return pl.pallas_call(
        _in_proj_kernel,
        grid=(T // tm,),
        in_specs=[
            tok(D_MODEL),
            _const_spec((1, D_MODEL)),
            _const_spec((D_MODEL, NA_WIDTH)),
            _const_spec((NA_WIDTH, D_MODEL)),
            _const_spec((D_MODEL, NA_WIDTH)),
            _const_spec((D_MODEL, F_WIDTH)),
            _const_spec((D_MODEL, MEM_WIDTH)),
            _const_spec((D_MODEL, N_BRANCHES * D_MODEL)),
            _const_spec((1, N_BRANCHES * D_MODEL)),
            _const_spec((1, NA_WIDTH)),
            _const_spec((NA_WIDTH, 1)),
            _const_spec((1, MEM_WIDTH)),
            pl.BlockSpec((1, MEM_WIDTH, M), lambda i: (batch_of(i), 0, 0)),
            pl.BlockSpec((1, M, MEM_WIDTH), lambda i: (batch_of(i), 0, 0)),
        ],
        out_specs=[
            tok(NA_WIDTH),
            pl.BlockSpec((1, slabs, NA_WIDTH, V7X_LANES),
                         lambda i: (batch_of(i), i % tiles_per_batch, 0, 0)),
            tok(NA_HEADS * V7X_LANES),
            tok(F_WIDTH),
            tok(MEM_WIDTH),
            tok(N_BRANCHES * D_MODEL),
        ],
        out_shape=[
            jax.ShapeDtypeStruct((T, NA_WIDTH), BF16),
            jax.ShapeDtypeStruct((B, tokens_per_batch // V7X_LANES, NA_WIDTH, V7X_LANES), BF16),
            jax.ShapeDtypeStruct((T, NA_HEADS * V7X_LANES), BF16),
            jax.ShapeDtypeStruct((T, F_WIDTH), BF16),
            jax.ShapeDtypeStruct((T, MEM_WIDTH), BF16),
            jax.ShapeDtypeStruct((T, N_BRANCHES * D_MODEL), BF16),
        ],
        compiler_params=_params(1),
        name="in_proj",
    )(x2, norm1_g.reshape(1, D_MODEL), wq, wkT, wv, wf, wm, wg,
      b_gate.reshape(1, N_BRANCHES * D_MODEL), qg, kg, mqg, memkT, memv)


def _na_bias_tables(rpb, rows):
    n_pairs = rows // 2
    wr = min(NA_WIN_ROWS, rows)
    reps = np.array([0, 1, 2, n_pairs - 2, n_pairs - 1])
    starts = np.clip(reps - 2, 0, n_pairs - NA_KEY_PAIRS)
    n_dr, n_dc = 2 * NA_WIN_ROWS - 1, 2 * NA_WIN_COLS - 1
    c = np.arange(GRID_W)[:, None]
    kc = np.arange(2 * GRID_W)[None, :] % GRID_W
    cs = np.clip(c - NA_WIN_COLS // 2, 0, GRID_W - NA_WIN_COLS)
    col_valid = (kc >= cs) & (kc < cs + NA_WIN_COLS)
    dc = np.clip(kc - c, -(NA_WIN_COLS - 1), NA_WIN_COLS - 1) + (NA_WIN_COLS - 1)
    onehot = (np.arange(n_dc)[:, None] == dc.reshape(1, -1)).astype(np.float32)
    cols = jnp.dot(rpb.astype(F32).reshape(NA_HEADS * n_dr, n_dc), onehot,
                   precision=lax.Precision.HIGHEST)
    cols = jnp.where(col_valid.reshape(1, -1), cols, NEG_INF)
    cols = cols.reshape(NA_HEADS, n_dr, GRID_W, 2 * GRID_W)
    cols = jnp.concatenate([cols, jnp.full((NA_HEADS, 1, GRID_W, 2 * GRID_W), NEG_INF, F32)], axis=1)
    low = (np.arange(2 * GRID_W) < GRID_W)[None, None, None, :]
    lo = jnp.where(low, cols, 0.0)
    hi = jnp.where(low, 0.0, cols)
    r = (2 * reps[:, None] + np.arange(2)[None, :])[:, :, None]
    kr = (2 * starts[:, None] + np.arange(2 * NA_KEY_PAIRS)[None, :])[:, None, :]
    rs = np.clip(r - wr // 2, 0, rows - wr)
    row_valid = (kr >= rs) & (kr < rs + wr)
    dr = np.where(row_valid, kr - r + (NA_WIN_ROWS - 1), n_dr)
    assert dr.min() >= 0 and dr.max() <= n_dr

    def assemble(lo_ref, hi_ref, o_ref):
        for case in range(len(reps)):
            for rr in range(2):
                for j in range(NA_KEY_PAIRS):
                    tile = lo_ref[0, int(dr[case, rr, 2 * j])] + hi_ref[0, int(dr[case, rr, 2 * j + 1])]
                    o_ref[case, 0, rr * GRID_W:(rr + 1) * GRID_W, j * V7X_LANES:(j + 1) * V7X_LANES] = tile

    half_spec = pl.BlockSpec((1, n_dr + 1, GRID_W, 2 * GRID_W), lambda h: (h, 0, 0, 0))
    return pl.pallas_call(
        assemble,
        grid=(NA_HEADS,),
        in_specs=[half_spec, half_spec],
        out_specs=pl.BlockSpec((len(reps), 1, NA_PAIR_TOKENS, NA_KEY_TOKENS), lambda h: (0, h, 0, 0)),
        out_shape=jax.ShapeDtypeStruct((len(reps), NA_HEADS, NA_PAIR_TOKENS, NA_KEY_TOKENS), F32),
        compiler_params=_params(1),
        name="na_bias",
    )(lo, hi)


def _na_kernel(q_ref, kT_ref, v_ref, bias_ref, o_ref, *, pairs_per_step, n_pairs):
    lane = lax.broadcasted_iota(jnp.int32, (NA_PAIR_TOKENS, 2 * NA_HEAD_DIM), 1)
    first_head = lane < NA_HEAD_DIM

    def pair_coords(i):
        p = pl.program_id(1) * pairs_per_step + i
        start = jnp.clip(p - 2, 0, n_pairs - NA_KEY_PAIRS)
        case = jnp.where(p < 2, p, jnp.where(p >= n_pairs - 2, p - (n_pairs - NA_BIAS_CASES), 2))
        qrow = pl.multiple_of(i * NA_PAIR_TOKENS, NA_PAIR_TOKENS)
        krow = pl.multiple_of(start * NA_PAIR_TOKENS, NA_PAIR_TOKENS)
        return start, case, qrow, krow

    def scores(coords, hd):
        start, case, qrow, _ = coords
        hsl = slice(hd * NA_HEAD_DIM, (hd + 1) * NA_HEAD_DIM)
        qh = q_ref[0, pl.ds(qrow, NA_PAIR_TOKENS), hsl]
        kslabs = kT_ref[0, pl.ds(start, NA_KEY_PAIRS), hsl, :]
        kT = jnp.concatenate([kslabs[j] for j in range(NA_KEY_PAIRS)], axis=1)
        return _dot(qh, kT) + bias_ref[case, hd]

    def body(it, carry):
        coords = [pair_coords(it * NA_PAIRS_PER_ITER + j) for j in range(NA_PAIRS_PER_ITER)]
        items = [(j, hd) for j in range(NA_PAIRS_PER_ITER) for hd in range(NA_HEADS)]
        pending = [scores(coords[j], hd) for j, hd in items[:NA_SCORES_AHEAD]]
        halves = []
        for n, (j, hd) in enumerate(items):
            s = pending.pop(0)
            if n + NA_SCORES_AHEAD < len(items):
                jn, hn = items[n + NA_SCORES_AHEAD]
                pending.append(scores(coords[jn], hn))
            _, _, qrow, krow = coords[j]
            vext = v_ref[0, pl.ds(krow, NA_KEY_TOKENS), hd * 2 * NA_HEAD_DIM:(hd + 1) * 2 * NA_HEAD_DIM]
            e = jnp.exp((s - jnp.max(s, axis=-1, keepdims=True)).astype(BF16))
            r = _dot(e, vext)
            halves.append(r / pltpu.roll(r, NA_HEAD_DIM, axis=1))
            if hd == NA_HEADS - 1:
                outs = [jnp.where(first_head, halves[2 * k], halves[2 * k + 1])
                        for k in range(NA_HEADS // 2)]
                o_ref[0, pl.ds(qrow, NA_PAIR_TOKENS), :] = jnp.concatenate(outs, axis=1).astype(BF16)
                halves = []
        return carry

    lax.fori_loop(0, pairs_per_step // NA_PAIRS_PER_ITER, body, 0)


def _na_attention(q, kT4, v, bias, pairs_per_step):
    B, S, _ = q.shape
    n_pairs = S // NA_PAIR_TOKENS
    steps = n_pairs // pairs_per_step
    tq = pairs_per_step * NA_PAIR_TOKENS
    return pl.pallas_call(
        functools.partial(_na_kernel, pairs_per_step=pairs_per_step, n_pairs=n_pairs),
        grid=(B, steps),
        in_specs=[
            pl.BlockSpec((1, tq, NA_WIDTH), lambda b, j: (b, j, 0)),
            pl.BlockSpec((1, n_pairs, NA_WIDTH, V7X_LANES), lambda b, j: (b, 0, 0, 0)),
            pl.BlockSpec((1, S, NA_HEADS * V7X_LANES), lambda b, j: (b, 0, 0)),
            _const_spec((NA_BIAS_CASES, NA_HEADS, NA_PAIR_TOKENS, NA_KEY_TOKENS)),
        ],
        out_specs=pl.BlockSpec((1, tq, NA_WIDTH), lambda b, j: (b, j, 0)),
        out_shape=jax.ShapeDtypeStruct((B, S, NA_WIDTH), BF16),
        compiler_params=_params(2),
        name="na_attn",
    )(q, kT4, v, bias)


DFT_CHUNK = 16


def _dft_tables():
    n, m = GRID_W, DFT_CHUNK
    a = np.arange(n)
    ang1 = 2.0 * np.pi * np.outer(a, a) / n
    w_ri = np.stack([np.cos(ang1), -np.sin(ang1)], axis=1) / np.sqrt(n)
    eye = np.eye(m)
    w_rows = np.einsum('cra,xy->crxay', w_ri, eye).reshape(n * 2 * m, n * m)
    perm = np.einsum('dx,cy->dcyx', eye, eye).reshape(m * m, m * m)
    c = np.arange(n)[:, None, None]
    d = np.arange(n)[None, :, None]
    b = np.arange(n)[None, None, :]
    ang2 = 2.0 * np.pi * ((c + n * d) * b % (n * n)) / (n * n)
    gr, gi = np.cos(ang2) / np.sqrt(n), -np.sin(ang2) / np.sqrt(n)
    w_cols = np.concatenate([np.concatenate([gr, -gi], axis=2),
                             np.concatenate([gi, gr], axis=2)], axis=1)
    ch = np.arange(F_GROUP_DIM)
    ang3 = 2.0 * np.pi * np.outer(ch, ch) / F_GROUP_DIM
    w_chan = np.concatenate([np.cos(ang3), np.sin(ang3)], axis=0) / np.sqrt(F_GROUP_DIM)
    return tuple(jnp.asarray(t, dtype=F32).astype(BF16) for t in (w_rows, w_cols, perm, w_chan))


def _dft_rows_kernel(w_ref, u_ref, z_ref):
    n, m = GRID_W, DFT_CHUNK
    x = u_ref[0].reshape(n * m, F_WIDTH)
    rows_per_dot = 2 * m * 8
    for i in range(n // 8):
        z = _dot(w_ref[i * rows_per_dot:(i + 1) * rows_per_dot, :], x)
        z_ref[0, i * 8:(i + 1) * 8] = z.astype(BF16).reshape(8, 2, m, F_WIDTH)


def _dft_rows(u4, w_rows):
    B, n = u4.shape[0], GRID_W
    m = DFT_CHUNK
    return pl.pallas_call(
        _dft_rows_kernel,
        grid=(B, n // m),
        in_specs=[
            _const_spec((n * 2 * m, n * m)),
            pl.BlockSpec((1, n, m, F_WIDTH), lambda b, j: (b, 0, j, 0)),
        ],
        out_specs=pl.BlockSpec((1, n, 2, m, F_WIDTH), lambda b, j: (b, 0, 0, j, 0)),
        out_shape=jax.ShapeDtypeStruct((B, n, 2, n, F_WIDTH), BF16),
        compiler_params=_params(2),
        name="dft_rows",
    )(w_rows, u4)


def _dft_cols_kernel(w_ref, perm_ref, z_ref, y_ref):
    n, m = GRID_W, DFT_CHUNK
    res = []
    for j in range(m):
        c = pl.program_id(1) * m + j
        data = z_ref[0, j].reshape(2 * n, F_WIDTH)
        res.append(_dot(w_ref[c], data).astype(BF16))
    for ri in range(2):
        for g in range(n // m):
            lo = ri * n + g * m
            stacked = jnp.concatenate([r[lo:lo + m] for r in res], axis=0)
            y = _dot(perm_ref[...], stacked).astype(BF16)
            y_ref[ri, 0, g * m:(g + 1) * m] = y.reshape(m, m, F_WIDTH)


def _dft_cols(z5, w_cols, perm):
    B, n, m = z5.shape[0], GRID_W, DFT_CHUNK
    return pl.pallas_call(
        _dft_cols_kernel,
        grid=(B, n // m),
        in_specs=[
            _const_spec((n, 2 * n, 2 * n)),
            _const_spec((m * m, m * m)),
            pl.BlockSpec((1, m, 2, n, F_WIDTH), lambda b, j: (b, j, 0, 0, 0)),
        ],
        out_specs=pl.BlockSpec((2, 1, n, m, F_WIDTH), lambda b, j: (0, b, 0, j, 0)),
        out_shape=jax.ShapeDtypeStruct((2, B, n, n, F_WIDTH), BF16),
        compiler_params=_params(2),
        name="dft_cols",
    )(w_cols, perm, z5)


def _out_ffn_kernel(x_ref, ona_ref, y_ref, om_ref, gate_ref, wch_ref, wna_ref, wf_ref, wmo_ref,
                    wout_ref, g2_ref, w1_ref, w2_ref, o_ref, *, ff_chunk):
    yr, yi = y_ref[0], y_ref[1]
    groups = []
    for g in range(F_GROUPS):
        sl = slice(g * F_GROUP_DIM, (g + 1) * F_GROUP_DIM)
        groups.append(_dot(jnp.concatenate([yr[:, sl], yi[:, sl]], axis=1), wch_ref[...]))
    yf = jnp.concatenate(groups, axis=1).astype(BF16)

    gate = lambda br: gate_ref[:, br * D_MODEL:(br + 1) * D_MODEL].astype(F32)
    merged = gate(0) * _dot(ona_ref[...], wna_ref[...])
    merged = merged + gate(1) * _dot(yf, wf_ref[...])
    merged = merged + gate(2) * _dot(om_ref[...], wmo_ref[...])
    x1 = x_ref[...] + _dot(merged.astype(BF16), wout_ref[...])

    ms = jnp.mean(x1 * x1, axis=-1, keepdims=True)
    h2 = (x1 * lax.rsqrt(ms + EPS) * g2_ref[...]).astype(BF16)
    acc = x1
    for j in range(D_FF // ff_chunk):
        sl = slice(j * ff_chunk, (j + 1) * ff_chunk)
        a = jnp.maximum(_dot(h2, w1_ref[:, sl]), 0.0)
        acc = acc + _dot((a * a).astype(BF16), w2_ref[sl, :])
    o_ref[...] = acc


def _out_ffn(x2, ona, y3, om, gates, w_chan, w_na_o, w_f, w_mem_o, w_out, norm2_g, w_ff1, w_ff2,
             tm, ff_chunk):
    T = x2.shape[0]
    tok = lambda w: pl.BlockSpec((tm, w), lambda i: (i, 0))
    return pl.pallas_call(
        functools.partial(_out_ffn_kernel, ff_chunk=ff_chunk),
        grid=(T // tm,),
        in_specs=[
            tok(D_MODEL),
            tok(NA_WIDTH),
            pl.BlockSpec((2, tm, F_WIDTH), lambda i: (0, i, 0)),
            tok(MEM_WIDTH),
            tok(N_BRANCHES * D_MODEL),
            _const_spec((2 * F_GROUP_DIM, F_GROUP_DIM)),
            _const_spec((NA_WIDTH, D_MODEL)),
            _const_spec((F_WIDTH, D_MODEL)),
            _const_spec((MEM_WIDTH, D_MODEL)),
            _const_spec((D_MODEL, D_MODEL)),
            _const_spec((1, D_MODEL)),
            _const_spec((D_MODEL, D_FF)),
            _const_spec((D_FF, D_MODEL)),
        ],
        out_specs=tok(D_MODEL),
        out_shape=jax.ShapeDtypeStruct((T, D_MODEL), F32),
        compiler_params=_params(1),
        name="out_ffn",
    )(x2, ona, y3, om, gates, w_chan, w_na_o.astype(BF16), w_f.astype(BF16), w_mem_o.astype(BF16),
      w_out.astype(BF16), norm2_g.reshape(1, D_MODEL), w_ff1.astype(BF16), w_ff2.astype(BF16))


def kernel(x, mem, norm1_g, w_in, b_gate, na_q_g, na_k_g, na_rpb, w_na_o, w_f, mem_norm_g,
           w_mem_kv, mem_q_g, mem_k_g, w_mem_o, w_out, norm2_g, w_ff1, w_ff2):
    B, S, _ = x.shape
    T = B * S
    rows = S // GRID_W
    assert rows == GRID_W and S % NA_PAIR_TOKENS == 0
    x2 = x.reshape(T, D_MODEL)
    gmean_mem = _group_mean_matrix(MEM_WIDTH, MEM_HEAD_DIM)
    w_rows, w_cols, perm, w_chan = _dft_tables()

    memkT, memv = _mem_kv(mem, mem_norm_g, w_mem_kv, gmean_mem, mem_k_g)
    q, kT4, v, fu, om, gates = _in_proj(x2, norm1_g, w_in, b_gate, na_q_g, na_k_g, mem_q_g,
                                        memkT, memv, S, tm=512)

    bias = _na_bias_tables(na_rpb, rows)
    ona = _na_attention(q.reshape(B, S, NA_WIDTH), kT4, v.reshape(B, S, NA_HEADS * V7X_LANES), bias,
                        pairs_per_step=8)

    z = _dft_rows(fu.reshape(B, GRID_W, GRID_W, F_WIDTH), w_rows)
    y = _dft_cols(z, w_cols, perm)
    y3 = y.reshape(2, T, F_WIDTH)

    out = _out_ffn(x2, ona.reshape(T, NA_WIDTH), y3, om, gates, w_chan, w_na_o, w_f, w_mem_o,
                   w_out, norm2_g, w_ff1, w_ff2, tm=512, ff_chunk=1024)
    return out.reshape(B, S, D_MODEL)
```

```python
import functools

import numpy as np
import jax
import jax.numpy as jnp
from jax import lax
from jax.experimental import pallas as pl
from jax.experimental.pallas import tpu as pltpu

D_MODEL = 1024
GRID_W = 64
NA_HEADS = 8
NA_HEAD_DIM = 64
NA_WIDTH = NA_HEADS * NA_HEAD_DIM
NA_WIN_ROWS = 8
NA_WIN_COLS = 16
F_GROUPS = 4
F_GROUP_DIM = 128
F_WIDTH = F_GROUPS * F_GROUP_DIM
MEM_HEADS = 4
MEM_HEAD_DIM = 128
MEM_WIDTH = MEM_HEADS * MEM_HEAD_DIM
N_BRANCHES = 3
D_FF = 4 * D_MODEL
EPS = 1e-6
NEG_INF = -1e30
LOG2E = 1.4426950408889634

V7X_LANES = 128
V7X_VMEM_LIMIT_BYTES = 60 * 1024 * 1024

NA_PAIR_TOKENS = 2 * GRID_W
NA_KEY_PAIRS = 5
NA_KEY_TOKENS = NA_KEY_PAIRS * NA_PAIR_TOKENS
NA_BIAS_CASES = 5
NA_PAIRS_PER_ITER = 2
NA_SCORES_AHEAD = 2

BF16 = jnp.bfloat16
F32 = jnp.float32


def _dot(a, b):
    return jnp.dot(a, b, preferred_element_type=F32)


def _dot_nt(a, b):
    return lax.dot_general(a, b, (((1,), (1,)), ((), ())), preferred_element_type=F32)


def _const_spec(shape):
    return pl.BlockSpec(shape, lambda *_: (0,) * len(shape), pipeline_mode=pl.Buffered(1))


def _params(n_axes):
    return pltpu.CompilerParams(
        dimension_semantics=("arbitrary",) * n_axes,
        vmem_limit_bytes=V7X_VMEM_LIMIT_BYTES,
    )


def _group_mean_matrix(width, group):
    idx = np.arange(width) // group
    return jnp.asarray((idx[:, None] == idx[None, :]).astype(np.float32) / group, dtype=BF16)


def _mem_kv_kernel(mem_ref, g_ref, w_ref, gm_ref, kg_ref, kT_ref, v_ref):
    m = mem_ref[0]
    ms = jnp.mean(m * m, axis=-1, keepdims=True)
    mn = (m * lax.rsqrt(ms + EPS) * g_ref[...]).astype(BF16)
    kv = _dot(mn, w_ref[...])
    k = kv[:, :MEM_WIDTH]
    msk = _dot((k * k).astype(BF16), gm_ref[...])
    kn = k * lax.rsqrt(msk + EPS) * kg_ref[...]
    kT_ref[0] = kn.T.astype(BF16)
    v_ref[0] = kv[:, MEM_WIDTH:].astype(BF16)


def _mem_kv(mem, mem_norm_g, w_mem_kv, gmean_mem, mem_k_g):
    B, M, _ = mem.shape
    return pl.pallas_call(
        _mem_kv_kernel,
        grid=(B,),
        in_specs=[
            pl.BlockSpec((1, M, D_MODEL), lambda b: (b, 0, 0)),
            _const_spec((1, D_MODEL)),
            _const_spec((D_MODEL, 2 * MEM_WIDTH)),
            _const_spec((MEM_WIDTH, MEM_WIDTH)),
            _const_spec((1, MEM_WIDTH)),
        ],
        out_specs=[
            pl.BlockSpec((1, MEM_WIDTH, M), lambda b: (b, 0, 0)),
            pl.BlockSpec((1, M, MEM_WIDTH), lambda b: (b, 0, 0)),
        ],
        out_shape=[
            jax.ShapeDtypeStruct((B, MEM_WIDTH, M), BF16),
            jax.ShapeDtypeStruct((B, M, MEM_WIDTH), BF16),
        ],
        compiler_params=_params(1),
        name="mem_kv",
    )(mem, mem_norm_g.reshape(1, D_MODEL), w_mem_kv.astype(BF16), gmean_mem,
      jnp.tile(mem_k_g, MEM_HEADS).reshape(1, MEM_WIDTH))


def _in_proj_kernel(x_ref, g1_ref, wq_ref, wkT_ref, wv_ref, wf_ref, wm_ref, wg_ref, bg_ref,
                    qg_ref, kg_ref, mqg_ref, memkT_ref, memv_ref,
                    q_ref, kT_ref, v_ref, fu_ref, om_ref, gate_ref):
    x = x_ref[...]
    ms = jnp.mean(x * x, axis=-1, keepdims=True)
    h = (x * lax.rsqrt(ms + EPS) * g1_ref[...]).astype(BF16)

    def gate_dot(br):
        sl = slice(br * D_MODEL, (br + 1) * D_MODEL)
        zg = _dot(h, wg_ref[:, sl]) + bg_ref[:, sl]
        gate_ref[:, sl] = jax.nn.sigmoid(zg).astype(BF16)

    zm = _dot(h, wm_ref[...])
    mem_scores = []
    for hd in range(MEM_HEADS):
        sl = slice(hd * MEM_HEAD_DIM, (hd + 1) * MEM_HEAD_DIM)
        zh = zm[:, sl]
        msm = jnp.mean(zh * zh, axis=-1, keepdims=True)
        qm = (zh * lax.rsqrt(msm + EPS) * mqg_ref[:, sl]).astype(BF16)
        if hd == 0:
            gate_dot(0)
        mem_scores.append(_dot(qm, memkT_ref[0, sl, :]))
    gate_dot(1)

    zq = _dot(h, wq_ref[...])
    low_half = lax.broadcasted_iota(jnp.int32, (zq.shape[0], V7X_LANES), 1) < NA_HEAD_DIM
    for hp in range(NA_HEADS // 2):
        sl = slice(hp * V7X_LANES, (hp + 1) * V7X_LANES)
        zz = zq[:, sl] * zq[:, sl]
        ms_lo = jnp.sum(jnp.where(low_half, zz, 0.0), axis=-1, keepdims=True)
        ms_hi = jnp.sum(jnp.where(low_half, 0.0, zz), axis=-1, keepdims=True)
        ms = jnp.where(low_half, ms_lo, ms_hi) * (1.0 / NA_HEAD_DIM)
        q_ref[:, sl] = (zq[:, sl] * lax.rsqrt(ms + EPS) * qg_ref[:, sl]).astype(BF16)

    zkT = _dot_nt(wkT_ref[...], h)
    zk3 = zkT.reshape(NA_HEADS, NA_HEAD_DIM, zkT.shape[1])
    msk = jnp.mean(zk3 * zk3, axis=1, keepdims=True)
    kT = ((zk3 * lax.rsqrt(msk + EPS)).reshape(zkT.shape) * kg_ref[...]).astype(BF16)
    for j in range(kT_ref.shape[1]):
        kT_ref[0, j] = kT[:, j * V7X_LANES:(j + 1) * V7X_LANES]

    zv = _dot(h, wv_ref[...])
    low_half = lax.broadcasted_iota(jnp.int32, (zv.shape[0], V7X_LANES), 1) < NA_HEAD_DIM
    for hp in range(NA_HEADS // 2):
        pair = zv[:, hp * V7X_LANES:(hp + 1) * V7X_LANES]
        v_ref[:, (2 * hp) * V7X_LANES:(2 * hp + 1) * V7X_LANES] = jnp.where(low_half, pair, 1.0).astype(BF16)
        v_ref[:, (2 * hp + 1) * V7X_LANES:(2 * hp + 2) * V7X_LANES] = jnp.where(low_half, 1.0, pair).astype(BF16)
    fu_ref[...] = _dot(h, wf_ref[...]).astype(BF16)
    gate_dot(2)

    outs = []
    for hd in range(MEM_HEADS):
        sl = slice(hd * MEM_HEAD_DIM, (hd + 1) * MEM_HEAD_DIM)
        s = mem_scores[hd]
        e = jnp.exp(s - jnp.max(s, axis=-1, keepdims=True))
        l = jnp.sum(e, axis=-1, keepdims=True)
        outs.append(_dot(e.astype(BF16), memv_ref[0, :, sl]) / l)
    om_ref[...] = jnp.concatenate(outs, axis=1).astype(BF16)


def _in_proj(x2, norm1_g, w_in, b_gate, na_q_g, na_k_g, mem_q_g,
             memkT, memv, tokens_per_batch, tm):
    T = x2.shape[0]
    M = memv.shape[1]
    tiles_per_batch = tokens_per_batch // tm
    slabs = tm // V7X_LANES
    B = T // tokens_per_batch
    wb = w_in.astype(BF16)
    o = 0
    wq = wb[:, o:o + NA_WIDTH]; o += NA_WIDTH
    wkT = wb[:, o:o + NA_WIDTH].T; o += NA_WIDTH
    wv = wb[:, o:o + NA_WIDTH]; o += NA_WIDTH
    wf = wb[:, o:o + F_WIDTH]; o += F_WIDTH
    wm = wb[:, o:o + MEM_WIDTH]; o += MEM_WIDTH
    wg = wb[:, o:]
    na_scale = np.float32(LOG2E / np.sqrt(NA_HEAD_DIM))
    mem_scale = np.float32(1.0 / np.sqrt(MEM_HEAD_DIM))
    qg = (jnp.tile(na_q_g, NA_HEADS) * na_scale).reshape(1, NA_WIDTH)
    kg = jnp.tile(na_k_g, NA_HEADS).reshape(NA_WIDTH, 1)
    mqg = (jnp.tile(mem_q_g, MEM_HEADS) * mem_scale).reshape(1, MEM_WIDTH)
    tok = lambda w: pl.BlockSpec((tm, w), lambda i: (i, 0))
    batch_of = lambda i: i // tiles_per_batch
    return pl.pallas_call(
        _in_proj_kernel,
        grid=(T // tm,),
        in_specs=[
            tok(D_MODEL),
            _const_spec((1, D_MODEL)),
            _const_spec((D_MODEL, NA_WIDTH)),
            _const_spec((NA_WIDTH, D_MODEL)),
            _const_spec((D_MODEL, NA_WIDTH)),
            _const_spec((D_MODEL, F_WIDTH)),
            _const_spec((D_MODEL, MEM_WIDTH)),
            _const_spec((D_MODEL, N_BRANCHES * D_MODEL)),
            _const_spec((1, N_BRANCHES * D_MODEL)),
            _const_spec((1, NA_WIDTH)),
            _const_spec((NA_WIDTH, 1)),
            _const_spec((1, MEM_WIDTH)),
            pl.BlockSpec((1, MEM_WIDTH, M), lambda i: (batch_of(i), 0, 0)),
            pl.BlockSpec((1, M, MEM_WIDTH), lambda i: (batch_of(i), 0, 0)),
        ],
        out_specs=[
            tok(NA_WIDTH),
            pl.BlockSpec((1, slabs, NA_WIDTH, V7X_LANES),
                         lambda i: (batch_of(i), i % tiles_per_batch, 0, 0)),
            tok(NA_HEADS * V7X_LANES),
            tok(F_WIDTH),
            tok(MEM_WIDTH),
            tok(N_BRANCHES * D_MODEL),
        ],
        out_shape=[
            jax.ShapeDtypeStruct((T, NA_WIDTH), BF16),
            jax.ShapeDtypeStruct((B, tokens_per_batch // V7X_LANES, NA_WIDTH, V7X_LANES), BF16),
            jax.ShapeDtypeStruct((T, NA_HEADS * V7X_LANES), BF16),
            jax.ShapeDtypeStruct((T, F_WIDTH), BF16),
            jax.ShapeDtypeStruct((T, MEM_WIDTH), BF16),
            jax.ShapeDtypeStruct((T, N_BRANCHES * D_MODEL), BF16),
        ],
        compiler_params=_params(1),
        name="in_proj",
    )(x2, norm1_g.reshape(1, D_MODEL), wq, wkT, wv, wf, wm, wg,
      b_gate.reshape(1, N_BRANCHES * D_MODEL), qg, kg, mqg, memkT, memv)


def _na_bias_tables(rpb, rows):
    n_pairs = rows // 2
    wr = min(NA_WIN_ROWS, rows)
    reps = np.array([0, 1, 2, n_pairs - 2, n_pairs - 1])
    starts = np.clip(reps - 2, 0, n_pairs - NA_KEY_PAIRS)
    n_dr, n_dc = 2 * NA_WIN_ROWS - 1, 2 * NA_WIN_COLS - 1
    c = np.arange(GRID_W)[:, None]
    kc = np.arange(2 * GRID_W)[None, :] % GRID_W
    cs = np.clip(c - NA_WIN_COLS // 2, 0, GRID_W - NA_WIN_COLS)
    col_valid = (kc >= cs) & (kc < cs + NA_WIN_COLS)
    dc = np.clip(kc - c, -(NA_WIN_COLS - 1), NA_WIN_COLS - 1) + (NA_WIN_COLS - 1)
    onehot = (np.arange(n_dc)[:, None] == dc.reshape(1, -1)).astype(np.float32)
    cols = jnp.dot(rpb.astype(F32).reshape(NA_HEADS * n_dr, n_dc), onehot,
                   precision=lax.Precision.HIGHEST)
    cols = jnp.where(col_valid.reshape(1, -1), cols * np.float32(LOG2E), NEG_INF)
    cols = cols.reshape(NA_HEADS, n_dr, GRID_W, 2 * GRID_W)
    cols = jnp.concatenate([cols, jnp.full((NA_HEADS, 1, GRID_W, 2 * GRID_W), NEG_INF, F32)], axis=1)
    low = (np.arange(2 * GRID_W) < GRID_W)[None, None, None, :]
    lo = jnp.where(low, cols, 0.0)
    hi = jnp.where(low, 0.0, cols)
    r = (2 * reps[:, None] + np.arange(2)[None, :])[:, :, None]
    kr = (2 * starts[:, None] + np.arange(2 * NA_KEY_PAIRS)[None, :])[:, None, :]
    rs = np.clip(r - wr // 2, 0, rows - wr)
    row_valid = (kr >= rs) & (kr < rs + wr)
    dr = np.where(row_valid, kr - r + (NA_WIN_ROWS - 1), n_dr)
    assert dr.min() >= 0 and dr.max() <= n_dr

    def assemble(lo_ref, hi_ref, o_ref):
        for case in range(len(reps)):
            for rr in range(2):
                for j in range(NA_KEY_PAIRS):
                    tile = lo_ref[0, int(dr[case, rr, 2 * j])] + hi_ref[0, int(dr[case, rr, 2 * j + 1])]
                    o_ref[case, 0, rr * GRID_W:(rr + 1) * GRID_W,
                          j * V7X_LANES:(j + 1) * V7X_LANES] = tile

    half_spec = pl.BlockSpec((1, n_dr + 1, GRID_W, 2 * GRID_W), lambda h: (h, 0, 0, 0))
    return pl.pallas_call(
        assemble,
        grid=(NA_HEADS,),
        in_specs=[half_spec, half_spec],
        out_specs=pl.BlockSpec((len(reps), 1, NA_PAIR_TOKENS, NA_KEY_TOKENS), lambda h: (0, h, 0, 0)),
        out_shape=jax.ShapeDtypeStruct((len(reps), NA_HEADS, NA_PAIR_TOKENS, NA_KEY_TOKENS), F32),
        compiler_params=_params(1),
        name="na_bias",
    )(lo, hi)


def _na_kernel(q_ref, kT_ref, v_ref, bias_ref, o_ref, *, pairs_per_step, n_pairs):
    lane = lax.broadcasted_iota(jnp.int32, (NA_PAIR_TOKENS, 2 * NA_HEAD_DIM), 1)
    first_head = lane < NA_HEAD_DIM

    def pair_coords(i):
        p = pl.program_id(1) * pairs_per_step + i
        start = jnp.clip(p - 2, 0, n_pairs - NA_KEY_PAIRS)
        case = jnp.where(p < 2, p, jnp.where(p >= n_pairs - 2, p - (n_pairs - NA_BIAS_CASES), 2))
        qrow = pl.multiple_of(i * NA_PAIR_TOKENS, NA_PAIR_TOKENS)
        krow = pl.multiple_of(start * NA_PAIR_TOKENS, NA_PAIR_TOKENS)
        return start, case, qrow, krow

    def scores(coords, hd):
        start, case, qrow, _ = coords
        hsl = slice(hd * NA_HEAD_DIM, (hd + 1) * NA_HEAD_DIM)
        qh = q_ref[0, pl.ds(qrow, NA_PAIR_TOKENS), hsl]
        kslabs = kT_ref[0, pl.ds(start, NA_KEY_PAIRS), hsl, :]
        kT = jnp.concatenate([kslabs[j] for j in range(NA_KEY_PAIRS)], axis=1)
        return _dot(qh, kT) + bias_ref[case, hd]

    def body(it, carry):
        coords = [pair_coords(it * NA_PAIRS_PER_ITER + j) for j in range(NA_PAIRS_PER_ITER)]
        items = [(j, hd) for j in range(NA_PAIRS_PER_ITER) for hd in range(NA_HEADS)]
        pending = [scores(coords[j], hd) for j, hd in items[:NA_SCORES_AHEAD]]
        halves = []
        for n, (j, hd) in enumerate(items):
            s = pending.pop(0)
            if n + NA_SCORES_AHEAD < len(items):
                jn, hn = items[n + NA_SCORES_AHEAD]
                pending.append(scores(coords[jn], hn))
            _, _, qrow, krow = coords[j]
            vext = v_ref[0, pl.ds(krow, NA_KEY_TOKENS), hd * 2 * NA_HEAD_DIM:(hd + 1) * 2 * NA_HEAD_DIM]
            e = jnp.exp2((s - jnp.max(s, axis=-1, keepdims=True)).astype(BF16))
            r = _dot(e, vext)
            halves.append(r / pltpu.roll(r, NA_HEAD_DIM, axis=1))
            if hd == NA_HEADS - 1:
                outs = [jnp.where(first_head, halves[2 * k], halves[2 * k + 1])
                        for k in range(NA_HEADS // 2)]
                o_ref[0, pl.ds(qrow, NA_PAIR_TOKENS), :] = jnp.concatenate(outs, axis=1).astype(BF16)
                halves = []
        return carry

    lax.fori_loop(0, pairs_per_step // NA_PAIRS_PER_ITER, body, 0)


def _na_attention(q, kT4, v, bias, pairs_per_step):
    B, S, _ = q.shape
    n_pairs = S // NA_PAIR_TOKENS
    steps = n_pairs // pairs_per_step
    tq = pairs_per_step * NA_PAIR_TOKENS
    return pl.pallas_call(
        functools.partial(_na_kernel, pairs_per_step=pairs_per_step, n_pairs=n_pairs),
        grid=(B, steps),
        in_specs=[
            pl.BlockSpec((1, tq, NA_WIDTH), lambda b, j: (b, j, 0)),
            pl.BlockSpec((1, n_pairs, NA_WIDTH, V7X_LANES), lambda b, j: (b, 0, 0, 0)),
            pl.BlockSpec((1, S, NA_HEADS * V7X_LANES), lambda b, j: (b, 0, 0)),
            _const_spec((NA_BIAS_CASES, NA_HEADS, NA_PAIR_TOKENS, NA_KEY_TOKENS)),
        ],
        out_specs=pl.BlockSpec((1, tq, NA_WIDTH), lambda b, j: (b, j, 0)),
        out_shape=jax.ShapeDtypeStruct((B, S, NA_WIDTH), BF16),
        compiler_params=_params(2),
        name="na_attn",
    )(q, kT4, v, bias)


DFT_CHUNK = 16


def _dft_tables():
    n, m = GRID_W, DFT_CHUNK
    a = np.arange(n)
    ang1 = 2.0 * np.pi * np.outer(a, a) / n
    w_ri = np.stack([np.cos(ang1), -np.sin(ang1)], axis=1) / np.sqrt(n)
    eye = np.eye(m)
    w_rows = np.einsum('cra,xy->crxay', w_ri, eye).reshape(n * 2 * m, n * m)
    perm = np.einsum('dx,cy->dcyx', eye, eye).reshape(m * m, m * m)
    c = np.arange(n)[:, None, None]
    d = np.arange(n)[None, :, None]
    b = np.arange(n)[None, None, :]
    ang2 = 2.0 * np.pi * ((c + n * d) * b % (n * n)) / (n * n)
    gr, gi = np.cos(ang2) / np.sqrt(n), -np.sin(ang2) / np.sqrt(n)
    w_cols = np.concatenate([np.concatenate([gr, -gi], axis=2),
                             np.concatenate([gi, gr], axis=2)], axis=1)
    ch = np.arange(F_GROUP_DIM)
    ang3 = 2.0 * np.pi * np.outer(ch, ch) / F_GROUP_DIM
    w_chan = np.concatenate([np.cos(ang3), np.sin(ang3)], axis=0) / np.sqrt(F_GROUP_DIM)
    return tuple(jnp.asarray(t, dtype=F32).astype(BF16) for t in (w_rows, w_cols, perm, w_chan))


def _dft_rows_kernel(w_ref, u_ref, z_ref):
    n, m = GRID_W, DFT_CHUNK
    x = u_ref[0].reshape(n * m, F_WIDTH)
    rows_per_dot = 2 * m * 8
    for i in range(n // 8):
        z = _dot(w_ref[i * rows_per_dot:(i + 1) * rows_per_dot, :], x)
        z_ref[0, i * 8:(i + 1) * 8] = z.astype(BF16).reshape(8, 2, m, F_WIDTH)


def _dft_rows(u4, w_rows):
    B, n = u4.shape[0], GRID_W
    m = DFT_CHUNK
    return pl.pallas_call(
        _dft_rows_kernel,
        grid=(B, n // m),
        in_specs=[
            _const_spec((n * 2 * m, n * m)),
            pl.BlockSpec((1, n, m, F_WIDTH), lambda b, j: (b, 0, j, 0)),
        ],
        out_specs=pl.BlockSpec((1, n, 2, m, F_WIDTH), lambda b, j: (b, 0, 0, j, 0)),
        out_shape=jax.ShapeDtypeStruct((B, n, 2, n, F_WIDTH), BF16),
        compiler_params=_params(2),
        name="dft_rows",
    )(w_rows, u4)


def _dft_cols_kernel(w_ref, perm_ref, z_ref, y_ref):
    n, m = GRID_W, DFT_CHUNK
    res = []
    for j in range(m):
        c = pl.program_id(1) * m + j
        data = z_ref[0, j].reshape(2 * n, F_WIDTH)
        res.append(_dot(w_ref[c], data).astype(BF16))
    for ri in range(2):
        for g in range(n // m):
            lo = ri * n + g * m
            stacked = jnp.concatenate([r[lo:lo + m] for r in res], axis=0)
            y = _dot(perm_ref[...], stacked).astype(BF16)
            y_ref[ri, 0, g * m:(g + 1) * m] = y.reshape(m, m, F_WIDTH)


def _dft_cols(z5, w_cols, perm):
    B, n, m = z5.shape[0], GRID_W, DFT_CHUNK
    return pl.pallas_call(
        _dft_cols_kernel,
        grid=(B, n // m),
        in_specs=[
            _const_spec((n, 2 * n, 2 * n)),
            _const_spec((m * m, m * m)),
            pl.BlockSpec((1, m, 2, n, F_WIDTH), lambda b, j: (b, j, 0, 0, 0)),
        ],
        out_specs=pl.BlockSpec((2, 1, n, m, F_WIDTH), lambda b, j: (0, b, 0, j, 0)),
        out_shape=jax.ShapeDtypeStruct((2, B, n, n, F_WIDTH), BF16),
        compiler_params=_params(2),
        name="dft_cols",
    )(w_cols, perm, z5)


def _out_ffn_kernel(x_ref, ona_ref, y_ref, om_ref, gate_ref, wch_ref, wna_ref, wf_ref, wmo_ref,
                    wout_ref, g2_ref, w1_ref, w2_ref, o_ref, *, ff_chunk):
    yr, yi = y_ref[0], y_ref[1]
    groups = []
    for g in range(F_GROUPS):
        sl = slice(g * F_GROUP_DIM, (g + 1) * F_GROUP_DIM)
        groups.append(_dot(jnp.concatenate([yr[:, sl], yi[:, sl]], axis=1), wch_ref[...]))
    yf = jnp.concatenate(groups, axis=1).astype(BF16)

    gate = lambda br: gate_ref[:, br * D_MODEL:(br + 1) * D_MODEL].astype(F32)
    merged = gate(0) * _dot(ona_ref[...], wna_ref[...])
    merged = merged + gate(1) * _dot(yf, wf_ref[...])
    merged = merged + gate(2) * _dot(om_ref[...], wmo_ref[...])
    x1 = x_ref[...] + _dot(merged.astype(BF16), wout_ref[...])

    ms = jnp.mean(x1 * x1, axis=-1, keepdims=True)
    h2 = (x1 * lax.rsqrt(ms + EPS) * g2_ref[...]).astype(BF16)
    acc = x1
    for j in range(D_FF // ff_chunk):
        sl = slice(j * ff_chunk, (j + 1) * ff_chunk)
        a = jnp.maximum(_dot(h2, w1_ref[:, sl]), 0.0)
        acc = acc + _dot((a * a).astype(BF16), w2_ref[sl, :])
    o_ref[...] = acc


def _out_ffn(x2, ona, y3, om, gates, w_chan, w_na_o, w_f, w_mem_o, w_out, norm2_g, w_ff1, w_ff2,
             tm, ff_chunk):
    T = x2.shape[0]
    tok = lambda w: pl.BlockSpec((tm, w), lambda i: (i, 0))
    return pl.pallas_call(
        functools.partial(_out_ffn_kernel, ff_chunk=ff_chunk),
        grid=(T // tm,),
        in_specs=[
            tok(D_MODEL),
            tok(NA_WIDTH),
            pl.BlockSpec((2, tm, F_WIDTH), lambda i: (0, i, 0)),
            tok(MEM_WIDTH),
            tok(N_BRANCHES * D_MODEL),
            _const_spec((2 * F_GROUP_DIM, F_GROUP_DIM)),
            _const_spec((NA_WIDTH, D_MODEL)),
            _const_spec((F_WIDTH, D_MODEL)),
            _const_spec((MEM_WIDTH, D_MODEL)),
            _const_spec((D_MODEL, D_MODEL)),
            _const_spec((1, D_MODEL)),
            _const_spec((D_MODEL, D_FF)),
            _const_spec((D_FF, D_MODEL)),
        ],
        out_specs=tok(D_MODEL),
        out_shape=jax.ShapeDtypeStruct((T, D_MODEL), F32),
        compiler_params=_params(1),
        name="out_ffn",
    )(x2, ona, y3, om, gates, w_chan, w_na_o.astype(BF16), w_f.astype(BF16), w_mem_o.astype(BF16),
      w_out.astype(BF16), norm2_g.reshape(1, D_MODEL), w_ff1.astype(BF16), w_ff2.astype(BF16))


def kernel(x, mem, norm1_g, w_in, b_gate, na_q_g, na_k_g, na_rpb, w_na_o, w_f, mem_norm_g,
           w_mem_kv, mem_q_g, mem_k_g, w_mem_o, w_out, norm2_g, w_ff1, w_ff2):
    B, S, _ = x.shape
    T = B * S
    rows = S // GRID_W
    assert rows == GRID_W and S % NA_PAIR_TOKENS == 0
    x2 = x.reshape(T, D_MODEL)
    gmean_mem = _group_mean_matrix(MEM_WIDTH, MEM_HEAD_DIM)
    w_rows, w_cols, perm, w_chan = _dft_tables()

    memkT, memv = _mem_kv(mem, mem_norm_g, w_mem_kv, gmean_mem, mem_k_g)
    q, kT4, v, fu, om, gates = _in_proj(x2, norm1_g, w_in, b_gate, na_q_g, na_k_g, mem_q_g,
                                        memkT, memv, S, tm=512)

    bias = _na_bias_tables(na_rpb, rows)
    ona = _na_attention(q.reshape(B, S, NA_WIDTH), kT4, v.reshape(B, S, NA_HEADS * V7X_LANES), bias,
                        pairs_per_step=16)

    z = _dft_rows(fu.reshape(B, GRID_W, GRID_W, F_WIDTH), w_rows)
    y = _dft_cols(z, w_cols, perm)
    y3 = y.reshape(2, T, F_WIDTH)

    out = _out_ffn(x2, ona.reshape(T, NA_WIDTH), y3, om, gates, w_chan, w_na_o, w_f, w_mem_o,
                   w_out, norm2_g, w_ff1, w_ff2, tm=512, ff_chunk=1024)
    return out.reshape(B, S, D_MODEL)
```

```python
import functools

import numpy as np
import jax
import jax.numpy as jnp
from jax import lax
from jax.experimental import pallas as pl
from jax.experimental.pallas import tpu as pltpu

D_MODEL = 1024
GRID_W = 64
NA_HEADS = 8
NA_HEAD_DIM = 64
NA_WIDTH = NA_HEADS * NA_HEAD_DIM
NA_WIN_ROWS = 8
NA_WIN_COLS = 16
F_GROUPS = 4
F_GROUP_DIM = 128
F_WIDTH = F_GROUPS * F_GROUP_DIM
MEM_HEADS = 4
MEM_HEAD_DIM = 128
MEM_WIDTH = MEM_HEADS * MEM_HEAD_DIM
N_BRANCHES = 3
D_FF = 4 * D_MODEL
EPS = 1e-6
NEG_INF = -1e30
LOG2E = 1.4426950408889634

V7X_LANES = 128
V7X_VMEM_LIMIT_BYTES = 60 * 1024 * 1024

NA_PAIR_TOKENS = 2 * GRID_W
NA_KEY_PAIRS = 5
NA_KEY_TOKENS = NA_KEY_PAIRS * NA_PAIR_TOKENS
NA_BIAS_CASES = 5
NA_PAIRS_PER_ITER = 2
NA_SCORES_AHEAD = 2

BF16 = jnp.bfloat16
F32 = jnp.float32


def _dot(a, b):
    return jnp.dot(a, b, preferred_element_type=F32)


def _dot_nt(a, b):
    return lax.dot_general(a, b, (((1,), (1,)), ((), ())), preferred_element_type=F32)


def _const_spec(shape):
    return pl.BlockSpec(shape, lambda *_: (0,) * len(shape), pipeline_mode=pl.Buffered(1))


def _params(n_axes):
    return pltpu.CompilerParams(
        dimension_semantics=("arbitrary",) * n_axes,
        vmem_limit_bytes=V7X_VMEM_LIMIT_BYTES,
    )


def _group_mean_matrix(width, group):
    idx = np.arange(width) // group
    return jnp.asarray((idx[:, None] == idx[None, :]).astype(np.float32) / group, dtype=BF16)


def _mem_kv_kernel(mem_ref, g_ref, w_ref, gm_ref, kg_ref, kT_ref, v_ref):
    m = mem_ref[0]
    ms = jnp.mean(m * m, axis=-1, keepdims=True)
    mn = (m * lax.rsqrt(ms + EPS) * g_ref[...]).astype(BF16)
    kv = _dot(mn, w_ref[...])
    k = kv[:, :MEM_WIDTH]
    msk = _dot((k * k).astype(BF16), gm_ref[...])
    kn = k * lax.rsqrt(msk + EPS) * kg_ref[...]
    kT_ref[0] = kn.T.astype(BF16)
    v_ref[0] = kv[:, MEM_WIDTH:].astype(BF16)


def _mem_kv(mem, mem_norm_g, w_mem_kv, gmean_mem, mem_k_g):
    B, M, _ = mem.shape
    return pl.pallas_call(
        _mem_kv_kernel,
        grid=(B,),
        in_specs=[
            pl.BlockSpec((1, M, D_MODEL), lambda b: (b, 0, 0)),
            _const_spec((1, D_MODEL)),
            _const_spec((D_MODEL, 2 * MEM_WIDTH)),
            _const_spec((MEM_WIDTH, MEM_WIDTH)),
            _const_spec((1, MEM_WIDTH)),
        ],
        out_specs=[
            pl.BlockSpec((1, MEM_WIDTH, M), lambda b: (b, 0, 0)),
            pl.BlockSpec((1, M, MEM_WIDTH), lambda b: (b, 0, 0)),
        ],
        out_shape=[
            jax.ShapeDtypeStruct((B, MEM_WIDTH, M), BF16),
            jax.ShapeDtypeStruct((B, M, MEM_WIDTH), BF16),
        ],
        compiler_params=_params(1),
        name="mem_kv",
    )(mem, mem_norm_g.reshape(1, D_MODEL), w_mem_kv.astype(BF16), gmean_mem,
      jnp.tile(mem_k_g, MEM_HEADS).reshape(1, MEM_WIDTH))


def _in_proj_kernel(x_ref, g1_ref, wq_ref, wkT_ref, wv_ref, wf_ref, wm_ref, wg_ref, bg_ref,
                    qg_ref, kg_ref, mqg_ref, memkT_ref, memv_ref,
                    q_ref, kT_ref, v_ref, fu_ref, om_ref, gate_ref):
    x = x_ref[...]
    ms = jnp.mean(x * x, axis=-1, keepdims=True)
    h = (x * lax.rsqrt(ms + EPS) * g1_ref[...]).astype(BF16)

    def gate_dot(br):
        sl = slice(br * D_MODEL, (br + 1) * D_MODEL)
        zg = _dot(h, wg_ref[:, sl]) + bg_ref[:, sl]
        gate_ref[:, sl] = jax.nn.sigmoid(zg).astype(BF16)

    zm = _dot(h, wm_ref[...])
    mem_scores = []
    for hd in range(MEM_HEADS):
        sl = slice(hd * MEM_HEAD_DIM, (hd + 1) * MEM_HEAD_DIM)
        zh = zm[:, sl]
        msm = jnp.mean(zh * zh, axis=-1, keepdims=True)
        qm = (zh * lax.rsqrt(msm + EPS) * mqg_ref[:, sl]).astype(BF16)
        if hd == 0:
            gate_dot(0)
        mem_scores.append(_dot(qm, memkT_ref[0, sl, :]))
    gate_dot(1)

    zq = _dot(h, wq_ref[...])
    low_half = lax.broadcasted_iota(jnp.int32, (zq.shape[0], V7X_LANES), 1) < NA_HEAD_DIM
    for hp in range(NA_HEADS // 2):
        sl = slice(hp * V7X_LANES, (hp + 1) * V7X_LANES)
        zz = zq[:, sl] * zq[:, sl]
        ms_lo = jnp.sum(jnp.where(low_half, zz, 0.0), axis=-1, keepdims=True)
        ms_hi = jnp.sum(jnp.where(low_half, 0.0, zz), axis=-1, keepdims=True)
        ms = jnp.where(low_half, ms_lo, ms_hi) * (1.0 / NA_HEAD_DIM)
        q_ref[:, sl] = (zq[:, sl] * lax.rsqrt(ms + EPS) * qg_ref[:, sl]).astype(BF16)

    zkT = _dot_nt(wkT_ref[...], h)
    zk3 = zkT.reshape(NA_HEADS, NA_HEAD_DIM, zkT.shape[1])
    msk = jnp.mean(zk3 * zk3, axis=1, keepdims=True)
    kT = ((zk3 * lax.rsqrt(msk + EPS)).reshape(zkT.shape) * kg_ref[...]).astype(BF16)
    for j in range(kT_ref.shape[1]):
        kT_ref[0, j] = kT[:, j * V7X_LANES:(j + 1) * V7X_LANES]

    zv = _dot(h, wv_ref[...])
    low_half = lax.broadcasted_iota(jnp.int32, (zv.shape[0], V7X_LANES), 1) < NA_HEAD_DIM
    for hp in range(NA_HEADS // 2):
        pair = zv[:, hp * V7X_LANES:(hp + 1) * V7X_LANES]
        v_ref[:, (2 * hp) * V7X_LANES:(2 * hp + 1) * V7X_LANES] = jnp.where(low_half, pair, 1.0).astype(BF16)
        v_ref[:, (2 * hp + 1) * V7X_LANES:(2 * hp + 2) * V7X_LANES] = jnp.where(low_half, 1.0, pair).astype(BF16)
    fu_ref[...] = _dot(h, wf_ref[...]).astype(BF16)
    gate_dot(2)

    outs = []
    for hd in range(MEM_HEADS):
        sl = slice(hd * MEM_HEAD_DIM, (hd + 1) * MEM_HEAD_DIM)
        s = mem_scores[hd]
        e = jnp.exp(s - jnp.max(s, axis=-1, keepdims=True))
        l = jnp.sum(e, axis=-1, keepdims=True)
        outs.append(_dot(e.astype(BF16), memv_ref[0, :, sl]) / l)
    om_ref[...] = jnp.concatenate(outs, axis=1).astype(BF16)


def _in_proj(x2, norm1_g, w_in, b_gate, na_q_g, na_k_g, mem_q_g,
             memkT, memv, tokens_per_batch, tm):
    T = x2.shape[0]
    M = memv.shape[1]
    tiles_per_batch = tokens_per_batch // tm
    slabs = tm // V7X_LANES
    B = T // tokens_per_batch
    wb = w_in.astype(BF16)
    o = 0
    wq = wb[:, o:o + NA_WIDTH]; o += NA_WIDTH
    wkT = wb[:, o:o + NA_WIDTH].T; o += NA_WIDTH
    wv = wb[:, o:o + NA_WIDTH]; o += NA_WIDTH
    wf = wb[:, o:o + F_WIDTH]; o += F_WIDTH
    wm = wb[:, o:o + MEM_WIDTH]; o += MEM_WIDTH
    wg = wb[:, o:]
    na_scale = np.float32(LOG2E / np.sqrt(NA_HEAD_DIM))
    mem_scale = np.float32(1.0 / np.sqrt(MEM_HEAD_DIM))
    qg = (jnp.tile(na_q_g, NA_HEADS) * na_scale).reshape(1, NA_WIDTH)
    kg = jnp.tile(na_k_g, NA_HEADS).reshape(NA_WIDTH, 1)
    mqg = (jnp.tile(mem_q_g, MEM_HEADS) * mem_scale).reshape(1, MEM_WIDTH)
    tok = lambda w: pl.BlockSpec((tm, w), lambda i: (i, 0))
    batch_of = lambda i: i // tiles_per_batch
    return pl.pallas_call(
        _in_proj_kernel,
        grid=(T // tm,),
        in_specs=[
            tok(D_MODEL),
            _const_spec((1, D_MODEL)),
            _const_spec((D_MODEL, NA_WIDTH)),
            _const_spec((NA_WIDTH, D_MODEL)),
            _const_spec((D_MODEL, NA_WIDTH)),
            _const_spec((D_MODEL, F_WIDTH)),
            _const_spec((D_MODEL, MEM_WIDTH)),
            _const_spec((D_MODEL, N_BRANCHES * D_MODEL)),
            _const_spec((1, N_BRANCHES * D_MODEL)),
            _const_spec((1, NA_WIDTH)),
            _const_spec((NA_WIDTH, 1)),
            _const_spec((1, MEM_WIDTH)),
            pl.BlockSpec((1, MEM_WIDTH, M), lambda i: (batch_of(i), 0, 0)),
            pl.BlockSpec((1, M, MEM_WIDTH), lambda i: (batch_of(i), 0, 0)),
        ],
        out_specs=[
            tok(NA_WIDTH),
            pl.BlockSpec((1, slabs, NA_WIDTH, V7X_LANES),
                         lambda i: (batch_of(i), i % tiles_per_batch, 0, 0)),
            tok(NA_HEADS * V7X_LANES),
            tok(F_WIDTH),
            tok(MEM_WIDTH),
            tok(N_BRANCHES * D_MODEL),
        ],
        out_shape=[
            jax.ShapeDtypeStruct((T, NA_WIDTH), BF16),
            jax.ShapeDtypeStruct((B, tokens_per_batch // V7X_LANES, NA_WIDTH, V7X_LANES), BF16),
            jax.ShapeDtypeStruct((T, NA_HEADS * V7X_LANES), BF16),
            jax.ShapeDtypeStruct((T, F_WIDTH), BF16),
            jax.ShapeDtypeStruct((T, MEM_WIDTH), BF16),
            jax.ShapeDtypeStruct((T, N_BRANCHES * D_MODEL), BF16),
        ],
        compiler_params=_params(1),
        name="in_proj",
    )(x2, norm1_g.reshape(1, D_MODEL), wq, wkT, wv, wf, wm, wg,
      b_gate.reshape(1, N_BRANCHES * D_MODEL), qg, kg, mqg, memkT, memv)


def _na_bias_tables(rpb, rows):
    n_pairs = rows // 2
    wr = min(NA_WIN_ROWS, rows)
    reps = np.array([0, 1, 2, n_pairs - 2, n_pairs - 1])
    starts = np.clip(reps - 2, 0, n_pairs - NA_KEY_PAIRS)
    n_dr, n_dc = 2 * NA_WIN_ROWS - 1, 2 * NA_WIN_COLS - 1
    c = np.arange(GRID_W)[:, None]
    kc = np.arange(2 * GRID_W)[None, :] % GRID_W
    cs = np.clip(c - NA_WIN_COLS // 2, 0, GRID_W - NA_WIN_COLS)
    col_valid = (kc >= cs) & (kc < cs + NA_WIN_COLS)
    dc = np.clip(kc - c, -(NA_WIN_COLS - 1), NA_WIN_COLS - 1) + (NA_WIN_COLS - 1)
    onehot = (np.arange(n_dc)[:, None] == dc.reshape(1, -1)).astype(np.float32)
    cols = jnp.dot(rpb.astype(F32).reshape(NA_HEADS * n_dr, n_dc), onehot,
                   precision=lax.Precision.HIGHEST)
    cols = jnp.where(col_valid.reshape(1, -1), cols * np.float32(LOG2E), NEG_INF)
    cols = cols.reshape(NA_HEADS, n_dr, GRID_W, 2 * GRID_W)
    cols = jnp.concatenate([cols, jnp.full((NA_HEADS, 1, GRID_W, 2 * GRID_W), NEG_INF, F32)], axis=1)
    low = (np.arange(2 * GRID_W) < GRID_W)[None, None, None, :]
    lo = jnp.where(low, cols, 0.0)
    hi = jnp.where(low, 0.0, cols)
    r = (2 * reps[:, None] + np.arange(2)[None, :])[:, :, None]
    kr = (2 * starts[:, None] + np.arange(2 * NA_KEY_PAIRS)[None, :])[:, None, :]
    rs = np.clip(r - wr // 2, 0, rows - wr)
    row_valid = (kr >= rs) & (kr < rs + wr)
    dr = np.where(row_valid, kr - r + (NA_WIN_ROWS - 1), n_dr)
    assert dr.min() >= 0 and dr.max() <= n_dr

    def assemble(lo_ref, hi_ref, o_ref):
        for case in range(len(reps)):
            for rr in range(2):
                for j in range(NA_KEY_PAIRS):
                    tile = lo_ref[0, int(dr[case, rr, 2 * j])] + hi_ref[0, int(dr[case, rr, 2 * j + 1])]
                    o_ref[case, 0, rr * GRID_W:(rr + 1) * GRID_W,
                          j * V7X_LANES:(j + 1) * V7X_LANES] = tile

    half_spec = pl.BlockSpec((1, n_dr + 1, GRID_W, 2 * GRID_W), lambda h: (h, 0, 0, 0))
    return pl.pallas_call(
        assemble,
        grid=(NA_HEADS,),
        in_specs=[half_spec, half_spec],
        out_specs=pl.BlockSpec((len(reps), 1, NA_PAIR_TOKENS, NA_KEY_TOKENS), lambda h: (0, h, 0, 0)),
        out_shape=jax.ShapeDtypeStruct((len(reps), NA_HEADS, NA_PAIR_TOKENS, NA_KEY_TOKENS), F32),
        compiler_params=_params(1),
        name="na_bias",
    )(lo, hi)


def _na_kernel(q_ref, kT_ref, v_ref, bias_ref, o_ref, *, pairs_per_step, n_pairs):
    lane = lax.broadcasted_iota(jnp.int32, (NA_PAIR_TOKENS, 2 * NA_HEAD_DIM), 1)
    first_head = lane < NA_HEAD_DIM

    def pair_coords(i):
        p = pl.program_id(1) * pairs_per_step + i
        start = jnp.clip(p - 2, 0, n_pairs - NA_KEY_PAIRS)
        case = jnp.where(p < 2, p, jnp.where(p >= n_pairs - 2, p - (n_pairs - NA_BIAS_CASES), 2))
        qrow = pl.multiple_of(i * NA_PAIR_TOKENS, NA_PAIR_TOKENS)
        krow = pl.multiple_of(start * NA_PAIR_TOKENS, NA_PAIR_TOKENS)
        return start, case, qrow, krow

    def scores(coords, hd):
        start, case, qrow, _ = coords
        hsl = slice(hd * NA_HEAD_DIM, (hd + 1) * NA_HEAD_DIM)
        qh = q_ref[0, pl.ds(qrow, NA_PAIR_TOKENS), hsl]
        kslabs = kT_ref[0, pl.ds(start, NA_KEY_PAIRS), hsl, :]
        kT = jnp.concatenate([kslabs[j] for j in range(NA_KEY_PAIRS)], axis=1)
        return _dot(qh, kT) + bias_ref[case, hd]

    def body(it, carry):
        coords = [pair_coords(it * NA_PAIRS_PER_ITER + j) for j in range(NA_PAIRS_PER_ITER)]
        items = [(j, hd) for j in range(NA_PAIRS_PER_ITER) for hd in range(NA_HEADS)]
        pending = [scores(coords[j], hd) for j, hd in items[:NA_SCORES_AHEAD]]
        halves = []
        for n, (j, hd) in enumerate(items):
            s = pending.pop(0)
            if n + NA_SCORES_AHEAD < len(items):
                jn, hn = items[n + NA_SCORES_AHEAD]
                pending.append(scores(coords[jn], hn))
            _, _, qrow, krow = coords[j]
            vext = v_ref[0, pl.ds(krow, NA_KEY_TOKENS), hd * 2 * NA_HEAD_DIM:(hd + 1) * 2 * NA_HEAD_DIM]
            e = jnp.exp2((s - jnp.max(s, axis=-1, keepdims=True)).astype(BF16))
            r = _dot(e, vext)
            halves.append(r / pltpu.roll(r, NA_HEAD_DIM, axis=1))
            if hd == NA_HEADS - 1:
                outs = [jnp.where(first_head, halves[2 * k], halves[2 * k + 1])
                        for k in range(NA_HEADS // 2)]
                o_ref[0, pl.ds(qrow, NA_PAIR_TOKENS), :] = jnp.concatenate(outs, axis=1).astype(BF16)
                halves = []
        return carry

    lax.fori_loop(0, pairs_per_step // NA_PAIRS_PER_ITER, body, 0)


def _na_attention(q, kT4, v, bias, pairs_per_step):
    B, S, _ = q.shape
    n_pairs = S // NA_PAIR_TOKENS
    steps = n_pairs // pairs_per_step
    tq = pairs_per_step * NA_PAIR_TOKENS
    return pl.pallas_call(
        functools.partial(_na_kernel, pairs_per_step=pairs_per_step, n_pairs=n_pairs),
        grid=(B, steps),
        in_specs=[
            pl.BlockSpec((1, tq, NA_WIDTH), lambda b, j: (b, j, 0)),
            pl.BlockSpec((1, n_pairs, NA_WIDTH, V7X_LANES), lambda b, j: (b, 0, 0, 0)),
            pl.BlockSpec((1, S, NA_HEADS * V7X_LANES), lambda b, j: (b, 0, 0)),
            _const_spec((NA_BIAS_CASES, NA_HEADS, NA_PAIR_TOKENS, NA_KEY_TOKENS)),
        ],
        out_specs=pl.BlockSpec((1, tq, NA_WIDTH), lambda b, j: (b, j, 0)),
        out_shape=jax.ShapeDtypeStruct((B, S, NA_WIDTH), BF16),
        compiler_params=_params(2),
        name="na_attn",
    )(q, kT4, v, bias)


DFT_CHUNK = 16


def _dft_tables():
    n, m = GRID_W, DFT_CHUNK
    a = np.arange(n)
    ang1 = 2.0 * np.pi * np.outer(a, a) / n
    w_ri = np.stack([np.cos(ang1), -np.sin(ang1)], axis=1) / np.sqrt(n)
    eye = np.eye(m)
    w_rows = np.einsum('cra,xy->crxay', w_ri, eye).reshape(n * 2 * m, n * m)
    perm = np.einsum('dx,cy->dcyx', eye, eye).reshape(m * m, m * m)
    c = np.arange(n)[:, None, None]
    d = np.arange(n)[None, :, None]
    b = np.arange(n)[None, None, :]
    ang2 = 2.0 * np.pi * ((c + n * d) * b % (n * n)) / (n * n)
    gr, gi = np.cos(ang2) / np.sqrt(n), -np.sin(ang2) / np.sqrt(n)
    w_cols = np.concatenate([np.concatenate([gr, -gi], axis=2),
                             np.concatenate([gi, gr], axis=2)], axis=1)
    ch = np.arange(F_GROUP_DIM)
    ang3 = 2.0 * np.pi * np.outer(ch, ch) / F_GROUP_DIM
    w_chan = np.concatenate([np.cos(ang3), np.sin(ang3)], axis=0) / np.sqrt(F_GROUP_DIM)
    return tuple(jnp.asarray(t, dtype=F32).astype(BF16) for t in (w_rows, w_cols, perm, w_chan))


def _dft_rows_kernel(w_ref, u_ref, z_ref):
    n, m = GRID_W, DFT_CHUNK
    x = u_ref[0].reshape(n * m, F_WIDTH)
    rows_per_dot = 2 * m * 8
    for i in range(n // 8):
        z = _dot(w_ref[i * rows_per_dot:(i + 1) * rows_per_dot, :], x)
        z_ref[0, i * 8:(i + 1) * 8] = z.astype(BF16).reshape(8, 2, m, F_WIDTH)


def _dft_rows(u4, w_rows):
    B, n = u4.shape[0], GRID_W
    m = DFT_CHUNK
    return pl.pallas_call(
        _dft_rows_kernel,
        grid=(B, n // m),
        in_specs=[
            _const_spec((n * 2 * m, n * m)),
            pl.BlockSpec((1, n, m, F_WIDTH), lambda b, j: (b, 0, j, 0)),
        ],
        out_specs=pl.BlockSpec((1, n, 2, m, F_WIDTH), lambda b, j: (b, 0, 0, j, 0)),
        out_shape=jax.ShapeDtypeStruct((B, n, 2, n, F_WIDTH), BF16),
        compiler_params=_params(2),
        name="dft_rows",
    )(w_rows, u4)


def _dft_cols_kernel(w_ref, perm_ref, z_ref, y_ref):
    n, m = GRID_W, DFT_CHUNK
    res = []
    for j in range(m):
        c = pl.program_id(1) * m + j
        data = z_ref[0, j].reshape(2 * n, F_WIDTH)
        res.append(_dot(w_ref[c], data).astype(BF16))
    for ri in range(2):
        for g in range(n // m):
            lo = ri * n + g * m
            stacked = jnp.concatenate([r[lo:lo + m] for r in res], axis=0)
            y = _dot(perm_ref[...], stacked).astype(BF16)
            y_ref[ri, 0, g * m:(g + 1) * m] = y.reshape(m, m, F_WIDTH)


def _dft_cols(z5, w_cols, perm):
    B, n, m = z5.shape[0], GRID_W, DFT_CHUNK
    return pl.pallas_call(
        _dft_cols_kernel,
        grid=(B, n // m),
        in_specs=[
            _const_spec((n, 2 * n, 2 * n)),
            _const_spec((m * m, m * m)),
            pl.BlockSpec((1, m, 2, n, F_WIDTH), lambda b, j: (b, j, 0, 0, 0)),
        ],
        out_specs=pl.BlockSpec((2, 1, n, m, F_WIDTH), lambda b, j: (0, b, 0, j, 0)),
        out_shape=jax.ShapeDtypeStruct((2, B, n, n, F_WIDTH), BF16),
        compiler_params=_params(2),
        name="dft_cols",
    )(w_cols, perm, z5)


def _out_ffn_kernel(x_ref, ona_ref, y_ref, om_ref, gate_ref, wch_ref, wna_ref, wf_ref, wmo_ref,
                    wout_ref, g2_ref, w1_ref, w2_ref, o_ref, *, ff_chunk):
    yr, yi = y_ref[0], y_ref[1]
    groups = []
    for g in range(F_GROUPS):
        sl = slice(g * F_GROUP_DIM, (g + 1) * F_GROUP_DIM)
        groups.append(_dot(jnp.concatenate([yr[:, sl], yi[:, sl]], axis=1), wch_ref[...]))
    yf = jnp.concatenate(groups, axis=1).astype(BF16)

    gate = lambda br: gate_ref[:, br * D_MODEL:(br + 1) * D_MODEL].astype(F32)
    merged = gate(0) * _dot(ona_ref[...], wna_ref[...])
    merged = merged + gate(1) * _dot(yf, wf_ref[...])
    merged = merged + gate(2) * _dot(om_ref[...], wmo_ref[...])
    x1 = x_ref[...] + _dot(merged.astype(BF16), wout_ref[...])

    ms = jnp.mean(x1 * x1, axis=-1, keepdims=True)
    h2 = (x1 * lax.rsqrt(ms + EPS) * g2_ref[...]).astype(BF16)
    acc = x1
    for j in range(D_FF // ff_chunk):
        sl = slice(j * ff_chunk, (j + 1) * ff_chunk)
        a = jnp.maximum(_dot(h2, w1_ref[:, sl]), 0.0)
        acc = acc + _dot((a * a).astype(BF16), w2_ref[sl, :])
    o_ref[...] = acc


def _out_ffn(x2, ona, y3, om, gates, w_chan, w_na_o, w_f, w_mem_o, w_out, norm2_g, w_ff1, w_ff2,
             tm, ff_chunk):
    T = x2.shape[0]
    tok = lambda w: pl.BlockSpec((tm, w), lambda i: (i, 0))
    return pl.pallas_call(
        functools.partial(_out_ffn_kernel, ff_chunk=ff_chunk),
        grid=(T // tm,),
        in_specs=[
            tok(D_MODEL),
            tok(NA_WIDTH),
            pl.BlockSpec((2, tm, F_WIDTH), lambda i: (0, i, 0)),
            tok(MEM_WIDTH),
            tok(N_BRANCHES * D_MODEL),
            _const_spec((2 * F_GROUP_DIM, F_GROUP_DIM)),
            _const_spec((NA_WIDTH, D_MODEL)),
            _const_spec((F_WIDTH, D_MODEL)),
            _const_spec((MEM_WIDTH, D_MODEL)),
            _const_spec((D_MODEL, D_MODEL)),
            _const_spec((1, D_MODEL)),
            _const_spec((D_MODEL, D_FF)),
            _const_spec((D_FF, D_MODEL)),
        ],
        out_specs=tok(D_MODEL),
        out_shape=jax.ShapeDtypeStruct((T, D_MODEL), F32),
        compiler_params=_params(1),
        name="out_ffn",
    )(x2, ona, y3, om, gates, w_chan, w_na_o.astype(BF16), w_f.astype(BF16), w_mem_o.astype(BF16),
      w_out.astype(BF16), norm2_g.reshape(1, D_MODEL), w_ff1.astype(BF16), w_ff2.astype(BF16))


def kernel(x, mem, norm1_g, w_in, b_gate, na_q_g, na_k_g, na_rpb, w_na_o, w_f, mem_norm_g,
           w_mem_kv, mem_q_g, mem_k_g, w_mem_o, w_out, norm2_g, w_ff1, w_ff2):
    B, S, _ = x.shape
    T = B * S
    rows = S // GRID_W
    assert rows == GRID_W and S % NA_PAIR_TOKENS == 0
    x2 = x.reshape(T, D_MODEL)
    gmean_mem = _group_mean_matrix(MEM_WIDTH, MEM_HEAD_DIM)
    w_rows, w_cols, perm, w_chan = _dft_tables()

    memkT, memv = _mem_kv(mem, mem_norm_g, w_mem_kv, gmean_mem, mem_k_g)
    q, kT4, v, fu, om, gates = _in_proj(x2, norm1_g, w_in, b_gate, na_q_g, na_k_g, mem_q_g,
                                        memkT, memv, S, tm=512)

    bias = _na_bias_tables(na_rpb, rows)
    ona = _na_attention(q.reshape(B, S, NA_WIDTH), kT4, v.reshape(B, S, NA_HEADS * V7X_LANES), bias,
                        pairs_per_step=4)

    z = _dft_rows(fu.reshape(B, GRID_W, GRID_W, F_WIDTH), w_rows)
    y = _dft_cols(z, w_cols, perm)
    y3 = y.reshape(2, T, F_WIDTH)

    out = _out_ffn(x2, ona.reshape(T, NA_WIDTH), y3, om, gates, w_chan, w_na_o, w_f, w_mem_o,
                   w_out, norm2_g, w_ff1, w_ff2, tm=512, ff_chunk=1024)
    return out.reshape(B, S, D_MODEL)
```

```python
import functools

import numpy as np
import jax
import jax.numpy as jnp
from jax import lax
from jax.experimental import pallas as pl
from jax.experimental.pallas import tpu as pltpu

D_MODEL = 1024
GRID_W = 64
NA_HEADS = 8
NA_HEAD_DIM = 64
NA_WIDTH = NA_HEADS * NA_HEAD_DIM
NA_WIN_ROWS = 8
NA_WIN_COLS = 16
F_GROUPS = 4
F_GROUP_DIM = 128
F_WIDTH = F_GROUPS * F_GROUP_DIM
MEM_HEADS = 4
MEM_HEAD_DIM = 128
MEM_WIDTH = MEM_HEADS * MEM_HEAD_DIM
N_BRANCHES = 3
D_FF = 4 * D_MODEL
EPS = 1e-6
NEG_INF = -1e30

V7X_LANES = 128
V7X_VMEM_LIMIT_BYTES = 60 * 1024 * 1024

NA_PAIR_TOKENS = 2 * GRID_W
NA_KEY_PAIRS = 5
NA_KEY_TOKENS = NA_KEY_PAIRS * NA_PAIR_TOKENS
NA_BIAS_CASES = 5
NA_PAIRS_PER_ITER = 2
NA_SCORES_AHEAD = 2

BF16 = jnp.bfloat16
F32 = jnp.float32


def _dot(a, b):
    return jnp.dot(a, b, preferred_element_type=F32)


def _dot_nt(a, b):
    return lax.dot_general(a, b, (((1,), (1,)), ((), ())), preferred_element_type=F32)


def _const_spec(shape):
    return pl.BlockSpec(shape, lambda *_: (0,) * len(shape), pipeline_mode=pl.Buffered(1))


def _params(n_axes):
    return pltpu.CompilerParams(
        dimension_semantics=("arbitrary",) * n_axes,
        vmem_limit_bytes=V7X_VMEM_LIMIT_BYTES,
    )


def _group_mean_matrix(width, group):
    idx = np.arange(width) // group
    return jnp.asarray((idx[:, None] == idx[None, :]).astype(np.float32) / group, dtype=BF16)


def _mem_kv_kernel(mem_ref, g_ref, w_ref, gm_ref, kg_ref, kT_ref, v_ref):
    m = mem_ref[0]
    ms = jnp.mean(m * m, axis=-1, keepdims=True)
    mn = (m * lax.rsqrt(ms + EPS) * g_ref[...]).astype(BF16)
    kv = _dot(mn, w_ref[...])
    k = kv[:, :MEM_WIDTH]
    msk = _dot((k * k).astype(BF16), gm_ref[...])
    kn = k * lax.rsqrt(msk + EPS) * kg_ref[...]
    kT_ref[0] = kn.T.astype(BF16)
    v_ref[0] = kv[:, MEM_WIDTH:].astype(BF16)


def _mem_kv(mem, mem_norm_g, w_mem_kv, gmean_mem, mem_k_g):
    B, M, _ = mem.shape
    return pl.pallas_call(
        _mem_kv_kernel,
        grid=(B,),
        in_specs=[
            pl.BlockSpec((1, M, D_MODEL), lambda b: (b, 0, 0)),
            _const_spec((1, D_MODEL)),
            _const_spec((D_MODEL, 2 * MEM_WIDTH)),
            _const_spec((MEM_WIDTH, MEM_WIDTH)),
            _const_spec((1, MEM_WIDTH)),
        ],
        out_specs=[
            pl.BlockSpec((1, MEM_WIDTH, M), lambda b: (b, 0, 0)),
            pl.BlockSpec((1, M, MEM_WIDTH), lambda b: (b, 0, 0)),
        ],
        out_shape=[
            jax.ShapeDtypeStruct((B, MEM_WIDTH, M), BF16),
            jax.ShapeDtypeStruct((B, M, MEM_WIDTH), BF16),
        ],
        compiler_params=_params(1),
        name="mem_kv",
    )(mem, mem_norm_g.reshape(1, D_MODEL), w_mem_kv.astype(BF16), gmean_mem,
      jnp.tile(mem_k_g, MEM_HEADS).reshape(1, MEM_WIDTH))


def _in_proj_kernel(x_ref, g1_ref, wq_ref, wkT_ref, wv_ref, wf_ref, wm_ref, wg_ref, bg_ref,
                    qg_ref, kg_ref, mqg_ref, memkT_ref, memv_ref, *rest, n_narrow):
    wide_refs = rest[:n_narrow]
    q_ref, kT_ref, v_ref, fu_ref, om_ref, gate_ref = rest[n_narrow:n_narrow + 6]
    for wide_ref, narrow_ref in zip(wide_refs, rest[n_narrow + 6:]):
        narrow_ref[...] = wide_ref[...].astype(BF16)

    x = x_ref[...]
    ms = jnp.mean(x * x, axis=-1, keepdims=True)
    h = (x * lax.rsqrt(ms + EPS) * g1_ref[...]).astype(BF16)

    def gate_dot(br):
        sl = slice(br * D_MODEL, (br + 1) * D_MODEL)
        zg = _dot(h, wg_ref[:, sl]) + bg_ref[:, sl]
        gate_ref[:, sl] = jax.nn.sigmoid(zg).astype(BF16)

    zm = _dot(h, wm_ref[...])
    mem_scores = []
    for hd in range(MEM_HEADS):
        sl = slice(hd * MEM_HEAD_DIM, (hd + 1) * MEM_HEAD_DIM)
        zh = zm[:, sl]
        msm = jnp.mean(zh * zh, axis=-1, keepdims=True)
        qm = (zh * lax.rsqrt(msm + EPS) * mqg_ref[:, sl]).astype(BF16)
        if hd == 0:
            gate_dot(0)
        mem_scores.append(_dot(qm, memkT_ref[0, sl, :]))
    gate_dot(1)

    zq = _dot(h, wq_ref[...])
    low_half = lax.broadcasted_iota(jnp.int32, (zq.shape[0], V7X_LANES), 1) < NA_HEAD_DIM
    for hp in range(NA_HEADS // 2):
        sl = slice(hp * V7X_LANES, (hp + 1) * V7X_LANES)
        zz = zq[:, sl] * zq[:, sl]
        ms_lo = jnp.sum(jnp.where(low_half, zz, 0.0), axis=-1, keepdims=True)
        ms_hi = jnp.sum(jnp.where(low_half, 0.0, zz), axis=-1, keepdims=True)
        ms = jnp.where(low_half, ms_lo, ms_hi) * (1.0 / NA_HEAD_DIM)
        q_ref[:, sl] = (zq[:, sl] * lax.rsqrt(ms + EPS) * qg_ref[:, sl]).astype(BF16)

    zkT = _dot_nt(wkT_ref[...], h)
    zk3 = zkT.reshape(NA_HEADS, NA_HEAD_DIM, zkT.shape[1])
    msk = jnp.mean(zk3 * zk3, axis=1, keepdims=True)
    kT = ((zk3 * lax.rsqrt(msk + EPS)).reshape(zkT.shape) * kg_ref[...]).astype(BF16)
    for j in range(kT_ref.shape[1]):
        kT_ref[0, j] = kT[:, j * V7X_LANES:(j + 1) * V7X_LANES]

    zv = _dot(h, wv_ref[...])
    low_half = lax.broadcasted_iota(jnp.int32, (zv.shape[0], V7X_LANES), 1) < NA_HEAD_DIM
    for hp in range(NA_HEADS // 2):
        pair = zv[:, hp * V7X_LANES:(hp + 1) * V7X_LANES]
        v_ref[:, (2 * hp) * V7X_LANES:(2 * hp + 1) * V7X_LANES] = jnp.where(low_half, pair, 1.0).astype(BF16)
        v_ref[:, (2 * hp + 1) * V7X_LANES:(2 * hp + 2) * V7X_LANES] = jnp.where(low_half, 1.0, pair).astype(BF16)
    fu_ref[...] = _dot(h, wf_ref[...]).astype(BF16)
    gate_dot(2)

    outs = []
    for hd in range(MEM_HEADS):
        sl = slice(hd * MEM_HEAD_DIM, (hd + 1) * MEM_HEAD_DIM)
        s = mem_scores[hd]
        e = jnp.exp(s - jnp.max(s, axis=-1, keepdims=True))
        l = jnp.sum(e, axis=-1, keepdims=True)
        outs.append(_dot(e.astype(BF16), memv_ref[0, :, sl]) / l)
    om_ref[...] = jnp.concatenate(outs, axis=1).astype(BF16)


def _in_proj(x2, norm1_g, w_in, b_gate, na_q_g, na_k_g, mem_q_g,
             memkT, memv, later_weights, tokens_per_batch, tm):
    T = x2.shape[0]
    M = memv.shape[1]
    tiles_per_batch = tokens_per_batch // tm
    slabs = tm // V7X_LANES
    B = T // tokens_per_batch
    wb = w_in.astype(BF16)
    o = 0
    wq = wb[:, o:o + NA_WIDTH]; o += NA_WIDTH
    wkT = wb[:, o:o + NA_WIDTH].T; o += NA_WIDTH
    wv = wb[:, o:o + NA_WIDTH]; o += NA_WIDTH
    wf = wb[:, o:o + F_WIDTH]; o += F_WIDTH
    wm = wb[:, o:o + MEM_WIDTH]; o += MEM_WIDTH
    wg = wb[:, o:]
    na_scale = np.float32(1.0 / np.sqrt(NA_HEAD_DIM))
    mem_scale = np.float32(1.0 / np.sqrt(MEM_HEAD_DIM))
    qg = (jnp.tile(na_q_g, NA_HEADS) * na_scale).reshape(1, NA_WIDTH)
    kg = jnp.tile(na_k_g, NA_HEADS).reshape(NA_WIDTH, 1)
    mqg = (jnp.tile(mem_q_g, MEM_HEADS) * mem_scale).reshape(1, MEM_WIDTH)
    tok = lambda w: pl.BlockSpec((tm, w), lambda i: (i, 0))
    batch_of = lambda i: i // tiles_per_batch
    steps = T // tm
    slab_specs = [pl.BlockSpec((w.shape[0] // steps, w.shape[1]), lambda i: (i, 0))
                  for w in later_weights]
    return pl.pallas_call(
        functools.partial(_in_proj_kernel, n_narrow=len(later_weights)),
        grid=(steps,),
        in_specs=[
            tok(D_MODEL),
            _const_spec((1, D_MODEL)),
            _const_spec((D_MODEL, NA_WIDTH)),
            _const_spec((NA_WIDTH, D_MODEL)),
            _const_spec((D_MODEL, NA_WIDTH)),
            _const_spec((D_MODEL, F_WIDTH)),
            _const_spec((D_MODEL, MEM_WIDTH)),
            _const_spec((D_MODEL, N_BRANCHES * D_MODEL)),
            _const_spec((1, N_BRANCHES * D_MODEL)),
            _const_spec((1, NA_WIDTH)),
            _const_spec((NA_WIDTH, 1)),
            _const_spec((1, MEM_WIDTH)),
            pl.BlockSpec((1, MEM_WIDTH, M), lambda i: (batch_of(i), 0, 0)),
            pl.BlockSpec((1, M, MEM_WIDTH), lambda i: (batch_of(i), 0, 0)),
        ] + slab_specs,
        out_specs=[
            tok(NA_WIDTH),
            pl.BlockSpec((1, slabs, NA_WIDTH, V7X_LANES),
                         lambda i: (batch_of(i), i % tiles_per_batch, 0, 0)),
            tok(NA_HEADS * V7X_LANES),
            tok(F_WIDTH),
            tok(MEM_WIDTH),
            tok(N_BRANCHES * D_MODEL),
        ] + slab_specs,
        out_shape=[
            jax.ShapeDtypeStruct((T, NA_WIDTH), BF16),
            jax.ShapeDtypeStruct((B, tokens_per_batch // V7X_LANES, NA_WIDTH, V7X_LANES), BF16),
            jax.ShapeDtypeStruct((T, NA_HEADS * V7X_LANES), BF16),
            jax.ShapeDtypeStruct((T, F_WIDTH), BF16),
            jax.ShapeDtypeStruct((T, MEM_WIDTH), BF16),
            jax.ShapeDtypeStruct((T, N_BRANCHES * D_MODEL), BF16),
        ] + [jax.ShapeDtypeStruct(w.shape, BF16) for w in later_weights],
        compiler_params=_params(1),
        name="in_proj",
    )(x2, norm1_g.reshape(1, D_MODEL), wq, wkT, wv, wf, wm, wg,
      b_gate.reshape(1, N_BRANCHES * D_MODEL), qg, kg, mqg, memkT, memv, *later_weights)


def _na_bias_tables(rpb, rows):
    n_pairs = rows // 2
    wr = min(NA_WIN_ROWS, rows)
    reps = np.array([0, 1, 2, n_pairs - 2, n_pairs - 1])
    starts = np.clip(reps - 2, 0, n_pairs - NA_KEY_PAIRS)
    n_dr, n_dc = 2 * NA_WIN_ROWS - 1, 2 * NA_WIN_COLS - 1
    c = np.arange(GRID_W)[:, None]
    kc = np.arange(2 * GRID_W)[None, :] % GRID_W
    cs = np.clip(c - NA_WIN_COLS // 2, 0, GRID_W - NA_WIN_COLS)
    col_valid = (kc >= cs) & (kc < cs + NA_WIN_COLS)
    dc = np.clip(kc - c, -(NA_WIN_COLS - 1), NA_WIN_COLS - 1) + (NA_WIN_COLS - 1)
    onehot = (np.arange(n_dc)[:, None] == dc.reshape(1, -1)).astype(np.float32)
    cols = jnp.dot(rpb.astype(F32).reshape(NA_HEADS * n_dr, n_dc), onehot,
                   precision=lax.Precision.HIGHEST)
    cols = jnp.where(col_valid.reshape(1, -1), cols, NEG_INF)
    cols = cols.reshape(NA_HEADS, n_dr, GRID_W, 2 * GRID_W)
    cols = jnp.concatenate([cols, jnp.full((NA_HEADS, 1, GRID_W, 2 * GRID_W), NEG_INF, F32)], axis=1)
    low = (np.arange(2 * GRID_W) < GRID_W)[None, None, None, :]
    lo = jnp.where(low, cols, 0.0)
    hi = jnp.where(low, 0.0, cols)
    r = (2 * reps[:, None] + np.arange(2)[None, :])[:, :, None]
    kr = (2 * starts[:, None] + np.arange(2 * NA_KEY_PAIRS)[None, :])[:, None, :]
    rs = np.clip(r - wr // 2, 0, rows - wr)
    row_valid = (kr >= rs) & (kr < rs + wr)
    dr = np.where(row_valid, kr - r + (NA_WIN_ROWS - 1), n_dr)
    assert dr.min() >= 0 and dr.max() <= n_dr

    def assemble(lo_ref, hi_ref, o_ref):
        for case in range(len(reps)):
            for rr in range(2):
                for j in range(NA_KEY_PAIRS):
                    tile = lo_ref[0, int(dr[case, rr, 2 * j])] + hi_ref[0, int(dr[case, rr, 2 * j + 1])]
                    o_ref[case, 0, rr * GRID_W:(rr + 1) * GRID_W,
                          j * V7X_LANES:(j + 1) * V7X_LANES] = tile

    half_spec = pl.BlockSpec((1, n_dr + 1, GRID_W, 2 * GRID_W), lambda h: (h, 0, 0, 0))
    return pl.pallas_call(
        assemble,
        grid=(NA_HEADS,),
        in_specs=[half_spec, half_spec],
        out_specs=pl.BlockSpec((len(reps), 1, NA_PAIR_TOKENS, NA_KEY_TOKENS), lambda h: (0, h, 0, 0)),
        out_shape=jax.ShapeDtypeStruct((len(reps), NA_HEADS, NA_PAIR_TOKENS, NA_KEY_TOKENS), F32),
        compiler_params=_params(1),
        name="na_bias",
    )(lo, hi)


def _na_kernel(q_ref, kT_ref, v_ref, bias_ref, o_ref, *, pairs_per_step, n_pairs):
    lane = lax.broadcasted_iota(jnp.int32, (NA_PAIR_TOKENS, 2 * NA_HEAD_DIM), 1)
    first_head = lane < NA_HEAD_DIM

    def pair_coords(i):
        p = pl.program_id(1) * pairs_per_step + i
        start = jnp.clip(p - 2, 0, n_pairs - NA_KEY_PAIRS)
        case = jnp.where(p < 2, p, jnp.where(p >= n_pairs - 2, p - (n_pairs - NA_BIAS_CASES), 2))
        qrow = pl.multiple_of(i * NA_PAIR_TOKENS, NA_PAIR_TOKENS)
        krow = pl.multiple_of(start * NA_PAIR_TOKENS, NA_PAIR_TOKENS)
        return start, case, qrow, krow

    def scores(coords, hd):
        start, case, qrow, _ = coords
        hsl = slice(hd * NA_HEAD_DIM, (hd + 1) * NA_HEAD_DIM)
        qh = q_ref[0, pl.ds(qrow, NA_PAIR_TOKENS), hsl]
        kslabs = kT_ref[0, pl.ds(start, NA_KEY_PAIRS), hsl, :]
        kT = jnp.concatenate([kslabs[j] for j in range(NA_KEY_PAIRS)], axis=1)
        return _dot(qh, kT) + bias_ref[case, hd]

    def body(it, carry):
        coords = [pair_coords(it * NA_PAIRS_PER_ITER + j) for j in range(NA_PAIRS_PER_ITER)]
        items = [(j, hd) for j in range(NA_PAIRS_PER_ITER) for hd in range(NA_HEADS)]
        pending = [scores(coords[j], hd) for j, hd in items[:NA_SCORES_AHEAD]]
        halves = []
        for n, (j, hd) in enumerate(items):
            s = pending.pop(0)
            if n + NA_SCORES_AHEAD < len(items):
                jn, hn = items[n + NA_SCORES_AHEAD]
                pending.append(scores(coords[jn], hn))
            _, _, qrow, krow = coords[j]
            vext = v_ref[0, pl.ds(krow, NA_KEY_TOKENS), hd * 2 * NA_HEAD_DIM:(hd + 1) * 2 * NA_HEAD_DIM]
            e = jnp.exp((s - jnp.max(s, axis=-1, keepdims=True)).astype(BF16))
            r = _dot(e, vext)
            halves.append(r / pltpu.roll(r, NA_HEAD_DIM, axis=1))
            if hd == NA_HEADS - 1:
                outs = [jnp.where(first_head, halves[2 * k], halves[2 * k + 1])
                        for k in range(NA_HEADS // 2)]
                o_ref[0, pl.ds(qrow, NA_PAIR_TOKENS), :] = jnp.concatenate(outs, axis=1).astype(BF16)
                halves = []
        return carry

    lax.fori_loop(0, pairs_per_step // NA_PAIRS_PER_ITER, body, 0)


def _na_attention(q, kT4, v, bias, pairs_per_step):
    B, S, _ = q.shape
    n_pairs = S // NA_PAIR_TOKENS
    steps = n_pairs // pairs_per_step
    tq = pairs_per_step * NA_PAIR_TOKENS
    return pl.pallas_call(
        functools.partial(_na_kernel, pairs_per_step=pairs_per_step, n_pairs=n_pairs),
        grid=(B, steps),
        in_specs=[
            pl.BlockSpec((1, tq, NA_WIDTH), lambda b, j: (b, j, 0)),
            pl.BlockSpec((1, n_pairs, NA_WIDTH, V7X_LANES), lambda b, j: (b, 0, 0, 0)),
            pl.BlockSpec((1, S, NA_HEADS * V7X_LANES), lambda b, j: (b, 0, 0)),
            _const_spec((NA_BIAS_CASES, NA_HEADS, NA_PAIR_TOKENS, NA_KEY_TOKENS)),
        ],
        out_specs=pl.BlockSpec((1, tq, NA_WIDTH), lambda b, j: (b, j, 0)),
        out_shape=jax.ShapeDtypeStruct((B, S, NA_WIDTH), BF16),
        compiler_params=_params(2),
        name="na_attn",
    )(q, kT4, v, bias)


DFT_CHUNK = 16


def _dft_tables():
    n, m = GRID_W, DFT_CHUNK
    a = np.arange(n)
    ang1 = 2.0 * np.pi * np.outer(a, a) / n
    w_ri = np.stack([np.cos(ang1), -np.sin(ang1)], axis=1) / np.sqrt(n)
    eye = np.eye(m)
    w_rows = np.einsum('cra,xy->crxay', w_ri, eye).reshape(n * 2 * m, n * m)
    perm = np.einsum('dx,cy->dcyx', eye, eye).reshape(m * m, m * m)
    c = np.arange(n)[:, None, None]
    d = np.arange(n)[None, :, None]
    b = np.arange(n)[None, None, :]
    ang2 = 2.0 * np.pi * ((c + n * d) * b % (n * n)) / (n * n)
    gr, gi = np.cos(ang2) / np.sqrt(n), -np.sin(ang2) / np.sqrt(n)
    w_cols = np.concatenate([np.concatenate([gr, -gi], axis=2),
                             np.concatenate([gi, gr], axis=2)], axis=1)
    ch = np.arange(F_GROUP_DIM)
    ang3 = 2.0 * np.pi * np.outer(ch, ch) / F_GROUP_DIM
    w_chan = np.concatenate([np.cos(ang3), np.sin(ang3)], axis=0) / np.sqrt(F_GROUP_DIM)
    return tuple(jnp.asarray(t, dtype=F32).astype(BF16) for t in (w_rows, w_cols, perm, w_chan))


def _dft_rows_kernel(w_ref, u_ref, z_ref):
    n, m = GRID_W, DFT_CHUNK
    x = u_ref[0].reshape(n * m, F_WIDTH)
    rows_per_dot = 2 * m * 8
    for i in range(n // 8):
        z = _dot(w_ref[i * rows_per_dot:(i + 1) * rows_per_dot, :], x)
        z_ref[0, i * 8:(i + 1) * 8] = z.astype(BF16).reshape(8, 2, m, F_WIDTH)


def _dft_rows(u4, w_rows):
    B, n = u4.shape[0], GRID_W
    m = DFT_CHUNK
    return pl.pallas_call(
        _dft_rows_kernel,
        grid=(B, n // m),
        in_specs=[
            _const_spec((n * 2 * m, n * m)),
            pl.BlockSpec((1, n, m, F_WIDTH), lambda b, j: (b, 0, j, 0)),
        ],
        out_specs=pl.BlockSpec((1, n, 2, m, F_WIDTH), lambda b, j: (b, 0, 0, j, 0)),
        out_shape=jax.ShapeDtypeStruct((B, n, 2, n, F_WIDTH), BF16),
        compiler_params=_params(2),
        name="dft_rows",
    )(w_rows, u4)


def _dft_cols_kernel(w_ref, perm_ref, z_ref, y_ref):
    n, m = GRID_W, DFT_CHUNK
    res = []
    for j in range(m):
        c = pl.program_id(1) * m + j
        data = z_ref[0, j].reshape(2 * n, F_WIDTH)
        res.append(_dot(w_ref[c], data).astype(BF16))
    for ri in range(2):
        for g in range(n // m):
            lo = ri * n + g * m
            stacked = jnp.concatenate([r[lo:lo + m] for r in res], axis=0)
            y = _dot(perm_ref[...], stacked).astype(BF16)
            y_ref[ri, 0, g * m:(g + 1) * m] = y.reshape(m, m, F_WIDTH)


def _dft_cols(z5, w_cols, perm):
    B, n, m = z5.shape[0], GRID_W, DFT_CHUNK
    return pl.pallas_call(
        _dft_cols_kernel,
        grid=(B, n // m),
        in_specs=[
            _const_spec((n, 2 * n, 2 * n)),
            _const_spec((m * m, m * m)),
            pl.BlockSpec((1, m, 2, n, F_WIDTH), lambda b, j: (b, j, 0, 0, 0)),
        ],
        out_specs=pl.BlockSpec((2, 1, n, m, F_WIDTH), lambda b, j: (0, b, 0, j, 0)),
        out_shape=jax.ShapeDtypeStruct((2, B, n, n, F_WIDTH), BF16),
        compiler_params=_params(2),
        name="dft_cols",
    )(w_cols, perm, z5)


def _out_ffn_kernel(x_ref, ona_ref, y_ref, om_ref, gate_ref, wch_ref, wna_ref, wf_ref, wmo_ref,
                    wout_ref, g2_ref, w1_ref, w2_ref, o_ref, *, ff_chunk):
    yr, yi = y_ref[0], y_ref[1]
    groups = []
    for g in range(F_GROUPS):
        sl = slice(g * F_GROUP_DIM, (g + 1) * F_GROUP_DIM)
        groups.append(_dot(jnp.concatenate([yr[:, sl], yi[:, sl]], axis=1), wch_ref[...]))
    yf = jnp.concatenate(groups, axis=1).astype(BF16)

    gate = lambda br: gate_ref[:, br * D_MODEL:(br + 1) * D_MODEL].astype(F32)
    merged = gate(0) * _dot(ona_ref[...], wna_ref[...])
    merged = merged + gate(1) * _dot(yf, wf_ref[...])
    merged = merged + gate(2) * _dot(om_ref[...], wmo_ref[...])
    x1 = x_ref[...] + _dot(merged.astype(BF16), wout_ref[...])

    ms = jnp.mean(x1 * x1, axis=-1, keepdims=True)
    h2 = (x1 * lax.rsqrt(ms + EPS) * g2_ref[...]).astype(BF16)
    acc = x1
    for j in range(D_FF // ff_chunk):
        sl = slice(j * ff_chunk, (j + 1) * ff_chunk)
        a = jnp.maximum(_dot(h2, w1_ref[:, sl]), 0.0)
        acc = acc + _dot((a * a).astype(BF16), w2_ref[sl, :])
    o_ref[...] = acc


def _out_ffn(x2, ona, y3, om, gates, w_chan, w_na_o, w_f, w_mem_o, w_out, norm2_g, w_ff1, w_ff2,
             tm, ff_chunk):
    T = x2.shape[0]
    tok = lambda w: pl.BlockSpec((tm, w), lambda i: (i, 0))
    return pl.pallas_call(
        functools.partial(_out_ffn_kernel, ff_chunk=ff_chunk),
        grid=(T // tm,),
        in_specs=[
            tok(D_MODEL),
            tok(NA_WIDTH),
            pl.BlockSpec((2, tm, F_WIDTH), lambda i: (0, i, 0)),
            tok(MEM_WIDTH),
            tok(N_BRANCHES * D_MODEL),
            _const_spec((2 * F_GROUP_DIM, F_GROUP_DIM)),
            _const_spec((NA_WIDTH, D_MODEL)),
            _const_spec((F_WIDTH, D_MODEL)),
            _const_spec((MEM_WIDTH, D_MODEL)),
            _const_spec((D_MODEL, D_MODEL)),
            _const_spec((1, D_MODEL)),
            _const_spec((D_MODEL, D_FF)),
            _const_spec((D_FF, D_MODEL)),
        ],
        out_specs=tok(D_MODEL),
        out_shape=jax.ShapeDtypeStruct((T, D_MODEL), F32),
        compiler_params=_params(1),
        name="out_ffn",
    )(x2, ona, y3, om, gates, w_chan, w_na_o, w_f, w_mem_o, w_out, norm2_g.reshape(1, D_MODEL),
      w_ff1, w_ff2)


def kernel(x, mem, norm1_g, w_in, b_gate, na_q_g, na_k_g, na_rpb, w_na_o, w_f, mem_norm_g,
           w_mem_kv, mem_q_g, mem_k_g, w_mem_o, w_out, norm2_g, w_ff1, w_ff2):
    B, S, _ = x.shape
    T = B * S
    rows = S // GRID_W
    assert rows == GRID_W and S % NA_PAIR_TOKENS == 0
    x2 = x.reshape(T, D_MODEL)
    gmean_mem = _group_mean_matrix(MEM_WIDTH, MEM_HEAD_DIM)
    w_rows, w_cols, perm, w_chan = _dft_tables()

    memkT, memv = _mem_kv(mem, mem_norm_g, w_mem_kv, gmean_mem, mem_k_g)
    later_weights = (w_na_o, w_f, w_mem_o, w_out, w_ff1, w_ff2)
    q, kT4, v, fu, om, gates, *narrow = _in_proj(x2, norm1_g, w_in, b_gate, na_q_g, na_k_g, mem_q_g,
                                                 memkT, memv, later_weights, S, tm=512)
    w_na_o, w_f, w_mem_o, w_out, w_ff1, w_ff2 = narrow

    bias = _na_bias_tables(na_rpb, rows)
    ona = _na_attention(q.reshape(B, S, NA_WIDTH), kT4, v.reshape(B, S, NA_HEADS * V7X_LANES), bias,
                        pairs_per_step=8)

    z = _dft_rows(fu.reshape(B, GRID_W, GRID_W, F_WIDTH), w_rows)
    y = _dft_cols(z, w_cols, perm)
    y3 = y.reshape(2, T, F_WIDTH)

    out = _out_ffn(x2, ona.reshape(T, NA_WIDTH), y3, om, gates, w_chan, w_na_o, w_f, w_mem_o,
                   w_out, norm2_g, w_ff1, w_ff2, tm=512, ff_chunk=1024)
    return out.reshape(B, S, D_MODEL)
```

```python
import functools

import numpy as np
import jax
import jax.numpy as jnp
from jax import lax
from jax.experimental import pallas as pl
from jax.experimental.pallas import tpu as pltpu

D_MODEL = 1024
GRID_W = 64
NA_HEADS = 8
NA_HEAD_DIM = 64
NA_WIDTH = NA_HEADS * NA_HEAD_DIM
NA_WIN_ROWS = 8
NA_WIN_COLS = 16
F_GROUPS = 4
F_GROUP_DIM = 128
F_WIDTH = F_GROUPS * F_GROUP_DIM
MEM_HEADS = 4
MEM_HEAD_DIM = 128
MEM_WIDTH = MEM_HEADS * MEM_HEAD_DIM
N_BRANCHES = 3
D_FF = 4 * D_MODEL
EPS = 1e-6
NEG_INF = -1e30

V7X_LANES = 128
V7X_VMEM_LIMIT_BYTES = 60 * 1024 * 1024

NA_PAIR_TOKENS = 2 * GRID_W
NA_KEY_PAIRS = 5
NA_KEY_TOKENS = NA_KEY_PAIRS * NA_PAIR_TOKENS
NA_BIAS_CASES = 5
NA_PAIRS_PER_ITER = 2
NA_SCORES_AHEAD = 2

BF16 = jnp.bfloat16
F32 = jnp.float32


def _dot(a, b):
    return jnp.dot(a, b, preferred_element_type=F32)


def _dot_nt(a, b):
    return lax.dot_general(a, b, (((1,), (1,)), ((), ())), preferred_element_type=F32)


def _const_spec(shape):
    return pl.BlockSpec(shape, lambda *_: (0,) * len(shape), pipeline_mode=pl.Buffered(1))


def _params(n_axes):
    return pltpu.CompilerParams(
        dimension_semantics=("arbitrary",) * n_axes,
        vmem_limit_bytes=V7X_VMEM_LIMIT_BYTES,
    )


def _group_mean_matrix(width, group):
    idx = np.arange(width) // group
    return jnp.asarray((idx[:, None] == idx[None, :]).astype(np.float32) / group, dtype=BF16)


def _mem_kv_kernel(mem_ref, g_ref, w_ref, gm_ref, kg_ref, kT_ref, v_ref):
    m = mem_ref[0]
    ms = jnp.mean(m * m, axis=-1, keepdims=True)
    mn = (m * lax.rsqrt(ms + EPS) * g_ref[...]).astype(BF16)
    kv = _dot(mn, w_ref[...])
    k = kv[:, :MEM_WIDTH]
    msk = _dot((k * k).astype(BF16), gm_ref[...])
    kn = k * lax.rsqrt(msk + EPS) * kg_ref[...]
    kT_ref[0] = kn.T.astype(BF16)
    v_ref[0] = kv[:, MEM_WIDTH:].astype(BF16)


def _mem_kv(mem, mem_norm_g, w_mem_kv, gmean_mem, mem_k_g):
    B, M, _ = mem.shape
    return pl.pallas_call(
        _mem_kv_kernel,
        grid=(B,),
        in_specs=[
            pl.BlockSpec((1, M, D_MODEL), lambda b: (b, 0, 0)),
            _const_spec((1, D_MODEL)),
            _const_spec((D_MODEL, 2 * MEM_WIDTH)),
            _const_spec((MEM_WIDTH, MEM_WIDTH)),
            _const_spec((1, MEM_WIDTH)),
        ],
        out_specs=[
            pl.BlockSpec((1, MEM_WIDTH, M), lambda b: (b, 0, 0)),
            pl.BlockSpec((1, M, MEM_WIDTH), lambda b: (b, 0, 0)),
        ],
        out_shape=[
            jax.ShapeDtypeStruct((B, MEM_WIDTH, M), BF16),
            jax.ShapeDtypeStruct((B, M, MEM_WIDTH), BF16),
        ],
        compiler_params=_params(1),
        name="mem_kv",
    )(mem, mem_norm_g.reshape(1, D_MODEL), w_mem_kv.astype(BF16), gmean_mem,
      jnp.tile(mem_k_g, MEM_HEADS).reshape(1, MEM_WIDTH))


def _in_proj_kernel(x_ref, g1_ref, wq_ref, wkT_ref, wv_ref, wf_ref, wm_ref, wg_ref, bg_ref,
                    qg_ref, kg_ref, mqg_ref, memkT_ref, memv_ref, blo_ref, bhi_ref, *rest,
                    n_narrow, bias_dr):
    wide_refs = rest[:n_narrow]
    q_ref, kT_ref, v_ref, fu_ref, om_ref, gate_ref, bias_ref = rest[n_narrow:n_narrow + 7]
    for wide_ref, narrow_ref in zip(wide_refs, rest[n_narrow + 7:]):
        narrow_ref[...] = wide_ref[...].astype(BF16)

    @pl.when(pl.program_id(0) < NA_HEADS)
    def _():
        _assemble_na_bias(blo_ref, bhi_ref, bias_ref, bias_dr)

    x = x_ref[...]
    ms = jnp.mean(x * x, axis=-1, keepdims=True)
    h = (x * lax.rsqrt(ms + EPS) * g1_ref[...]).astype(BF16)

    def gate_dot(br):
        sl = slice(br * D_MODEL, (br + 1) * D_MODEL)
        zg = _dot(h, wg_ref[:, sl]) + bg_ref[:, sl]
        gate_ref[:, sl] = jax.nn.sigmoid(zg).astype(BF16)

    zm = _dot(h, wm_ref[...])
    mem_scores = []
    for hd in range(MEM_HEADS):
        sl = slice(hd * MEM_HEAD_DIM, (hd + 1) * MEM_HEAD_DIM)
        zh = zm[:, sl]
        msm = jnp.mean(zh * zh, axis=-1, keepdims=True)
        qm = (zh * lax.rsqrt(msm + EPS) * mqg_ref[:, sl]).astype(BF16)
        if hd == 0:
            gate_dot(0)
        mem_scores.append(_dot(qm, memkT_ref[0, sl, :]))
    gate_dot(1)

    zq = _dot(h, wq_ref[...])
    low_half = lax.broadcasted_iota(jnp.int32, (zq.shape[0], V7X_LANES), 1) < NA_HEAD_DIM
    for hp in range(NA_HEADS // 2):
        sl = slice(hp * V7X_LANES, (hp + 1) * V7X_LANES)
        zz = zq[:, sl] * zq[:, sl]
        ms_lo = jnp.sum(jnp.where(low_half, zz, 0.0), axis=-1, keepdims=True)
        ms_hi = jnp.sum(jnp.where(low_half, 0.0, zz), axis=-1, keepdims=True)
        ms = jnp.where(low_half, ms_lo, ms_hi) * (1.0 / NA_HEAD_DIM)
        q_ref[:, sl] = (zq[:, sl] * lax.rsqrt(ms + EPS) * qg_ref[:, sl]).astype(BF16)

    zkT = _dot_nt(wkT_ref[...], h)
    zk3 = zkT.reshape(NA_HEADS, NA_HEAD_DIM, zkT.shape[1])
    msk = jnp.mean(zk3 * zk3, axis=1, keepdims=True)
    kT = ((zk3 * lax.rsqrt(msk + EPS)).reshape(zkT.shape) * kg_ref[...]).astype(BF16)
    for j in range(kT_ref.shape[1]):
        kT_ref[0, j] = kT[:, j * V7X_LANES:(j + 1) * V7X_LANES]

    zv = _dot(h, wv_ref[...])
    low_half = lax.broadcasted_iota(jnp.int32, (zv.shape[0], V7X_LANES), 1) < NA_HEAD_DIM
    for hp in range(NA_HEADS // 2):
        pair = zv[:, hp * V7X_LANES:(hp + 1) * V7X_LANES]
        v_ref[:, (2 * hp) * V7X_LANES:(2 * hp + 1) * V7X_LANES] = jnp.where(low_half, pair, 1.0).astype(BF16)
        v_ref[:, (2 * hp + 1) * V7X_LANES:(2 * hp + 2) * V7X_LANES] = jnp.where(low_half, 1.0, pair).astype(BF16)
    fu_ref[...] = _dot(h, wf_ref[...]).astype(BF16)
    gate_dot(2)

    outs = []
    for hd in range(MEM_HEADS):
        sl = slice(hd * MEM_HEAD_DIM, (hd + 1) * MEM_HEAD_DIM)
        s = mem_scores[hd]
        e = jnp.exp(s - jnp.max(s, axis=-1, keepdims=True))
        l = jnp.sum(e, axis=-1, keepdims=True)
        outs.append(_dot(e.astype(BF16), memv_ref[0, :, sl]) / l)
    om_ref[...] = jnp.concatenate(outs, axis=1).astype(BF16)


def _in_proj(x2, norm1_g, w_in, b_gate, na_q_g, na_k_g, mem_q_g,
             memkT, memv, bias_blocks, later_weights, tokens_per_batch, tm):
    T = x2.shape[0]
    bias_lo, bias_hi, bias_dr = bias_blocks
    n_cases = bias_dr.shape[0]
    M = memv.shape[1]
    tiles_per_batch = tokens_per_batch // tm
    slabs = tm // V7X_LANES
    B = T // tokens_per_batch
    wb = w_in.astype(BF16)
    o = 0
    wq = wb[:, o:o + NA_WIDTH]; o += NA_WIDTH
    wkT = wb[:, o:o + NA_WIDTH].T; o += NA_WIDTH
    wv = wb[:, o:o + NA_WIDTH]; o += NA_WIDTH
    wf = wb[:, o:o + F_WIDTH]; o += F_WIDTH
    wm = wb[:, o:o + MEM_WIDTH]; o += MEM_WIDTH
    wg = wb[:, o:]
    na_scale = np.float32(1.0 / np.sqrt(NA_HEAD_DIM))
    mem_scale = np.float32(1.0 / np.sqrt(MEM_HEAD_DIM))
    qg = (jnp.tile(na_q_g, NA_HEADS) * na_scale).reshape(1, NA_WIDTH)
    kg = jnp.tile(na_k_g, NA_HEADS).reshape(NA_WIDTH, 1)
    mqg = (jnp.tile(mem_q_g, MEM_HEADS) * mem_scale).reshape(1, MEM_WIDTH)
    tok = lambda w: pl.BlockSpec((tm, w), lambda i: (i, 0))
    batch_of = lambda i: i // tiles_per_batch
    steps = T // tm
    assert steps >= NA_HEADS
    slab_specs = [pl.BlockSpec((w.shape[0] // steps, w.shape[1]), lambda i: (i, 0))
                  for w in later_weights]
    head_of = lambda i: jnp.minimum(i, NA_HEADS - 1)
    half_spec = pl.BlockSpec((1,) + bias_lo.shape[1:], lambda i: (head_of(i), 0, 0, 0))
    return pl.pallas_call(
        functools.partial(_in_proj_kernel, n_narrow=len(later_weights), bias_dr=bias_dr),
        grid=(steps,),
        in_specs=[
            tok(D_MODEL),
            _const_spec((1, D_MODEL)),
            _const_spec((D_MODEL, NA_WIDTH)),
            _const_spec((NA_WIDTH, D_MODEL)),
            _const_spec((D_MODEL, NA_WIDTH)),
            _const_spec((D_MODEL, F_WIDTH)),
            _const_spec((D_MODEL, MEM_WIDTH)),
            _const_spec((D_MODEL, N_BRANCHES * D_MODEL)),
            _const_spec((1, N_BRANCHES * D_MODEL)),
            _const_spec((1, NA_WIDTH)),
            _const_spec((NA_WIDTH, 1)),
            _const_spec((1, MEM_WIDTH)),
            pl.BlockSpec((1, MEM_WIDTH, M), lambda i: (batch_of(i), 0, 0)),
            pl.BlockSpec((1, M, MEM_WIDTH), lambda i: (batch_of(i), 0, 0)),
            half_spec,
            half_spec,
        ] + slab_specs,
        out_specs=[
            tok(NA_WIDTH),
            pl.BlockSpec((1, slabs, NA_WIDTH, V7X_LANES),
                         lambda i: (batch_of(i), i % tiles_per_batch, 0, 0)),
            tok(NA_HEADS * V7X_LANES),
            tok(F_WIDTH),
            tok(MEM_WIDTH),
            tok(N_BRANCHES * D_MODEL),
            pl.BlockSpec((n_cases, 1, NA_PAIR_TOKENS, NA_KEY_TOKENS), lambda i: (0, head_of(i), 0, 0)),
        ] + slab_specs,
        out_shape=[
            jax.ShapeDtypeStruct((T, NA_WIDTH), BF16),
            jax.ShapeDtypeStruct((B, tokens_per_batch // V7X_LANES, NA_WIDTH, V7X_LANES), BF16),
            jax.ShapeDtypeStruct((T, NA_HEADS * V7X_LANES), BF16),
            jax.ShapeDtypeStruct((T, F_WIDTH), BF16),
            jax.ShapeDtypeStruct((T, MEM_WIDTH), BF16),
            jax.ShapeDtypeStruct((T, N_BRANCHES * D_MODEL), BF16),
            jax.ShapeDtypeStruct((n_cases, NA_HEADS, NA_PAIR_TOKENS, NA_KEY_TOKENS), F32),
        ] + [jax.ShapeDtypeStruct(w.shape, BF16) for w in later_weights],
        compiler_params=_params(1),
        name="in_proj",
    )(x2, norm1_g.reshape(1, D_MODEL), wq, wkT, wv, wf, wm, wg,
      b_gate.reshape(1, N_BRANCHES * D_MODEL), qg, kg, mqg, memkT, memv, bias_lo, bias_hi,
      *later_weights)


def _na_bias_blocks(rpb, rows):
    n_pairs = rows // 2
    wr = min(NA_WIN_ROWS, rows)
    reps = np.array([0, 1, 2, n_pairs - 2, n_pairs - 1])
    starts = np.clip(reps - 2, 0, n_pairs - NA_KEY_PAIRS)
    n_dr, n_dc = 2 * NA_WIN_ROWS - 1, 2 * NA_WIN_COLS - 1
    c = np.arange(GRID_W)[:, None]
    kc = np.arange(2 * GRID_W)[None, :] % GRID_W
    cs = np.clip(c - NA_WIN_COLS // 2, 0, GRID_W - NA_WIN_COLS)
    col_valid = (kc >= cs) & (kc < cs + NA_WIN_COLS)
    dc = np.clip(kc - c, -(NA_WIN_COLS - 1), NA_WIN_COLS - 1) + (NA_WIN_COLS - 1)
    onehot = (np.arange(n_dc)[:, None] == dc.reshape(1, -1)).astype(np.float32)
    cols = jnp.dot(rpb.astype(F32).reshape(NA_HEADS * n_dr, n_dc), onehot,
                   precision=lax.Precision.HIGHEST)
    cols = jnp.where(col_valid.reshape(1, -1), cols, NEG_INF)
    cols = cols.reshape(NA_HEADS, n_dr, GRID_W, 2 * GRID_W)
    cols = jnp.concatenate([cols, jnp.full((NA_HEADS, 1, GRID_W, 2 * GRID_W), NEG_INF, F32)], axis=1)
    low = (np.arange(2 * GRID_W) < GRID_W)[None, None, None, :]
    lo = jnp.where(low, cols, 0.0)
    hi = jnp.where(low, 0.0, cols)
    r = (2 * reps[:, None] + np.arange(2)[None, :])[:, :, None]
    kr = (2 * starts[:, None] + np.arange(2 * NA_KEY_PAIRS)[None, :])[:, None, :]
    rs = np.clip(r - wr // 2, 0, rows - wr)
    row_valid = (kr >= rs) & (kr < rs + wr)
    dr = np.where(row_valid, kr - r + (NA_WIN_ROWS - 1), n_dr)
    assert dr.min() >= 0 and dr.max() <= n_dr
    return lo, hi, dr


def _assemble_na_bias(lo_ref, hi_ref, o_ref, dr):
    for case in range(dr.shape[0]):
        for rr in range(2):
            for j in range(NA_KEY_PAIRS):
                tile = lo_ref[0, int(dr[case, rr, 2 * j])] + hi_ref[0, int(dr[case, rr, 2 * j + 1])]
                o_ref[case, 0, rr * GRID_W:(rr + 1) * GRID_W,
                      j * V7X_LANES:(j + 1) * V7X_LANES] = tile


def _na_kernel(q_ref, kT_ref, v_ref, bias_ref, o_ref, *, pairs_per_step, n_pairs):
    lane = lax.broadcasted_iota(jnp.int32, (NA_PAIR_TOKENS, 2 * NA_HEAD_DIM), 1)
    first_head = lane < NA_HEAD_DIM

    def pair_coords(i):
        p = pl.program_id(1) * pairs_per_step + i
        start = jnp.clip(p - 2, 0, n_pairs - NA_KEY_PAIRS)
        case = jnp.where(p < 2, p, jnp.where(p >= n_pairs - 2, p - (n_pairs - NA_BIAS_CASES), 2))
        qrow = pl.multiple_of(i * NA_PAIR_TOKENS, NA_PAIR_TOKENS)
        krow = pl.multiple_of(start * NA_PAIR_TOKENS, NA_PAIR_TOKENS)
        return start, case, qrow, krow

    def scores(coords, hd):
        start, case, qrow, _ = coords
        hsl = slice(hd * NA_HEAD_DIM, (hd + 1) * NA_HEAD_DIM)
        qh = q_ref[0, pl.ds(qrow, NA_PAIR_TOKENS), hsl]
        kslabs = kT_ref[0, pl.ds(start, NA_KEY_PAIRS), hsl, :]
        kT = jnp.concatenate([kslabs[j] for j in range(NA_KEY_PAIRS)], axis=1)
        return _dot(qh, kT) + bias_ref[case, hd]

    def body(it, carry):
        coords = [pair_coords(it * NA_PAIRS_PER_ITER + j) for j in range(NA_PAIRS_PER_ITER)]
        items = [(j, hd) for j in range(NA_PAIRS_PER_ITER) for hd in range(NA_HEADS)]
        pending = [scores(coords[j], hd) for j, hd in items[:NA_SCORES_AHEAD]]
        halves = []
        for n, (j, hd) in enumerate(items):
            s = pending.pop(0)
            if n + NA_SCORES_AHEAD < len(items):
                jn, hn = items[n + NA_SCORES_AHEAD]
                pending.append(scores(coords[jn], hn))
            _, _, qrow, krow = coords[j]
            vext = v_ref[0, pl.ds(krow, NA_KEY_TOKENS), hd * 2 * NA_HEAD_DIM:(hd + 1) * 2 * NA_HEAD_DIM]
            e = jnp.exp((s - jnp.max(s, axis=-1, keepdims=True)).astype(BF16))
            r = _dot(e, vext)
            halves.append(r / pltpu.roll(r, NA_HEAD_DIM, axis=1))
            if hd == NA_HEADS - 1:
                outs = [jnp.where(first_head, halves[2 * k], halves[2 * k + 1])
                        for k in range(NA_HEADS // 2)]
                o_ref[0, pl.ds(qrow, NA_PAIR_TOKENS), :] = jnp.concatenate(outs, axis=1).astype(BF16)
                halves = []
        return carry

    lax.fori_loop(0, pairs_per_step // NA_PAIRS_PER_ITER, body, 0)


def _na_attention(q, kT4, v, bias, pairs_per_step):
    B, S, _ = q.shape
    n_pairs = S // NA_PAIR_TOKENS
    steps = n_pairs // pairs_per_step
    tq = pairs_per_step * NA_PAIR_TOKENS
    return pl.pallas_call(
        functools.partial(_na_kernel, pairs_per_step=pairs_per_step, n_pairs=n_pairs),
        grid=(B, steps),
        in_specs=[
            pl.BlockSpec((1, tq, NA_WIDTH), lambda b, j: (b, j, 0)),
            pl.BlockSpec((1, n_pairs, NA_WIDTH, V7X_LANES), lambda b, j: (b, 0, 0, 0)),
            pl.BlockSpec((1, S, NA_HEADS * V7X_LANES), lambda b, j: (b, 0, 0)),
            _const_spec((NA_BIAS_CASES, NA_HEADS, NA_PAIR_TOKENS, NA_KEY_TOKENS)),
        ],
        out_specs=pl.BlockSpec((1, tq, NA_WIDTH), lambda b, j: (b, j, 0)),
        out_shape=jax.ShapeDtypeStruct((B, S, NA_WIDTH), BF16),
        compiler_params=_params(2),
        name="na_attn",
    )(q, kT4, v, bias)


DFT_CHUNK = 16


def _dft_tables():
    n, m = GRID_W, DFT_CHUNK
    a = np.arange(n)
    ang1 = 2.0 * np.pi * np.outer(a, a) / n
    w_ri = np.stack([np.cos(ang1), -np.sin(ang1)], axis=1) / np.sqrt(n)
    eye = np.eye(m)
    w_rows = w_ri.reshape(2 * n, n)
    perm = np.einsum('dx,cy->dcyx', eye, eye).reshape(m * m, m * m)
    c = np.arange(n)[:, None, None]
    d = np.arange(n)[None, :, None]
    b = np.arange(n)[None, None, :]
    ang2 = 2.0 * np.pi * ((c + n * d) * b % (n * n)) / (n * n)
    gr, gi = np.cos(ang2) / np.sqrt(n), -np.sin(ang2) / np.sqrt(n)
    w_cols = np.concatenate([np.concatenate([gr, -gi], axis=2),
                             np.concatenate([gi, gr], axis=2)], axis=1)
    ch = np.arange(F_GROUP_DIM)
    ang3 = 2.0 * np.pi * np.outer(ch, ch) / F_GROUP_DIM
    w_chan = np.concatenate([np.cos(ang3), np.sin(ang3)], axis=0) / np.sqrt(F_GROUP_DIM)
    return tuple(jnp.asarray(t, dtype=F32).astype(BF16) for t in (w_rows, w_cols, perm, w_chan))


def _dft_rows_kernel(w_ref, perm_ref, u_ref, z_ref):
    n, m = GRID_W, DFT_CHUNK
    x = u_ref[0].reshape(n * m, F_WIDTH)
    xp = [_dot(perm_ref[...], x[g * m * m:(g + 1) * m * m]).astype(BF16) for g in range(n // m)]
    zb = []
    for b in range(m):
        xb = jnp.concatenate([xp[g][b * m:(b + 1) * m] for g in range(n // m)], axis=0)
        zb.append(_dot(w_ref[...], xb).astype(BF16))
    for t in range(2 * n // m):
        st = jnp.concatenate([zb[b][t * m:(t + 1) * m] for b in range(m)], axis=0)
        z = _dot(perm_ref[...], st).astype(BF16)
        z_ref[0, t * 8:(t + 1) * 8] = z.reshape(8, 2, m, F_WIDTH)


def _dft_rows(u4, w_rows, perm):
    B, n = u4.shape[0], GRID_W
    m = DFT_CHUNK
    return pl.pallas_call(
        _dft_rows_kernel,
        grid=(B, n // m),
        in_specs=[
            _const_spec((2 * n, n)),
            _const_spec((m * m, m * m)),
            pl.BlockSpec((1, n, m, F_WIDTH), lambda b, j: (b, 0, j, 0)),
        ],
        out_specs=pl.BlockSpec((1, n, 2, m, F_WIDTH), lambda b, j: (b, 0, 0, j, 0)),
        out_shape=jax.ShapeDtypeStruct((B, n, 2, n, F_WIDTH), BF16),
        compiler_params=_params(2),
        name="dft_rows",
    )(w_rows, perm, u4)


def _dft_cols_kernel(w_ref, perm_ref, z_ref, y_ref):
    n, m = GRID_W, DFT_CHUNK
    res = []
    for j in range(m):
        c = pl.program_id(1) * m + j
        data = z_ref[0, j].reshape(2 * n, F_WIDTH)
        res.append(_dot(w_ref[c], data).astype(BF16))
    for ri in range(2):
        for g in range(n // m):
            lo = ri * n + g * m
            stacked = jnp.concatenate([r[lo:lo + m] for r in res], axis=0)
            y = _dot(perm_ref[...], stacked).astype(BF16)
            y_ref[ri, 0, g * m:(g + 1) * m] = y.reshape(m, m, F_WIDTH)


def _dft_cols(z5, w_cols, perm):
    B, n, m = z5.shape[0], GRID_W, DFT_CHUNK
    return pl.pallas_call(
        _dft_cols_kernel,
        grid=(B, n // m),
        in_specs=[
            _const_spec((n, 2 * n, 2 * n)),
            _const_spec((m * m, m * m)),
            pl.BlockSpec((1, m, 2, n, F_WIDTH), lambda b, j: (b, j, 0, 0, 0)),
        ],
        out_specs=pl.BlockSpec((2, 1, n, m, F_WIDTH), lambda b, j: (0, b, 0, j, 0)),
        out_shape=jax.ShapeDtypeStruct((2, B, n, n, F_WIDTH), BF16),
        compiler_params=_params(2),
        name="dft_cols",
    )(w_cols, perm, z5)


def _out_ffn_kernel(x_ref, ona_ref, y_ref, om_ref, gate_ref, wch_ref, wna_ref, wf_ref, wmo_ref,
                    wout_ref, g2_ref, w1_ref, w2_ref, o_ref, *, ff_chunk):
    yr, yi = y_ref[0], y_ref[1]
    groups = []
    for g in range(F_GROUPS):
        sl = slice(g * F_GROUP_DIM, (g + 1) * F_GROUP_DIM)
        groups.append(_dot(jnp.concatenate([yr[:, sl], yi[:, sl]], axis=1), wch_ref[...]))
    yf = jnp.concatenate(groups, axis=1).astype(BF16)

    gate = lambda br: gate_ref[:, br * D_MODEL:(br + 1) * D_MODEL].astype(F32)
    merged = gate(0) * _dot(ona_ref[...], wna_ref[...])
    merged = merged + gate(1) * _dot(yf, wf_ref[...])
    merged = merged + gate(2) * _dot(om_ref[...], wmo_ref[...])
    x1 = x_ref[...] + _dot(merged.astype(BF16), wout_ref[...])

    ms = jnp.mean(x1 * x1, axis=-1, keepdims=True)
    h2 = (x1 * lax.rsqrt(ms + EPS) * g2_ref[...]).astype(BF16)
    acc = x1
    for j in range(D_FF // ff_chunk):
        sl = slice(j * ff_chunk, (j + 1) * ff_chunk)
        a = jnp.maximum(_dot(h2, w1_ref[:, sl]), 0.0)
        acc = acc + _dot((a * a).astype(BF16), w2_ref[sl, :])
    o_ref[...] = acc


def _out_ffn(x2, ona, y3, om, gates, w_chan, w_na_o, w_f, w_mem_o, w_out, norm2_g, w_ff1, w_ff2,
             tm, ff_chunk):
    T = x2.shape[0]
    tok = lambda w: pl.BlockSpec((tm, w), lambda i: (i, 0))
    return pl.pallas_call(
        functools.partial(_out_ffn_kernel, ff_chunk=ff_chunk),
        grid=(T // tm,),
        in_specs=[
            tok(D_MODEL),
            tok(NA_WIDTH),
            pl.BlockSpec((2, tm, F_WIDTH), lambda i: (0, i, 0)),
            tok(MEM_WIDTH),
            tok(N_BRANCHES * D_MODEL),
            _const_spec((2 * F_GROUP_DIM, F_GROUP_DIM)),
            _const_spec((NA_WIDTH, D_MODEL)),
            _const_spec((F_WIDTH, D_MODEL)),
            _const_spec((MEM_WIDTH, D_MODEL)),
            _const_spec((D_MODEL, D_MODEL)),
            _const_spec((1, D_MODEL)),
            _const_spec((D_MODEL, D_FF)),
            _const_spec((D_FF, D_MODEL)),
        ],
        out_specs=tok(D_MODEL),
        out_shape=jax.ShapeDtypeStruct((T, D_MODEL), F32),
        compiler_params=_params(1),
        name="out_ffn",
    )(x2, ona, y3, om, gates, w_chan, w_na_o, w_f, w_mem_o, w_out, norm2_g.reshape(1, D_MODEL),
      w_ff1, w_ff2)


def kernel(x, mem, norm1_g, w_in, b_gate, na_q_g, na_k_g, na_rpb, w_na_o, w_f, mem_norm_g,
           w_mem_kv, mem_q_g, mem_k_g, w_mem_o, w_out, norm2_g, w_ff1, w_ff2):
    B, S, _ = x.shape
    T = B * S
    rows = S // GRID_W
    assert rows == GRID_W and S % NA_PAIR_TOKENS == 0
    x2 = x.reshape(T, D_MODEL)
    gmean_mem = _group_mean_matrix(MEM_WIDTH, MEM_HEAD_DIM)
    w_rows, w_cols, perm, w_chan = _dft_tables()

    memkT, memv = _mem_kv(mem, mem_norm_g, w_mem_kv, gmean_mem, mem_k_g)
    later_weights = (w_na_o, w_f, w_mem_o, w_out, w_ff1, w_ff2)
    q, kT4, v, fu, om, gates, bias, *narrow = _in_proj(
        x2, norm1_g, w_in, b_gate, na_q_g, na_k_g, mem_q_g, memkT, memv,
        _na_bias_blocks(na_rpb, rows), later_weights, S, tm=512)
    w_na_o, w_f, w_mem_o, w_out, w_ff1, w_ff2 = narrow

    ona = _na_attention(q.reshape(B, S, NA_WIDTH), kT4, v.reshape(B, S, NA_HEADS * V7X_LANES), bias,
                        pairs_per_step=8)

    z = _dft_rows(fu.reshape(B, GRID_W, GRID_W, F_WIDTH), w_rows, perm)
    y = _dft_cols(z, w_cols, perm)
    y3 = y.reshape(2, T, F_WIDTH)

    out = _out_ffn(x2, ona.reshape(T, NA_WIDTH), y3, om, gates, w_chan, w_na_o, w_f, w_mem_o,
                   w_out, norm2_g, w_ff1, w_ff2, tm=512, ff_chunk=1024)
    return out.reshape(B, S, D_MODEL)
```

```python
import functools

import numpy as np
import jax
import jax.numpy as jnp
from jax import lax
from jax.experimental import pallas as pl
from jax.experimental.pallas import tpu as pltpu

D_MODEL = 1024
GRID_W = 64
NA_HEADS = 8
NA_HEAD_DIM = 64
NA_WIDTH = NA_HEADS * NA_HEAD_DIM
NA_WIN_ROWS = 8
NA_WIN_COLS = 16
F_GROUPS = 4
F_GROUP_DIM = 128
F_WIDTH = F_GROUPS * F_GROUP_DIM
MEM_HEADS = 4
MEM_HEAD_DIM = 128
MEM_WIDTH = MEM_HEADS * MEM_HEAD_DIM
N_BRANCHES = 3
D_FF = 4 * D_MODEL
EPS = 1e-6
NEG_INF = -1e30

V7X_LANES = 128
V7X_VMEM_LIMIT_BYTES = 60 * 1024 * 1024

NA_PAIR_TOKENS = 2 * GRID_W
NA_KEY_PAIRS = 5
NA_KEY_TOKENS = NA_KEY_PAIRS * NA_PAIR_TOKENS
NA_BIAS_CASES = 5
NA_PAIRS_PER_ITER = 2
NA_SCORES_AHEAD = 2

BF16 = jnp.bfloat16
F32 = jnp.float32


def _dot(a, b):
    return jnp.dot(a, b, preferred_element_type=F32)


def _dot_nt(a, b):
    return lax.dot_general(a, b, (((1,), (1,)), ((), ())), preferred_element_type=F32)


def _const_spec(shape):
    return pl.BlockSpec(shape, lambda *_: (0,) * len(shape), pipeline_mode=pl.Buffered(1))


def _params(n_axes):
    return pltpu.CompilerParams(
        dimension_semantics=("arbitrary",) * n_axes,
        vmem_limit_bytes=V7X_VMEM_LIMIT_BYTES,
    )


def _group_mean_matrix(width, group):
    idx = np.arange(width) // group
    return jnp.asarray((idx[:, None] == idx[None, :]).astype(np.float32) / group, dtype=BF16)


def _mem_kv_kernel(mem_ref, g_ref, w_ref, gm_ref, kg_ref, kT_ref, v_ref):
    m = mem_ref[0]
    ms = jnp.mean(m * m, axis=-1, keepdims=True)
    mn = (m * lax.rsqrt(ms + EPS) * g_ref[...]).astype(BF16)
    kv = _dot(mn, w_ref[...])
    k = kv[:, :MEM_WIDTH]
    msk = _dot((k * k).astype(BF16), gm_ref[...])
    kn = k * lax.rsqrt(msk + EPS) * kg_ref[...]
    kT_ref[0] = kn.T.astype(BF16)
    v_ref[0] = kv[:, MEM_WIDTH:].astype(BF16)


def _mem_kv(mem, mem_norm_g, w_mem_kv, gmean_mem, mem_k_g):
    B, M, _ = mem.shape
    return pl.pallas_call(
        _mem_kv_kernel,
        grid=(B,),
        in_specs=[
            pl.BlockSpec((1, M, D_MODEL), lambda b: (b, 0, 0)),
            _const_spec((1, D_MODEL)),
            _const_spec((D_MODEL, 2 * MEM_WIDTH)),
            _const_spec((MEM_WIDTH, MEM_WIDTH)),
            _const_spec((1, MEM_WIDTH)),
        ],
        out_specs=[
            pl.BlockSpec((1, MEM_WIDTH, M), lambda b: (b, 0, 0)),
            pl.BlockSpec((1, M, MEM_WIDTH), lambda b: (b, 0, 0)),
        ],
        out_shape=[
            jax.ShapeDtypeStruct((B, MEM_WIDTH, M), BF16),
            jax.ShapeDtypeStruct((B, M, MEM_WIDTH), BF16),
        ],
        compiler_params=_params(1),
        name="mem_kv",
    )(mem, mem_norm_g.reshape(1, D_MODEL), w_mem_kv.astype(BF16), gmean_mem,
      jnp.tile(mem_k_g, MEM_HEADS).reshape(1, MEM_WIDTH))


def _in_proj_kernel(x_ref, g1_ref, wq_ref, wkT_ref, wv_ref, wf_ref, wm_ref, wg_ref, bg_ref,
                    qg_ref, kg_ref, mqg_ref, memkT_ref, memv_ref, blo_ref, bhi_ref, *rest,
                    n_narrow, bias_dr):
    wide_refs = rest[:n_narrow]
    q_ref, kT_ref, v_ref, fu_ref, om_ref, gate_ref, bias_ref = rest[n_narrow:n_narrow + 7]
    for wide_ref, narrow_ref in zip(wide_refs, rest[n_narrow + 7:]):
        narrow_ref[...] = wide_ref[...].astype(BF16)

    @pl.when(pl.program_id(0) < NA_HEADS)
    def _():
        _assemble_na_bias(blo_ref, bhi_ref, bias_ref, bias_dr)

    x = x_ref[...]
    ms = jnp.mean(x * x, axis=-1, keepdims=True)
    h = (x * lax.rsqrt(ms + EPS) * g1_ref[...]).astype(BF16)

    def gate_dot(br):
        sl = slice(br * D_MODEL, (br + 1) * D_MODEL)
        zg = _dot(h, wg_ref[:, sl]) + bg_ref[:, sl]
        gate_ref[:, sl] = jax.nn.sigmoid(zg).astype(BF16)

    zm = _dot(h, wm_ref[...])
    mem_scores = []
    for hd in range(MEM_HEADS):
        sl = slice(hd * MEM_HEAD_DIM, (hd + 1) * MEM_HEAD_DIM)
        zh = zm[:, sl]
        msm = jnp.mean(zh * zh, axis=-1, keepdims=True)
        qm = (zh * lax.rsqrt(msm + EPS) * mqg_ref[:, sl]).astype(BF16)
        if hd == 0:
            gate_dot(0)
        mem_scores.append(_dot(qm, memkT_ref[0, sl, :]))
    gate_dot(1)

    zq = _dot(h, wq_ref[...])
    low_half = lax.broadcasted_iota(jnp.int32, (zq.shape[0], V7X_LANES), 1) < NA_HEAD_DIM
    for hp in range(NA_HEADS // 2):
        sl = slice(hp * V7X_LANES, (hp + 1) * V7X_LANES)
        zz = zq[:, sl] * zq[:, sl]
        ms_lo = jnp.sum(jnp.where(low_half, zz, 0.0), axis=-1, keepdims=True)
        ms_hi = jnp.sum(jnp.where(low_half, 0.0, zz), axis=-1, keepdims=True)
        ms = jnp.where(low_half, ms_lo, ms_hi) * (1.0 / NA_HEAD_DIM)
        q_ref[:, sl] = (zq[:, sl] * lax.rsqrt(ms + EPS) * qg_ref[:, sl]).astype(BF16)

    zkT = _dot_nt(wkT_ref[...], h)
    zk3 = zkT.reshape(NA_HEADS, NA_HEAD_DIM, zkT.shape[1])
    msk = jnp.mean(zk3 * zk3, axis=1, keepdims=True)
    kT = ((zk3 * lax.rsqrt(msk + EPS)).reshape(zkT.shape) * kg_ref[...]).astype(BF16)
    for j in range(kT_ref.shape[1]):
        kT_ref[0, j] = kT[:, j * V7X_LANES:(j + 1) * V7X_LANES]

    zv = _dot(h, wv_ref[...])
    low_half = lax.broadcasted_iota(jnp.int32, (zv.shape[0], V7X_LANES), 1) < NA_HEAD_DIM
    for hp in range(NA_HEADS // 2):
        pair = zv[:, hp * V7X_LANES:(hp + 1) * V7X_LANES]
        v_ref[:, (2 * hp) * V7X_LANES:(2 * hp + 1) * V7X_LANES] = jnp.where(low_half, pair, 1.0).astype(BF16)
        v_ref[:, (2 * hp + 1) * V7X_LANES:(2 * hp + 2) * V7X_LANES] = jnp.where(low_half, 1.0, pair).astype(BF16)
    fu_ref[...] = _dot(h, wf_ref[...]).astype(BF16)
    gate_dot(2)

    outs = []
    for hd in range(MEM_HEADS):
        sl = slice(hd * MEM_HEAD_DIM, (hd + 1) * MEM_HEAD_DIM)
        s = mem_scores[hd]
        e = jnp.exp(s - jnp.max(s, axis=-1, keepdims=True))
        l = jnp.sum(e, axis=-1, keepdims=True)
        outs.append(_dot(e.astype(BF16), memv_ref[0, :, sl]) / l)
    om_ref[...] = jnp.concatenate(outs, axis=1).astype(BF16)


def _in_proj(x2, norm1_g, w_in, b_gate, na_q_g, na_k_g, mem_q_g,
             memkT, memv, bias_blocks, later_weights, tokens_per_batch, tm):
    T = x2.shape[0]
    bias_lo, bias_hi, bias_dr = bias_blocks
    n_cases = bias_dr.shape[0]
    M = memv.shape[1]
    tiles_per_batch = tokens_per_batch // tm
    slabs = tm // V7X_LANES
    B = T // tokens_per_batch
    wb = w_in.astype(BF16)
    o = 0
    wq = wb[:, o:o + NA_WIDTH]; o += NA_WIDTH
    wkT = wb[:, o:o + NA_WIDTH].T; o += NA_WIDTH
    wv = wb[:, o:o + NA_WIDTH]; o += NA_WIDTH
    wf = wb[:, o:o + F_WIDTH]; o += F_WIDTH
    wm = wb[:, o:o + MEM_WIDTH]; o += MEM_WIDTH
    wg = wb[:, o:]
    na_scale = np.float32(1.0 / np.sqrt(NA_HEAD_DIM))
    mem_scale = np.float32(1.0 / np.sqrt(MEM_HEAD_DIM))
    qg = (jnp.tile(na_q_g, NA_HEADS) * na_scale).reshape(1, NA_WIDTH)
    kg = jnp.tile(na_k_g, NA_HEADS).reshape(NA_WIDTH, 1)
    mqg = (jnp.tile(mem_q_g, MEM_HEADS) * mem_scale).reshape(1, MEM_WIDTH)
    tok = lambda w: pl.BlockSpec((tm, w), lambda i: (i, 0))
    batch_of = lambda i: i // tiles_per_batch
    steps = T // tm
    assert steps >= NA_HEADS
    slab_specs = [pl.BlockSpec((w.shape[0] // steps, w.shape[1]), lambda i: (i, 0))
                  for w in later_weights]
    head_of = lambda i: jnp.minimum(i, NA_HEADS - 1)
    half_spec = pl.BlockSpec((1,) + bias_lo.shape[1:], lambda i: (head_of(i), 0, 0, 0))
    return pl.pallas_call(
        functools.partial(_in_proj_kernel, n_narrow=len(later_weights), bias_dr=bias_dr),
        grid=(steps,),
        in_specs=[
            tok(D_MODEL),
            _const_spec((1, D_MODEL)),
            _const_spec((D_MODEL, NA_WIDTH)),
            _const_spec((NA_WIDTH, D_MODEL)),
            _const_spec((D_MODEL, NA_WIDTH)),
            _const_spec((D_MODEL, F_WIDTH)),
            _const_spec((D_MODEL, MEM_WIDTH)),
            _const_spec((D_MODEL, N_BRANCHES * D_MODEL)),
            _const_spec((1, N_BRANCHES * D_MODEL)),
            _const_spec((1, NA_WIDTH)),
            _const_spec((NA_WIDTH, 1)),
            _const_spec((1, MEM_WIDTH)),
            pl.BlockSpec((1, MEM_WIDTH, M), lambda i: (batch_of(i), 0, 0)),
            pl.BlockSpec((1, M, MEM_WIDTH), lambda i: (batch_of(i), 0, 0)),
            half_spec,
            half_spec,
        ] + slab_specs,
        out_specs=[
            tok(NA_WIDTH),
            pl.BlockSpec((1, slabs, NA_WIDTH, V7X_LANES),
                         lambda i: (batch_of(i), i % tiles_per_batch, 0, 0)),
            tok(NA_HEADS * V7X_LANES),
            tok(F_WIDTH),
            tok(MEM_WIDTH),
            tok(N_BRANCHES * D_MODEL),
            pl.BlockSpec((n_cases, 1, NA_PAIR_TOKENS, NA_KEY_TOKENS), lambda i: (0, head_of(i), 0, 0)),
        ] + slab_specs,
        out_shape=[
            jax.ShapeDtypeStruct((T, NA_WIDTH), BF16),
            jax.ShapeDtypeStruct((B, tokens_per_batch // V7X_LANES, NA_WIDTH, V7X_LANES), BF16),
            jax.ShapeDtypeStruct((T, NA_HEADS * V7X_LANES), BF16),
            jax.ShapeDtypeStruct((T, F_WIDTH), BF16),
            jax.ShapeDtypeStruct((T, MEM_WIDTH), BF16),
            jax.ShapeDtypeStruct((T, N_BRANCHES * D_MODEL), BF16),
            jax.ShapeDtypeStruct((n_cases, NA_HEADS, NA_PAIR_TOKENS, NA_KEY_TOKENS), F32),
        ] + [jax.ShapeDtypeStruct(w.shape, BF16) for w in later_weights],
        compiler_params=_params(1),
        name="in_proj",
    )(x2, norm1_g.reshape(1, D_MODEL), wq, wkT, wv, wf, wm, wg,
      b_gate.reshape(1, N_BRANCHES * D_MODEL), qg, kg, mqg, memkT, memv, bias_lo, bias_hi,
      *later_weights)


def _na_bias_blocks(rpb, rows):
    n_pairs = rows // 2
    wr = min(NA_WIN_ROWS, rows)
    reps = np.array([0, 1, 2, n_pairs - 2, n_pairs - 1])
    starts = np.clip(reps - 2, 0, n_pairs - NA_KEY_PAIRS)
    n_dr, n_dc = 2 * NA_WIN_ROWS - 1, 2 * NA_WIN_COLS - 1
    c = np.arange(GRID_W)[:, None]
    kc = np.arange(2 * GRID_W)[None, :] % GRID_W
    cs = np.clip(c - NA_WIN_COLS // 2, 0, GRID_W - NA_WIN_COLS)
    col_valid = (kc >= cs) & (kc < cs + NA_WIN_COLS)
    dc = np.clip(kc - c, -(NA_WIN_COLS - 1), NA_WIN_COLS - 1) + (NA_WIN_COLS - 1)
    onehot = (np.arange(n_dc)[:, None] == dc.reshape(1, -1)).astype(np.float32)
    cols = jnp.dot(rpb.astype(F32).reshape(NA_HEADS * n_dr, n_dc), onehot,
                   precision=lax.Precision.HIGHEST)
    cols = jnp.where(col_valid.reshape(1, -1), cols, NEG_INF)
    cols = cols.reshape(NA_HEADS, n_dr, GRID_W, 2 * GRID_W)
    cols = jnp.concatenate([cols, jnp.full((NA_HEADS, 1, GRID_W, 2 * GRID_W), NEG_INF, F32)], axis=1)
    low = (np.arange(2 * GRID_W) < GRID_W)[None, None, None, :]
    lo = jnp.where(low, cols, 0.0)
    hi = jnp.where(low, 0.0, cols)
    r = (2 * reps[:, None] + np.arange(2)[None, :])[:, :, None]
    kr = (2 * starts[:, None] + np.arange(2 * NA_KEY_PAIRS)[None, :])[:, None, :]
    rs = np.clip(r - wr // 2, 0, rows - wr)
    row_valid = (kr >= rs) & (kr < rs + wr)
    dr = np.where(row_valid, kr - r + (NA_WIN_ROWS - 1), n_dr)
    assert dr.min() >= 0 and dr.max() <= n_dr
    return lo, hi, dr


def _assemble_na_bias(lo_ref, hi_ref, o_ref, dr):
    for case in range(dr.shape[0]):
        for rr in range(2):
            for j in range(NA_KEY_PAIRS):
                tile = lo_ref[0, int(dr[case, rr, 2 * j])] + hi_ref[0, int(dr[case, rr, 2 * j + 1])]
                o_ref[case, 0, rr * GRID_W:(rr + 1) * GRID_W,
                      j * V7X_LANES:(j + 1) * V7X_LANES] = tile


def _na_kernel(q_ref, kT_ref, v_ref, bias_ref, o_ref, *, pairs_per_step, n_pairs):
    lane = lax.broadcasted_iota(jnp.int32, (NA_PAIR_TOKENS, 2 * NA_HEAD_DIM), 1)
    first_head = lane < NA_HEAD_DIM

    def pair_coords(i):
        p = pl.program_id(1) * pairs_per_step + i
        start = jnp.clip(p - 2, 0, n_pairs - NA_KEY_PAIRS)
        case = jnp.where(p < 2, p, jnp.where(p >= n_pairs - 2, p - (n_pairs - NA_BIAS_CASES), 2))
        qrow = pl.multiple_of(i * NA_PAIR_TOKENS, NA_PAIR_TOKENS)
        krow = pl.multiple_of(start * NA_PAIR_TOKENS, NA_PAIR_TOKENS)
        return start, case, qrow, krow

    def scores(coords, hd):
        start, case, qrow, _ = coords
        hsl = slice(hd * NA_HEAD_DIM, (hd + 1) * NA_HEAD_DIM)
        qh = q_ref[0, pl.ds(qrow, NA_PAIR_TOKENS), hsl]
        kslabs = kT_ref[0, pl.ds(start, NA_KEY_PAIRS), hsl, :]
        kT = jnp.concatenate([kslabs[j] for j in range(NA_KEY_PAIRS)], axis=1)
        return _dot(qh, kT) + bias_ref[case, hd]

    def body(it, carry):
        coords = [pair_coords(it * NA_PAIRS_PER_ITER + j) for j in range(NA_PAIRS_PER_ITER)]
        items = [(j, hd) for j in range(NA_PAIRS_PER_ITER) for hd in range(NA_HEADS)]
        pending = [scores(coords[j], hd) for j, hd in items[:NA_SCORES_AHEAD]]
        halves = []
        for n, (j, hd) in enumerate(items):
            s = pending.pop(0)
            if n + NA_SCORES_AHEAD < len(items):
                jn, hn = items[n + NA_SCORES_AHEAD]
                pending.append(scores(coords[jn], hn))
            _, _, qrow, krow = coords[j]
            vext = v_ref[0, pl.ds(krow, NA_KEY_TOKENS), hd * 2 * NA_HEAD_DIM:(hd + 1) * 2 * NA_HEAD_DIM]
            e = jnp.exp((s - jnp.max(s, axis=-1, keepdims=True)).astype(BF16))
            r = _dot(e, vext)
            halves.append(r / pltpu.roll(r, NA_HEAD_DIM, axis=1))
            if hd == NA_HEADS - 1:
                outs = [jnp.where(first_head, halves[2 * k], halves[2 * k + 1])
                        for k in range(NA_HEADS // 2)]
                o_ref[0, pl.ds(qrow, NA_PAIR_TOKENS), :] = jnp.concatenate(outs, axis=1).astype(BF16)
                halves = []
        return carry

    lax.fori_loop(0, pairs_per_step // NA_PAIRS_PER_ITER, body, 0)


def _na_attention(q, kT4, v, bias, pairs_per_step):
    B, S, _ = q.shape
    n_pairs = S // NA_PAIR_TOKENS
    steps = n_pairs // pairs_per_step
    tq = pairs_per_step * NA_PAIR_TOKENS
    return pl.pallas_call(
        functools.partial(_na_kernel, pairs_per_step=pairs_per_step, n_pairs=n_pairs),
        grid=(B, steps),
        in_specs=[
            pl.BlockSpec((1, tq, NA_WIDTH), lambda b, j: (b, j, 0)),
            pl.BlockSpec((1, n_pairs, NA_WIDTH, V7X_LANES), lambda b, j: (b, 0, 0, 0)),
            pl.BlockSpec((1, S, NA_HEADS * V7X_LANES), lambda b, j: (b, 0, 0)),
            _const_spec((NA_BIAS_CASES, NA_HEADS, NA_PAIR_TOKENS, NA_KEY_TOKENS)),
        ],
        out_specs=pl.BlockSpec((1, tq, NA_WIDTH), lambda b, j: (b, j, 0)),
        out_shape=jax.ShapeDtypeStruct((B, S, NA_WIDTH), BF16),
        compiler_params=_params(2),
        name="na_attn",
    )(q, kT4, v, bias)


DFT_CHUNK = 16


def _dft_tables():
    n, m = GRID_W, DFT_CHUNK
    a = np.arange(n)
    ang1 = 2.0 * np.pi * np.outer(a, a) / n
    w_ri = np.stack([np.cos(ang1), -np.sin(ang1)], axis=1) / np.sqrt(n)
    eye = np.eye(m)
    w_rows = w_ri.reshape(2 * n, n)
    perm = np.einsum('dx,cy->dcyx', eye, eye).reshape(m * m, m * m)
    c = np.arange(n)[:, None, None]
    d = np.arange(n)[None, :, None]
    b = np.arange(n)[None, None, :]
    ang2 = 2.0 * np.pi * ((c + n * d) * b % (n * n)) / (n * n)
    gr, gi = np.cos(ang2) / np.sqrt(n), -np.sin(ang2) / np.sqrt(n)
    w_cols = np.concatenate([np.concatenate([gr, -gi], axis=2),
                             np.concatenate([gi, gr], axis=2)], axis=1)
    ch = np.arange(F_GROUP_DIM)
    ang3 = 2.0 * np.pi * np.outer(ch, ch) / F_GROUP_DIM
    w_chan = np.concatenate([np.cos(ang3), np.sin(ang3)], axis=0) / np.sqrt(F_GROUP_DIM)
    return tuple(jnp.asarray(t, dtype=F32).astype(BF16) for t in (w_rows, w_cols, perm, w_chan))


def _dft_kernel(wr_ref, wc_ref, perm_ref, u_ref, y_ref, z_ref):
    n, m = GRID_W, DFT_CHUNK

    def rows_stage(j, carry):
        col0 = pl.multiple_of(j * m, m)
        x = u_ref[0, :, pl.ds(col0, m), :].reshape(n * m, F_WIDTH)
        xp = [_dot(perm_ref[...], x[g * m * m:(g + 1) * m * m]).astype(BF16) for g in range(n // m)]
        zb = []
        for b in range(m):
            xb = jnp.concatenate([xp[g][b * m:(b + 1) * m] for g in range(n // m)], axis=0)
            zb.append(_dot(wr_ref[...], xb).astype(BF16))
        for t in range(2 * n // m):
            st = jnp.concatenate([zb[b][t * m:(t + 1) * m] for b in range(m)], axis=0)
            z = _dot(perm_ref[...], st).astype(BF16)
            z_ref[t * 8:(t + 1) * 8, :, pl.ds(col0, m), :] = z.reshape(8, 2, m, F_WIDTH)
        return carry

    def cols_stage(j, carry):
        col0 = pl.multiple_of(j * m, m)
        res = []
        for c_lo in range(m):
            c = j * m + c_lo
            data = z_ref[c].reshape(2 * n, F_WIDTH)
            res.append(_dot(wc_ref[c], data).astype(BF16))
        for ri in range(2):
            for g in range(n // m):
                lo = ri * n + g * m
                stacked = jnp.concatenate([r[lo:lo + m] for r in res], axis=0)
                y = _dot(perm_ref[...], stacked).astype(BF16)
                y_ref[ri, 0, g * m:(g + 1) * m, pl.ds(col0, m), :] = y.reshape(m, m, F_WIDTH)
        return carry

    lax.fori_loop(0, n // m, rows_stage, 0)
    lax.fori_loop(0, n // m, cols_stage, 0)


def _position_dft(u4, w_rows, w_cols, perm):
    B, n, m = u4.shape[0], GRID_W, DFT_CHUNK
    return pl.pallas_call(
        _dft_kernel,
        grid=(B,),
        in_specs=[
            _const_spec((2 * n, n)),
            _const_spec((n, 2 * n, 2 * n)),
            _const_spec((m * m, m * m)),
            pl.BlockSpec((1, n, n, F_WIDTH), lambda b: (b, 0, 0, 0)),
        ],
        out_specs=pl.BlockSpec((2, 1, n, n, F_WIDTH), lambda b: (0, b, 0, 0, 0)),
        out_shape=jax.ShapeDtypeStruct((2, B, n, n, F_WIDTH), BF16),
        scratch_shapes=[pltpu.VMEM((n, 2, n, F_WIDTH), BF16)],
        compiler_params=_params(1),
        name="position_dft",
    )(w_rows, w_cols, perm, u4)


def _out_ffn_kernel(x_ref, ona_ref, y_ref, om_ref, gate_ref, wch_ref, wna_ref, wf_ref, wmo_ref,
                    wout_ref, g2_ref, w1_ref, w2_ref, o_ref, *, ff_chunk):
    yr, yi = y_ref[0], y_ref[1]
    groups = []
    for g in range(F_GROUPS):
        sl = slice(g * F_GROUP_DIM, (g + 1) * F_GROUP_DIM)
        groups.append(_dot(jnp.concatenate([yr[:, sl], yi[:, sl]], axis=1), wch_ref[...]))
    yf = jnp.concatenate(groups, axis=1).astype(BF16)

    gate = lambda br: gate_ref[:, br * D_MODEL:(br + 1) * D_MODEL].astype(F32)
    merged = gate(0) * _dot(ona_ref[...], wna_ref[...])
    merged = merged + gate(1) * _dot(yf, wf_ref[...])
    merged = merged + gate(2) * _dot(om_ref[...], wmo_ref[...])
    x1 = x_ref[...] + _dot(merged.astype(BF16), wout_ref[...])

    ms = jnp.mean(x1 * x1, axis=-1, keepdims=True)
    h2 = (x1 * lax.rsqrt(ms + EPS) * g2_ref[...]).astype(BF16)
    acc = x1
    for j in range(D_FF // ff_chunk):
        sl = slice(j * ff_chunk, (j + 1) * ff_chunk)
        a = jnp.maximum(_dot(h2, w1_ref[:, sl]), 0.0)
        acc = acc + _dot((a * a).astype(BF16), w2_ref[sl, :])
    o_ref[...] = acc


def _out_ffn(x2, ona, y3, om, gates, w_chan, w_na_o, w_f, w_mem_o, w_out, norm2_g, w_ff1, w_ff2,
             tm, ff_chunk):
    T = x2.shape[0]
    tok = lambda w: pl.BlockSpec((tm, w), lambda i: (i, 0))
    return pl.pallas_call(
        functools.partial(_out_ffn_kernel, ff_chunk=ff_chunk),
        grid=(T // tm,),
        in_specs=[
            tok(D_MODEL),
            tok(NA_WIDTH),
            pl.BlockSpec((2, tm, F_WIDTH), lambda i: (0, i, 0)),
            tok(MEM_WIDTH),
            tok(N_BRANCHES * D_MODEL),
            _const_spec((2 * F_GROUP_DIM, F_GROUP_DIM)),
            _const_spec((NA_WIDTH, D_MODEL)),
            _const_spec((F_WIDTH, D_MODEL)),
            _const_spec((MEM_WIDTH, D_MODEL)),
            _const_spec((D_MODEL, D_MODEL)),
            _const_spec((1, D_MODEL)),
            _const_spec((D_MODEL, D_FF)),
            _const_spec((D_FF, D_MODEL)),
        ],
        out_specs=tok(D_MODEL),
        out_shape=jax.ShapeDtypeStruct((T, D_MODEL), F32),
        compiler_params=_params(1),
        name="out_ffn",
    )(x2, ona, y3, om, gates, w_chan, w_na_o, w_f, w_mem_o, w_out, norm2_g.reshape(1, D_MODEL),
      w_ff1, w_ff2)


def kernel(x, mem, norm1_g, w_in, b_gate, na_q_g, na_k_g, na_rpb, w_na_o, w_f, mem_norm_g,
           w_mem_kv, mem_q_g, mem_k_g, w_mem_o, w_out, norm2_g, w_ff1, w_ff2):
    B, S, _ = x.shape
    T = B * S
    rows = S // GRID_W
    assert rows == GRID_W and S % NA_PAIR_TOKENS == 0
    x2 = x.reshape(T, D_MODEL)
    gmean_mem = _group_mean_matrix(MEM_WIDTH, MEM_HEAD_DIM)
    w_rows, w_cols, perm, w_chan = _dft_tables()

    memkT, memv = _mem_kv(mem, mem_norm_g, w_mem_kv, gmean_mem, mem_k_g)
    later_weights = (w_na_o, w_f, w_mem_o, w_out, w_ff1, w_ff2)
    q, kT4, v, fu, om, gates, bias, *narrow = _in_proj(
        x2, norm1_g, w_in, b_gate, na_q_g, na_k_g, mem_q_g, memkT, memv,
        _na_bias_blocks(na_rpb, rows), later_weights, S, tm=512)
    w_na_o, w_f, w_mem_o, w_out, w_ff1, w_ff2 = narrow

    ona = _na_attention(q.reshape(B, S, NA_WIDTH), kT4, v.reshape(B, S, NA_HEADS * V7X_LANES), bias,
                        pairs_per_step=8)

    y = _position_dft(fu.reshape(B, GRID_W, GRID_W, F_WIDTH), w_rows, w_cols, perm)
    y3 = y.reshape(2, T, F_WIDTH)

    out = _out_ffn(x2, ona.reshape(T, NA_WIDTH), y3, om, gates, w_chan, w_na_o, w_f, w_mem_o,
                   w_out, norm2_g, w_ff1, w_ff2, tm=512, ff_chunk=1024)
    return out.reshape(B, S, D_MODEL)
```

```python
import functools

import numpy as np
import jax
import jax.numpy as jnp
from jax import lax
from jax.experimental import pallas as pl
from jax.experimental.pallas import tpu as pltpu

D_MODEL = 1024
GRID_W = 64
NA_HEADS = 8
NA_HEAD_DIM = 64
NA_WIDTH = NA_HEADS * NA_HEAD_DIM
NA_WIN_ROWS = 8
NA_WIN_COLS = 16
F_GROUPS = 4
F_GROUP_DIM = 128
F_WIDTH = F_GROUPS * F_GROUP_DIM
MEM_HEADS = 4
MEM_HEAD_DIM = 128
MEM_WIDTH = MEM_HEADS * MEM_HEAD_DIM
N_BRANCHES = 3
D_FF = 4 * D_MODEL
EPS = 1e-6
NEG_INF = -1e30

V7X_LANES = 128
V7X_VMEM_LIMIT_BYTES = 60 * 1024 * 1024

NA_PAIR_TOKENS = 2 * GRID_W
NA_KEY_PAIRS = 5
NA_KEY_TOKENS = NA_KEY_PAIRS * NA_PAIR_TOKENS
NA_BIAS_CASES = 5
NA_PAIRS_PER_ITER = 2
NA_SCORES_AHEAD = 2

BF16 = jnp.bfloat16
F32 = jnp.float32


def _dot(a, b):
    return jnp.dot(a, b, preferred_element_type=F32)


def _dot_nt(a, b):
    return lax.dot_general(a, b, (((1,), (1,)), ((), ())), preferred_element_type=F32)


def _const_spec(shape):
    return pl.BlockSpec(shape, lambda *_: (0,) * len(shape), pipeline_mode=pl.Buffered(1))


def _params(n_axes, flags=None):
    return pltpu.CompilerParams(
        dimension_semantics=("arbitrary",) * n_axes,
        vmem_limit_bytes=V7X_VMEM_LIMIT_BYTES,
        flags=flags,
    )


def _group_mean_matrix(width, group):
    idx = np.arange(width) // group
    return jnp.asarray((idx[:, None] == idx[None, :]).astype(np.float32) / group, dtype=BF16)


def _mem_kv_kernel(mem_ref, g_ref, w_ref, gm_ref, kg_ref, kT_ref, v_ref):
    m = mem_ref[0]
    ms = jnp.mean(m * m, axis=-1, keepdims=True)
    mn = (m * lax.rsqrt(ms + EPS) * g_ref[...]).astype(BF16)
    kv = _dot(mn, w_ref[...])
    k = kv[:, :MEM_WIDTH]
    msk = _dot((k * k).astype(BF16), gm_ref[...])
    kn = k * lax.rsqrt(msk + EPS) * kg_ref[...]
    kT_ref[0] = kn.T.astype(BF16)
    v_ref[0] = kv[:, MEM_WIDTH:].astype(BF16)


def _mem_kv(mem, mem_norm_g, w_mem_kv, gmean_mem, mem_k_g):
    B, M, _ = mem.shape
    return pl.pallas_call(
        _mem_kv_kernel,
        grid=(B,),
        in_specs=[
            pl.BlockSpec((1, M, D_MODEL), lambda b: (b, 0, 0)),
            _const_spec((1, D_MODEL)),
            _const_spec((D_MODEL, 2 * MEM_WIDTH)),
            _const_spec((MEM_WIDTH, MEM_WIDTH)),
            _const_spec((1, MEM_WIDTH)),
        ],
        out_specs=[
            pl.BlockSpec((1, MEM_WIDTH, M), lambda b: (b, 0, 0)),
            pl.BlockSpec((1, M, MEM_WIDTH), lambda b: (b, 0, 0)),
        ],
        out_shape=[
            jax.ShapeDtypeStruct((B, MEM_WIDTH, M), BF16),
            jax.ShapeDtypeStruct((B, M, MEM_WIDTH), BF16),
        ],
        compiler_params=_params(1),
        name="mem_kv",
    )(mem, mem_norm_g.reshape(1, D_MODEL), w_mem_kv.astype(BF16), gmean_mem,
      jnp.tile(mem_k_g, MEM_HEADS).reshape(1, MEM_WIDTH))


def _in_proj_kernel(x_ref, g1_ref, wq_ref, wkT_ref, wv_ref, wf_ref, wm_ref, wg_ref, bg_ref,
                    qg_ref, kg_ref, mqg_ref, memkT_ref, memv_ref, blo_ref, bhi_ref, *rest,
                    n_narrow, bias_dr):
    wide_refs = rest[:n_narrow]
    q_ref, kT_ref, v_ref, fu_ref, om_ref, gate_ref, bias_ref = rest[n_narrow:n_narrow + 7]

    x = x_ref[...]
    ms = jnp.mean(x * x, axis=-1, keepdims=True)
    h = (x * lax.rsqrt(ms + EPS) * g1_ref[...]).astype(BF16)

    def gate_dot(br):
        sl = slice(br * D_MODEL, (br + 1) * D_MODEL)
        zg = _dot(h, wg_ref[:, sl]) + bg_ref[:, sl]
        gate_ref[:, sl] = jax.nn.sigmoid(zg).astype(BF16)

    zm = _dot(h, wm_ref[...])
    mem_scores = []
    for hd in range(MEM_HEADS):
        sl = slice(hd * MEM_HEAD_DIM, (hd + 1) * MEM_HEAD_DIM)
        zh = zm[:, sl]
        msm = jnp.mean(zh * zh, axis=-1, keepdims=True)
        qm = (zh * lax.rsqrt(msm + EPS) * mqg_ref[:, sl]).astype(BF16)
        if hd == 0:
            gate_dot(0)
        mem_scores.append(_dot(qm, memkT_ref[0, sl, :]))
    gate_dot(1)

    zq = _dot(h, wq_ref[...])
    low_half = lax.broadcasted_iota(jnp.int32, (zq.shape[0], V7X_LANES), 1) < NA_HEAD_DIM
    for hp in range(NA_HEADS // 2):
        sl = slice(hp * V7X_LANES, (hp + 1) * V7X_LANES)
        zz = zq[:, sl] * zq[:, sl]
        ms_lo = jnp.sum(jnp.where(low_half, zz, 0.0), axis=-1, keepdims=True)
        ms_hi = jnp.sum(jnp.where(low_half, 0.0, zz), axis=-1, keepdims=True)
        ms = jnp.where(low_half, ms_lo, ms_hi) * (1.0 / NA_HEAD_DIM)
        q_ref[:, sl] = (zq[:, sl] * lax.rsqrt(ms + EPS) * qg_ref[:, sl]).astype(BF16)

    zkT = _dot_nt(wkT_ref[...], h)
    zk3 = zkT.reshape(NA_HEADS, NA_HEAD_DIM, zkT.shape[1])
    msk = jnp.mean(zk3 * zk3, axis=1, keepdims=True)
    kT = ((zk3 * lax.rsqrt(msk + EPS)).reshape(zkT.shape) * kg_ref[...]).astype(BF16)
    for j in range(kT_ref.shape[1]):
        kT_ref[0, j] = kT[:, j * V7X_LANES:(j + 1) * V7X_LANES]

    zv = _dot(h, wv_ref[...])
    low_half = lax.broadcasted_iota(jnp.int32, (zv.shape[0], V7X_LANES), 1) < NA_HEAD_DIM
    for hp in range(NA_HEADS // 2):
        pair = zv[:, hp * V7X_LANES:(hp + 1) * V7X_LANES]
        v_ref[:, (2 * hp) * V7X_LANES:(2 * hp + 1) * V7X_LANES] = jnp.where(low_half, pair, 1.0).astype(BF16)
        v_ref[:, (2 * hp + 1) * V7X_LANES:(2 * hp + 2) * V7X_LANES] = jnp.where(low_half, 1.0, pair).astype(BF16)
    fu_ref[...] = _dot(h, wf_ref[...]).astype(BF16)
    gate_dot(2)

    outs = []
    for hd in range(MEM_HEADS):
        sl = slice(hd * MEM_HEAD_DIM, (hd + 1) * MEM_HEAD_DIM)
        s = mem_scores[hd]
        e = jnp.exp(s - jnp.max(s, axis=-1, keepdims=True))
        l = jnp.sum(e, axis=-1, keepdims=True)
        outs.append(_dot(e.astype(BF16), memv_ref[0, :, sl]) / l)
    om_ref[...] = jnp.concatenate(outs, axis=1).astype(BF16)

    for wide_ref, narrow_ref in zip(wide_refs, rest[n_narrow + 7:]):
        narrow_ref[...] = wide_ref[...].astype(BF16)

    @pl.when(pl.program_id(0) < NA_HEADS)
    def _():
        _assemble_na_bias(blo_ref, bhi_ref, bias_ref, bias_dr)


def _in_proj(x2, norm1_g, w_in, b_gate, na_q_g, na_k_g, mem_q_g,
             memkT, memv, bias_blocks, later_weights, tokens_per_batch, tm):
    T = x2.shape[0]
    bias_lo, bias_hi, bias_dr = bias_blocks
    n_cases = bias_dr.shape[0]
    M = memv.shape[1]
    tiles_per_batch = tokens_per_batch // tm
    slabs = tm // V7X_LANES
    B = T // tokens_per_batch
    wb = w_in.astype(BF16)
    o = 0
    wq = wb[:, o:o + NA_WIDTH]; o += NA_WIDTH
    wkT = wb[:, o:o + NA_WIDTH].T; o += NA_WIDTH
    wv = wb[:, o:o + NA_WIDTH]; o += NA_WIDTH
    wf = wb[:, o:o + F_WIDTH]; o += F_WIDTH
    wm = wb[:, o:o + MEM_WIDTH]; o += MEM_WIDTH
    wg = wb[:, o:]
    na_scale = np.float32(1.0 / np.sqrt(NA_HEAD_DIM))
    mem_scale = np.float32(1.0 / np.sqrt(MEM_HEAD_DIM))
    qg = (jnp.tile(na_q_g, NA_HEADS) * na_scale).reshape(1, NA_WIDTH)
    kg = jnp.tile(na_k_g, NA_HEADS).reshape(NA_WIDTH, 1)
    mqg = (jnp.tile(mem_q_g, MEM_HEADS) * mem_scale).reshape(1, MEM_WIDTH)
    tok = lambda w: pl.BlockSpec((tm, w), lambda i: (i, 0))
    batch_of = lambda i: i // tiles_per_batch
    steps = T // tm
    assert steps >= NA_HEADS
    slab_specs = [pl.BlockSpec((w.shape[0] // steps, w.shape[1]), lambda i: (i, 0))
                  for w in later_weights]
    head_of = lambda i: jnp.minimum(i, NA_HEADS - 1)
    half_spec = pl.BlockSpec((1,) + bias_lo.shape[1:], lambda i: (head_of(i), 0, 0, 0))
    return pl.pallas_call(
        functools.partial(_in_proj_kernel, n_narrow=len(later_weights), bias_dr=bias_dr),
        grid=(steps,),
        in_specs=[
            tok(D_MODEL),
            _const_spec((1, D_MODEL)),
            _const_spec((D_MODEL, NA_WIDTH)),
            _const_spec((NA_WIDTH, D_MODEL)),
            _const_spec((D_MODEL, NA_WIDTH)),
            _const_spec((D_MODEL, F_WIDTH)),
            _const_spec((D_MODEL, MEM_WIDTH)),
            _const_spec((D_MODEL, N_BRANCHES * D_MODEL)),
            _const_spec((1, N_BRANCHES * D_MODEL)),
            _const_spec((1, NA_WIDTH)),
            _const_spec((NA_WIDTH, 1)),
            _const_spec((1, MEM_WIDTH)),
            pl.BlockSpec((1, MEM_WIDTH, M), lambda i: (batch_of(i), 0, 0)),
            pl.BlockSpec((1, M, MEM_WIDTH), lambda i: (batch_of(i), 0, 0)),
            half_spec,
            half_spec,
        ] + slab_specs,
        out_specs=[
            tok(NA_WIDTH),
            pl.BlockSpec((1, slabs, NA_WIDTH, V7X_LANES),
                         lambda i: (batch_of(i), i % tiles_per_batch, 0, 0)),
            tok(NA_HEADS * V7X_LANES),
            tok(F_WIDTH),
            tok(MEM_WIDTH),
            tok(N_BRANCHES * D_MODEL),
            pl.BlockSpec((n_cases, 1, NA_PAIR_TOKENS, NA_KEY_TOKENS), lambda i: (0, head_of(i), 0, 0)),
        ] + slab_specs,
        out_shape=[
            jax.ShapeDtypeStruct((T, NA_WIDTH), BF16),
            jax.ShapeDtypeStruct((B, tokens_per_batch // V7X_LANES, NA_WIDTH, V7X_LANES), BF16),
            jax.ShapeDtypeStruct((T, NA_HEADS * V7X_LANES), BF16),
            jax.ShapeDtypeStruct((T, F_WIDTH), BF16),
            jax.ShapeDtypeStruct((T, MEM_WIDTH), BF16),
            jax.ShapeDtypeStruct((T, N_BRANCHES * D_MODEL), BF16),
            jax.ShapeDtypeStruct((n_cases, NA_HEADS, NA_PAIR_TOKENS, NA_KEY_TOKENS), F32),
        ] + [jax.ShapeDtypeStruct(w.shape, BF16) for w in later_weights],
        compiler_params=_params(1),
        name="in_proj",
    )(x2, norm1_g.reshape(1, D_MODEL), wq, wkT, wv, wf, wm, wg,
      b_gate.reshape(1, N_BRANCHES * D_MODEL), qg, kg, mqg, memkT, memv, bias_lo, bias_hi,
      *later_weights)


def _na_bias_blocks(rpb, rows):
    n_pairs = rows // 2
    wr = min(NA_WIN_ROWS, rows)
    reps = np.array([0, 1, 2, n_pairs - 2, n_pairs - 1])
    starts = np.clip(reps - 2, 0, n_pairs - NA_KEY_PAIRS)
    n_dr, n_dc = 2 * NA_WIN_ROWS - 1, 2 * NA_WIN_COLS - 1
    c = np.arange(GRID_W)[:, None]
    kc = np.arange(2 * GRID_W)[None, :] % GRID_W
    cs = np.clip(c - NA_WIN_COLS // 2, 0, GRID_W - NA_WIN_COLS)
    col_valid = (kc >= cs) & (kc < cs + NA_WIN_COLS)
    dc = np.clip(kc - c, -(NA_WIN_COLS - 1), NA_WIN_COLS - 1) + (NA_WIN_COLS - 1)
    onehot = (np.arange(n_dc)[:, None] == dc.reshape(1, -1)).astype(np.float32)
    cols = jnp.dot(rpb.astype(F32).reshape(NA_HEADS * n_dr, n_dc), onehot,
                   precision=lax.Precision.HIGHEST)
    cols = jnp.where(col_valid.reshape(1, -1), cols, NEG_INF)
    cols = cols.reshape(NA_HEADS, n_dr, GRID_W, 2 * GRID_W)
    cols = jnp.concatenate([cols, jnp.full((NA_HEADS, 1, GRID_W, 2 * GRID_W), NEG_INF, F32)], axis=1)
    low = (np.arange(2 * GRID_W) < GRID_W)[None, None, None, :]
    lo = jnp.where(low, cols, 0.0)
    hi = jnp.where(low, 0.0, cols)
    r = (2 * reps[:, None] + np.arange(2)[None, :])[:, :, None]
    kr = (2 * starts[:, None] + np.arange(2 * NA_KEY_PAIRS)[None, :])[:, None, :]
    rs = np.clip(r - wr // 2, 0, rows - wr)
    row_valid = (kr >= rs) & (kr < rs + wr)
    dr = np.where(row_valid, kr - r + (NA_WIN_ROWS - 1), n_dr)
    assert dr.min() >= 0 and dr.max() <= n_dr
    return lo, hi, dr


def _assemble_na_bias(lo_ref, hi_ref, o_ref, dr):
    for case in range(dr.shape[0]):
        for rr in range(2):
            for j in range(NA_KEY_PAIRS):
                tile = lo_ref[0, int(dr[case, rr, 2 * j])] + hi_ref[0, int(dr[case, rr, 2 * j + 1])]
                o_ref[case, 0, rr * GRID_W:(rr + 1) * GRID_W,
                      j * V7X_LANES:(j + 1) * V7X_LANES] = tile


def _na_kernel(q_ref, kT_ref, v_ref, bias_ref, o_ref, *, pairs_per_step, n_pairs):
    lane = lax.broadcasted_iota(jnp.int32, (NA_PAIR_TOKENS, 2 * NA_HEAD_DIM), 1)
    first_head = lane < NA_HEAD_DIM

    def pair_coords(i):
        p = pl.program_id(1) * pairs_per_step + i
        start = jnp.clip(p - 2, 0, n_pairs - NA_KEY_PAIRS)
        case = jnp.where(p < 2, p, jnp.where(p >= n_pairs - 2, p - (n_pairs - NA_BIAS_CASES), 2))
        qrow = pl.multiple_of(i * NA_PAIR_TOKENS, NA_PAIR_TOKENS)
        krow = pl.multiple_of(start * NA_PAIR_TOKENS, NA_PAIR_TOKENS)
        return start, case, qrow, krow

    def scores(coords, hd):
        start, case, qrow, _ = coords
        hsl = slice(hd * NA_HEAD_DIM, (hd + 1) * NA_HEAD_DIM)
        qh = q_ref[0, pl.ds(qrow, NA_PAIR_TOKENS), hsl]
        kslabs = kT_ref[0, pl.ds(start, NA_KEY_PAIRS), hsl, :]
        kT = jnp.concatenate([kslabs[j] for j in range(NA_KEY_PAIRS)], axis=1)
        return _dot(qh, kT) + bias_ref[case, hd]

    def body(it, carry):
        coords = [pair_coords(it * NA_PAIRS_PER_ITER + j) for j in range(NA_PAIRS_PER_ITER)]
        items = [(j, hd) for j in range(NA_PAIRS_PER_ITER) for hd in range(NA_HEADS)]
        pending = [scores(coords[j], hd) for j, hd in items[:NA_SCORES_AHEAD]]
        halves = []
        for n, (j, hd) in enumerate(items):
            s = pending.pop(0)
            if n + NA_SCORES_AHEAD < len(items):
                jn, hn = items[n + NA_SCORES_AHEAD]
                pending.append(scores(coords[jn], hn))
            _, _, qrow, krow = coords[j]
            vext = v_ref[0, pl.ds(krow, NA_KEY_TOKENS), hd * 2 * NA_HEAD_DIM:(hd + 1) * 2 * NA_HEAD_DIM]
            e = jnp.exp(s - jnp.max(s, axis=-1, keepdims=True)).astype(BF16)
            r = _dot(e, vext)
            halves.append(r / pltpu.roll(r, NA_HEAD_DIM, axis=1))
            if hd == NA_HEADS - 1:
                outs = [jnp.where(first_head, halves[2 * k], halves[2 * k + 1])
                        for k in range(NA_HEADS // 2)]
                o_ref[0, pl.ds(qrow, NA_PAIR_TOKENS), :] = jnp.concatenate(outs, axis=1).astype(BF16)
                halves = []
        return carry

    lax.fori_loop(0, pairs_per_step // NA_PAIRS_PER_ITER, body, 0)


def _na_attention(q, kT4, v, bias, pairs_per_step):
    B, S, _ = q.shape
    n_pairs = S // NA_PAIR_TOKENS
    steps = n_pairs // pairs_per_step
    tq = pairs_per_step * NA_PAIR_TOKENS
    return pl.pallas_call(
        functools.partial(_na_kernel, pairs_per_step=pairs_per_step, n_pairs=n_pairs),
        grid=(B, steps),
        in_specs=[
            pl.BlockSpec((1, tq, NA_WIDTH), lambda b, j: (b, j, 0)),
            pl.BlockSpec((1, n_pairs, NA_WIDTH, V7X_LANES), lambda b, j: (b, 0, 0, 0)),
            pl.BlockSpec((1, S, NA_HEADS * V7X_LANES), lambda b, j: (b, 0, 0)),
            _const_spec((NA_BIAS_CASES, NA_HEADS, NA_PAIR_TOKENS, NA_KEY_TOKENS)),
        ],
        out_specs=pl.BlockSpec((1, tq, NA_WIDTH), lambda b, j: (b, j, 0)),
        out_shape=jax.ShapeDtypeStruct((B, S, NA_WIDTH), BF16),
        compiler_params=_params(2),
        name="na_attn",
    )(q, kT4, v, bias)


DFT_CHUNK = 16


def _dft_tables():
    n, m = GRID_W, DFT_CHUNK
    a = np.arange(n)
    ang1 = 2.0 * np.pi * np.outer(a, a) / n
    w_ri = np.stack([np.cos(ang1), -np.sin(ang1)], axis=1) / np.sqrt(n)
    eye = np.eye(m)
    w_rows = w_ri.reshape(2 * n, n)
    perm = np.einsum('dx,cy->dcyx', eye, eye).reshape(m * m, m * m)
    c = np.arange(n)[:, None, None]
    d = np.arange(n)[None, :, None]
    b = np.arange(n)[None, None, :]
    ang2 = 2.0 * np.pi * ((c + n * d) * b % (n * n)) / (n * n)
    gr, gi = np.cos(ang2) / np.sqrt(n), -np.sin(ang2) / np.sqrt(n)
    w_cols = np.concatenate([np.concatenate([gr, -gi], axis=2),
                             np.concatenate([gi, gr], axis=2)], axis=1)
    ch = np.arange(F_GROUP_DIM)
    ang3 = 2.0 * np.pi * np.outer(ch, ch) / F_GROUP_DIM
    w_chan = np.concatenate([np.cos(ang3), np.sin(ang3)], axis=0) / np.sqrt(F_GROUP_DIM)
    return tuple(jnp.asarray(t, dtype=F32).astype(BF16) for t in (w_rows, w_cols, perm, w_chan))


def _dft_kernel(wr_ref, wc_ref, perm_ref, u_ref, y_ref, z_ref):
    n, m = GRID_W, DFT_CHUNK

    def rows_stage(j, carry):
        col0 = pl.multiple_of(j * m, m)
        x = u_ref[0, :, pl.ds(col0, m), :].reshape(n * m, F_WIDTH)
        xp = [_dot(perm_ref[...], x[g * m * m:(g + 1) * m * m]).astype(BF16) for g in range(n // m)]
        zb = []
        for b in range(m):
            xb = jnp.concatenate([xp[g][b * m:(b + 1) * m] for g in range(n // m)], axis=0)
            zb.append(_dot(wr_ref[...], xb).astype(BF16))
        for t in range(2 * n // m):
            st = jnp.concatenate([zb[b][t * m:(t + 1) * m] for b in range(m)], axis=0)
            z = _dot(perm_ref[...], st).astype(BF16)
            z_ref[t * 8:(t + 1) * 8, :, pl.ds(col0, m), :] = z.reshape(8, 2, m, F_WIDTH)
        return carry

    def cols_stage(j, carry):
        col0 = pl.multiple_of(j * m, m)
        res = []
        for c_lo in range(m):
            c = j * m + c_lo
            data = z_ref[c].reshape(2 * n, F_WIDTH)
            res.append(_dot(wc_ref[c], data).astype(BF16))
        for ri in range(2):
            for g in range(n // m):
                lo = ri * n + g * m
                stacked = jnp.concatenate([r[lo:lo + m] for r in res], axis=0)
                y = _dot(perm_ref[...], stacked).astype(BF16)
                y_ref[ri, 0, g * m:(g + 1) * m, pl.ds(col0, m), :] = y.reshape(m, m, F_WIDTH)
        return carry

    lax.fori_loop(0, n // m, rows_stage, 0)
    lax.fori_loop(0, n // m, cols_stage, 0)


def _position_dft(u4, w_rows, w_cols, perm):
    B, n, m = u4.shape[0], GRID_W, DFT_CHUNK
    return pl.pallas_call(
        _dft_kernel,
        grid=(B,),
        in_specs=[
            _const_spec((2 * n, n)),
            _const_spec((n, 2 * n, 2 * n)),
            _const_spec((m * m, m * m)),
            pl.BlockSpec((1, n, n, F_WIDTH), lambda b: (b, 0, 0, 0)),
        ],
        out_specs=pl.BlockSpec((2, 1, n, n, F_WIDTH), lambda b: (0, b, 0, 0, 0)),
        out_shape=jax.ShapeDtypeStruct((2, B, n, n, F_WIDTH), BF16),
        scratch_shapes=[pltpu.VMEM((n, 2, n, F_WIDTH), BF16)],
        compiler_params=_params(1),
        name="position_dft",
    )(w_rows, w_cols, perm, u4)


def _out_ffn_kernel(x_ref, ona_ref, y_ref, om_ref, gate_ref, wch_ref, wna_ref, wf_ref, wmo_ref,
                    wout_ref, g2_ref, w1_ref, w2_ref, o_ref, *, ff_chunk):
    yr, yi = y_ref[0], y_ref[1]
    groups = []
    for g in range(F_GROUPS):
        sl = slice(g * F_GROUP_DIM, (g + 1) * F_GROUP_DIM)
        groups.append(_dot(jnp.concatenate([yr[:, sl], yi[:, sl]], axis=1), wch_ref[...]))
    yf = jnp.concatenate(groups, axis=1).astype(BF16)

    gate = lambda br: gate_ref[:, br * D_MODEL:(br + 1) * D_MODEL].astype(F32)
    merged = gate(0) * _dot(ona_ref[...], wna_ref[...])
    merged = merged + gate(1) * _dot(yf, wf_ref[...])
    merged = merged + gate(2) * _dot(om_ref[...], wmo_ref[...])
    x1 = x_ref[...] + _dot(merged.astype(BF16), wout_ref[...])

    ms = jnp.mean(x1 * x1, axis=-1, keepdims=True)
    h2 = (x1 * lax.rsqrt(ms + EPS) * g2_ref[...]).astype(BF16)
    acc = x1
    for j in range(D_FF // ff_chunk):
        sl = slice(j * ff_chunk, (j + 1) * ff_chunk)
        a = jnp.maximum(_dot(h2, w1_ref[:, sl]), 0.0)
        acc = acc + _dot((a * a).astype(BF16), w2_ref[sl, :])
    o_ref[...] = acc


def _out_ffn(x2, ona, y3, om, gates, w_chan, w_na_o, w_f, w_mem_o, w_out, norm2_g, w_ff1, w_ff2,
             tm, ff_chunk):
    T = x2.shape[0]
    tok = lambda w: pl.BlockSpec((tm, w), lambda i: (i, 0))
    return pl.pallas_call(
        functools.partial(_out_ffn_kernel, ff_chunk=ff_chunk),
        grid=(T // tm,),
        in_specs=[
            tok(D_MODEL),
            tok(NA_WIDTH),
            pl.BlockSpec((2, tm, F_WIDTH), lambda i: (0, i, 0)),
            tok(MEM_WIDTH),
            tok(N_BRANCHES * D_MODEL),
            _const_spec((2 * F_GROUP_DIM, F_GROUP_DIM)),
            _const_spec((NA_WIDTH, D_MODEL)),
            _const_spec((F_WIDTH, D_MODEL)),
            _const_spec((MEM_WIDTH, D_MODEL)),
            _const_spec((D_MODEL, D_MODEL)),
            _const_spec((1, D_MODEL)),
            _const_spec((D_MODEL, D_FF)),
            _const_spec((D_FF, D_MODEL)),
        ],
        out_specs=tok(D_MODEL),
        out_shape=jax.ShapeDtypeStruct((T, D_MODEL), F32),
        compiler_params=_params(1),
        name="out_ffn",
    )(x2, ona, y3, om, gates, w_chan, w_na_o, w_f, w_mem_o, w_out, norm2_g.reshape(1, D_MODEL),
      w_ff1, w_ff2)


def kernel(x, mem, norm1_g, w_in, b_gate, na_q_g, na_k_g, na_rpb, w_na_o, w_f, mem_norm_g,
           w_mem_kv, mem_q_g, mem_k_g, w_mem_o, w_out, norm2_g, w_ff1, w_ff2):
    B, S, _ = x.shape
    T = B * S
    rows = S // GRID_W
    assert rows == GRID_W and S % NA_PAIR_TOKENS == 0
    x2 = x.reshape(T, D_MODEL)
    gmean_mem = _group_mean_matrix(MEM_WIDTH, MEM_HEAD_DIM)
    w_rows, w_cols, perm, w_chan = _dft_tables()

    memkT, memv = _mem_kv(mem, mem_norm_g, w_mem_kv, gmean_mem, mem_k_g)
    later_weights = (w_na_o, w_f, w_mem_o, w_out, w_ff1, w_ff2)
    q, kT4, v, fu, om, gates, bias, *narrow = _in_proj(
        x2, norm1_g, w_in, b_gate, na_q_g, na_k_g, mem_q_g, memkT, memv,
        _na_bias_blocks(na_rpb, rows), later_weights, S, tm=512)
    w_na_o, w_f, w_mem_o, w_out, w_ff1, w_ff2 = narrow

    ona = _na_attention(q.reshape(B, S, NA_WIDTH), kT4, v.reshape(B, S, NA_HEADS * V7X_LANES), bias,
                        pairs_per_step=8)

    y = _position_dft(fu.reshape(B, GRID_W, GRID_W, F_WIDTH), w_rows, w_cols, perm)
    y3 = y.reshape(2, T, F_WIDTH)

    out = _out_ffn(x2, ona.reshape(T, NA_WIDTH), y3, om, gates, w_chan, w_na_o, w_f, w_mem_o,
                   w_out, norm2_g, w_ff1, w_ff2, tm=512, ff_chunk=1024)
    return out.reshape(B, S, D_MODEL)
```

```python
import functools

import numpy as np
import jax
import jax.numpy as jnp
from jax import lax
from jax.experimental import pallas as pl
from jax.experimental.pallas import tpu as pltpu

D_MODEL = 1024
GRID_W = 64
NA_HEADS = 8
NA_HEAD_DIM = 64
NA_WIDTH = NA_HEADS * NA_HEAD_DIM
NA_WIN_ROWS = 8
NA_WIN_COLS = 16
F_GROUPS = 4
F_GROUP_DIM = 128
F_WIDTH = F_GROUPS * F_GROUP_DIM
MEM_HEADS = 4
MEM_HEAD_DIM = 128
MEM_WIDTH = MEM_HEADS * MEM_HEAD_DIM
N_BRANCHES = 3
D_FF = 4 * D_MODEL
EPS = 1e-6
NEG_INF = -1e30

V7X_LANES = 128
V7X_VMEM_LIMIT_BYTES = 60 * 1024 * 1024

NA_PAIR_TOKENS = 2 * GRID_W
NA_KEY_PAIRS = 5
NA_KEY_TOKENS = NA_KEY_PAIRS * NA_PAIR_TOKENS
NA_BIAS_CASES = 5
NA_PAIRS_PER_ITER = 2
NA_SCORES_AHEAD = 1

IN_PROJ_TOKENS = 512
OUT_FFN_TOKENS = 512
FF_CHUNK = 1024
NA_PAIRS_PER_STEP = 8

BF16 = jnp.bfloat16
F32 = jnp.float32


def _dot(a, b):
    return jnp.dot(a, b, preferred_element_type=F32)


def _dot_nt(a, b):
    return lax.dot_general(a, b, (((1,), (1,)), ((), ())), preferred_element_type=F32)


def _const_spec(shape):
    return pl.BlockSpec(shape, lambda *_: (0,) * len(shape), pipeline_mode=pl.Buffered(1))


def _params(n_axes):
    return pltpu.CompilerParams(
        dimension_semantics=("arbitrary",) * n_axes,
        vmem_limit_bytes=V7X_VMEM_LIMIT_BYTES,
    )


def _group_mean_matrix(width, group):
    idx = np.arange(width) // group
    return jnp.asarray((idx[:, None] == idx[None, :]).astype(np.float32) / group, dtype=BF16)


def _mem_kv_kernel(mem_ref, g_ref, w_ref, gm_ref, kg_ref, kT_ref, v_ref):
    m = mem_ref[0]
    ms = jnp.mean(m * m, axis=-1, keepdims=True)
    mn = (m * lax.rsqrt(ms + EPS) * g_ref[...]).astype(BF16)
    kv = _dot(mn, w_ref[...])
    k = kv[:, :MEM_WIDTH]
    msk = _dot((k * k).astype(BF16), gm_ref[...])
    kn = k * lax.rsqrt(msk + EPS) * kg_ref[...]
    kT_ref[0] = kn.T.astype(BF16)
    v_ref[0] = kv[:, MEM_WIDTH:].astype(BF16)


def _mem_kv(mem, mem_norm_g, w_mem_kv, gmean_mem, mem_k_g):
    B, M, _ = mem.shape
    return pl.pallas_call(
        _mem_kv_kernel,
        grid=(B,),
        in_specs=[
            pl.BlockSpec((1, M, D_MODEL), lambda b: (b, 0, 0)),
            _const_spec((1, D_MODEL)),
            _const_spec((D_MODEL, 2 * MEM_WIDTH)),
            _const_spec((MEM_WIDTH, MEM_WIDTH)),
            _const_spec((1, MEM_WIDTH)),
        ],
        out_specs=[
            pl.BlockSpec((1, MEM_WIDTH, M), lambda b: (b, 0, 0)),
            pl.BlockSpec((1, M, MEM_WIDTH), lambda b: (b, 0, 0)),
        ],
        out_shape=[
            jax.ShapeDtypeStruct((B, MEM_WIDTH, M), BF16),
            jax.ShapeDtypeStruct((B, M, MEM_WIDTH), BF16),
        ],
        compiler_params=_params(1),
        name="mem_kv",
    )(mem, mem_norm_g.reshape(1, D_MODEL), w_mem_kv.astype(BF16), gmean_mem,
      jnp.tile(mem_k_g, MEM_HEADS).reshape(1, MEM_WIDTH))


def _in_proj_kernel(x_ref, g1_ref, wq_ref, wkT_ref, wv_ref, wf_ref, wm_ref, wg_ref, bg_ref,
                    qg_ref, kg_ref, mqg_ref, memkT_ref, memv_ref, blo_ref, bhi_ref, *rest,
                    n_narrow, bias_dr):
    wide_refs = rest[:n_narrow]
    q_ref, kT_ref, v_ref, fu_ref, om_ref, gate_ref, bias_ref = rest[n_narrow:n_narrow + 7]

    x = x_ref[...]
    ms = jnp.mean(x * x, axis=-1, keepdims=True)
    h = (x * lax.rsqrt(ms + EPS) * g1_ref[...]).astype(BF16)

    def gate_dot(br):
        sl = slice(br * D_MODEL, (br + 1) * D_MODEL)
        zg = _dot(h, wg_ref[:, sl]) + bg_ref[:, sl]
        gate_ref[:, sl] = jax.nn.sigmoid(zg).astype(BF16)

    zm = _dot(h, wm_ref[...])
    mem_scores = []
    for hd in range(MEM_HEADS):
        sl = slice(hd * MEM_HEAD_DIM, (hd + 1) * MEM_HEAD_DIM)
        zh = zm[:, sl]
        msm = jnp.mean(zh * zh, axis=-1, keepdims=True)
        qm = (zh * lax.rsqrt(msm + EPS) * mqg_ref[:, sl]).astype(BF16)
        if hd == 0:
            gate_dot(0)
        mem_scores.append(_dot(qm, memkT_ref[0, sl, :]))
    gate_dot(1)

    zq = _dot(h, wq_ref[...])
    low_half = lax.broadcasted_iota(jnp.int32, (zq.shape[0], V7X_LANES), 1) < NA_HEAD_DIM
    for hp in range(NA_HEADS // 2):
        sl = slice(hp * V7X_LANES, (hp + 1) * V7X_LANES)
        zz = zq[:, sl] * zq[:, sl]
        ms_lo = jnp.sum(jnp.where(low_half, zz, 0.0), axis=-1, keepdims=True)
        ms_hi = jnp.sum(jnp.where(low_half, 0.0, zz), axis=-1, keepdims=True)
        ms = jnp.where(low_half, ms_lo, ms_hi) * (1.0 / NA_HEAD_DIM)
        q_ref[:, sl] = (zq[:, sl] * lax.rsqrt(ms + EPS) * qg_ref[:, sl]).astype(BF16)

    zkT = _dot_nt(wkT_ref[...], h)
    zk3 = zkT.reshape(NA_HEADS, NA_HEAD_DIM, zkT.shape[1])
    msk = jnp.mean(zk3 * zk3, axis=1, keepdims=True)
    kT = ((zk3 * lax.rsqrt(msk + EPS)).reshape(zkT.shape) * kg_ref[...]).astype(BF16)
    for j in range(kT_ref.shape[1]):
        kT_ref[0, j] = kT[:, j * V7X_LANES:(j + 1) * V7X_LANES]

    zv = _dot(h, wv_ref[...])
    low_half = lax.broadcasted_iota(jnp.int32, (zv.shape[0], V7X_LANES), 1) < NA_HEAD_DIM
    for hp in range(NA_HEADS // 2):
        pair = zv[:, hp * V7X_LANES:(hp + 1) * V7X_LANES]
        v_ref[:, (2 * hp) * V7X_LANES:(2 * hp + 1) * V7X_LANES] = jnp.where(low_half, pair, 1.0).astype(BF16)
        v_ref[:, (2 * hp + 1) * V7X_LANES:(2 * hp + 2) * V7X_LANES] = jnp.where(low_half, 1.0, pair).astype(BF16)
    fu_ref[...] = _dot(h, wf_ref[...]).astype(BF16)
    gate_dot(2)

    outs = []
    for hd in range(MEM_HEADS):
        sl = slice(hd * MEM_HEAD_DIM, (hd + 1) * MEM_HEAD_DIM)
        s = mem_scores[hd]
        e = jnp.exp(s - jnp.max(s, axis=-1, keepdims=True))
        l = jnp.sum(e, axis=-1, keepdims=True)
        outs.append(_dot(e.astype(BF16), memv_ref[0, :, sl]) / l)
    om_ref[...] = jnp.concatenate(outs, axis=1).astype(BF16)

    for wide_ref, narrow_ref in zip(wide_refs, rest[n_narrow + 7:]):
        narrow_ref[...] = wide_ref[...].astype(BF16)

    @pl.when(pl.program_id(0) < NA_HEADS)
    def _():
        _assemble_na_bias(blo_ref, bhi_ref, bias_ref, bias_dr)


def _in_proj(x2, norm1_g, w_in, b_gate, na_q_g, na_k_g, mem_q_g,
             memkT, memv, bias_blocks, later_weights, tokens_per_batch, tm):
    T = x2.shape[0]
    bias_lo, bias_hi, bias_dr = bias_blocks
    n_cases = bias_dr.shape[0]
    M = memv.shape[1]
    tiles_per_batch = tokens_per_batch // tm
    slabs = tm // V7X_LANES
    B = T // tokens_per_batch
    wb = w_in.astype(BF16)
    o = 0
    wq = wb[:, o:o + NA_WIDTH]; o += NA_WIDTH
    wkT = wb[:, o:o + NA_WIDTH].T; o += NA_WIDTH
    wv = wb[:, o:o + NA_WIDTH]; o += NA_WIDTH
    wf = wb[:, o:o + F_WIDTH]; o += F_WIDTH
    wm = wb[:, o:o + MEM_WIDTH]; o += MEM_WIDTH
    wg = wb[:, o:]
    na_scale = np.float32(1.0 / np.sqrt(NA_HEAD_DIM))
    mem_scale = np.float32(1.0 / np.sqrt(MEM_HEAD_DIM))
    qg = (jnp.tile(na_q_g, NA_HEADS) * na_scale).reshape(1, NA_WIDTH)
    kg = jnp.tile(na_k_g, NA_HEADS).reshape(NA_WIDTH, 1)
    mqg = (jnp.tile(mem_q_g, MEM_HEADS) * mem_scale).reshape(1, MEM_WIDTH)
    tok = lambda w: pl.BlockSpec((tm, w), lambda i: (i, 0))
    batch_of = lambda i: i // tiles_per_batch
    steps = T // tm
    assert steps >= NA_HEADS
    slab_specs = [pl.BlockSpec((w.shape[0] // steps, w.shape[1]), lambda i: (i, 0))
                  for w in later_weights]
    head_of = lambda i: jnp.minimum(i, NA_HEADS - 1)
    half_spec = pl.BlockSpec((1,) + bias_lo.shape[1:], lambda i: (head_of(i), 0, 0, 0))
    return pl.pallas_call(
        functools.partial(_in_proj_kernel, n_narrow=len(later_weights), bias_dr=bias_dr),
        grid=(steps,),
        in_specs=[
            tok(D_MODEL),
            _const_spec((1, D_MODEL)),
            _const_spec((D_MODEL, NA_WIDTH)),
            _const_spec((NA_WIDTH, D_MODEL)),
            _const_spec((D_MODEL, NA_WIDTH)),
            _const_spec((D_MODEL, F_WIDTH)),
            _const_spec((D_MODEL, MEM_WIDTH)),
            _const_spec((D_MODEL, N_BRANCHES * D_MODEL)),
            _const_spec((1, N_BRANCHES * D_MODEL)),
            _const_spec((1, NA_WIDTH)),
            _const_spec((NA_WIDTH, 1)),
            _const_spec((1, MEM_WIDTH)),
            pl.BlockSpec((1, MEM_WIDTH, M), lambda i: (batch_of(i), 0, 0)),
            pl.BlockSpec((1, M, MEM_WIDTH), lambda i: (batch_of(i), 0, 0)),
            half_spec,
            half_spec,
        ] + slab_specs,
        out_specs=[
            tok(NA_WIDTH),
            pl.BlockSpec((1, slabs, NA_WIDTH, V7X_LANES),
                         lambda i: (batch_of(i), i % tiles_per_batch, 0, 0)),
            tok(NA_HEADS * V7X_LANES),
            tok(F_WIDTH),
            tok(MEM_WIDTH),
            tok(N_BRANCHES * D_MODEL),
            pl.BlockSpec((n_cases, 1, NA_PAIR_TOKENS, NA_KEY_TOKENS), lambda i: (0, head_of(i), 0, 0)),
        ] + slab_specs,
        out_shape=[
            jax.ShapeDtypeStruct((T, NA_WIDTH), BF16),
            jax.ShapeDtypeStruct((B, tokens_per_batch // V7X_LANES, NA_WIDTH, V7X_LANES), BF16),
            jax.ShapeDtypeStruct((T, NA_HEADS * V7X_LANES), BF16),
            jax.ShapeDtypeStruct((T, F_WIDTH), BF16),
            jax.ShapeDtypeStruct((T, MEM_WIDTH), BF16),
            jax.ShapeDtypeStruct((T, N_BRANCHES * D_MODEL), BF16),
            jax.ShapeDtypeStruct((n_cases, NA_HEADS, NA_PAIR_TOKENS, NA_KEY_TOKENS), F32),
        ] + [jax.ShapeDtypeStruct(w.shape, BF16) for w in later_weights],
        compiler_params=_params(1),
        name="in_proj",
    )(x2, norm1_g.reshape(1, D_MODEL), wq, wkT, wv, wf, wm, wg,
      b_gate.reshape(1, N_BRANCHES * D_MODEL), qg, kg, mqg, memkT, memv, bias_lo, bias_hi,
      *later_weights)


def _na_bias_blocks(rpb, rows):
    n_pairs = rows // 2
    wr = min(NA_WIN_ROWS, rows)
    reps = np.array([0, 1, 2, n_pairs - 2, n_pairs - 1])
    starts = np.clip(reps - 2, 0, n_pairs - NA_KEY_PAIRS)
    n_dr, n_dc = 2 * NA_WIN_ROWS - 1, 2 * NA_WIN_COLS - 1
    c = np.arange(GRID_W)[:, None]
    kc = np.arange(2 * GRID_W)[None, :] % GRID_W
    cs = np.clip(c - NA_WIN_COLS // 2, 0, GRID_W - NA_WIN_COLS)
    col_valid = (kc >= cs) & (kc < cs + NA_WIN_COLS)
    dc = np.clip(kc - c, -(NA_WIN_COLS - 1), NA_WIN_COLS - 1) + (NA_WIN_COLS - 1)
    onehot = (np.arange(n_dc)[:, None] == dc.reshape(1, -1)).astype(np.float32)
    cols = jnp.dot(rpb.astype(F32).reshape(NA_HEADS * n_dr, n_dc), onehot,
                   precision=lax.Precision.HIGHEST)
    cols = jnp.where(col_valid.reshape(1, -1), cols, NEG_INF)
    cols = cols.reshape(NA_HEADS, n_dr, GRID_W, 2 * GRID_W)
    cols = jnp.concatenate([cols, jnp.full((NA_HEADS, 1, GRID_W, 2 * GRID_W), NEG_INF, F32)], axis=1)
    low = (np.arange(2 * GRID_W) < GRID_W)[None, None, None, :]
    lo = jnp.where(low, cols, 0.0)
    hi = jnp.where(low, 0.0, cols)
    r = (2 * reps[:, None] + np.arange(2)[None, :])[:, :, None]
    kr = (2 * starts[:, None] + np.arange(2 * NA_KEY_PAIRS)[None, :])[:, None, :]
    rs = np.clip(r - wr // 2, 0, rows - wr)
    row_valid = (kr >= rs) & (kr < rs + wr)
    dr = np.where(row_valid, kr - r + (NA_WIN_ROWS - 1), n_dr)
    assert dr.min() >= 0 and dr.max() <= n_dr
    return lo, hi, dr


def _assemble_na_bias(lo_ref, hi_ref, o_ref, dr):
    for case in range(dr.shape[0]):
        for rr in range(2):
            for j in range(NA_KEY_PAIRS):
                tile = lo_ref[0, int(dr[case, rr, 2 * j])] + hi_ref[0, int(dr[case, rr, 2 * j + 1])]
                o_ref[case, 0, rr * GRID_W:(rr + 1) * GRID_W,
                      j * V7X_LANES:(j + 1) * V7X_LANES] = tile


def _na_kernel(q_ref, kT_ref, v_ref, bias_ref, o_ref, *, pairs_per_step, n_pairs):
    lane = lax.broadcasted_iota(jnp.int32, (NA_PAIR_TOKENS, 2 * NA_HEAD_DIM), 1)
    first_head = lane < NA_HEAD_DIM

    def pair_coords(i):
        p = pl.program_id(1) * pairs_per_step + i
        start = jnp.clip(p - 2, 0, n_pairs - NA_KEY_PAIRS)
        case = jnp.where(p < 2, p, jnp.where(p >= n_pairs - 2, p - (n_pairs - NA_BIAS_CASES), 2))
        qrow = pl.multiple_of(i * NA_PAIR_TOKENS, NA_PAIR_TOKENS)
        krow = pl.multiple_of(start * NA_PAIR_TOKENS, NA_PAIR_TOKENS)
        return start, case, qrow, krow

    kzero = jnp.zeros((NA_HEAD_DIM, NA_KEY_TOKENS), BF16)
    vzero = jnp.zeros((NA_KEY_TOKENS, 2 * NA_HEAD_DIM), BF16)

    def scores(coords, hp):
        start, case, qrow, _ = coords
        psl = slice(hp * 2 * NA_HEAD_DIM, (hp + 1) * 2 * NA_HEAD_DIM)
        qp = q_ref[0, pl.ds(qrow, NA_PAIR_TOKENS), psl]
        kslabs = kT_ref[0, pl.ds(start, NA_KEY_PAIRS), psl, :]
        kT = jnp.concatenate([kslabs[j] for j in range(NA_KEY_PAIRS)], axis=1)
        keys = jnp.concatenate([jnp.concatenate([kT[:NA_HEAD_DIM], kzero], axis=1),
                                jnp.concatenate([kzero, kT[NA_HEAD_DIM:]], axis=1)], axis=0)
        bias = jnp.concatenate([bias_ref[case, 2 * hp], bias_ref[case, 2 * hp + 1]], axis=1)
        return _dot(qp, keys) + bias

    def body(it, carry):
        coords = [pair_coords(it * NA_PAIRS_PER_ITER + j) for j in range(NA_PAIRS_PER_ITER)]
        items = [(j, hp) for j in range(NA_PAIRS_PER_ITER) for hp in range(NA_HEADS // 2)]
        pending = [scores(coords[j], hp) for j, hp in items[:NA_SCORES_AHEAD]]
        outs = []
        for n, (j, hp) in enumerate(items):
            s = pending.pop(0)
            if n + NA_SCORES_AHEAD < len(items):
                jn, hn = items[n + NA_SCORES_AHEAD]
                pending.append(scores(coords[jn], hn))
            _, _, qrow, krow = coords[j]
            es, vs = [], []
            for sub in range(2):
                hd = 2 * hp + sub
                sh = s[:, sub * NA_KEY_TOKENS:(sub + 1) * NA_KEY_TOKENS]
                es.append(jnp.exp((sh - jnp.max(sh, axis=-1, keepdims=True)).astype(BF16)))
                vs.append(v_ref[0, pl.ds(krow, NA_KEY_TOKENS), hd * 2 * NA_HEAD_DIM:(hd + 1) * 2 * NA_HEAD_DIM])
            values = jnp.concatenate([jnp.concatenate([vs[0], vzero], axis=1),
                                      jnp.concatenate([vzero, vs[1]], axis=1)], axis=0)
            r = _dot(jnp.concatenate(es, axis=1), values)
            halves = [r[:, sub * 2 * NA_HEAD_DIM:(sub + 1) * 2 * NA_HEAD_DIM] for sub in range(2)]
            halves = [h / pltpu.roll(h, NA_HEAD_DIM, axis=1) for h in halves]
            outs.append(jnp.where(first_head, halves[0], halves[1]))
            if hp == NA_HEADS // 2 - 1:
                o_ref[0, pl.ds(qrow, NA_PAIR_TOKENS), :] = jnp.concatenate(outs, axis=1).astype(BF16)
                outs = []
        return carry

    lax.fori_loop(0, pairs_per_step // NA_PAIRS_PER_ITER, body, 0)


def _na_attention(q, kT4, v, bias, pairs_per_step):
    B, S, _ = q.shape
    n_pairs = S // NA_PAIR_TOKENS
    steps = n_pairs // pairs_per_step
    tq = pairs_per_step * NA_PAIR_TOKENS
    return pl.pallas_call(
        functools.partial(_na_kernel, pairs_per_step=pairs_per_step, n_pairs=n_pairs),
        grid=(B, steps),
        in_specs=[
            pl.BlockSpec((1, tq, NA_WIDTH), lambda b, j: (b, j, 0)),
            pl.BlockSpec((1, n_pairs, NA_WIDTH, V7X_LANES), lambda b, j: (b, 0, 0, 0)),
            pl.BlockSpec((1, S, NA_HEADS * V7X_LANES), lambda b, j: (b, 0, 0)),
            _const_spec((NA_BIAS_CASES, NA_HEADS, NA_PAIR_TOKENS, NA_KEY_TOKENS)),
        ],
        out_specs=pl.BlockSpec((1, tq, NA_WIDTH), lambda b, j: (b, j, 0)),
        out_shape=jax.ShapeDtypeStruct((B, S, NA_WIDTH), BF16),
        compiler_params=_params(2),
        name="na_attn",
    )(q, kT4, v, bias)


DFT_CHUNK = 16


def _dft_tables():
    n, m = GRID_W, DFT_CHUNK
    a = np.arange(n)
    ang1 = 2.0 * np.pi * np.outer(a, a) / n
    w_ri = np.stack([np.cos(ang1), -np.sin(ang1)], axis=1) / np.sqrt(n)
    eye = np.eye(m)
    w_rows = w_ri.reshape(2 * n, n)
    perm = np.einsum('dx,cy->dcyx', eye, eye).reshape(m * m, m * m)
    c = np.arange(n)[:, None, None]
    d = np.arange(n)[None, :, None]
    b = np.arange(n)[None, None, :]
    ang2 = 2.0 * np.pi * ((c + n * d) * b % (n * n)) / (n * n)
    gr, gi = np.cos(ang2) / np.sqrt(n), -np.sin(ang2) / np.sqrt(n)
    w_cols = np.concatenate([np.concatenate([gr, -gi], axis=2),
                             np.concatenate([gi, gr], axis=2)], axis=1)
    ch = np.arange(F_GROUP_DIM)
    ang3 = 2.0 * np.pi * np.outer(ch, ch) / F_GROUP_DIM
    w_chan = np.concatenate([np.cos(ang3), np.sin(ang3)], axis=0) / np.sqrt(F_GROUP_DIM)
    return tuple(jnp.asarray(t, dtype=F32).astype(BF16) for t in (w_rows, w_cols, perm, w_chan))


def _dft_kernel(wr_ref, wc_ref, perm_ref, u_ref, y_ref, z_ref):
    n, m = GRID_W, DFT_CHUNK

    def rows_stage(j, carry):
        col0 = pl.multiple_of(j * m, m)
        x = u_ref[0, :, pl.ds(col0, m), :].reshape(n * m, F_WIDTH)
        xp = [_dot(perm_ref[...], x[g * m * m:(g + 1) * m * m]).astype(BF16) for g in range(n // m)]
        zb = []
        for b in range(m):
            xb = jnp.concatenate([xp[g][b * m:(b + 1) * m] for g in range(n // m)], axis=0)
            zb.append(_dot(wr_ref[...], xb).astype(BF16))
        for t in range(2 * n // m):
            st = jnp.concatenate([zb[b][t * m:(t + 1) * m] for b in range(m)], axis=0)
            z = _dot(perm_ref[...], st).astype(BF16)
            z_ref[t * 8:(t + 1) * 8, :, pl.ds(col0, m), :] = z.reshape(8, 2, m, F_WIDTH)
        return carry

    def cols_stage(j, carry):
        col0 = pl.multiple_of(j * m, m)
        res = []
        for c_lo in range(m):
            c = j * m + c_lo
            data = z_ref[c].reshape(2 * n, F_WIDTH)
            res.append(_dot(wc_ref[c], data).astype(BF16))
        for ri in range(2):
            for g in range(n // m):
                lo = ri * n + g * m
                stacked = jnp.concatenate([r[lo:lo + m] for r in res], axis=0)
                y = _dot(perm_ref[...], stacked).astype(BF16)
                y_ref[ri, 0, g * m:(g + 1) * m, pl.ds(col0, m), :] = y.reshape(m, m, F_WIDTH)
        return carry

    lax.fori_loop(0, n // m, rows_stage, 0)
    lax.fori_loop(0, n // m, cols_stage, 0)


def _position_dft(u4, w_rows, w_cols, perm):
    B, n, m = u4.shape[0], GRID_W, DFT_CHUNK
    return pl.pallas_call(
        _dft_kernel,
        grid=(B,),
        in_specs=[
            _const_spec((2 * n, n)),
            _const_spec((n, 2 * n, 2 * n)),
            _const_spec((m * m, m * m)),
            pl.BlockSpec((1, n, n, F_WIDTH), lambda b: (b, 0, 0, 0)),
        ],
        out_specs=pl.BlockSpec((2, 1, n, n, F_WIDTH), lambda b: (0, b, 0, 0, 0)),
        out_shape=jax.ShapeDtypeStruct((2, B, n, n, F_WIDTH), BF16),
        scratch_shapes=[pltpu.VMEM((n, 2, n, F_WIDTH), BF16)],
        compiler_params=_params(1),
        name="position_dft",
    )(w_rows, w_cols, perm, u4)


def _out_ffn_kernel(x_ref, ona_ref, y_ref, om_ref, gate_ref, wch_ref, wna_ref, wf_ref, wmo_ref,
                    wout_ref, g2_ref, w1_ref, w2_ref, o_ref, *, ff_chunk):
    yr, yi = y_ref[0], y_ref[1]
    groups = []
    for g in range(F_GROUPS):
        sl = slice(g * F_GROUP_DIM, (g + 1) * F_GROUP_DIM)
        groups.append(_dot(jnp.concatenate([yr[:, sl], yi[:, sl]], axis=1), wch_ref[...]))
    yf = jnp.concatenate(groups, axis=1).astype(BF16)

    gate = lambda br: gate_ref[:, br * D_MODEL:(br + 1) * D_MODEL].astype(F32)
    merged = gate(0) * _dot(ona_ref[...], wna_ref[...])
    merged = merged + gate(1) * _dot(yf, wf_ref[...])
    merged = merged + gate(2) * _dot(om_ref[...], wmo_ref[...])
    x1 = x_ref[...] + _dot(merged.astype(BF16), wout_ref[...])

    ms = jnp.mean(x1 * x1, axis=-1, keepdims=True)
    h2 = (x1 * lax.rsqrt(ms + EPS) * g2_ref[...]).astype(BF16)
    acc = x1
    for j in range(D_FF // ff_chunk):
        sl = slice(j * ff_chunk, (j + 1) * ff_chunk)
        a = jnp.maximum(_dot(h2, w1_ref[:, sl]), 0.0)
        acc = acc + _dot((a * a).astype(BF16), w2_ref[sl, :])
    o_ref[...] = acc


def _out_ffn(x2, ona, y3, om, gates, w_chan, w_na_o, w_f, w_mem_o, w_out, norm2_g, w_ff1, w_ff2,
             tm, ff_chunk):
    T = x2.shape[0]
    tok = lambda w: pl.BlockSpec((tm, w), lambda i: (i, 0))
    return pl.pallas_call(
        functools.partial(_out_ffn_kernel, ff_chunk=ff_chunk),
        grid=(T // tm,),
        in_specs=[
            tok(D_MODEL),
            tok(NA_WIDTH),
            pl.BlockSpec((2, tm, F_WIDTH), lambda i: (0, i, 0)),
            tok(MEM_WIDTH),
            tok(N_BRANCHES * D_MODEL),
            _const_spec((2 * F_GROUP_DIM, F_GROUP_DIM)),
            _const_spec((NA_WIDTH, D_MODEL)),
            _const_spec((F_WIDTH, D_MODEL)),
            _const_spec((MEM_WIDTH, D_MODEL)),
            _const_spec((D_MODEL, D_MODEL)),
            _const_spec((1, D_MODEL)),
            _const_spec((D_MODEL, D_FF)),
            _const_spec((D_FF, D_MODEL)),
        ],
        out_specs=tok(D_MODEL),
        out_shape=jax.ShapeDtypeStruct((T, D_MODEL), F32),
        compiler_params=_params(1),
        name="out_ffn",
    )(x2, ona, y3, om, gates, w_chan, w_na_o, w_f, w_mem_o, w_out, norm2_g.reshape(1, D_MODEL),
      w_ff1, w_ff2)


def kernel(x, mem, norm1_g, w_in, b_gate, na_q_g, na_k_g, na_rpb, w_na_o, w_f, mem_norm_g,
           w_mem_kv, mem_q_g, mem_k_g, w_mem_o, w_out, norm2_g, w_ff1, w_ff2):
    B, S, _ = x.shape
    T = B * S
    rows = S // GRID_W
    assert rows == GRID_W and S % NA_PAIR_TOKENS == 0
    x2 = x.reshape(T, D_MODEL)
    gmean_mem = _group_mean_matrix(MEM_WIDTH, MEM_HEAD_DIM)
    w_rows, w_cols, perm, w_chan = _dft_tables()

    memkT, memv = _mem_kv(mem, mem_norm_g, w_mem_kv, gmean_mem, mem_k_g)
    later_weights = (w_na_o, w_f, w_mem_o, w_out, w_ff1, w_ff2)
    q, kT4, v, fu, om, gates, bias, *narrow = _in_proj(
        x2, norm1_g, w_in, b_gate, na_q_g, na_k_g, mem_q_g, memkT, memv,
        _na_bias_blocks(na_rpb, rows), later_weights, S, tm=IN_PROJ_TOKENS)
    w_na_o, w_f, w_mem_o, w_out, w_ff1, w_ff2 = narrow

    ona = _na_attention(q.reshape(B, S, NA_WIDTH), kT4, v.reshape(B, S, NA_HEADS * V7X_LANES), bias,
                        pairs_per_step=NA_PAIRS_PER_STEP)

    y = _position_dft(fu.reshape(B, GRID_W, GRID_W, F_WIDTH), w_rows, w_cols, perm)
    y3 = y.reshape(2, T, F_WIDTH)

    out = _out_ffn(x2, ona.reshape(T, NA_WIDTH), y3, om, gates, w_chan, w_na_o, w_f, w_mem_o,
                   w_out, norm2_g, w_ff1, w_ff2, tm=OUT_FFN_TOKENS, ff_chunk=FF_CHUNK)
    return out.reshape(B, S, D_MODEL)
```

```python
import functools

import numpy as np
import jax
import jax.numpy as jnp
from jax import lax
from jax.experimental import pallas as pl
from jax.experimental.pallas import tpu as pltpu

D_MODEL = 1024
GRID_W = 64
NA_HEADS = 8
NA_HEAD_DIM = 64
NA_WIDTH = NA_HEADS * NA_HEAD_DIM
NA_WIN_ROWS = 8
NA_WIN_COLS = 16
F_GROUPS = 4
F_GROUP_DIM = 128
F_WIDTH = F_GROUPS * F_GROUP_DIM
MEM_HEADS = 4
MEM_HEAD_DIM = 128
MEM_WIDTH = MEM_HEADS * MEM_HEAD_DIM
N_BRANCHES = 3
D_FF = 4 * D_MODEL
EPS = 1e-6
NEG_INF = -1e30

V7X_LANES = 128
V7X_VMEM_LIMIT_BYTES = 60 * 1024 * 1024

NA_PAIR_TOKENS = 2 * GRID_W
NA_KEY_PAIRS = 5
NA_KEY_TOKENS = NA_KEY_PAIRS * NA_PAIR_TOKENS
NA_BIAS_CASES = 5
NA_PAIRS_PER_ITER = 8
NA_SCORES_AHEAD = 1

IN_PROJ_TOKENS = 512
OUT_FFN_TOKENS = 512
FF_CHUNK = 1024
NA_PAIRS_PER_STEP = 8

BF16 = jnp.bfloat16
F32 = jnp.float32


def _dot(a, b):
    return jnp.dot(a, b, preferred_element_type=F32)


def _dot_nt(a, b):
    return lax.dot_general(a, b, (((1,), (1,)), ((), ())), preferred_element_type=F32)


def _const_spec(shape):
    return pl.BlockSpec(shape, lambda *_: (0,) * len(shape), pipeline_mode=pl.Buffered(1))


def _params(n_axes):
    return pltpu.CompilerParams(
        dimension_semantics=("arbitrary",) * n_axes,
        vmem_limit_bytes=V7X_VMEM_LIMIT_BYTES,
    )


def _group_mean_matrix(width, group):
    idx = np.arange(width) // group
    return jnp.asarray((idx[:, None] == idx[None, :]).astype(np.float32) / group, dtype=BF16)


def _mem_kv_kernel(mem_ref, g_ref, w_ref, gm_ref, kg_ref, kT_ref, v_ref):
    m = mem_ref[0]
    ms = jnp.mean(m * m, axis=-1, keepdims=True)
    mn = (m * lax.rsqrt(ms + EPS) * g_ref[...]).astype(BF16)
    kv = _dot(mn, w_ref[...])
    k = kv[:, :MEM_WIDTH]
    msk = _dot((k * k).astype(BF16), gm_ref[...])
    kn = k * lax.rsqrt(msk + EPS) * kg_ref[...]
    kT_ref[0] = kn.T.astype(BF16)
    v_ref[0] = kv[:, MEM_WIDTH:].astype(BF16)


def _mem_kv(mem, mem_norm_g, w_mem_kv, gmean_mem, mem_k_g):
    B, M, _ = mem.shape
    return pl.pallas_call(
        _mem_kv_kernel,
        grid=(B,),
        in_specs=[
            pl.BlockSpec((1, M, D_MODEL), lambda b: (b, 0, 0)),
            _const_spec((1, D_MODEL)),
            _const_spec((D_MODEL, 2 * MEM_WIDTH)),
            _const_spec((MEM_WIDTH, MEM_WIDTH)),
            _const_spec((1, MEM_WIDTH)),
        ],
        out_specs=[
            pl.BlockSpec((1, MEM_WIDTH, M), lambda b: (b, 0, 0)),
            pl.BlockSpec((1, M, MEM_WIDTH), lambda b: (b, 0, 0)),
        ],
        out_shape=[
            jax.ShapeDtypeStruct((B, MEM_WIDTH, M), BF16),
            jax.ShapeDtypeStruct((B, M, MEM_WIDTH), BF16),
        ],
        compiler_params=_params(1),
        name="mem_kv",
    )(mem, mem_norm_g.reshape(1, D_MODEL), w_mem_kv.astype(BF16), gmean_mem,
      jnp.tile(mem_k_g, MEM_HEADS).reshape(1, MEM_WIDTH))


def _in_proj_kernel(x_ref, g1_ref, wq_ref, wkT_ref, wv_ref, wf_ref, wm_ref, wg_ref, bg_ref,
                    qg_ref, kg_ref, mqg_ref, memkT_ref, memv_ref, blo_ref, bhi_ref, *rest,
                    n_narrow, bias_dr):
    wide_refs = rest[:n_narrow]
    q_ref, kT_ref, v_ref, fu_ref, om_ref, gate_ref, bias_ref = rest[n_narrow:n_narrow + 7]

    x = x_ref[...]
    ms = jnp.mean(x * x, axis=-1, keepdims=True)
    h = (x * lax.rsqrt(ms + EPS) * g1_ref[...]).astype(BF16)

    def gate_dot(br):
        sl = slice(br * D_MODEL, (br + 1) * D_MODEL)
        zg = _dot(h, wg_ref[:, sl]) + bg_ref[:, sl]
        gate_ref[:, sl] = jax.nn.sigmoid(zg).astype(BF16)

    zm = _dot(h, wm_ref[...])
    mem_scores = []
    for hd in range(MEM_HEADS):
        sl = slice(hd * MEM_HEAD_DIM, (hd + 1) * MEM_HEAD_DIM)
        zh = zm[:, sl]
        msm = jnp.mean(zh * zh, axis=-1, keepdims=True)
        qm = (zh * lax.rsqrt(msm + EPS) * mqg_ref[:, sl]).astype(BF16)
        if hd == 0:
            gate_dot(0)
        mem_scores.append(_dot(qm, memkT_ref[0, sl, :]))
    gate_dot(1)

    zq = _dot(h, wq_ref[...])
    low_half = lax.broadcasted_iota(jnp.int32, (zq.shape[0], V7X_LANES), 1) < NA_HEAD_DIM
    for hp in range(NA_HEADS // 2):
        sl = slice(hp * V7X_LANES, (hp + 1) * V7X_LANES)
        zz = zq[:, sl] * zq[:, sl]
        ms_lo = jnp.sum(jnp.where(low_half, zz, 0.0), axis=-1, keepdims=True)
        ms_hi = jnp.sum(jnp.where(low_half, 0.0, zz), axis=-1, keepdims=True)
        ms = jnp.where(low_half, ms_lo, ms_hi) * (1.0 / NA_HEAD_DIM)
        q_ref[:, sl] = (zq[:, sl] * lax.rsqrt(ms + EPS) * qg_ref[:, sl]).astype(BF16)

    zkT = _dot_nt(wkT_ref[...], h)
    zk3 = zkT.reshape(NA_HEADS, NA_HEAD_DIM, zkT.shape[1])
    msk = jnp.mean(zk3 * zk3, axis=1, keepdims=True)
    kT = ((zk3 * lax.rsqrt(msk + EPS)).reshape(zkT.shape) * kg_ref[...]).astype(BF16)
    for j in range(kT_ref.shape[1]):
        kT_ref[0, j] = kT[:, j * V7X_LANES:(j + 1) * V7X_LANES]

    zv = _dot(h, wv_ref[...])
    low_half = lax.broadcasted_iota(jnp.int32, (zv.shape[0], V7X_LANES), 1) < NA_HEAD_DIM
    for hp in range(NA_HEADS // 2):
        pair = zv[:, hp * V7X_LANES:(hp + 1) * V7X_LANES]
        v_ref[:, (2 * hp) * V7X_LANES:(2 * hp + 1) * V7X_LANES] = jnp.where(low_half, pair, 1.0).astype(BF16)
        v_ref[:, (2 * hp + 1) * V7X_LANES:(2 * hp + 2) * V7X_LANES] = jnp.where(low_half, 1.0, pair).astype(BF16)
    fu_ref[...] = _dot(h, wf_ref[...]).astype(BF16)
    gate_dot(2)

    outs = []
    for hd in range(MEM_HEADS):
        sl = slice(hd * MEM_HEAD_DIM, (hd + 1) * MEM_HEAD_DIM)
        s = mem_scores[hd]
        e = jnp.exp(s - jnp.max(s, axis=-1, keepdims=True))
        l = jnp.sum(e, axis=-1, keepdims=True)
        outs.append(_dot(e.astype(BF16), memv_ref[0, :, sl]) / l)
    om_ref[...] = jnp.concatenate(outs, axis=1).astype(BF16)

    for wide_ref, narrow_ref in zip(wide_refs, rest[n_narrow + 7:]):
        narrow_ref[...] = wide_ref[...].astype(BF16)

    @pl.when(pl.program_id(0) < NA_HEADS)
    def _():
        _assemble_na_bias(blo_ref, bhi_ref, bias_ref, bias_dr)


def _in_proj(x2, norm1_g, w_in, b_gate, na_q_g, na_k_g, mem_q_g,
             memkT, memv, bias_blocks, later_weights, tokens_per_batch, tm):
    T = x2.shape[0]
    bias_lo, bias_hi, bias_dr = bias_blocks
    n_cases = bias_dr.shape[0]
    M = memv.shape[1]
    tiles_per_batch = tokens_per_batch // tm
    slabs = tm // V7X_LANES
    B = T // tokens_per_batch
    wb = w_in.astype(BF16)
    o = 0
    wq = wb[:, o:o + NA_WIDTH]; o += NA_WIDTH
    wkT = wb[:, o:o + NA_WIDTH].T; o += NA_WIDTH
    wv = wb[:, o:o + NA_WIDTH]; o += NA_WIDTH
    wf = wb[:, o:o + F_WIDTH]; o += F_WIDTH
    wm = wb[:, o:o + MEM_WIDTH]; o += MEM_WIDTH
    wg = wb[:, o:]
    na_scale = np.float32(1.0 / np.sqrt(NA_HEAD_DIM))
    mem_scale = np.float32(1.0 / np.sqrt(MEM_HEAD_DIM))
    qg = (jnp.tile(na_q_g, NA_HEADS) * na_scale).reshape(1, NA_WIDTH)
    kg = jnp.tile(na_k_g, NA_HEADS).reshape(NA_WIDTH, 1)
    mqg = (jnp.tile(mem_q_g, MEM_HEADS) * mem_scale).reshape(1, MEM_WIDTH)
    tok = lambda w: pl.BlockSpec((tm, w), lambda i: (i, 0))
    batch_of = lambda i: i // tiles_per_batch
    steps = T // tm
    assert steps >= NA_HEADS
    slab_specs = [pl.BlockSpec((w.shape[0] // steps, w.shape[1]), lambda i: (i, 0))
                  for w in later_weights]
    head_of = lambda i: jnp.minimum(i, NA_HEADS - 1)
    half_spec = pl.BlockSpec((1,) + bias_lo.shape[1:], lambda i: (head_of(i), 0, 0, 0))
    return pl.pallas_call(
        functools.partial(_in_proj_kernel, n_narrow=len(later_weights), bias_dr=bias_dr),
        grid=(steps,),
        in_specs=[
            tok(D_MODEL),
            _const_spec((1, D_MODEL)),
            _const_spec((D_MODEL, NA_WIDTH)),
            _const_spec((NA_WIDTH, D_MODEL)),
            _const_spec((D_MODEL, NA_WIDTH)),
            _const_spec((D_MODEL, F_WIDTH)),
            _const_spec((D_MODEL, MEM_WIDTH)),
            _const_spec((D_MODEL, N_BRANCHES * D_MODEL)),
            _const_spec((1, N_BRANCHES * D_MODEL)),
            _const_spec((1, NA_WIDTH)),
            _const_spec((NA_WIDTH, 1)),
            _const_spec((1, MEM_WIDTH)),
            pl.BlockSpec((1, MEM_WIDTH, M), lambda i: (batch_of(i), 0, 0)),
            pl.BlockSpec((1, M, MEM_WIDTH), lambda i: (batch_of(i), 0, 0)),
            half_spec,
            half_spec,
        ] + slab_specs,
        out_specs=[
            tok(NA_WIDTH),
            pl.BlockSpec((1, slabs, NA_WIDTH, V7X_LANES),
                         lambda i: (batch_of(i), i % tiles_per_batch, 0, 0)),
            tok(NA_HEADS * V7X_LANES),
            tok(F_WIDTH),
            tok(MEM_WIDTH),
            tok(N_BRANCHES * D_MODEL),
            pl.BlockSpec((n_cases, 1, NA_PAIR_TOKENS, NA_KEY_TOKENS), lambda i: (0, head_of(i), 0, 0)),
        ] + slab_specs,
        out_shape=[
            jax.ShapeDtypeStruct((T, NA_WIDTH), BF16),
            jax.ShapeDtypeStruct((B, tokens_per_batch // V7X_LANES, NA_WIDTH, V7X_LANES), BF16),
            jax.ShapeDtypeStruct((T, NA_HEADS * V7X_LANES), BF16),
            jax.ShapeDtypeStruct((T, F_WIDTH), BF16),
            jax.ShapeDtypeStruct((T, MEM_WIDTH), BF16),
            jax.ShapeDtypeStruct((T, N_BRANCHES * D_MODEL), BF16),
            jax.ShapeDtypeStruct((n_cases, NA_HEADS, NA_PAIR_TOKENS, NA_KEY_TOKENS), F32),
        ] + [jax.ShapeDtypeStruct(w.shape, BF16) for w in later_weights],
        compiler_params=_params(1),
        name="in_proj",
    )(x2, norm1_g.reshape(1, D_MODEL), wq, wkT, wv, wf, wm, wg,
      b_gate.reshape(1, N_BRANCHES * D_MODEL), qg, kg, mqg, memkT, memv, bias_lo, bias_hi,
      *later_weights)


def _na_bias_blocks(rpb, rows):
    n_pairs = rows // 2
    wr = min(NA_WIN_ROWS, rows)
    reps = np.array([0, 1, 2, n_pairs - 2, n_pairs - 1])
    starts = np.clip(reps - 2, 0, n_pairs - NA_KEY_PAIRS)
    n_dr, n_dc = 2 * NA_WIN_ROWS - 1, 2 * NA_WIN_COLS - 1
    c = np.arange(GRID_W)[:, None]
    kc = np.arange(2 * GRID_W)[None, :] % GRID_W
    cs = np.clip(c - NA_WIN_COLS // 2, 0, GRID_W - NA_WIN_COLS)
    col_valid = (kc >= cs) & (kc < cs + NA_WIN_COLS)
    dc = np.clip(kc - c, -(NA_WIN_COLS - 1), NA_WIN_COLS - 1) + (NA_WIN_COLS - 1)
    onehot = (np.arange(n_dc)[:, None] == dc.reshape(1, -1)).astype(np.float32)
    cols = jnp.dot(rpb.astype(F32).reshape(NA_HEADS * n_dr, n_dc), onehot,
                   precision=lax.Precision.HIGHEST)
    cols = jnp.where(col_valid.reshape(1, -1), cols, NEG_INF)
    cols = cols.reshape(NA_HEADS, n_dr, GRID_W, 2 * GRID_W)
    cols = jnp.concatenate([cols, jnp.full((NA_HEADS, 1, GRID_W, 2 * GRID_W), NEG_INF, F32)], axis=1)
    low = (np.arange(2 * GRID_W) < GRID_W)[None, None, None, :]
    lo = jnp.where(low, cols, 0.0)
    hi = jnp.where(low, 0.0, cols)
    r = (2 * reps[:, None] + np.arange(2)[None, :])[:, :, None]
    kr = (2 * starts[:, None] + np.arange(2 * NA_KEY_PAIRS)[None, :])[:, None, :]
    rs = np.clip(r - wr // 2, 0, rows - wr)
    row_valid = (kr >= rs) & (kr < rs + wr)
    dr = np.where(row_valid, kr - r + (NA_WIN_ROWS - 1), n_dr)
    assert dr.min() >= 0 and dr.max() <= n_dr
    return lo, hi, dr


def _assemble_na_bias(lo_ref, hi_ref, o_ref, dr):
    for case in range(dr.shape[0]):
        for rr in range(2):
            for j in range(NA_KEY_PAIRS):
                tile = lo_ref[0, int(dr[case, rr, 2 * j])] + hi_ref[0, int(dr[case, rr, 2 * j + 1])]
                o_ref[case, 0, rr * GRID_W:(rr + 1) * GRID_W,
                      j * V7X_LANES:(j + 1) * V7X_LANES] = tile


def _na_kernel(q_ref, kT_ref, v_ref, bias_ref, o_ref, *, pairs_per_step, n_pairs):
    lane = lax.broadcasted_iota(jnp.int32, (NA_PAIR_TOKENS, 2 * NA_HEAD_DIM), 1)
    first_head = lane < NA_HEAD_DIM

    def pair_coords(i):
        p = pl.program_id(1) * pairs_per_step + i
        start = jnp.clip(p - 2, 0, n_pairs - NA_KEY_PAIRS)
        case = jnp.where(p < 2, p, jnp.where(p >= n_pairs - 2, p - (n_pairs - NA_BIAS_CASES), 2))
        qrow = pl.multiple_of(i * NA_PAIR_TOKENS, NA_PAIR_TOKENS)
        krow = pl.multiple_of(start * NA_PAIR_TOKENS, NA_PAIR_TOKENS)
        return start, case, qrow, krow

    kzero = jnp.zeros((NA_HEAD_DIM, NA_KEY_TOKENS), BF16)
    vzero = jnp.zeros((NA_KEY_TOKENS, 2 * NA_HEAD_DIM), BF16)

    def scores(coords, hp):
        start, case, qrow, _ = coords
        psl = slice(hp * 2 * NA_HEAD_DIM, (hp + 1) * 2 * NA_HEAD_DIM)
        qp = q_ref[0, pl.ds(qrow, NA_PAIR_TOKENS), psl]
        kslabs = kT_ref[0, pl.ds(start, NA_KEY_PAIRS), psl, :]
        kT = jnp.concatenate([kslabs[j] for j in range(NA_KEY_PAIRS)], axis=1)
        keys = jnp.concatenate([jnp.concatenate([kT[:NA_HEAD_DIM], kzero], axis=1),
                                jnp.concatenate([kzero, kT[NA_HEAD_DIM:]], axis=1)], axis=0)
        bias = jnp.concatenate([bias_ref[case, 2 * hp], bias_ref[case, 2 * hp + 1]], axis=1)
        return _dot(qp, keys) + bias

    def body(it, carry):
        coords = [pair_coords(it * NA_PAIRS_PER_ITER + j) for j in range(NA_PAIRS_PER_ITER)]
        items = [(j, hp) for j in range(NA_PAIRS_PER_ITER) for hp in range(NA_HEADS // 2)]
        pending = [scores(coords[j], hp) for j, hp in items[:NA_SCORES_AHEAD]]
        outs = []
        for n, (j, hp) in enumerate(items):
            s = pending.pop(0)
            if n + NA_SCORES_AHEAD < len(items):
                jn, hn = items[n + NA_SCORES_AHEAD]
                pending.append(scores(coords[jn], hn))
            _, _, qrow, krow = coords[j]
            es, vs = [], []
            for sub in range(2):
                hd = 2 * hp + sub
                sh = s[:, sub * NA_KEY_TOKENS:(sub + 1) * NA_KEY_TOKENS]
                es.append(jnp.exp((sh - jnp.max(sh, axis=-1, keepdims=True)).astype(BF16)))
                vs.append(v_ref[0, pl.ds(krow, NA_KEY_TOKENS), hd * 2 * NA_HEAD_DIM:(hd + 1) * 2 * NA_HEAD_DIM])
            values = jnp.concatenate([jnp.concatenate([vs[0], vzero], axis=1),
                                      jnp.concatenate([vzero, vs[1]], axis=1)], axis=0)
            r = _dot(jnp.concatenate(es, axis=1), values)
            halves = [r[:, sub * 2 * NA_HEAD_DIM:(sub + 1) * 2 * NA_HEAD_DIM] for sub in range(2)]
            halves = [h / pltpu.roll(h, NA_HEAD_DIM, axis=1) for h in halves]
            outs.append(jnp.where(first_head, halves[0], halves[1]))
            if hp == NA_HEADS // 2 - 1:
                o_ref[0, pl.ds(qrow, NA_PAIR_TOKENS), :] = jnp.concatenate(outs, axis=1).astype(BF16)
                outs = []
        return carry

    lax.fori_loop(0, pairs_per_step // NA_PAIRS_PER_ITER, body, 0)


def _na_attention(q, kT4, v, bias, pairs_per_step):
    B, S, _ = q.shape
    n_pairs = S // NA_PAIR_TOKENS
    steps = n_pairs // pairs_per_step
    tq = pairs_per_step * NA_PAIR_TOKENS
    return pl.pallas_call(
        functools.partial(_na_kernel, pairs_per_step=pairs_per_step, n_pairs=n_pairs),
        grid=(B, steps),
        in_specs=[
            pl.BlockSpec((1, tq, NA_WIDTH), lambda b, j: (b, j, 0)),
            pl.BlockSpec((1, n_pairs, NA_WIDTH, V7X_LANES), lambda b, j: (b, 0, 0, 0)),
            pl.BlockSpec((1, S, NA_HEADS * V7X_LANES), lambda b, j: (b, 0, 0)),
            _const_spec((NA_BIAS_CASES, NA_HEADS, NA_PAIR_TOKENS, NA_KEY_TOKENS)),
        ],
        out_specs=pl.BlockSpec((1, tq, NA_WIDTH), lambda b, j: (b, j, 0)),
        out_shape=jax.ShapeDtypeStruct((B, S, NA_WIDTH), BF16),
        compiler_params=_params(2),
        name="na_attn",
    )(q, kT4, v, bias)


DFT_CHUNK = 16


def _dft_tables():
    n, m = GRID_W, DFT_CHUNK
    a = np.arange(n)
    ang1 = 2.0 * np.pi * np.outer(a, a) / n
    w_ri = np.stack([np.cos(ang1), -np.sin(ang1)], axis=1) / np.sqrt(n)
    eye = np.eye(m)
    w_rows = w_ri.reshape(2 * n, n)
    perm = np.einsum('dx,cy->dcyx', eye, eye).reshape(m * m, m * m)
    c = np.arange(n)[:, None, None]
    d = np.arange(n)[None, :, None]
    b = np.arange(n)[None, None, :]
    ang2 = 2.0 * np.pi * ((c + n * d) * b % (n * n)) / (n * n)
    gr, gi = np.cos(ang2) / np.sqrt(n), -np.sin(ang2) / np.sqrt(n)
    w_cols = np.concatenate([np.concatenate([gr, -gi], axis=2),
                             np.concatenate([gi, gr], axis=2)], axis=1)
    ch = np.arange(F_GROUP_DIM)
    ang3 = 2.0 * np.pi * np.outer(ch, ch) / F_GROUP_DIM
    w_chan = np.concatenate([np.cos(ang3), np.sin(ang3)], axis=0) / np.sqrt(F_GROUP_DIM)
    return tuple(jnp.asarray(t, dtype=F32).astype(BF16) for t in (w_rows, w_cols, perm, w_chan))


def _dft_kernel(wr_ref, wc_ref, perm_ref, u_ref, y_ref, z_ref):
    n, m = GRID_W, DFT_CHUNK

    def rows_stage(j, carry):
        col0 = pl.multiple_of(j * m, m)
        x = u_ref[0, :, pl.ds(col0, m), :].reshape(n * m, F_WIDTH)
        xp = [_dot(perm_ref[...], x[g * m * m:(g + 1) * m * m]).astype(BF16) for g in range(n // m)]
        zb = []
        for b in range(m):
            xb = jnp.concatenate([xp[g][b * m:(b + 1) * m] for g in range(n // m)], axis=0)
            zb.append(_dot(wr_ref[...], xb).astype(BF16))
        for t in range(2 * n // m):
            st = jnp.concatenate([zb[b][t * m:(t + 1) * m] for b in range(m)], axis=0)
            z = _dot(perm_ref[...], st).astype(BF16)
            z_ref[t * 8:(t + 1) * 8, :, pl.ds(col0, m), :] = z.reshape(8, 2, m, F_WIDTH)
        return carry

    def cols_stage(j, carry):
        col0 = pl.multiple_of(j * m, m)
        res = []
        for c_lo in range(m):
            c = j * m + c_lo
            data = z_ref[c].reshape(2 * n, F_WIDTH)
            res.append(_dot(wc_ref[c], data).astype(BF16))
        for ri in range(2):
            for g in range(n // m):
                lo = ri * n + g * m
                stacked = jnp.concatenate([r[lo:lo + m] for r in res], axis=0)
                y = _dot(perm_ref[...], stacked).astype(BF16)
                y_ref[ri, 0, g * m:(g + 1) * m, pl.ds(col0, m), :] = y.reshape(m, m, F_WIDTH)
        return carry

    lax.fori_loop(0, n // m, rows_stage, 0)
    lax.fori_loop(0, n // m, cols_stage, 0)


def _position_dft(u4, w_rows, w_cols, perm):
    B, n, m = u4.shape[0], GRID_W, DFT_CHUNK
    return pl.pallas_call(
        _dft_kernel,
        grid=(B,),
        in_specs=[
            _const_spec((2 * n, n)),
            _const_spec((n, 2 * n, 2 * n)),
            _const_spec((m * m, m * m)),
            pl.BlockSpec((1, n, n, F_WIDTH), lambda b: (b, 0, 0, 0)),
        ],
        out_specs=pl.BlockSpec((2, 1, n, n, F_WIDTH), lambda b: (0, b, 0, 0, 0)),
        out_shape=jax.ShapeDtypeStruct((2, B, n, n, F_WIDTH), BF16),
        scratch_shapes=[pltpu.VMEM((n, 2, n, F_WIDTH), BF16)],
        compiler_params=_params(1),
        name="position_dft",
    )(w_rows, w_cols, perm, u4)


def _out_ffn_kernel(x_ref, ona_ref, y_ref, om_ref, gate_ref, wch_ref, wna_ref, wf_ref, wmo_ref,
                    wout_ref, g2_ref, w1_ref, w2_ref, o_ref, *, ff_chunk):
    yr, yi = y_ref[0], y_ref[1]
    groups = []
    for g in range(F_GROUPS):
        sl = slice(g * F_GROUP_DIM, (g + 1) * F_GROUP_DIM)
        groups.append(_dot(jnp.concatenate([yr[:, sl], yi[:, sl]], axis=1), wch_ref[...]))
    yf = jnp.concatenate(groups, axis=1).astype(BF16)

    gate = lambda br: gate_ref[:, br * D_MODEL:(br + 1) * D_MODEL].astype(F32)
    merged = gate(0) * _dot(ona_ref[...], wna_ref[...])
    merged = merged + gate(1) * _dot(yf, wf_ref[...])
    merged = merged + gate(2) * _dot(om_ref[...], wmo_ref[...])
    x1 = x_ref[...] + _dot(merged.astype(BF16), wout_ref[...])

    ms = jnp.mean(x1 * x1, axis=-1, keepdims=True)
    h2 = (x1 * lax.rsqrt(ms + EPS) * g2_ref[...]).astype(BF16)
    acc = x1
    for j in range(D_FF // ff_chunk):
        sl = slice(j * ff_chunk, (j + 1) * ff_chunk)
        a = jnp.maximum(_dot(h2, w1_ref[:, sl]), 0.0)
        acc = acc + _dot((a * a).astype(BF16), w2_ref[sl, :])
    o_ref[...] = acc


def _out_ffn(x2, ona, y3, om, gates, w_chan, w_na_o, w_f, w_mem_o, w_out, norm2_g, w_ff1, w_ff2,
             tm, ff_chunk):
    T = x2.shape[0]
    tok = lambda w: pl.BlockSpec((tm, w), lambda i: (i, 0))
    return pl.pallas_call(
        functools.partial(_out_ffn_kernel, ff_chunk=ff_chunk),
        grid=(T // tm,),
        in_specs=[
            tok(D_MODEL),
            tok(NA_WIDTH),
            pl.BlockSpec((2, tm, F_WIDTH), lambda i: (0, i, 0)),
            tok(MEM_WIDTH),
            tok(N_BRANCHES * D_MODEL),
            _const_spec((2 * F_GROUP_DIM, F_GROUP_DIM)),
            _const_spec((NA_WIDTH, D_MODEL)),
            _const_spec((F_WIDTH, D_MODEL)),
            _const_spec((MEM_WIDTH, D_MODEL)),
            _const_spec((D_MODEL, D_MODEL)),
            _const_spec((1, D_MODEL)),
            _const_spec((D_MODEL, D_FF)),
            _const_spec((D_FF, D_MODEL)),
        ],
        out_specs=tok(D_MODEL),
        out_shape=jax.ShapeDtypeStruct((T, D_MODEL), F32),
        compiler_params=_params(1),
        name="out_ffn",
    )(x2, ona, y3, om, gates, w_chan, w_na_o, w_f, w_mem_o, w_out, norm2_g.reshape(1, D_MODEL),
      w_ff1, w_ff2)


def kernel(x, mem, norm1_g, w_in, b_gate, na_q_g, na_k_g, na_rpb, w_na_o, w_f, mem_norm_g,
           w_mem_kv, mem_q_g, mem_k_g, w_mem_o, w_out, norm2_g, w_ff1, w_ff2):
    B, S, _ = x.shape
    T = B * S
    rows = S // GRID_W
    assert rows == GRID_W and S % NA_PAIR_TOKENS == 0
    x2 = x.reshape(T, D_MODEL)
    gmean_mem = _group_mean_matrix(MEM_WIDTH, MEM_HEAD_DIM)
    w_rows, w_cols, perm, w_chan = _dft_tables()

    memkT, memv = _mem_kv(mem, mem_norm_g, w_mem_kv, gmean_mem, mem_k_g)
    later_weights = (w_na_o, w_f, w_mem_o, w_out, w_ff1, w_ff2)
    q, kT4, v, fu, om, gates, bias, *narrow = _in_proj(
        x2, norm1_g, w_in, b_gate, na_q_g, na_k_g, mem_q_g, memkT, memv,
        _na_bias_blocks(na_rpb, rows), later_weights, S, tm=IN_PROJ_TOKENS)
    w_na_o, w_f, w_mem_o, w_out, w_ff1, w_ff2 = narrow

    ona = _na_attention(q.reshape(B, S, NA_WIDTH), kT4, v.reshape(B, S, NA_HEADS * V7X_LANES), bias,
                        pairs_per_step=NA_PAIRS_PER_STEP)

    y = _position_dft(fu.reshape(B, GRID_W, GRID_W, F_WIDTH), w_rows, w_cols, perm)
    y3 = y.reshape(2, T, F_WIDTH)

    out = _out_ffn(x2, ona.reshape(T, NA_WIDTH), y3, om, gates, w_chan, w_na_o, w_f, w_mem_o,
                   w_out, norm2_g, w_ff1, w_ff2, tm=OUT_FFN_TOKENS, ff_chunk=FF_CHUNK)
    return out.reshape(B, S, D_MODEL)
```

```python
import functools

import numpy as np
import jax
import jax.numpy as jnp
from jax import lax
from jax.experimental import pallas as pl
from jax.experimental.pallas import tpu as pltpu

D_MODEL = 1024
GRID_W = 64
NA_HEADS = 8
NA_HEAD_DIM = 64
NA_WIDTH = NA_HEADS * NA_HEAD_DIM
NA_WIN_ROWS = 8
NA_WIN_COLS = 16
F_GROUPS = 4
F_GROUP_DIM = 128
F_WIDTH = F_GROUPS * F_GROUP_DIM
MEM_HEADS = 4
MEM_HEAD_DIM = 128
MEM_WIDTH = MEM_HEADS * MEM_HEAD_DIM
N_BRANCHES = 3
D_FF = 4 * D_MODEL
EPS = 1e-6
NEG_INF = -1e30

V7X_LANES = 128
V7X_VMEM_LIMIT_BYTES = 60 * 1024 * 1024

NA_PAIR_TOKENS = 2 * GRID_W
NA_KEY_PAIRS = 5
NA_KEY_TOKENS = NA_KEY_PAIRS * NA_PAIR_TOKENS
NA_BIAS_CASES = 5
NA_PAIRS_PER_ITER = 16
NA_SCORES_AHEAD = 1

IN_PROJ_TOKENS = 512
OUT_FFN_TOKENS = 512
FF_CHUNK = 1024
NA_PAIRS_PER_STEP = 16

BF16 = jnp.bfloat16
F32 = jnp.float32


def _dot(a, b):
    return jnp.dot(a, b, preferred_element_type=F32)


def _dot_nt(a, b):
    return lax.dot_general(a, b, (((1,), (1,)), ((), ())), preferred_element_type=F32)


def _const_spec(shape):
    return pl.BlockSpec(shape, lambda *_: (0,) * len(shape), pipeline_mode=pl.Buffered(1))


def _params(n_axes):
    return pltpu.CompilerParams(
        dimension_semantics=("arbitrary",) * n_axes,
        vmem_limit_bytes=V7X_VMEM_LIMIT_BYTES,
    )


def _group_mean_matrix(width, group):
    idx = np.arange(width) // group
    return jnp.asarray((idx[:, None] == idx[None, :]).astype(np.float32) / group, dtype=BF16)


def _mem_kv_kernel(mem_ref, g_ref, w_ref, gm_ref, kg_ref, kT_ref, v_ref):
    m = mem_ref[0]
    ms = jnp.mean(m * m, axis=-1, keepdims=True)
    mn = (m * lax.rsqrt(ms + EPS) * g_ref[...]).astype(BF16)
    kv = _dot(mn, w_ref[...])
    k = kv[:, :MEM_WIDTH]
    msk = _dot((k * k).astype(BF16), gm_ref[...])
    kn = k * lax.rsqrt(msk + EPS) * kg_ref[...]
    kT_ref[0] = kn.T.astype(BF16)
    v_ref[0] = kv[:, MEM_WIDTH:].astype(BF16)


def _mem_kv(mem, mem_norm_g, w_mem_kv, gmean_mem, mem_k_g):
    B, M, _ = mem.shape
    return pl.pallas_call(
        _mem_kv_kernel,
        grid=(B,),
        in_specs=[
            pl.BlockSpec((1, M, D_MODEL), lambda b: (b, 0, 0)),
            _const_spec((1, D_MODEL)),
            _const_spec((D_MODEL, 2 * MEM_WIDTH)),
            _const_spec((MEM_WIDTH, MEM_WIDTH)),
            _const_spec((1, MEM_WIDTH)),
        ],
        out_specs=[
            pl.BlockSpec((1, MEM_WIDTH, M), lambda b: (b, 0, 0)),
            pl.BlockSpec((1, M, MEM_WIDTH), lambda b: (b, 0, 0)),
        ],
        out_shape=[
            jax.ShapeDtypeStruct((B, MEM_WIDTH, M), BF16),
            jax.ShapeDtypeStruct((B, M, MEM_WIDTH), BF16),
        ],
        compiler_params=_params(1),
        name="mem_kv",
    )(mem, mem_norm_g.reshape(1, D_MODEL), w_mem_kv.astype(BF16), gmean_mem,
      jnp.tile(mem_k_g, MEM_HEADS).reshape(1, MEM_WIDTH))


def _in_proj_kernel(x_ref, g1_ref, wq_ref, wkT_ref, wv_ref, wf_ref, wm_ref, wg_ref, bg_ref,
                    qg_ref, kg_ref, mqg_ref, memkT_ref, memv_ref, blo_ref, bhi_ref, *rest,
                    n_narrow, bias_dr):
    wide_refs = rest[:n_narrow]
    q_ref, kT_ref, v_ref, fu_ref, om_ref, gate_ref, bias_ref = rest[n_narrow:n_narrow + 7]

    x = x_ref[...]
    ms = jnp.mean(x * x, axis=-1, keepdims=True)
    h = (x * lax.rsqrt(ms + EPS) * g1_ref[...]).astype(BF16)

    def gate_dot(br):
        sl = slice(br * D_MODEL, (br + 1) * D_MODEL)
        zg = _dot(h, wg_ref[:, sl]) + bg_ref[:, sl]
        gate_ref[:, sl] = jax.nn.sigmoid(zg).astype(BF16)

    zm = _dot(h, wm_ref[...])
    mem_scores = []
    for hd in range(MEM_HEADS):
        sl = slice(hd * MEM_HEAD_DIM, (hd + 1) * MEM_HEAD_DIM)
        zh = zm[:, sl]
        msm = jnp.mean(zh * zh, axis=-1, keepdims=True)
        qm = (zh * lax.rsqrt(msm + EPS) * mqg_ref[:, sl]).astype(BF16)
        if hd == 0:
            gate_dot(0)
        mem_scores.append(_dot(qm, memkT_ref[0, sl, :]))
    gate_dot(1)

    zq = _dot(h, wq_ref[...])
    low_half = lax.broadcasted_iota(jnp.int32, (zq.shape[0], V7X_LANES), 1) < NA_HEAD_DIM
    for hp in range(NA_HEADS // 2):
        sl = slice(hp * V7X_LANES, (hp + 1) * V7X_LANES)
        zz = zq[:, sl] * zq[:, sl]
        ms_lo = jnp.sum(jnp.where(low_half, zz, 0.0), axis=-1, keepdims=True)
        ms_hi = jnp.sum(jnp.where(low_half, 0.0, zz), axis=-1, keepdims=True)
        ms = jnp.where(low_half, ms_lo, ms_hi) * (1.0 / NA_HEAD_DIM)
        q_ref[:, sl] = (zq[:, sl] * lax.rsqrt(ms + EPS) * qg_ref[:, sl]).astype(BF16)

    zkT = _dot_nt(wkT_ref[...], h)
    zk3 = zkT.reshape(NA_HEADS, NA_HEAD_DIM, zkT.shape[1])
    msk = jnp.mean(zk3 * zk3, axis=1, keepdims=True)
    kT = ((zk3 * lax.rsqrt(msk + EPS)).reshape(zkT.shape) * kg_ref[...]).astype(BF16)
    for j in range(kT_ref.shape[1]):
        kT_ref[0, j] = kT[:, j * V7X_LANES:(j + 1) * V7X_LANES]

    zv = _dot(h, wv_ref[...])
    low_half = lax.broadcasted_iota(jnp.int32, (zv.shape[0], V7X_LANES), 1) < NA_HEAD_DIM
    for hp in range(NA_HEADS // 2):
        pair = zv[:, hp * V7X_LANES:(hp + 1) * V7X_LANES]
        v_ref[:, (2 * hp) * V7X_LANES:(2 * hp + 1) * V7X_LANES] = jnp.where(low_half, pair, 1.0).astype(BF16)
        v_ref[:, (2 * hp + 1) * V7X_LANES:(2 * hp + 2) * V7X_LANES] = jnp.where(low_half, 1.0, pair).astype(BF16)
    fu_ref[...] = _dot(h, wf_ref[...]).astype(BF16)
    gate_dot(2)

    outs = []
    for hd in range(MEM_HEADS):
        sl = slice(hd * MEM_HEAD_DIM, (hd + 1) * MEM_HEAD_DIM)
        s = mem_scores[hd]
        e = jnp.exp(s - jnp.max(s, axis=-1, keepdims=True))
        l = jnp.sum(e, axis=-1, keepdims=True)
        outs.append(_dot(e.astype(BF16), memv_ref[0, :, sl]) / l)
    om_ref[...] = jnp.concatenate(outs, axis=1).astype(BF16)

    for wide_ref, narrow_ref in zip(wide_refs, rest[n_narrow + 7:]):
        narrow_ref[...] = wide_ref[...].astype(BF16)

    @pl.when(pl.program_id(0) < NA_HEADS)
    def _():
        _assemble_na_bias(blo_ref, bhi_ref, bias_ref, bias_dr)


def _in_proj(x2, norm1_g, w_in, b_gate, na_q_g, na_k_g, mem_q_g,
             memkT, memv, bias_blocks, later_weights, tokens_per_batch, tm):
    T = x2.shape[0]
    bias_lo, bias_hi, bias_dr = bias_blocks
    n_cases = bias_dr.shape[0]
    M = memv.shape[1]
    tiles_per_batch = tokens_per_batch // tm
    slabs = tm // V7X_LANES
    B = T // tokens_per_batch
    wb = w_in.astype(BF16)
    o = 0
    wq = wb[:, o:o + NA_WIDTH]; o += NA_WIDTH
    wkT = wb[:, o:o + NA_WIDTH].T; o += NA_WIDTH
    wv = wb[:, o:o + NA_WIDTH]; o += NA_WIDTH
    wf = wb[:, o:o + F_WIDTH]; o += F_WIDTH
    wm = wb[:, o:o + MEM_WIDTH]; o += MEM_WIDTH
    wg = wb[:, o:]
    na_scale = np.float32(1.0 / np.sqrt(NA_HEAD_DIM))
    mem_scale = np.float32(1.0 / np.sqrt(MEM_HEAD_DIM))
    qg = (jnp.tile(na_q_g, NA_HEADS) * na_scale).reshape(1, NA_WIDTH)
    kg = jnp.tile(na_k_g, NA_HEADS).reshape(NA_WIDTH, 1)
    mqg = (jnp.tile(mem_q_g, MEM_HEADS) * mem_scale).reshape(1, MEM_WIDTH)
    tok = lambda w: pl.BlockSpec((tm, w), lambda i: (i, 0))
    batch_of = lambda i: i // tiles_per_batch
    steps = T // tm
    assert steps >= NA_HEADS
    slab_specs = [pl.BlockSpec((w.shape[0] // steps, w.shape[1]), lambda i: (i, 0))
                  for w in later_weights]
    head_of = lambda i: jnp.minimum(i, NA_HEADS - 1)
    half_spec = pl.BlockSpec((1,) + bias_lo.shape[1:], lambda i: (head_of(i), 0, 0, 0))
    return pl.pallas_call(
        functools.partial(_in_proj_kernel, n_narrow=len(later_weights), bias_dr=bias_dr),
        grid=(steps,),
        in_specs=[
            tok(D_MODEL),
            _const_spec((1, D_MODEL)),
            _const_spec((D_MODEL, NA_WIDTH)),
            _const_spec((NA_WIDTH, D_MODEL)),
            _const_spec((D_MODEL, NA_WIDTH)),
            _const_spec((D_MODEL, F_WIDTH)),
            _const_spec((D_MODEL, MEM_WIDTH)),
            _const_spec((D_MODEL, N_BRANCHES * D_MODEL)),
            _const_spec((1, N_BRANCHES * D_MODEL)),
            _const_spec((1, NA_WIDTH)),
            _const_spec((NA_WIDTH, 1)),
            _const_spec((1, MEM_WIDTH)),
            pl.BlockSpec((1, MEM_WIDTH, M), lambda i: (batch_of(i), 0, 0)),
            pl.BlockSpec((1, M, MEM_WIDTH), lambda i: (batch_of(i), 0, 0)),
            half_spec,
            half_spec,
        ] + slab_specs,
        out_specs=[
            tok(NA_WIDTH),
            pl.BlockSpec((1, slabs, NA_WIDTH, V7X_LANES),
                         lambda i: (batch_of(i), i % tiles_per_batch, 0, 0)),
            tok(NA_HEADS * V7X_LANES),
            tok(F_WIDTH),
            tok(MEM_WIDTH),
            tok(N_BRANCHES * D_MODEL),
            pl.BlockSpec((n_cases, 1, NA_PAIR_TOKENS, NA_KEY_TOKENS), lambda i: (0, head_of(i), 0, 0)),
        ] + slab_specs,
        out_shape=[
            jax.ShapeDtypeStruct((T, NA_WIDTH), BF16),
            jax.ShapeDtypeStruct((B, tokens_per_batch // V7X_LANES, NA_WIDTH, V7X_LANES), BF16),
            jax.ShapeDtypeStruct((T, NA_HEADS * V7X_LANES), BF16),
            jax.ShapeDtypeStruct((T, F_WIDTH), BF16),
            jax.ShapeDtypeStruct((T, MEM_WIDTH), BF16),
            jax.ShapeDtypeStruct((T, N_BRANCHES * D_MODEL), BF16),
            jax.ShapeDtypeStruct((n_cases, NA_HEADS, NA_PAIR_TOKENS, NA_KEY_TOKENS), F32),
        ] + [jax.ShapeDtypeStruct(w.shape, BF16) for w in later_weights],
        compiler_params=_params(1),
        name="in_proj",
    )(x2, norm1_g.reshape(1, D_MODEL), wq, wkT, wv, wf, wm, wg,
      b_gate.reshape(1, N_BRANCHES * D_MODEL), qg, kg, mqg, memkT, memv, bias_lo, bias_hi,
      *later_weights)


def _na_bias_blocks(rpb, rows):
    n_pairs = rows // 2
    wr = min(NA_WIN_ROWS, rows)
    reps = np.array([0, 1, 2, n_pairs - 2, n_pairs - 1])
    starts = np.clip(reps - 2, 0, n_pairs - NA_KEY_PAIRS)
    n_dr, n_dc = 2 * NA_WIN_ROWS - 1, 2 * NA_WIN_COLS - 1
    c = np.arange(GRID_W)[:, None]
    kc = np.arange(2 * GRID_W)[None, :] % GRID_W
    cs = np.clip(c - NA_WIN_COLS // 2, 0, GRID_W - NA_WIN_COLS)
    col_valid = (kc >= cs) & (kc < cs + NA_WIN_COLS)
    dc = np.clip(kc - c, -(NA_WIN_COLS - 1), NA_WIN_COLS - 1) + (NA_WIN_COLS - 1)
    onehot = (np.arange(n_dc)[:, None] == dc.reshape(1, -1)).astype(np.float32)
    cols = jnp.dot(rpb.astype(F32).reshape(NA_HEADS * n_dr, n_dc), onehot,
                   precision=lax.Precision.HIGHEST)
    cols = jnp.where(col_valid.reshape(1, -1), cols, NEG_INF)
    cols = cols.reshape(NA_HEADS, n_dr, GRID_W, 2 * GRID_W)
    cols = jnp.concatenate([cols, jnp.full((NA_HEADS, 1, GRID_W, 2 * GRID_W), NEG_INF, F32)], axis=1)
    low = (np.arange(2 * GRID_W) < GRID_W)[None, None, None, :]
    lo = jnp.where(low, cols, 0.0)
    hi = jnp.where(low, 0.0, cols)
    r = (2 * reps[:, None] + np.arange(2)[None, :])[:, :, None]
    kr = (2 * starts[:, None] + np.arange(2 * NA_KEY_PAIRS)[None, :])[:, None, :]
    rs = np.clip(r - wr // 2, 0, rows - wr)
    row_valid = (kr >= rs) & (kr < rs + wr)
    dr = np.where(row_valid, kr - r + (NA_WIN_ROWS - 1), n_dr)
    assert dr.min() >= 0 and dr.max() <= n_dr
    return lo, hi, dr


def _assemble_na_bias(lo_ref, hi_ref, o_ref, dr):
    for case in range(dr.shape[0]):
        for rr in range(2):
            for j in range(NA_KEY_PAIRS):
                tile = lo_ref[0, int(dr[case, rr, 2 * j])] + hi_ref[0, int(dr[case, rr, 2 * j + 1])]
                o_ref[case, 0, rr * GRID_W:(rr + 1) * GRID_W,
                      j * V7X_LANES:(j + 1) * V7X_LANES] = tile


def _na_kernel(q_ref, kT_ref, v_ref, bias_ref, o_ref, *, pairs_per_step, n_pairs):
    lane = lax.broadcasted_iota(jnp.int32, (NA_PAIR_TOKENS, 2 * NA_HEAD_DIM), 1)
    first_head = lane < NA_HEAD_DIM

    def pair_coords(i):
        p = pl.program_id(1) * pairs_per_step + i
        start = jnp.clip(p - 2, 0, n_pairs - NA_KEY_PAIRS)
        case = jnp.where(p < 2, p, jnp.where(p >= n_pairs - 2, p - (n_pairs - NA_BIAS_CASES), 2))
        qrow = pl.multiple_of(i * NA_PAIR_TOKENS, NA_PAIR_TOKENS)
        krow = pl.multiple_of(start * NA_PAIR_TOKENS, NA_PAIR_TOKENS)
        return start, case, qrow, krow

    kzero = jnp.zeros((NA_HEAD_DIM, NA_KEY_TOKENS), BF16)
    vzero = jnp.zeros((NA_KEY_TOKENS, 2 * NA_HEAD_DIM), BF16)

    def scores(coords, hp):
        start, case, qrow, _ = coords
        psl = slice(hp * 2 * NA_HEAD_DIM, (hp + 1) * 2 * NA_HEAD_DIM)
        qp = q_ref[0, pl.ds(qrow, NA_PAIR_TOKENS), psl]
        kslabs = kT_ref[0, pl.ds(start, NA_KEY_PAIRS), psl, :]
        kT = jnp.concatenate([kslabs[j] for j in range(NA_KEY_PAIRS)], axis=1)
        keys = jnp.concatenate([jnp.concatenate([kT[:NA_HEAD_DIM], kzero], axis=1),
                                jnp.concatenate([kzero, kT[NA_HEAD_DIM:]], axis=1)], axis=0)
        bias = jnp.concatenate([bias_ref[case, 2 * hp], bias_ref[case, 2 * hp + 1]], axis=1)
        return _dot(qp, keys) + bias

    def body(it, carry):
        coords = [pair_coords(it * NA_PAIRS_PER_ITER + j) for j in range(NA_PAIRS_PER_ITER)]
        items = [(j, hp) for j in range(NA_PAIRS_PER_ITER) for hp in range(NA_HEADS // 2)]
        pending = [scores(coords[j], hp) for j, hp in items[:NA_SCORES_AHEAD]]
        outs = []
        for n, (j, hp) in enumerate(items):
            s = pending.pop(0)
            if n + NA_SCORES_AHEAD < len(items):
                jn, hn = items[n + NA_SCORES_AHEAD]
                pending.append(scores(coords[jn], hn))
            _, _, qrow, krow = coords[j]
            es, vs = [], []
            for sub in range(2):
                hd = 2 * hp + sub
                sh = s[:, sub * NA_KEY_TOKENS:(sub + 1) * NA_KEY_TOKENS]
                es.append(jnp.exp((sh - jnp.max(sh, axis=-1, keepdims=True)).astype(BF16)))
                vs.append(v_ref[0, pl.ds(krow, NA_KEY_TOKENS), hd * 2 * NA_HEAD_DIM:(hd + 1) * 2 * NA_HEAD_DIM])
            values = jnp.concatenate([jnp.concatenate([vs[0], vzero], axis=1),
                                      jnp.concatenate([vzero, vs[1]], axis=1)], axis=0)
            r = _dot(jnp.concatenate(es, axis=1), values)
            halves = [r[:, sub * 2 * NA_HEAD_DIM:(sub + 1) * 2 * NA_HEAD_DIM] for sub in range(2)]
            halves = [h / pltpu.roll(h, NA_HEAD_DIM, axis=1) for h in halves]
            outs.append(jnp.where(first_head, halves[0], halves[1]))
            if hp == NA_HEADS // 2 - 1:
                o_ref[0, pl.ds(qrow, NA_PAIR_TOKENS), :] = jnp.concatenate(outs, axis=1).astype(BF16)
                outs = []
        return carry

    lax.fori_loop(0, pairs_per_step // NA_PAIRS_PER_ITER, body, 0)


def _na_attention(q, kT4, v, bias, pairs_per_step):
    B, S, _ = q.shape
    n_pairs = S // NA_PAIR_TOKENS
    steps = n_pairs // pairs_per_step
    tq = pairs_per_step * NA_PAIR_TOKENS
    return pl.pallas_call(
        functools.partial(_na_kernel, pairs_per_step=pairs_per_step, n_pairs=n_pairs),
        grid=(B, steps),
        in_specs=[
            pl.BlockSpec((1, tq, NA_WIDTH), lambda b, j: (b, j, 0)),
            pl.BlockSpec((1, n_pairs, NA_WIDTH, V7X_LANES), lambda b, j: (b, 0, 0, 0)),
            pl.BlockSpec((1, S, NA_HEADS * V7X_LANES), lambda b, j: (b, 0, 0)),
            _const_spec((NA_BIAS_CASES, NA_HEADS, NA_PAIR_TOKENS, NA_KEY_TOKENS)),
        ],
        out_specs=pl.BlockSpec((1, tq, NA_WIDTH), lambda b, j: (b, j, 0)),
        out_shape=jax.ShapeDtypeStruct((B, S, NA_WIDTH), BF16),
        compiler_params=_params(2),
        name="na_attn",
    )(q, kT4, v, bias)


DFT_CHUNK = 16


def _dft_tables():
    n, m = GRID_W, DFT_CHUNK
    a = np.arange(n)
    ang1 = 2.0 * np.pi * np.outer(a, a) / n
    w_ri = np.stack([np.cos(ang1), -np.sin(ang1)], axis=1) / np.sqrt(n)
    eye = np.eye(m)
    w_rows = w_ri.reshape(2 * n, n)
    perm = np.einsum('dx,cy->dcyx', eye, eye).reshape(m * m, m * m)
    c = np.arange(n)[:, None, None]
    d = np.arange(n)[None, :, None]
    b = np.arange(n)[None, None, :]
    ang2 = 2.0 * np.pi * ((c + n * d) * b % (n * n)) / (n * n)
    gr, gi = np.cos(ang2) / np.sqrt(n), -np.sin(ang2) / np.sqrt(n)
    w_col = np.concatenate([np.concatenate([gr, -gi], axis=2),
                            np.concatenate([gi, gr], axis=2)], axis=1)
    w_cols = np.zeros((n // 2, 4 * n, 4 * n))
    w_cols[:, :2 * n, :2 * n] = w_col[0::2]
    w_cols[:, 2 * n:, 2 * n:] = w_col[1::2]
    ch = np.arange(F_GROUP_DIM)
    ang3 = 2.0 * np.pi * np.outer(ch, ch) / F_GROUP_DIM
    w_chan = np.concatenate([np.cos(ang3), np.sin(ang3)], axis=0) / np.sqrt(F_GROUP_DIM)
    return tuple(jnp.asarray(t, dtype=F32).astype(BF16) for t in (w_rows, w_cols, perm, w_chan))


def _dft_kernel(wr_ref, wc_ref, perm_ref, u_ref, y_ref, z_ref):
    n, m = GRID_W, DFT_CHUNK

    def lanes(blocks):
        return jnp.concatenate(blocks, axis=1)

    def rows_stage(j, carry):
        col0 = pl.multiple_of(j * m, m)
        x = u_ref[0, :, pl.ds(col0, m), :].reshape(n * m, F_WIDTH)
        xp = _dot(perm_ref[...], lanes([x[g * m * m:(g + 1) * m * m] for g in range(n // m)]))
        xp = xp.astype(BF16)
        xb = [jnp.concatenate([xp[b * m:(b + 1) * m, g * F_WIDTH:(g + 1) * F_WIDTH]
                               for g in range(n // m)], axis=0) for b in range(m)]
        zb = _dot(wr_ref[...], lanes(xb)).astype(BF16)
        st = [jnp.concatenate([zb[t * m:(t + 1) * m, b * F_WIDTH:(b + 1) * F_WIDTH]
                               for b in range(m)], axis=0) for t in range(2 * n // m)]
        z = _dot(perm_ref[...], lanes(st)).astype(BF16)
        for t in range(2 * n // m):
            z_ref[t * 8:(t + 1) * 8, :, pl.ds(col0, m), :] = (
                z[:, t * F_WIDTH:(t + 1) * F_WIDTH].reshape(8, 2, m, F_WIDTH))
        return carry

    def cols_stage(j, carry):
        col0 = pl.multiple_of(j * m, m)
        res = []
        for cp in range(m // 2):
            c = j * m + 2 * cp
            data = jnp.concatenate([z_ref[c].reshape(2 * n, F_WIDTH),
                                    z_ref[c + 1].reshape(2 * n, F_WIDTH)], axis=0)
            both = _dot(wc_ref[j * (m // 2) + cp], data).astype(BF16)
            res += [both[:2 * n], both[2 * n:]]
        groups = [(ri, g) for ri in range(2) for g in range(n // m)]
        stacked = [jnp.concatenate([r[ri * n + g * m:ri * n + (g + 1) * m] for r in res], axis=0)
                   for ri, g in groups]
        y = _dot(perm_ref[...], lanes(stacked)).astype(BF16)
        for k, (ri, g) in enumerate(groups):
            y_ref[ri, 0, g * m:(g + 1) * m, pl.ds(col0, m), :] = (
                y[:, k * F_WIDTH:(k + 1) * F_WIDTH].reshape(m, m, F_WIDTH))
        return carry

    lax.fori_loop(0, n // m, rows_stage, 0)
    lax.fori_loop(0, n // m, cols_stage, 0)


def _position_dft(u4, w_rows, w_cols, perm):
    B, n, m = u4.shape[0], GRID_W, DFT_CHUNK
    return pl.pallas_call(
        _dft_kernel,
        grid=(B,),
        in_specs=[
            _const_spec((2 * n, n)),
            _const_spec((n // 2, 4 * n, 4 * n)),
            _const_spec((m * m, m * m)),
            pl.BlockSpec((1, n, n, F_WIDTH), lambda b: (b, 0, 0, 0)),
        ],
        out_specs=pl.BlockSpec((2, 1, n, n, F_WIDTH), lambda b: (0, b, 0, 0, 0)),
        out_shape=jax.ShapeDtypeStruct((2, B, n, n, F_WIDTH), BF16),
        scratch_shapes=[pltpu.VMEM((n, 2, n, F_WIDTH), BF16)],
        compiler_params=_params(1),
        name="position_dft",
    )(w_rows, w_cols, perm, u4)


def _out_ffn_kernel(x_ref, ona_ref, y_ref, om_ref, gate_ref, wch_ref, wna_ref, wf_ref, wmo_ref,
                    wout_ref, g2_ref, w1_ref, w2_ref, o_ref, *, ff_chunk):
    yr, yi = y_ref[0], y_ref[1]
    groups = []
    for g in range(F_GROUPS):
        sl = slice(g * F_GROUP_DIM, (g + 1) * F_GROUP_DIM)
        groups.append(_dot(jnp.concatenate([yr[:, sl], yi[:, sl]], axis=1), wch_ref[...]))
    yf = jnp.concatenate(groups, axis=1).astype(BF16)

    gate = lambda br: gate_ref[:, br * D_MODEL:(br + 1) * D_MODEL].astype(F32)
    merged = gate(0) * _dot(ona_ref[...], wna_ref[...])
    merged = merged + gate(1) * _dot(yf, wf_ref[...])
    merged = merged + gate(2) * _dot(om_ref[...], wmo_ref[...])
    x1 = x_ref[...] + _dot(merged.astype(BF16), wout_ref[...])

    ms = jnp.mean(x1 * x1, axis=-1, keepdims=True)
    h2 = (x1 * lax.rsqrt(ms + EPS) * g2_ref[...]).astype(BF16)
    acc = x1
    for j in range(D_FF // ff_chunk):
        sl = slice(j * ff_chunk, (j + 1) * ff_chunk)
        a = jnp.maximum(_dot(h2, w1_ref[:, sl]), 0.0)
        acc = acc + _dot((a * a).astype(BF16), w2_ref[sl, :])
    o_ref[...] = acc


def _out_ffn(x2, ona, y3, om, gates, w_chan, w_na_o, w_f, w_mem_o, w_out, norm2_g, w_ff1, w_ff2,
             tm, ff_chunk):
    T = x2.shape[0]
    tok = lambda w: pl.BlockSpec((tm, w), lambda i: (i, 0))
    return pl.pallas_call(
        functools.partial(_out_ffn_kernel, ff_chunk=ff_chunk),
        grid=(T // tm,),
        in_specs=[
            tok(D_MODEL),
            tok(NA_WIDTH),
            pl.BlockSpec((2, tm, F_WIDTH), lambda i: (0, i, 0)),
            tok(MEM_WIDTH),
            tok(N_BRANCHES * D_MODEL),
            _const_spec((2 * F_GROUP_DIM, F_GROUP_DIM)),
            _const_spec((NA_WIDTH, D_MODEL)),
            _const_spec((F_WIDTH, D_MODEL)),
            _const_spec((MEM_WIDTH, D_MODEL)),
            _const_spec((D_MODEL, D_MODEL)),
            _const_spec((1, D_MODEL)),
            _const_spec((D_MODEL, D_FF)),
            _const_spec((D_FF, D_MODEL)),
        ],
        out_specs=tok(D_MODEL),
        out_shape=jax.ShapeDtypeStruct((T, D_MODEL), F32),
        compiler_params=_params(1),
        name="out_ffn",
    )(x2, ona, y3, om, gates, w_chan, w_na_o, w_f, w_mem_o, w_out, norm2_g.reshape(1, D_MODEL),
      w_ff1, w_ff2)


def kernel(x, mem, norm1_g, w_in, b_gate, na_q_g, na_k_g, na_rpb, w_na_o, w_f, mem_norm_g,
           w_mem_kv, mem_q_g, mem_k_g, w_mem_o, w_out, norm2_g, w_ff1, w_ff2):
    B, S, _ = x.shape
    T = B * S
    rows = S // GRID_W
    assert rows == GRID_W and S % NA_PAIR_TOKENS == 0
    x2 = x.reshape(T, D_MODEL)
    gmean_mem = _group_mean_matrix(MEM_WIDTH, MEM_HEAD_DIM)
    w_rows, w_cols, perm, w_chan = _dft_tables()

    memkT, memv = _mem_kv(mem, mem_norm_g, w_mem_kv, gmean_mem, mem_k_g)
    later_weights = (w_na_o, w_f, w_mem_o, w_out, w_ff1, w_ff2)
    q, kT4, v, fu, om, gates, bias, *narrow = _in_proj(
        x2, norm1_g, w_in, b_gate, na_q_g, na_k_g, mem_q_g, memkT, memv,
        _na_bias_blocks(na_rpb, rows), later_weights, S, tm=IN_PROJ_TOKENS)
    w_na_o, w_f, w_mem_o, w_out, w_ff1, w_ff2 = narrow

    ona = _na_attention(q.reshape(B, S, NA_WIDTH), kT4, v.reshape(B, S, NA_HEADS * V7X_LANES), bias,
                        pairs_per_step=NA_PAIRS_PER_STEP)

    y = _position_dft(fu.reshape(B, GRID_W, GRID_W, F_WIDTH), w_rows, w_cols, perm)
    y3 = y.reshape(2, T, F_WIDTH)

    out = _out_ffn(x2, ona.reshape(T, NA_WIDTH), y3, om, gates, w_chan, w_na_o, w_f, w_mem_o,
                   w_out, norm2_g, w_ff1, w_ff2, tm=OUT_FFN_TOKENS, ff_chunk=FF_CHUNK)
    return out.reshape(B, S, D_MODEL)
```

```python
import functools

import numpy as np
import jax
import jax.numpy as jnp
from jax import lax
from jax.experimental import pallas as pl
from jax.experimental.pallas import tpu as pltpu

D_MODEL = 1024
GRID_W = 64
NA_HEADS = 8
NA_HEAD_DIM = 64
NA_WIDTH = NA_HEADS * NA_HEAD_DIM
NA_WIN_ROWS = 8
NA_WIN_COLS = 16
F_GROUPS = 4
F_GROUP_DIM = 128
F_WIDTH = F_GROUPS * F_GROUP_DIM
MEM_HEADS = 4
MEM_HEAD_DIM = 128
MEM_WIDTH = MEM_HEADS * MEM_HEAD_DIM
N_BRANCHES = 3
D_FF = 4 * D_MODEL
IN_Q0 = 0
IN_K0 = IN_Q0 + NA_WIDTH
IN_V0 = IN_K0 + NA_WIDTH
IN_F0 = IN_V0 + NA_WIDTH
IN_M0 = IN_F0 + F_WIDTH
IN_G0 = IN_M0 + MEM_WIDTH
IN_WIDTH = IN_G0 + N_BRANCHES * D_MODEL
EPS = 1e-6
NEG_INF = -1e30

V7X_LANES = 128
V7X_VMEM_LIMIT_BYTES = 60 * 1024 * 1024

NA_PAIR_TOKENS = 2 * GRID_W
NA_KEY_PAIRS = 5
NA_KEY_TOKENS = NA_KEY_PAIRS * NA_PAIR_TOKENS
NA_BIAS_CASES = 5
NA_PAIRS_PER_ITER = 8
NA_SCORES_AHEAD = 1

IN_PROJ_TOKENS = 512
OUT_FFN_TOKENS = 512
FF_CHUNK = 1024
NA_PAIRS_PER_STEP = 8

BF16 = jnp.bfloat16
F32 = jnp.float32


def _dot(a, b):
    return jnp.dot(a, b, preferred_element_type=F32)


def _dot_nt(a, b):
    return lax.dot_general(a, b, (((1,), (1,)), ((), ())), preferred_element_type=F32)


def _const_spec(shape):
    return pl.BlockSpec(shape, lambda *_: (0,) * len(shape), pipeline_mode=pl.Buffered(1))


def _params(n_axes):
    return pltpu.CompilerParams(
        dimension_semantics=("arbitrary",) * n_axes,
        vmem_limit_bytes=V7X_VMEM_LIMIT_BYTES,
    )


def _group_mean_matrix(width, group):
    idx = np.arange(width) // group
    return jnp.asarray((idx[:, None] == idx[None, :]).astype(np.float32) / group, dtype=BF16)


PREP_STEPS = 8


def _mem_kv_kernel(mem_ref, g_ref, w_ref, gm_ref, kg_ref, win_ref, kT_ref, v_ref, wb_ref, wkT_ref,
                   *, n_batches):
    slab = win_ref[...]
    wb_ref[...] = slab.astype(BF16)
    wkT_ref[...] = slab[:, IN_K0:IN_K0 + NA_WIDTH].T.astype(BF16)

    @pl.when(pl.program_id(0) < n_batches)
    def _():
        m = mem_ref[0]
        ms = jnp.mean(m * m, axis=-1, keepdims=True)
        mn = (m * lax.rsqrt(ms + EPS) * g_ref[...]).astype(BF16)
        kv = _dot(mn, w_ref[...].astype(BF16))
        k = kv[:, :MEM_WIDTH]
        msk = _dot((k * k).astype(BF16), gm_ref[...])
        kn = k * lax.rsqrt(msk + EPS) * kg_ref[...]
        kT_ref[0] = kn.T.astype(BF16)
        v_ref[0] = kv[:, MEM_WIDTH:].astype(BF16)


def _mem_kv(mem, mem_norm_g, w_mem_kv, gmean_mem, mem_k_g, w_in):
    B, M, _ = mem.shape
    assert B <= PREP_STEPS
    rows = D_MODEL // PREP_STEPS
    batch_of = lambda i: jnp.minimum(i, B - 1)
    return pl.pallas_call(
        functools.partial(_mem_kv_kernel, n_batches=B),
        grid=(PREP_STEPS,),
        in_specs=[
            pl.BlockSpec((1, M, D_MODEL), lambda i: (batch_of(i), 0, 0)),
            _const_spec((1, D_MODEL)),
            _const_spec((D_MODEL, 2 * MEM_WIDTH)),
            _const_spec((MEM_WIDTH, MEM_WIDTH)),
            _const_spec((1, MEM_WIDTH)),
            pl.BlockSpec((rows, IN_WIDTH), lambda i: (i, 0)),
        ],
        out_specs=[
            pl.BlockSpec((1, MEM_WIDTH, M), lambda i: (batch_of(i), 0, 0)),
            pl.BlockSpec((1, M, MEM_WIDTH), lambda i: (batch_of(i), 0, 0)),
            pl.BlockSpec((rows, IN_WIDTH), lambda i: (i, 0)),
            pl.BlockSpec((NA_WIDTH, rows), lambda i: (0, i)),
        ],
        out_shape=[
            jax.ShapeDtypeStruct((B, MEM_WIDTH, M), BF16),
            jax.ShapeDtypeStruct((B, M, MEM_WIDTH), BF16),
            jax.ShapeDtypeStruct((D_MODEL, IN_WIDTH), BF16),
            jax.ShapeDtypeStruct((NA_WIDTH, D_MODEL), BF16),
        ],
        compiler_params=_params(1),
        name="mem_kv",
    )(mem, mem_norm_g.reshape(1, D_MODEL), w_mem_kv, gmean_mem,
      jnp.tile(mem_k_g, MEM_HEADS).reshape(1, MEM_WIDTH), w_in)


def _in_proj_kernel(x_ref, g1_ref, w_ref, wkT_ref, bg_ref,
                    qg_ref, kg_ref, mqg_ref, memkT_ref, memv_ref, blo_ref, bhi_ref, *rest,
                    n_narrow, bias_dr):
    wide_refs = rest[:n_narrow]
    q_ref, kT_ref, v_ref, fu_ref, om_ref, gate_ref, bias_ref = rest[n_narrow:n_narrow + 7]

    x = x_ref[...]
    ms = jnp.mean(x * x, axis=-1, keepdims=True)
    h = (x * lax.rsqrt(ms + EPS) * g1_ref[...]).astype(BF16)

    def gate_dot(br):
        sl = slice(br * D_MODEL, (br + 1) * D_MODEL)
        zg = _dot(h, w_ref[:, IN_G0 + br * D_MODEL:IN_G0 + (br + 1) * D_MODEL]) + bg_ref[:, sl]
        gate_ref[:, sl] = jax.nn.sigmoid(zg).astype(BF16)

    zm = _dot(h, w_ref[:, IN_M0:IN_M0 + MEM_WIDTH])
    mem_scores = []
    for hd in range(MEM_HEADS):
        sl = slice(hd * MEM_HEAD_DIM, (hd + 1) * MEM_HEAD_DIM)
        zh = zm[:, sl]
        msm = jnp.mean(zh * zh, axis=-1, keepdims=True)
        qm = (zh * lax.rsqrt(msm + EPS) * mqg_ref[:, sl]).astype(BF16)
        if hd == 0:
            gate_dot(0)
        mem_scores.append(_dot(qm, memkT_ref[0, sl, :]))
    gate_dot(1)

    zq = _dot(h, w_ref[:, IN_Q0:IN_Q0 + NA_WIDTH])
    low_half = lax.broadcasted_iota(jnp.int32, (zq.shape[0], V7X_LANES), 1) < NA_HEAD_DIM
    for hp in range(NA_HEADS // 2):
        sl = slice(hp * V7X_LANES, (hp + 1) * V7X_LANES)
        zz = zq[:, sl] * zq[:, sl]
        ms_lo = jnp.sum(jnp.where(low_half, zz, 0.0), axis=-1, keepdims=True)
        ms_hi = jnp.sum(jnp.where(low_half, 0.0, zz), axis=-1, keepdims=True)
        ms = jnp.where(low_half, ms_lo, ms_hi) * (1.0 / NA_HEAD_DIM)
        q_ref[:, sl] = (zq[:, sl] * lax.rsqrt(ms + EPS) * qg_ref[:, sl]).astype(BF16)

    zkT = _dot_nt(wkT_ref[...], h)
    zk3 = zkT.reshape(NA_HEADS, NA_HEAD_DIM, zkT.shape[1])
    msk = jnp.mean(zk3 * zk3, axis=1, keepdims=True)
    kT = ((zk3 * lax.rsqrt(msk + EPS)).reshape(zkT.shape) * kg_ref[...]).astype(BF16)
    for j in range(kT_ref.shape[1]):
        kT_ref[0, j] = kT[:, j * V7X_LANES:(j + 1) * V7X_LANES]

    zv = _dot(h, w_ref[:, IN_V0:IN_V0 + NA_WIDTH])
    low_half = lax.broadcasted_iota(jnp.int32, (zv.shape[0], V7X_LANES), 1) < NA_HEAD_DIM
    for hp in range(NA_HEADS // 2):
        pair = zv[:, hp * V7X_LANES:(hp + 1) * V7X_LANES]
        v_ref[:, (2 * hp) * V7X_LANES:(2 * hp + 1) * V7X_LANES] = jnp.where(low_half, pair, 1.0).astype(BF16)
        v_ref[:, (2 * hp + 1) * V7X_LANES:(2 * hp + 2) * V7X_LANES] = jnp.where(low_half, 1.0, pair).astype(BF16)
    fu_ref[...] = _dot(h, w_ref[:, IN_F0:IN_F0 + F_WIDTH]).astype(BF16)
    gate_dot(2)

    outs = []
    for hd in range(MEM_HEADS):
        sl = slice(hd * MEM_HEAD_DIM, (hd + 1) * MEM_HEAD_DIM)
        s = mem_scores[hd]
        e = jnp.exp(s - jnp.max(s, axis=-1, keepdims=True))
        l = jnp.sum(e, axis=-1, keepdims=True)
        outs.append(_dot(e.astype(BF16), memv_ref[0, :, sl]) / l)
    om_ref[...] = jnp.concatenate(outs, axis=1).astype(BF16)

    for wide_ref, narrow_ref in zip(wide_refs, rest[n_narrow + 7:]):
        narrow_ref[...] = wide_ref[...].astype(BF16)

    @pl.when(pl.program_id(0) < NA_HEADS)
    def _():
        _assemble_na_bias(blo_ref, bhi_ref, bias_ref, bias_dr)


def _in_proj(x2, norm1_g, w_in_bf16, w_kT, b_gate, na_q_g, na_k_g, mem_q_g,
             memkT, memv, bias_blocks, later_weights, tokens_per_batch, tm):
    T = x2.shape[0]
    bias_lo, bias_hi, bias_dr = bias_blocks
    n_cases = bias_dr.shape[0]
    M = memv.shape[1]
    tiles_per_batch = tokens_per_batch // tm
    slabs = tm // V7X_LANES
    B = T // tokens_per_batch
    na_scale = np.float32(1.0 / np.sqrt(NA_HEAD_DIM))
    mem_scale = np.float32(1.0 / np.sqrt(MEM_HEAD_DIM))
    qg = (jnp.tile(na_q_g, NA_HEADS) * na_scale).reshape(1, NA_WIDTH)
    kg = jnp.tile(na_k_g, NA_HEADS).reshape(NA_WIDTH, 1)
    mqg = (jnp.tile(mem_q_g, MEM_HEADS) * mem_scale).reshape(1, MEM_WIDTH)
    tok = lambda w: pl.BlockSpec((tm, w), lambda i: (i, 0))
    batch_of = lambda i: i // tiles_per_batch
    steps = T // tm
    assert steps >= NA_HEADS
    slab_specs = [pl.BlockSpec((w.shape[0] // steps, w.shape[1]), lambda i: (i, 0))
                  for w in later_weights]
    head_of = lambda i: jnp.minimum(i, NA_HEADS - 1)
    half_spec = pl.BlockSpec((1,) + bias_lo.shape[1:], lambda i: (head_of(i), 0, 0, 0))
    return pl.pallas_call(
        functools.partial(_in_proj_kernel, n_narrow=len(later_weights), bias_dr=bias_dr),
        grid=(steps,),
        in_specs=[
            tok(D_MODEL),
            _const_spec((1, D_MODEL)),
            _const_spec((D_MODEL, IN_WIDTH)),
            _const_spec((NA_WIDTH, D_MODEL)),
            _const_spec((1, N_BRANCHES * D_MODEL)),
            _const_spec((1, NA_WIDTH)),
            _const_spec((NA_WIDTH, 1)),
            _const_spec((1, MEM_WIDTH)),
            pl.BlockSpec((1, MEM_WIDTH, M), lambda i: (batch_of(i), 0, 0)),
            pl.BlockSpec((1, M, MEM_WIDTH), lambda i: (batch_of(i), 0, 0)),
            half_spec,
            half_spec,
        ] + slab_specs,
        out_specs=[
            tok(NA_WIDTH),
            pl.BlockSpec((1, slabs, NA_WIDTH, V7X_LANES),
                         lambda i: (batch_of(i), i % tiles_per_batch, 0, 0)),
            tok(NA_HEADS * V7X_LANES),
            tok(F_WIDTH),
            tok(MEM_WIDTH),
            tok(N_BRANCHES * D_MODEL),
            pl.BlockSpec((n_cases, 1, NA_PAIR_TOKENS, NA_KEY_TOKENS), lambda i: (0, head_of(i), 0, 0)),
        ] + slab_specs,
        out_shape=[
            jax.ShapeDtypeStruct((T, NA_WIDTH), BF16),
            jax.ShapeDtypeStruct((B, tokens_per_batch // V7X_LANES, NA_WIDTH, V7X_LANES), BF16),
            jax.ShapeDtypeStruct((T, NA_HEADS * V7X_LANES), BF16),
            jax.ShapeDtypeStruct((T, F_WIDTH), BF16),
            jax.ShapeDtypeStruct((T, MEM_WIDTH), BF16),
            jax.ShapeDtypeStruct((T, N_BRANCHES * D_MODEL), BF16),
            jax.ShapeDtypeStruct((n_cases, NA_HEADS, NA_PAIR_TOKENS, NA_KEY_TOKENS), F32),
        ] + [jax.ShapeDtypeStruct(w.shape, BF16) for w in later_weights],
        compiler_params=_params(1),
        name="in_proj",
    )(x2, norm1_g.reshape(1, D_MODEL), w_in_bf16, w_kT,
      b_gate.reshape(1, N_BRANCHES * D_MODEL), qg, kg, mqg, memkT, memv, bias_lo, bias_hi,
      *later_weights)


def _na_bias_blocks(rpb, rows):
    n_pairs = rows // 2
    wr = min(NA_WIN_ROWS, rows)
    reps = np.array([0, 1, 2, n_pairs - 2, n_pairs - 1])
    starts = np.clip(reps - 2, 0, n_pairs - NA_KEY_PAIRS)
    n_dr, n_dc = 2 * NA_WIN_ROWS - 1, 2 * NA_WIN_COLS - 1
    c = np.arange(GRID_W)[:, None]
    kc = np.arange(2 * GRID_W)[None, :] % GRID_W
    cs = np.clip(c - NA_WIN_COLS // 2, 0, GRID_W - NA_WIN_COLS)
    col_valid = (kc >= cs) & (kc < cs + NA_WIN_COLS)
    dc = np.clip(kc - c, -(NA_WIN_COLS - 1), NA_WIN_COLS - 1) + (NA_WIN_COLS - 1)
    onehot = (np.arange(n_dc)[:, None] == dc.reshape(1, -1)).astype(np.float32)
    cols = jnp.dot(rpb.astype(F32).reshape(NA_HEADS * n_dr, n_dc), onehot,
                   precision=lax.Precision.HIGHEST)
    cols = jnp.where(col_valid.reshape(1, -1), cols, NEG_INF)
    cols = cols.reshape(NA_HEADS, n_dr, GRID_W, 2 * GRID_W)
    cols = jnp.concatenate([cols, jnp.full((NA_HEADS, 1, GRID_W, 2 * GRID_W), NEG_INF, F32)], axis=1)
    low = (np.arange(2 * GRID_W) < GRID_W)[None, None, None, :]
    lo = jnp.where(low, cols, 0.0)
    hi = jnp.where(low, 0.0, cols)
    r = (2 * reps[:, None] + np.arange(2)[None, :])[:, :, None]
    kr = (2 * starts[:, None] + np.arange(2 * NA_KEY_PAIRS)[None, :])[:, None, :]
    rs = np.clip(r - wr // 2, 0, rows - wr)
    row_valid = (kr >= rs) & (kr < rs + wr)
    dr = np.where(row_valid, kr - r + (NA_WIN_ROWS - 1), n_dr)
    assert dr.min() >= 0 and dr.max() <= n_dr
    return lo, hi, dr


def _assemble_na_bias(lo_ref, hi_ref, o_ref, dr):
    for case in range(dr.shape[0]):
        for rr in range(2):
            for j in range(NA_KEY_PAIRS):
                tile = lo_ref[0, int(dr[case, rr, 2 * j])] + hi_ref[0, int(dr[case, rr, 2 * j + 1])]
                o_ref[case, 0, rr * GRID_W:(rr + 1) * GRID_W,
                      j * V7X_LANES:(j + 1) * V7X_LANES] = tile


def _na_kernel(q_ref, kT_ref, v_ref, bias_ref, o_ref, *, pairs_per_step, n_pairs):
    lane = lax.broadcasted_iota(jnp.int32, (NA_PAIR_TOKENS, 2 * NA_HEAD_DIM), 1)
    first_head = lane < NA_HEAD_DIM

    def pair_coords(i):
        p = pl.program_id(1) * pairs_per_step + i
        start = jnp.clip(p - 2, 0, n_pairs - NA_KEY_PAIRS)
        case = jnp.where(p < 2, p, jnp.where(p >= n_pairs - 2, p - (n_pairs - NA_BIAS_CASES), 2))
        qrow = pl.multiple_of(i * NA_PAIR_TOKENS, NA_PAIR_TOKENS)
        krow = pl.multiple_of(start * NA_PAIR_TOKENS, NA_PAIR_TOKENS)
        return start, case, qrow, krow

    kzero = jnp.zeros((NA_HEAD_DIM, NA_KEY_TOKENS), BF16)
    vzero = jnp.zeros((NA_KEY_TOKENS, 2 * NA_HEAD_DIM), BF16)

    def scores(coords, hp):
        start, case, qrow, _ = coords
        psl = slice(hp * 2 * NA_HEAD_DIM, (hp + 1) * 2 * NA_HEAD_DIM)
        qp = q_ref[0, pl.ds(qrow, NA_PAIR_TOKENS), psl]
        kslabs = kT_ref[0, pl.ds(start, NA_KEY_PAIRS), psl, :]
        kT = jnp.concatenate([kslabs[j] for j in range(NA_KEY_PAIRS)], axis=1)
        keys = jnp.concatenate([jnp.concatenate([kT[:NA_HEAD_DIM], kzero], axis=1),
                                jnp.concatenate([kzero, kT[NA_HEAD_DIM:]], axis=1)], axis=0)
        bias = jnp.concatenate([bias_ref[case, 2 * hp], bias_ref[case, 2 * hp + 1]], axis=1)
        return _dot(qp, keys) + bias

    def body(it, carry):
        coords = [pair_coords(it * NA_PAIRS_PER_ITER + j) for j in range(NA_PAIRS_PER_ITER)]
        items = [(j, hp) for j in range(NA_PAIRS_PER_ITER) for hp in range(NA_HEADS // 2)]
        pending = [scores(coords[j], hp) for j, hp in items[:NA_SCORES_AHEAD]]
        outs = []
        for n, (j, hp) in enumerate(items):
            s = pending.pop(0)
            if n + NA_SCORES_AHEAD < len(items):
                jn, hn = items[n + NA_SCORES_AHEAD]
                pending.append(scores(coords[jn], hn))
            _, _, qrow, krow = coords[j]
            es, vs = [], []
            for sub in range(2):
                hd = 2 * hp + sub
                sh = s[:, sub * NA_KEY_TOKENS:(sub + 1) * NA_KEY_TOKENS]
                es.append(jnp.exp((sh - jnp.max(sh, axis=-1, keepdims=True)).astype(BF16)))
                vs.append(v_ref[0, pl.ds(krow, NA_KEY_TOKENS), hd * 2 * NA_HEAD_DIM:(hd + 1) * 2 * NA_HEAD_DIM])
            values = jnp.concatenate([jnp.concatenate([vs[0], vzero], axis=1),
                                      jnp.concatenate([vzero, vs[1]], axis=1)], axis=0)
            r = _dot(jnp.concatenate(es, axis=1), values)
            halves = [r[:, sub * 2 * NA_HEAD_DIM:(sub + 1) * 2 * NA_HEAD_DIM] for sub in range(2)]
            halves = [h / pltpu.roll(h, NA_HEAD_DIM, axis=1) for h in halves]
            outs.append(jnp.where(first_head, halves[0], halves[1]))
            if hp == NA_HEADS // 2 - 1:
                o_ref[0, pl.ds(qrow, NA_PAIR_TOKENS), :] = jnp.concatenate(outs, axis=1).astype(BF16)
                outs = []
        return carry

    lax.fori_loop(0, pairs_per_step // NA_PAIRS_PER_ITER, body, 0)


def _na_attention(q, kT4, v, bias, pairs_per_step):
    B, S, _ = q.shape
    n_pairs = S // NA_PAIR_TOKENS
    steps = n_pairs // pairs_per_step
    tq = pairs_per_step * NA_PAIR_TOKENS
    return pl.pallas_call(
        functools.partial(_na_kernel, pairs_per_step=pairs_per_step, n_pairs=n_pairs),
        grid=(B, steps),
        in_specs=[
            pl.BlockSpec((1, tq, NA_WIDTH), lambda b, j: (b, j, 0)),
            pl.BlockSpec((1, n_pairs, NA_WIDTH, V7X_LANES), lambda b, j: (b, 0, 0, 0)),
            pl.BlockSpec((1, S, NA_HEADS * V7X_LANES), lambda b, j: (b, 0, 0)),
            _const_spec((NA_BIAS_CASES, NA_HEADS, NA_PAIR_TOKENS, NA_KEY_TOKENS)),
        ],
        out_specs=pl.BlockSpec((1, tq, NA_WIDTH), lambda b, j: (b, j, 0)),
        out_shape=jax.ShapeDtypeStruct((B, S, NA_WIDTH), BF16),
        compiler_params=_params(2),
        name="na_attn",
    )(q, kT4, v, bias)


DFT_CHUNK = 16


def _dft_tables():
    n, m = GRID_W, DFT_CHUNK
    a = np.arange(n)
    ang1 = 2.0 * np.pi * np.outer(a, a) / n
    w_ri = np.stack([np.cos(ang1), -np.sin(ang1)], axis=1) / np.sqrt(n)
    eye = np.eye(m)
    w_rows = w_ri.reshape(2 * n, n)
    perm = np.einsum('dx,cy->dcyx', eye, eye).reshape(m * m, m * m)
    c = np.arange(n)[:, None, None]
    d = np.arange(n)[None, :, None]
    b = np.arange(n)[None, None, :]
    ang2 = 2.0 * np.pi * ((c + n * d) * b % (n * n)) / (n * n)
    gr, gi = np.cos(ang2) / np.sqrt(n), -np.sin(ang2) / np.sqrt(n)
    w_cols = np.concatenate([np.concatenate([gr, -gi], axis=2),
                             np.concatenate([gi, gr], axis=2)], axis=1)
    ch = np.arange(F_GROUP_DIM)
    ang3 = 2.0 * np.pi * np.outer(ch, ch) / F_GROUP_DIM
    w_chan = np.concatenate([np.cos(ang3), np.sin(ang3)], axis=0) / np.sqrt(F_GROUP_DIM)
    return tuple(jnp.asarray(t, dtype=F32).astype(BF16) for t in (w_rows, w_cols, perm, w_chan))


def _dft_kernel(wr_ref, wc_ref, perm_ref, u_ref, y_ref, z_ref):
    n, m = GRID_W, DFT_CHUNK

    def rows_stage(j, carry):
        col0 = pl.multiple_of(j * m, m)
        x = u_ref[0, :, pl.ds(col0, m), :].reshape(n * m, F_WIDTH)
        xp = [_dot(perm_ref[...], x[g * m * m:(g + 1) * m * m]).astype(BF16) for g in range(n // m)]
        zb = []
        for b in range(m):
            xb = jnp.concatenate([xp[g][b * m:(b + 1) * m] for g in range(n // m)], axis=0)
            zb.append(_dot(wr_ref[...], xb).astype(BF16))
        for t in range(2 * n // m):
            st = jnp.concatenate([zb[b][t * m:(t + 1) * m] for b in range(m)], axis=0)
            z = _dot(perm_ref[...], st).astype(BF16)
            z_ref[t * 8:(t + 1) * 8, :, pl.ds(col0, m), :] = z.reshape(8, 2, m, F_WIDTH)
        return carry

    def cols_stage(j, carry):
        col0 = pl.multiple_of(j * m, m)
        res = []
        for c_lo in range(m):
            c = j * m + c_lo
            data = z_ref[c].reshape(2 * n, F_WIDTH)
            res.append(_dot(wc_ref[c], data).astype(BF16))
        for ri in range(2):
            for g in range(n // m):
                lo = ri * n + g * m
                stacked = jnp.concatenate([r[lo:lo + m] for r in res], axis=0)
                y = _dot(perm_ref[...], stacked).astype(BF16)
                y_ref[ri, 0, g * m:(g + 1) * m, pl.ds(col0, m), :] = y.reshape(m, m, F_WIDTH)
        return carry

    lax.fori_loop(0, n // m, rows_stage, 0)
    lax.fori_loop(0, n // m, cols_stage, 0)


def _position_dft(u4, w_rows, w_cols, perm):
    B, n, m = u4.shape[0], GRID_W, DFT_CHUNK
    return pl.pallas_call(
        _dft_kernel,
        grid=(B,),
        in_specs=[
            _const_spec((2 * n, n)),
            _const_spec((n, 2 * n, 2 * n)),
            _const_spec((m * m, m * m)),
            pl.BlockSpec((1, n, n, F_WIDTH), lambda b: (b, 0, 0, 0)),
        ],
        out_specs=pl.BlockSpec((2, 1, n, n, F_WIDTH), lambda b: (0, b, 0, 0, 0)),
        out_shape=jax.ShapeDtypeStruct((2, B, n, n, F_WIDTH), BF16),
        scratch_shapes=[pltpu.VMEM((n, 2, n, F_WIDTH), BF16)],
        compiler_params=_params(1),
        name="position_dft",
    )(w_rows, w_cols, perm, u4)


def _out_ffn_kernel(x_ref, ona_ref, y_ref, om_ref, gate_ref, wch_ref, wna_ref, wf_ref, wmo_ref,
                    wout_ref, g2_ref, w1_ref, w2_ref, o_ref, *, ff_chunk):
    yr, yi = y_ref[0], y_ref[1]
    groups = []
    for g in range(F_GROUPS):
        sl = slice(g * F_GROUP_DIM, (g + 1) * F_GROUP_DIM)
        groups.append(_dot(jnp.concatenate([yr[:, sl], yi[:, sl]], axis=1), wch_ref[...]))
    yf = jnp.concatenate(groups, axis=1).astype(BF16)

    gate = lambda br: gate_ref[:, br * D_MODEL:(br + 1) * D_MODEL].astype(F32)
    merged = gate(0) * _dot(ona_ref[...], wna_ref[...])
    merged = merged + gate(1) * _dot(yf, wf_ref[...])
    merged = merged + gate(2) * _dot(om_ref[...], wmo_ref[...])
    x1 = x_ref[...] + _dot(merged.astype(BF16), wout_ref[...])

    ms = jnp.mean(x1 * x1, axis=-1, keepdims=True)
    h2 = (x1 * lax.rsqrt(ms + EPS) * g2_ref[...]).astype(BF16)
    acc = x1
    for j in range(D_FF // ff_chunk):
        sl = slice(j * ff_chunk, (j + 1) * ff_chunk)
        a = jnp.maximum(_dot(h2, w1_ref[:, sl]), 0.0)
        acc = acc + _dot((a * a).astype(BF16), w2_ref[sl, :])
    o_ref[...] = acc


def _out_ffn(x2, ona, y3, om, gates, w_chan, w_na_o, w_f, w_mem_o, w_out, norm2_g, w_ff1, w_ff2,
             tm, ff_chunk):
    T = x2.shape[0]
    tok = lambda w: pl.BlockSpec((tm, w), lambda i: (i, 0))
    return pl.pallas_call(
        functools.partial(_out_ffn_kernel, ff_chunk=ff_chunk),
        grid=(T // tm,),
        in_specs=[
            tok(D_MODEL),
            tok(NA_WIDTH),
            pl.BlockSpec((2, tm, F_WIDTH), lambda i: (0, i, 0)),
            tok(MEM_WIDTH),
            tok(N_BRANCHES * D_MODEL),
            _const_spec((2 * F_GROUP_DIM, F_GROUP_DIM)),
            _const_spec((NA_WIDTH, D_MODEL)),
            _const_spec((F_WIDTH, D_MODEL)),
            _const_spec((MEM_WIDTH, D_MODEL)),
            _const_spec((D_MODEL, D_MODEL)),
            _const_spec((1, D_MODEL)),
            _const_spec((D_MODEL, D_FF)),
            _const_spec((D_FF, D_MODEL)),
        ],
        out_specs=tok(D_MODEL),
        out_shape=jax.ShapeDtypeStruct((T, D_MODEL), F32),
        compiler_params=_params(1),
        name="out_ffn",
    )(x2, ona, y3, om, gates, w_chan, w_na_o, w_f, w_mem_o, w_out, norm2_g.reshape(1, D_MODEL),
      w_ff1, w_ff2)


def kernel(x, mem, norm1_g, w_in, b_gate, na_q_g, na_k_g, na_rpb, w_na_o, w_f, mem_norm_g,
           w_mem_kv, mem_q_g, mem_k_g, w_mem_o, w_out, norm2_g, w_ff1, w_ff2):
    B, S, _ = x.shape
    T = B * S
    rows = S // GRID_W
    assert rows == GRID_W and S % NA_PAIR_TOKENS == 0
    x2 = x.reshape(T, D_MODEL)
    gmean_mem = _group_mean_matrix(MEM_WIDTH, MEM_HEAD_DIM)
    w_rows, w_cols, perm, w_chan = _dft_tables()

    memkT, memv, w_in_bf16, w_kT = _mem_kv(mem, mem_norm_g, w_mem_kv, gmean_mem, mem_k_g, w_in)
    later_weights = (w_na_o, w_f, w_mem_o, w_out, w_ff1, w_ff2)
    q, kT4, v, fu, om, gates, bias, *narrow = _in_proj(
        x2, norm1_g, w_in_bf16, w_kT, b_gate, na_q_g, na_k_g, mem_q_g, memkT, memv,
        _na_bias_blocks(na_rpb, rows), later_weights, S, tm=IN_PROJ_TOKENS)
    w_na_o, w_f, w_mem_o, w_out, w_ff1, w_ff2 = narrow

    ona = _na_attention(q.reshape(B, S, NA_WIDTH), kT4, v.reshape(B, S, NA_HEADS * V7X_LANES), bias,
                        pairs_per_step=NA_PAIRS_PER_STEP)

    y = _position_dft(fu.reshape(B, GRID_W, GRID_W, F_WIDTH), w_rows, w_cols, perm)
    y3 = y.reshape(2, T, F_WIDTH)

    out = _out_ffn(x2, ona.reshape(T, NA_WIDTH), y3, om, gates, w_chan, w_na_o, w_f, w_mem_o,
                   w_out, norm2_g, w_ff1, w_ff2, tm=OUT_FFN_TOKENS, ff_chunk=FF_CHUNK)
    return out.reshape(B, S, D_MODEL)
```

```python
import functools

import numpy as np
import jax
import jax.numpy as jnp
from jax import lax
from jax.experimental import pallas as pl
from jax.experimental.pallas import tpu as pltpu

D_MODEL = 1024
GRID_W = 64
NA_HEADS = 8
NA_HEAD_DIM = 64
NA_WIDTH = NA_HEADS * NA_HEAD_DIM
NA_WIN_ROWS = 8
NA_WIN_COLS = 16
F_GROUPS = 4
F_GROUP_DIM = 128
F_WIDTH = F_GROUPS * F_GROUP_DIM
MEM_HEADS = 4
MEM_HEAD_DIM = 128
MEM_WIDTH = MEM_HEADS * MEM_HEAD_DIM
N_BRANCHES = 3
D_FF = 4 * D_MODEL
IN_Q0 = 0
IN_K0 = IN_Q0 + NA_WIDTH
IN_V0 = IN_K0 + NA_WIDTH
IN_F0 = IN_V0 + NA_WIDTH
IN_M0 = IN_F0 + F_WIDTH
IN_G0 = IN_M0 + MEM_WIDTH
IN_WIDTH = IN_G0 + N_BRANCHES * D_MODEL
EPS = 1e-6
NEG_INF = -1e30

V7X_LANES = 128
V7X_VMEM_LIMIT_BYTES = 60 * 1024 * 1024

NA_PAIR_TOKENS = 2 * GRID_W
NA_KEY_PAIRS = 5
NA_KEY_TOKENS = NA_KEY_PAIRS * NA_PAIR_TOKENS
NA_BIAS_CASES = 5
NA_PAIRS_PER_ITER = 8
NA_SCORES_AHEAD = 1

IN_PROJ_TOKENS = 512
OUT_FFN_TOKENS = 512
FF_CHUNK = 1024
NA_PAIRS_PER_STEP = 8

BF16 = jnp.bfloat16
F32 = jnp.float32


def _dot(a, b):
    return jnp.dot(a, b, preferred_element_type=F32)


def _dot_nt(a, b):
    return lax.dot_general(a, b, (((1,), (1,)), ((), ())), preferred_element_type=F32)


def _const_spec(shape):
    return pl.BlockSpec(shape, lambda *_: (0,) * len(shape), pipeline_mode=pl.Buffered(1))


def _params(n_axes):
    return pltpu.CompilerParams(
        dimension_semantics=("arbitrary",) * n_axes,
        vmem_limit_bytes=V7X_VMEM_LIMIT_BYTES,
    )


def _group_mean_matrix(width, group):
    idx = np.arange(width) // group
    return jnp.asarray((idx[:, None] == idx[None, :]).astype(np.float32) / group, dtype=BF16)


PREP_STEPS = 8


def _mem_kv_kernel(mem_ref, g_ref, w_ref, gm_ref, kg_ref, win_ref, kT_ref, v_ref, wb_ref, wkT_ref,
                   *, n_batches):
    slab = win_ref[...]
    wb_ref[...] = slab.astype(BF16)
    wkT_ref[...] = slab[:, IN_K0:IN_K0 + NA_WIDTH].T.astype(BF16)

    @pl.when(pl.program_id(0) < n_batches)
    def _():
        m = mem_ref[0]
        ms = jnp.mean(m * m, axis=-1, keepdims=True)
        mn = (m * lax.rsqrt(ms + EPS) * g_ref[...]).astype(BF16)
        kv = _dot(mn, w_ref[...].astype(BF16))
        k = kv[:, :MEM_WIDTH]
        msk = _dot((k * k).astype(BF16), gm_ref[...])
        kn = k * lax.rsqrt(msk + EPS) * kg_ref[...]
        kT_ref[0] = kn.T.astype(BF16)
        v_ref[0] = kv[:, MEM_WIDTH:].astype(BF16)


def _mem_kv(mem, mem_norm_g, w_mem_kv, gmean_mem, mem_k_g, w_in):
    B, M, _ = mem.shape
    assert B <= PREP_STEPS
    rows = D_MODEL // PREP_STEPS
    batch_of = lambda i: jnp.minimum(i, B - 1)
    return pl.pallas_call(
        functools.partial(_mem_kv_kernel, n_batches=B),
        grid=(PREP_STEPS,),
        in_specs=[
            pl.BlockSpec((1, M, D_MODEL), lambda i: (batch_of(i), 0, 0)),
            _const_spec((1, D_MODEL)),
            _const_spec((D_MODEL, 2 * MEM_WIDTH)),
            _const_spec((MEM_WIDTH, MEM_WIDTH)),
            _const_spec((1, MEM_WIDTH)),
            pl.BlockSpec((rows, IN_WIDTH), lambda i: (i, 0)),
        ],
        out_specs=[
            pl.BlockSpec((1, MEM_WIDTH, M), lambda i: (batch_of(i), 0, 0)),
            pl.BlockSpec((1, M, MEM_WIDTH), lambda i: (batch_of(i), 0, 0)),
            pl.BlockSpec((rows, IN_WIDTH), lambda i: (i, 0)),
            pl.BlockSpec((NA_WIDTH, rows), lambda i: (0, i)),
        ],
        out_shape=[
            jax.ShapeDtypeStruct((B, MEM_WIDTH, M), BF16),
            jax.ShapeDtypeStruct((B, M, MEM_WIDTH), BF16),
            jax.ShapeDtypeStruct((D_MODEL, IN_WIDTH), BF16),
            jax.ShapeDtypeStruct((NA_WIDTH, D_MODEL), BF16),
        ],
        compiler_params=_params(1),
        name="mem_kv",
    )(mem, mem_norm_g.reshape(1, D_MODEL), w_mem_kv, gmean_mem,
      jnp.tile(mem_k_g, MEM_HEADS).reshape(1, MEM_WIDTH), w_in)


def _in_proj_kernel(x_ref, g1_ref, w_ref, wkT_ref, bg_ref,
                    qg_ref, kg_ref, mqg_ref, memkT_ref, memv_ref, brow_ref, *rest,
                    n_narrow, bias_dr):
    wide_refs = rest[:n_narrow]
    q_ref, kT_ref, v_ref, fu_ref, om_ref, gate_ref, bias_ref = rest[n_narrow:n_narrow + 7]

    x = x_ref[...]
    ms = jnp.mean(x * x, axis=-1, keepdims=True)
    h = (x * lax.rsqrt(ms + EPS) * g1_ref[...]).astype(BF16)

    def gate_dot(br):
        sl = slice(br * D_MODEL, (br + 1) * D_MODEL)
        zg = _dot(h, w_ref[:, IN_G0 + br * D_MODEL:IN_G0 + (br + 1) * D_MODEL]) + bg_ref[:, sl]
        gate_ref[:, sl] = jax.nn.sigmoid(zg).astype(BF16)

    zm = _dot(h, w_ref[:, IN_M0:IN_M0 + MEM_WIDTH])
    mem_scores = []
    for hd in range(MEM_HEADS):
        sl = slice(hd * MEM_HEAD_DIM, (hd + 1) * MEM_HEAD_DIM)
        zh = zm[:, sl]
        msm = jnp.mean(zh * zh, axis=-1, keepdims=True)
        qm = (zh * lax.rsqrt(msm + EPS) * mqg_ref[:, sl]).astype(BF16)
        if hd == 0:
            gate_dot(0)
        mem_scores.append(_dot(qm, memkT_ref[0, sl, :]))
    gate_dot(1)

    zq = _dot(h, w_ref[:, IN_Q0:IN_Q0 + NA_WIDTH])
    low_half = lax.broadcasted_iota(jnp.int32, (zq.shape[0], V7X_LANES), 1) < NA_HEAD_DIM
    for hp in range(NA_HEADS // 2):
        sl = slice(hp * V7X_LANES, (hp + 1) * V7X_LANES)
        zz = zq[:, sl] * zq[:, sl]
        ms_lo = jnp.sum(jnp.where(low_half, zz, 0.0), axis=-1, keepdims=True)
        ms_hi = jnp.sum(jnp.where(low_half, 0.0, zz), axis=-1, keepdims=True)
        ms = jnp.where(low_half, ms_lo, ms_hi) * (1.0 / NA_HEAD_DIM)
        q_ref[:, sl] = (zq[:, sl] * lax.rsqrt(ms + EPS) * qg_ref[:, sl]).astype(BF16)

    zkT = _dot_nt(wkT_ref[...], h)
    zk3 = zkT.reshape(NA_HEADS, NA_HEAD_DIM, zkT.shape[1])
    msk = jnp.mean(zk3 * zk3, axis=1, keepdims=True)
    kT = ((zk3 * lax.rsqrt(msk + EPS)).reshape(zkT.shape) * kg_ref[...]).astype(BF16)
    for j in range(kT_ref.shape[1]):
        kT_ref[0, j] = kT[:, j * V7X_LANES:(j + 1) * V7X_LANES]

    zv = _dot(h, w_ref[:, IN_V0:IN_V0 + NA_WIDTH])
    low_half = lax.broadcasted_iota(jnp.int32, (zv.shape[0], V7X_LANES), 1) < NA_HEAD_DIM
    for hp in range(NA_HEADS // 2):
        pair = zv[:, hp * V7X_LANES:(hp + 1) * V7X_LANES]
        v_ref[:, (2 * hp) * V7X_LANES:(2 * hp + 1) * V7X_LANES] = jnp.where(low_half, pair, 1.0).astype(BF16)
        v_ref[:, (2 * hp + 1) * V7X_LANES:(2 * hp + 2) * V7X_LANES] = jnp.where(low_half, 1.0, pair).astype(BF16)
    fu_ref[...] = _dot(h, w_ref[:, IN_F0:IN_F0 + F_WIDTH]).astype(BF16)
    gate_dot(2)

    outs = []
    for hd in range(MEM_HEADS):
        sl = slice(hd * MEM_HEAD_DIM, (hd + 1) * MEM_HEAD_DIM)
        s = mem_scores[hd]
        e = jnp.exp(s - jnp.max(s, axis=-1, keepdims=True))
        l = jnp.sum(e, axis=-1, keepdims=True)
        outs.append(_dot(e.astype(BF16), memv_ref[0, :, sl]) / l)
    om_ref[...] = jnp.concatenate(outs, axis=1).astype(BF16)

    for wide_ref, narrow_ref in zip(wide_refs, rest[n_narrow + 7:]):
        narrow_ref[...] = wide_ref[...].astype(BF16)

    @pl.when(pl.program_id(0) < NA_HEADS)
    def _():
        _assemble_na_bias(brow_ref, bias_ref, bias_dr)


def _in_proj(x2, norm1_g, w_in_bf16, w_kT, b_gate, na_q_g, na_k_g, mem_q_g,
             memkT, memv, bias_blocks, later_weights, tokens_per_batch, tm):
    T = x2.shape[0]
    bias_rows, bias_dr = bias_blocks
    n_cases = bias_dr.shape[0]
    M = memv.shape[1]
    tiles_per_batch = tokens_per_batch // tm
    slabs = tm // V7X_LANES
    B = T // tokens_per_batch
    na_scale = np.float32(1.0 / np.sqrt(NA_HEAD_DIM))
    mem_scale = np.float32(1.0 / np.sqrt(MEM_HEAD_DIM))
    qg = (jnp.tile(na_q_g, NA_HEADS) * na_scale).reshape(1, NA_WIDTH)
    kg = jnp.tile(na_k_g, NA_HEADS).reshape(NA_WIDTH, 1)
    mqg = (jnp.tile(mem_q_g, MEM_HEADS) * mem_scale).reshape(1, MEM_WIDTH)
    tok = lambda w: pl.BlockSpec((tm, w), lambda i: (i, 0))
    batch_of = lambda i: i // tiles_per_batch
    steps = T // tm
    assert steps >= NA_HEADS
    slab_specs = [pl.BlockSpec((w.shape[0] // steps, w.shape[1]), lambda i: (i, 0))
                  for w in later_weights]
    head_of = lambda i: jnp.minimum(i, NA_HEADS - 1)
    bias_rows_spec = pl.BlockSpec((1,) + bias_rows.shape[1:], lambda i: (head_of(i), 0, 0))
    return pl.pallas_call(
        functools.partial(_in_proj_kernel, n_narrow=len(later_weights), bias_dr=bias_dr),
        grid=(steps,),
        in_specs=[
            tok(D_MODEL),
            _const_spec((1, D_MODEL)),
            _const_spec((D_MODEL, IN_WIDTH)),
            _const_spec((NA_WIDTH, D_MODEL)),
            _const_spec((1, N_BRANCHES * D_MODEL)),
            _const_spec((1, NA_WIDTH)),
            _const_spec((NA_WIDTH, 1)),
            _const_spec((1, MEM_WIDTH)),
            pl.BlockSpec((1, MEM_WIDTH, M), lambda i: (batch_of(i), 0, 0)),
            pl.BlockSpec((1, M, MEM_WIDTH), lambda i: (batch_of(i), 0, 0)),
            bias_rows_spec,
        ] + slab_specs,
        out_specs=[
            tok(NA_WIDTH),
            pl.BlockSpec((1, slabs, NA_WIDTH, V7X_LANES),
                         lambda i: (batch_of(i), i % tiles_per_batch, 0, 0)),
            tok(NA_HEADS * V7X_LANES),
            tok(F_WIDTH),
            tok(MEM_WIDTH),
            tok(N_BRANCHES * D_MODEL),
            pl.BlockSpec((n_cases, 1, NA_PAIR_TOKENS, NA_KEY_TOKENS), lambda i: (0, head_of(i), 0, 0)),
        ] + slab_specs,
        out_shape=[
            jax.ShapeDtypeStruct((T, NA_WIDTH), BF16),
            jax.ShapeDtypeStruct((B, tokens_per_batch // V7X_LANES, NA_WIDTH, V7X_LANES), BF16),
            jax.ShapeDtypeStruct((T, NA_HEADS * V7X_LANES), BF16),
            jax.ShapeDtypeStruct((T, F_WIDTH), BF16),
            jax.ShapeDtypeStruct((T, MEM_WIDTH), BF16),
            jax.ShapeDtypeStruct((T, N_BRANCHES * D_MODEL), BF16),
            jax.ShapeDtypeStruct((n_cases, NA_HEADS, NA_PAIR_TOKENS, NA_KEY_TOKENS), F32),
        ] + [jax.ShapeDtypeStruct(w.shape, BF16) for w in later_weights],
        compiler_params=_params(1),
        name="in_proj",
    )(x2, norm1_g.reshape(1, D_MODEL), w_in_bf16, w_kT,
      b_gate.reshape(1, N_BRANCHES * D_MODEL), qg, kg, mqg, memkT, memv, bias_rows,
      *later_weights)


def _na_bias_blocks(rpb, rows):
    n_pairs = rows // 2
    wr = min(NA_WIN_ROWS, rows)
    reps = np.array([0, 1, 2, n_pairs - 2, n_pairs - 1])
    starts = np.clip(reps - 2, 0, n_pairs - NA_KEY_PAIRS)
    n_dr, n_dc = 2 * NA_WIN_ROWS - 1, 2 * NA_WIN_COLS - 1
    offsets = np.arange(n_dc) - (NA_WIN_COLS - 1)
    place = (np.arange(2 * GRID_W)[None, :] % GRID_W == (offsets % GRID_W)[:, None]).astype(np.float32)
    rows128 = jnp.dot(rpb.astype(F32).reshape(NA_HEADS * n_dr, n_dc), place,
                      precision=lax.Precision.HIGHEST).reshape(NA_HEADS, n_dr, 2 * GRID_W)
    r = (2 * reps[:, None] + np.arange(2)[None, :])[:, :, None]
    kr = (2 * starts[:, None] + np.arange(2 * NA_KEY_PAIRS)[None, :])[:, None, :]
    rs = np.clip(r - wr // 2, 0, rows - wr)
    row_valid = (kr >= rs) & (kr < rs + wr)
    dr = np.where(row_valid, kr - r + (NA_WIN_ROWS - 1), n_dr)
    assert dr.min() >= 0 and dr.max() <= n_dr
    return rows128, dr


def _assemble_na_bias(rows_ref, o_ref, dr):
    n_dr = rows_ref.shape[1]
    c = lax.broadcasted_iota(jnp.int32, (GRID_W, 2 * GRID_W), 0)
    lane = lax.broadcasted_iota(jnp.int32, (GRID_W, 2 * GRID_W), 1)
    kc = lane & (GRID_W - 1)
    cs = jnp.clip(c - NA_WIN_COLS // 2, 0, GRID_W - NA_WIN_COLS)
    col_valid = (kc >= cs) & (kc < cs + NA_WIN_COLS)
    masked = jnp.full((GRID_W, 2 * GRID_W), NEG_INF, F32)
    blocks = []
    for i in range(n_dr):
        row = jnp.broadcast_to(rows_ref[0, i:i + 1, :], (GRID_W, 2 * GRID_W))
        toeplitz = pltpu.roll(row, 0, axis=1, stride=1, stride_axis=0)
        blocks.append(jnp.where(col_valid, toeplitz, masked))
    blocks.append(masked)
    even_row = lane < GRID_W
    for case in range(dr.shape[0]):
        for rr in range(2):
            for j in range(NA_KEY_PAIRS):
                tile = jnp.where(even_row, blocks[int(dr[case, rr, 2 * j])],
                                 blocks[int(dr[case, rr, 2 * j + 1])])
                o_ref[case, 0, rr * GRID_W:(rr + 1) * GRID_W,
                      j * V7X_LANES:(j + 1) * V7X_LANES] = tile


def _na_kernel(q_ref, kT_ref, v_ref, bias_ref, o_ref, *, pairs_per_step, n_pairs):
    lane = lax.broadcasted_iota(jnp.int32, (NA_PAIR_TOKENS, 2 * NA_HEAD_DIM), 1)
    first_head = lane < NA_HEAD_DIM

    def pair_coords(i):
        p = pl.program_id(1) * pairs_per_step + i
        start = jnp.clip(p - 2, 0, n_pairs - NA_KEY_PAIRS)
        case = jnp.where(p < 2, p, jnp.where(p >= n_pairs - 2, p - (n_pairs - NA_BIAS_CASES), 2))
        qrow = pl.multiple_of(i * NA_PAIR_TOKENS, NA_PAIR_TOKENS)
        krow = pl.multiple_of(start * NA_PAIR_TOKENS, NA_PAIR_TOKENS)
        return start, case, qrow, krow

    kzero = jnp.zeros((NA_HEAD_DIM, NA_KEY_TOKENS), BF16)
    vzero = jnp.zeros((NA_KEY_TOKENS, 2 * NA_HEAD_DIM), BF16)

    def scores(coords, hp):
        start, case, qrow, _ = coords
        psl = slice(hp * 2 * NA_HEAD_DIM, (hp + 1) * 2 * NA_HEAD_DIM)
        qp = q_ref[0, pl.ds(qrow, NA_PAIR_TOKENS), psl]
        kslabs = kT_ref[0, pl.ds(start, NA_KEY_PAIRS), psl, :]
        kT = jnp.concatenate([kslabs[j] for j in range(NA_KEY_PAIRS)], axis=1)
        keys = jnp.concatenate([jnp.concatenate([kT[:NA_HEAD_DIM], kzero], axis=1),
                                jnp.concatenate([kzero, kT[NA_HEAD_DIM:]], axis=1)], axis=0)
        bias = jnp.concatenate([bias_ref[case, 2 * hp], bias_ref[case, 2 * hp + 1]], axis=1)
        return _dot(qp, keys) + bias

    def body(it, carry):
        coords = [pair_coords(it * NA_PAIRS_PER_ITER + j) for j in range(NA_PAIRS_PER_ITER)]
        items = [(j, hp) for j in range(NA_PAIRS_PER_ITER) for hp in range(NA_HEADS // 2)]
        pending = [scores(coords[j], hp) for j, hp in items[:NA_SCORES_AHEAD]]
        outs = []
        for n, (j, hp) in enumerate(items):
            s = pending.pop(0)
            if n + NA_SCORES_AHEAD < len(items):
                jn, hn = items[n + NA_SCORES_AHEAD]
                pending.append(scores(coords[jn], hn))
            _, _, qrow, krow = coords[j]
            es, vs = [], []
            for sub in range(2):
                hd = 2 * hp + sub
                sh = s[:, sub * NA_KEY_TOKENS:(sub + 1) * NA_KEY_TOKENS]
                es.append(jnp.exp((sh - jnp.max(sh, axis=-1, keepdims=True)).astype(BF16)))
                vs.append(v_ref[0, pl.ds(krow, NA_KEY_TOKENS), hd * 2 * NA_HEAD_DIM:(hd + 1) * 2 * NA_HEAD_DIM])
            values = jnp.concatenate([jnp.concatenate([vs[0], vzero], axis=1),
                                      jnp.concatenate([vzero, vs[1]], axis=1)], axis=0)
            r = _dot(jnp.concatenate(es, axis=1), values)
            halves = [r[:, sub * 2 * NA_HEAD_DIM:(sub + 1) * 2 * NA_HEAD_DIM] for sub in range(2)]
            halves = [h / pltpu.roll(h, NA_HEAD_DIM, axis=1) for h in halves]
            outs.append(jnp.where(first_head, halves[0], halves[1]))
            if hp == NA_HEADS // 2 - 1:
                o_ref[0, pl.ds(qrow, NA_PAIR_TOKENS), :] = jnp.concatenate(outs, axis=1).astype(BF16)
                outs = []
        return carry

    lax.fori_loop(0, pairs_per_step // NA_PAIRS_PER_ITER, body, 0)


def _na_attention(q, kT4, v, bias, pairs_per_step):
    B, S, _ = q.shape
    n_pairs = S // NA_PAIR_TOKENS
    steps = n_pairs // pairs_per_step
    tq = pairs_per_step * NA_PAIR_TOKENS
    return pl.pallas_call(
        functools.partial(_na_kernel, pairs_per_step=pairs_per_step, n_pairs=n_pairs),
        grid=(B, steps),
        in_specs=[
            pl.BlockSpec((1, tq, NA_WIDTH), lambda b, j: (b, j, 0)),
            pl.BlockSpec((1, n_pairs, NA_WIDTH, V7X_LANES), lambda b, j: (b, 0, 0, 0)),
            pl.BlockSpec((1, S, NA_HEADS * V7X_LANES), lambda b, j: (b, 0, 0)),
            _const_spec((NA_BIAS_CASES, NA_HEADS, NA_PAIR_TOKENS, NA_KEY_TOKENS)),
        ],
        out_specs=pl.BlockSpec((1, tq, NA_WIDTH), lambda b, j: (b, j, 0)),
        out_shape=jax.ShapeDtypeStruct((B, S, NA_WIDTH), BF16),
        compiler_params=_params(2),
        name="na_attn",
    )(q, kT4, v, bias)


DFT_CHUNK = 16


def _dft_tables():
    n, m = GRID_W, DFT_CHUNK
    a = np.arange(n)
    ang1 = 2.0 * np.pi * np.outer(a, a) / n
    w_ri = np.stack([np.cos(ang1), -np.sin(ang1)], axis=1) / np.sqrt(n)
    eye = np.eye(m)
    w_rows = w_ri.reshape(2 * n, n)
    perm = np.einsum('dx,cy->dcyx', eye, eye).reshape(m * m, m * m)
    c = np.arange(n)[:, None, None]
    d = np.arange(n)[None, :, None]
    b = np.arange(n)[None, None, :]
    ang2 = 2.0 * np.pi * ((c + n * d) * b % (n * n)) / (n * n)
    gr, gi = np.cos(ang2) / np.sqrt(n), -np.sin(ang2) / np.sqrt(n)
    w_cols = np.concatenate([np.concatenate([gr, -gi], axis=2),
                             np.concatenate([gi, gr], axis=2)], axis=1)
    ch = np.arange(F_GROUP_DIM)
    ang3 = 2.0 * np.pi * np.outer(ch, ch) / F_GROUP_DIM
    w_chan = np.concatenate([np.cos(ang3), np.sin(ang3)], axis=0) / np.sqrt(F_GROUP_DIM)
    return tuple(jnp.asarray(t, dtype=F32).astype(BF16) for t in (w_rows, w_cols, perm, w_chan))


def _dft_kernel(wr_ref, wc_ref, perm_ref, u_ref, y_ref, z_ref):
    n, m = GRID_W, DFT_CHUNK

    def rows_stage(j, carry):
        col0 = pl.multiple_of(j * m, m)
        x = u_ref[0, :, pl.ds(col0, m), :].reshape(n * m, F_WIDTH)
        xp = [_dot(perm_ref[...], x[g * m * m:(g + 1) * m * m]).astype(BF16) for g in range(n // m)]
        zb = []
        for b in range(m):
            xb = jnp.concatenate([xp[g][b * m:(b + 1) * m] for g in range(n // m)], axis=0)
            zb.append(_dot(wr_ref[...], xb).astype(BF16))
        for t in range(2 * n // m):
            st = jnp.concatenate([zb[b][t * m:(t + 1) * m] for b in range(m)], axis=0)
            z = _dot(perm_ref[...], st).astype(BF16)
            z_ref[t * 8:(t + 1) * 8, :, pl.ds(col0, m), :] = z.reshape(8, 2, m, F_WIDTH)
        return carry

    def cols_stage(j, carry):
        col0 = pl.multiple_of(j * m, m)
        res = []
        for c_lo in range(m):
            c = j * m + c_lo
            data = z_ref[c].reshape(2 * n, F_WIDTH)
            res.append(_dot(wc_ref[c], data).astype(BF16))
        for ri in range(2):
            for g in range(n // m):
                lo = ri * n + g * m
                stacked = jnp.concatenate([r[lo:lo + m] for r in res], axis=0)
                y = _dot(perm_ref[...], stacked).astype(BF16)
                y_ref[ri, 0, g * m:(g + 1) * m, pl.ds(col0, m), :] = y.reshape(m, m, F_WIDTH)
        return carry

    lax.fori_loop(0, n // m, rows_stage, 0)
    lax.fori_loop(0, n // m, cols_stage, 0)


def _position_dft(u4, w_rows, w_cols, perm):
    B, n, m = u4.shape[0], GRID_W, DFT_CHUNK
    return pl.pallas_call(
        _dft_kernel,
        grid=(B,),
        in_specs=[
            _const_spec((2 * n, n)),
            _const_spec((n, 2 * n, 2 * n)),
            _const_spec((m * m, m * m)),
            pl.BlockSpec((1, n, n, F_WIDTH), lambda b: (b, 0, 0, 0)),
        ],
        out_specs=pl.BlockSpec((2, 1, n, n, F_WIDTH), lambda b: (0, b, 0, 0, 0)),
        out_shape=jax.ShapeDtypeStruct((2, B, n, n, F_WIDTH), BF16),
        scratch_shapes=[pltpu.VMEM((n, 2, n, F_WIDTH), BF16)],
        compiler_params=_params(1),
        name="position_dft",
    )(w_rows, w_cols, perm, u4)


def _out_ffn_kernel(x_ref, ona_ref, y_ref, om_ref, gate_ref, wch_ref, wna_ref, wf_ref, wmo_ref,
                    wout_ref, g2_ref, w1_ref, w2_ref, o_ref, *, ff_chunk):
    yr, yi = y_ref[0], y_ref[1]
    groups = []
    for g in range(F_GROUPS):
        sl = slice(g * F_GROUP_DIM, (g + 1) * F_GROUP_DIM)
        groups.append(_dot(jnp.concatenate([yr[:, sl], yi[:, sl]], axis=1), wch_ref[...]))
    yf = jnp.concatenate(groups, axis=1).astype(BF16)

    gate = lambda br: gate_ref[:, br * D_MODEL:(br + 1) * D_MODEL].astype(F32)
    merged = gate(0) * _dot(ona_ref[...], wna_ref[...])
    merged = merged + gate(1) * _dot(yf, wf_ref[...])
    merged = merged + gate(2) * _dot(om_ref[...], wmo_ref[...])
    x1 = x_ref[...] + _dot(merged.astype(BF16), wout_ref[...])

    ms = jnp.mean(x1 * x1, axis=-1, keepdims=True)
    h2 = (x1 * lax.rsqrt(ms + EPS) * g2_ref[...]).astype(BF16)
    acc = x1
    for j in range(D_FF // ff_chunk):
        sl = slice(j * ff_chunk, (j + 1) * ff_chunk)
        a = jnp.maximum(_dot(h2, w1_ref[:, sl]), 0.0)
        acc = acc + _dot((a * a).astype(BF16), w2_ref[sl, :])
    o_ref[...] = acc


def _out_ffn(x2, ona, y3, om, gates, w_chan, w_na_o, w_f, w_mem_o, w_out, norm2_g, w_ff1, w_ff2,
             tm, ff_chunk):
    T = x2.shape[0]
    tok = lambda w: pl.BlockSpec((tm, w), lambda i: (i, 0))
    return pl.pallas_call(
        functools.partial(_out_ffn_kernel, ff_chunk=ff_chunk),
        grid=(T // tm,),
        in_specs=[
            tok(D_MODEL),
            tok(NA_WIDTH),
            pl.BlockSpec((2, tm, F_WIDTH), lambda i: (0, i, 0)),
            tok(MEM_WIDTH),
            tok(N_BRANCHES * D_MODEL),
            _const_spec((2 * F_GROUP_DIM, F_GROUP_DIM)),
            _const_spec((NA_WIDTH, D_MODEL)),
            _const_spec((F_WIDTH, D_MODEL)),
            _const_spec((MEM_WIDTH, D_MODEL)),
            _const_spec((D_MODEL, D_MODEL)),
            _const_spec((1, D_MODEL)),
            _const_spec((D_MODEL, D_FF)),
            _const_spec((D_FF, D_MODEL)),
        ],
        out_specs=tok(D_MODEL),
        out_shape=jax.ShapeDtypeStruct((T, D_MODEL), F32),
        compiler_params=_params(1),
        name="out_ffn",
    )(x2, ona, y3, om, gates, w_chan, w_na_o, w_f, w_mem_o, w_out, norm2_g.reshape(1, D_MODEL),
      w_ff1, w_ff2)


def kernel(x, mem, norm1_g, w_in, b_gate, na_q_g, na_k_g, na_rpb, w_na_o, w_f, mem_norm_g,
           w_mem_kv, mem_q_g, mem_k_g, w_mem_o, w_out, norm2_g, w_ff1, w_ff2):
    B, S, _ = x.shape
    T = B * S
    rows = S // GRID_W
    assert rows == GRID_W and S % NA_PAIR_TOKENS == 0
    x2 = x.reshape(T, D_MODEL)
    gmean_mem = _group_mean_matrix(MEM_WIDTH, MEM_HEAD_DIM)
    w_rows, w_cols, perm, w_chan = _dft_tables()

    memkT, memv, w_in_bf16, w_kT = _mem_kv(mem, mem_norm_g, w_mem_kv, gmean_mem, mem_k_g, w_in)
    later_weights = (w_na_o, w_f, w_mem_o, w_out, w_ff1, w_ff2)
    q, kT4, v, fu, om, gates, bias, *narrow = _in_proj(
        x2, norm1_g, w_in_bf16, w_kT, b_gate, na_q_g, na_k_g, mem_q_g, memkT, memv,
        _na_bias_blocks(na_rpb, rows), later_weights, S, tm=IN_PROJ_TOKENS)
    w_na_o, w_f, w_mem_o, w_out, w_ff1, w_ff2 = narrow

    ona = _na_attention(q.reshape(B, S, NA_WIDTH), kT4, v.reshape(B, S, NA_HEADS * V7X_LANES), bias,
                        pairs_per_step=NA_PAIRS_PER_STEP)

    y = _position_dft(fu.reshape(B, GRID_W, GRID_W, F_WIDTH), w_rows, w_cols, perm)
    y3 = y.reshape(2, T, F_WIDTH)

    out = _out_ffn(x2, ona.reshape(T, NA_WIDTH), y3, om, gates, w_chan, w_na_o, w_f, w_mem_o,
                   w_out, norm2_g, w_ff1, w_ff2, tm=OUT_FFN_TOKENS, ff_chunk=FF_CHUNK)
    return out.reshape(B, S, D_MODEL)
```

```python
import functools

import numpy as np
import jax
import jax.numpy as jnp
from jax import lax
from jax.experimental import pallas as pl
from jax.experimental.pallas import tpu as pltpu

D_MODEL = 1024
GRID_W = 64
NA_HEADS = 8
NA_HEAD_DIM = 64
NA_WIDTH = NA_HEADS * NA_HEAD_DIM
NA_WIN_ROWS = 8
NA_WIN_COLS = 16
F_GROUPS = 4
F_GROUP_DIM = 128
F_WIDTH = F_GROUPS * F_GROUP_DIM
MEM_HEADS = 4
MEM_HEAD_DIM = 128
MEM_WIDTH = MEM_HEADS * MEM_HEAD_DIM
N_BRANCHES = 3
D_FF = 4 * D_MODEL
IN_Q0 = 0
IN_K0 = IN_Q0 + NA_WIDTH
IN_V0 = IN_K0 + NA_WIDTH
IN_F0 = IN_V0 + NA_WIDTH
IN_M0 = IN_F0 + F_WIDTH
IN_G0 = IN_M0 + MEM_WIDTH
IN_WIDTH = IN_G0 + N_BRANCHES * D_MODEL
EPS = 1e-6
NEG_INF = -1e30

V7X_LANES = 128
V7X_VMEM_LIMIT_BYTES = 60 * 1024 * 1024

NA_PAIR_TOKENS = 2 * GRID_W
NA_KEY_PAIRS = 5
NA_KEY_TOKENS = NA_KEY_PAIRS * NA_PAIR_TOKENS
NA_BIAS_CASES = 5
NA_PAIRS_PER_ITER = 8
NA_SCORES_AHEAD = 2

IN_PROJ_TOKENS = 512
OUT_FFN_TOKENS = 512
FF_CHUNK = 1024
NA_PAIRS_PER_STEP = 8

BF16 = jnp.bfloat16
F32 = jnp.float32


def _dot(a, b):
    return jnp.dot(a, b, preferred_element_type=F32)


def _dot_nt(a, b):
    return lax.dot_general(a, b, (((1,), (1,)), ((), ())), preferred_element_type=F32)


def _const_spec(shape):
    return pl.BlockSpec(shape, lambda *_: (0,) * len(shape), pipeline_mode=pl.Buffered(1))


def _params(n_axes):
    return pltpu.CompilerParams(
        dimension_semantics=("arbitrary",) * n_axes,
        vmem_limit_bytes=V7X_VMEM_LIMIT_BYTES,
    )


def _group_mean_matrix(width, group):
    idx = np.arange(width) // group
    return jnp.asarray((idx[:, None] == idx[None, :]).astype(np.float32) / group, dtype=BF16)


PREP_STEPS = 4


def _mem_kv_kernel(mem_ref, g_ref, w_ref, gm_ref, kg_ref, win_ref, kT_ref, v_ref, wb_ref, wkT_ref,
                   *, n_batches):
    slab = win_ref[...]
    wb_ref[...] = slab.astype(BF16)
    wkT_ref[...] = slab[:, IN_K0:IN_K0 + NA_WIDTH].T.astype(BF16)

    @pl.when(pl.program_id(0) < n_batches)
    def _():
        m = mem_ref[0]
        ms = jnp.mean(m * m, axis=-1, keepdims=True)
        mn = (m * lax.rsqrt(ms + EPS) * g_ref[...]).astype(BF16)
        kv = _dot(mn, w_ref[...].astype(BF16))
        k = kv[:, :MEM_WIDTH]
        msk = _dot((k * k).astype(BF16), gm_ref[...])
        kn = k * lax.rsqrt(msk + EPS) * kg_ref[...]
        kT_ref[0] = kn.T.astype(BF16)
        v_ref[0] = kv[:, MEM_WIDTH:].astype(BF16)


def _mem_kv(mem, mem_norm_g, w_mem_kv, gmean_mem, mem_k_g, w_in):
    B, M, _ = mem.shape
    assert B <= PREP_STEPS
    rows = D_MODEL // PREP_STEPS
    batch_of = lambda i: jnp.minimum(i, B - 1)
    return pl.pallas_call(
        functools.partial(_mem_kv_kernel, n_batches=B),
        grid=(PREP_STEPS,),
        in_specs=[
            pl.BlockSpec((1, M, D_MODEL), lambda i: (batch_of(i), 0, 0)),
            _const_spec((1, D_MODEL)),
            _const_spec((D_MODEL, 2 * MEM_WIDTH)),
            _const_spec((MEM_WIDTH, MEM_WIDTH)),
            _const_spec((1, MEM_WIDTH)),
            pl.BlockSpec((rows, IN_WIDTH), lambda i: (i, 0)),
        ],
        out_specs=[
            pl.BlockSpec((1, MEM_WIDTH, M), lambda i: (batch_of(i), 0, 0)),
            pl.BlockSpec((1, M, MEM_WIDTH), lambda i: (batch_of(i), 0, 0)),
            pl.BlockSpec((rows, IN_WIDTH), lambda i: (i, 0)),
            pl.BlockSpec((NA_WIDTH, rows), lambda i: (0, i)),
        ],
        out_shape=[
            jax.ShapeDtypeStruct((B, MEM_WIDTH, M), BF16),
            jax.ShapeDtypeStruct((B, M, MEM_WIDTH), BF16),
            jax.ShapeDtypeStruct((D_MODEL, IN_WIDTH), BF16),
            jax.ShapeDtypeStruct((NA_WIDTH, D_MODEL), BF16),
        ],
        compiler_params=_params(1),
        name="mem_kv",
    )(mem, mem_norm_g.reshape(1, D_MODEL), w_mem_kv, gmean_mem,
      jnp.tile(mem_k_g, MEM_HEADS).reshape(1, MEM_WIDTH), w_in)


def _in_proj_kernel(x_ref, g1_ref, w_ref, wkT_ref, bg_ref,
                    qg_ref, kg_ref, mqg_ref, memkT_ref, memv_ref, brow_ref, *rest,
                    n_narrow, bias_dr):
    wide_refs = rest[:n_narrow]
    q_ref, kT_ref, v_ref, fu_ref, om_ref, gate_ref, bias_ref = rest[n_narrow:n_narrow + 7]

    x = x_ref[...]
    ms = jnp.mean(x * x, axis=-1, keepdims=True)
    h = (x * lax.rsqrt(ms + EPS) * g1_ref[...]).astype(BF16)

    def gate_dot(br):
        sl = slice(br * D_MODEL, (br + 1) * D_MODEL)
        zg = _dot(h, w_ref[:, IN_G0 + br * D_MODEL:IN_G0 + (br + 1) * D_MODEL]) + bg_ref[:, sl]
        gate_ref[:, sl] = jax.nn.sigmoid(zg).astype(BF16)

    zm = _dot(h, w_ref[:, IN_M0:IN_M0 + MEM_WIDTH])
    mem_scores = []
    for hd in range(MEM_HEADS):
        sl = slice(hd * MEM_HEAD_DIM, (hd + 1) * MEM_HEAD_DIM)
        zh = zm[:, sl]
        msm = jnp.mean(zh * zh, axis=-1, keepdims=True)
        qm = (zh * lax.rsqrt(msm + EPS) * mqg_ref[:, sl]).astype(BF16)
        if hd == 0:
            gate_dot(0)
        mem_scores.append(_dot(qm, memkT_ref[0, sl, :]))
    gate_dot(1)

    zq = _dot(h, w_ref[:, IN_Q0:IN_Q0 + NA_WIDTH])
    low_half = lax.broadcasted_iota(jnp.int32, (zq.shape[0], V7X_LANES), 1) < NA_HEAD_DIM
    for hp in range(NA_HEADS // 2):
        sl = slice(hp * V7X_LANES, (hp + 1) * V7X_LANES)
        zz = zq[:, sl] * zq[:, sl]
        ms_lo = jnp.sum(jnp.where(low_half, zz, 0.0), axis=-1, keepdims=True)
        ms_hi = jnp.sum(jnp.where(low_half, 0.0, zz), axis=-1, keepdims=True)
        ms = jnp.where(low_half, ms_lo, ms_hi) * (1.0 / NA_HEAD_DIM)
        q_ref[:, sl] = (zq[:, sl] * lax.rsqrt(ms + EPS) * qg_ref[:, sl]).astype(BF16)

    zkT = _dot_nt(wkT_ref[...], h)
    zk3 = zkT.reshape(NA_HEADS, NA_HEAD_DIM, zkT.shape[1])
    msk = jnp.mean(zk3 * zk3, axis=1, keepdims=True)
    kT = ((zk3 * lax.rsqrt(msk + EPS)).reshape(zkT.shape) * kg_ref[...]).astype(BF16)
    for j in range(kT_ref.shape[1]):
        kT_ref[0, j] = kT[:, j * V7X_LANES:(j + 1) * V7X_LANES]

    zv = _dot(h, w_ref[:, IN_V0:IN_V0 + NA_WIDTH])
    low_half = lax.broadcasted_iota(jnp.int32, (zv.shape[0], V7X_LANES), 1) < NA_HEAD_DIM
    for hp in range(NA_HEADS // 2):
        pair = zv[:, hp * V7X_LANES:(hp + 1) * V7X_LANES]
        v_ref[:, (2 * hp) * V7X_LANES:(2 * hp + 1) * V7X_LANES] = jnp.where(low_half, pair, 1.0).astype(BF16)
        v_ref[:, (2 * hp + 1) * V7X_LANES:(2 * hp + 2) * V7X_LANES] = jnp.where(low_half, 1.0, pair).astype(BF16)
    fu_ref[...] = _dot(h, w_ref[:, IN_F0:IN_F0 + F_WIDTH]).astype(BF16)
    gate_dot(2)

    outs = []
    for hd in range(MEM_HEADS):
        sl = slice(hd * MEM_HEAD_DIM, (hd + 1) * MEM_HEAD_DIM)
        s = mem_scores[hd]
        e = jnp.exp(s - jnp.max(s, axis=-1, keepdims=True))
        l = jnp.sum(e, axis=-1, keepdims=True)
        outs.append(_dot(e.astype(BF16), memv_ref[0, :, sl]) / l)
    om_ref[...] = jnp.concatenate(outs, axis=1).astype(BF16)

    for wide_ref, narrow_ref in zip(wide_refs, rest[n_narrow + 7:]):
        narrow_ref[...] = wide_ref[...].astype(BF16)

    @pl.when(pl.program_id(0) < NA_HEADS)
    def _():
        _assemble_na_bias(brow_ref, bias_ref, bias_dr)


def _in_proj(x2, norm1_g, w_in_bf16, w_kT, b_gate, na_q_g, na_k_g, mem_q_g,
             memkT, memv, bias_blocks, later_weights, tokens_per_batch, tm):
    T = x2.shape[0]
    bias_rows, bias_dr = bias_blocks
    n_cases = bias_dr.shape[0]
    M = memv.shape[1]
    tiles_per_batch = tokens_per_batch // tm
    slabs = tm // V7X_LANES
    B = T // tokens_per_batch
    na_scale = np.float32(1.0 / np.sqrt(NA_HEAD_DIM))
    mem_scale = np.float32(1.0 / np.sqrt(MEM_HEAD_DIM))
    qg = (jnp.tile(na_q_g, NA_HEADS) * na_scale).reshape(1, NA_WIDTH)
    kg = jnp.tile(na_k_g, NA_HEADS).reshape(NA_WIDTH, 1)
    mqg = (jnp.tile(mem_q_g, MEM_HEADS) * mem_scale).reshape(1, MEM_WIDTH)
    tok = lambda w: pl.BlockSpec((tm, w), lambda i: (i, 0))
    batch_of = lambda i: i // tiles_per_batch
    steps = T // tm
    assert steps >= NA_HEADS
    slab_specs = [pl.BlockSpec((w.shape[0] // steps, w.shape[1]), lambda i: (i, 0))
                  for w in later_weights]
    head_of = lambda i: jnp.minimum(i, NA_HEADS - 1)
    bias_rows_spec = pl.BlockSpec((1,) + bias_rows.shape[1:], lambda i: (head_of(i), 0, 0))
    return pl.pallas_call(
        functools.partial(_in_proj_kernel, n_narrow=len(later_weights), bias_dr=bias_dr),
        grid=(steps,),
        in_specs=[
            tok(D_MODEL),
            _const_spec((1, D_MODEL)),
            _const_spec((D_MODEL, IN_WIDTH)),
            _const_spec((NA_WIDTH, D_MODEL)),
            _const_spec((1, N_BRANCHES * D_MODEL)),
            _const_spec((1, NA_WIDTH)),
            _const_spec((NA_WIDTH, 1)),
            _const_spec((1, MEM_WIDTH)),
            pl.BlockSpec((1, MEM_WIDTH, M), lambda i: (batch_of(i), 0, 0)),
            pl.BlockSpec((1, M, MEM_WIDTH), lambda i: (batch_of(i), 0, 0)),
            bias_rows_spec,
        ] + slab_specs,
        out_specs=[
            tok(NA_WIDTH),
            pl.BlockSpec((1, slabs, NA_WIDTH, V7X_LANES),
                         lambda i: (batch_of(i), i % tiles_per_batch, 0, 0)),
            tok(NA_HEADS * V7X_LANES),
            tok(F_WIDTH),
            tok(MEM_WIDTH),
            tok(N_BRANCHES * D_MODEL),
            pl.BlockSpec((n_cases, 1, NA_PAIR_TOKENS, NA_KEY_TOKENS), lambda i: (0, head_of(i), 0, 0)),
        ] + slab_specs,
        out_shape=[
            jax.ShapeDtypeStruct((T, NA_WIDTH), BF16),
            jax.ShapeDtypeStruct((B, tokens_per_batch // V7X_LANES, NA_WIDTH, V7X_LANES), BF16),
            jax.ShapeDtypeStruct((T, NA_HEADS * V7X_LANES), BF16),
            jax.ShapeDtypeStruct((T, F_WIDTH), BF16),
            jax.ShapeDtypeStruct((T, MEM_WIDTH), BF16),
            jax.ShapeDtypeStruct((T, N_BRANCHES * D_MODEL), BF16),
            jax.ShapeDtypeStruct((n_cases, NA_HEADS, NA_PAIR_TOKENS, NA_KEY_TOKENS), F32),
        ] + [jax.ShapeDtypeStruct(w.shape, BF16) for w in later_weights],
        compiler_params=_params(1),
        name="in_proj",
    )(x2, norm1_g.reshape(1, D_MODEL), w_in_bf16, w_kT,
      b_gate.reshape(1, N_BRANCHES * D_MODEL), qg, kg, mqg, memkT, memv, bias_rows,
      *later_weights)


def _na_bias_blocks(rpb, rows):
    n_pairs = rows // 2
    wr = min(NA_WIN_ROWS, rows)
    reps = np.array([0, 1, 2, n_pairs - 2, n_pairs - 1])
    starts = np.clip(reps - 2, 0, n_pairs - NA_KEY_PAIRS)
    n_dr, n_dc = 2 * NA_WIN_ROWS - 1, 2 * NA_WIN_COLS - 1
    offsets = np.arange(n_dc) - (NA_WIN_COLS - 1)
    place = (np.arange(2 * GRID_W)[None, :] % GRID_W == (offsets % GRID_W)[:, None]).astype(np.float32)
    rows128 = jnp.dot(rpb.astype(F32).reshape(NA_HEADS * n_dr, n_dc), place,
                      precision=lax.Precision.HIGHEST).reshape(NA_HEADS, n_dr, 2 * GRID_W)
    r = (2 * reps[:, None] + np.arange(2)[None, :])[:, :, None]
    kr = (2 * starts[:, None] + np.arange(2 * NA_KEY_PAIRS)[None, :])[:, None, :]
    rs = np.clip(r - wr // 2, 0, rows - wr)
    row_valid = (kr >= rs) & (kr < rs + wr)
    dr = np.where(row_valid, kr - r + (NA_WIN_ROWS - 1), n_dr)
    assert dr.min() >= 0 and dr.max() <= n_dr
    return rows128, dr


def _assemble_na_bias(rows_ref, o_ref, dr):
    n_dr = rows_ref.shape[1]
    c = lax.broadcasted_iota(jnp.int32, (GRID_W, 2 * GRID_W), 0)
    lane = lax.broadcasted_iota(jnp.int32, (GRID_W, 2 * GRID_W), 1)
    kc = lane & (GRID_W - 1)
    cs = jnp.clip(c - NA_WIN_COLS // 2, 0, GRID_W - NA_WIN_COLS)
    col_valid = (kc >= cs) & (kc < cs + NA_WIN_COLS)
    masked = jnp.full((GRID_W, 2 * GRID_W), NEG_INF, F32)
    blocks = []
    for i in range(n_dr):
        row = jnp.broadcast_to(rows_ref[0, i:i + 1, :], (GRID_W, 2 * GRID_W))
        toeplitz = pltpu.roll(row, 0, axis=1, stride=1, stride_axis=0)
        blocks.append(jnp.where(col_valid, toeplitz, masked))
    blocks.append(masked)
    even_row = lane < GRID_W
    for case in range(dr.shape[0]):
        for rr in range(2):
            for j in range(NA_KEY_PAIRS):
                tile = jnp.where(even_row, blocks[int(dr[case, rr, 2 * j])],
                                 blocks[int(dr[case, rr, 2 * j + 1])])
                o_ref[case, 0, rr * GRID_W:(rr + 1) * GRID_W,
                      j * V7X_LANES:(j + 1) * V7X_LANES] = tile


def _na_kernel(q_ref, kT_ref, v_ref, bias_ref, o_ref, *, pairs_per_step, n_pairs):
    lane = lax.broadcasted_iota(jnp.int32, (NA_PAIR_TOKENS, 2 * NA_HEAD_DIM), 1)
    first_head = lane < NA_HEAD_DIM

    def pair_coords(i):
        p = pl.program_id(1) * pairs_per_step + i
        start = jnp.clip(p - 2, 0, n_pairs - NA_KEY_PAIRS)
        case = jnp.where(p < 2, p, jnp.where(p >= n_pairs - 2, p - (n_pairs - NA_BIAS_CASES), 2))
        qrow = pl.multiple_of(i * NA_PAIR_TOKENS, NA_PAIR_TOKENS)
        krow = pl.multiple_of(start * NA_PAIR_TOKENS, NA_PAIR_TOKENS)
        return start, case, qrow, krow

    kzero = jnp.zeros((NA_HEAD_DIM, NA_KEY_TOKENS), BF16)
    vzero = jnp.zeros((NA_KEY_TOKENS, 2 * NA_HEAD_DIM), BF16)

    def scores(coords, hp):
        start, case, qrow, _ = coords
        psl = slice(hp * 2 * NA_HEAD_DIM, (hp + 1) * 2 * NA_HEAD_DIM)
        qp = q_ref[0, pl.ds(qrow, NA_PAIR_TOKENS), psl]
        kslabs = kT_ref[0, pl.ds(start, NA_KEY_PAIRS), psl, :]
        kT = jnp.concatenate([kslabs[j] for j in range(NA_KEY_PAIRS)], axis=1)
        keys = jnp.concatenate([jnp.concatenate([kT[:NA_HEAD_DIM], kzero], axis=1),
                                jnp.concatenate([kzero, kT[NA_HEAD_DIM:]], axis=1)], axis=0)
        bias = jnp.concatenate([bias_ref[case, 2 * hp], bias_ref[case, 2 * hp + 1]], axis=1)
        return _dot(qp, keys) + bias

    def body(it, carry):
        coords = [pair_coords(it * NA_PAIRS_PER_ITER + j) for j in range(NA_PAIRS_PER_ITER)]
        items = [(j, hp) for j in range(NA_PAIRS_PER_ITER) for hp in range(NA_HEADS // 2)]
        pending = [scores(coords[j], hp) for j, hp in items[:NA_SCORES_AHEAD]]
        outs = []
        for n, (j, hp) in enumerate(items):
            s = pending.pop(0)
            if n + NA_SCORES_AHEAD < len(items):
                jn, hn = items[n + NA_SCORES_AHEAD]
                pending.append(scores(coords[jn], hn))
            _, _, qrow, krow = coords[j]
            es, vs = [], []
            for sub in range(2):
                hd = 2 * hp + sub
                sh = s[:, sub * NA_KEY_TOKENS:(sub + 1) * NA_KEY_TOKENS]
                es.append(jnp.exp((sh - jnp.max(sh, axis=-1, keepdims=True)).astype(BF16)))
                vs.append(v_ref[0, pl.ds(krow, NA_KEY_TOKENS), hd * 2 * NA_HEAD_DIM:(hd + 1) * 2 * NA_HEAD_DIM])
            values = jnp.concatenate([jnp.concatenate([vs[0], vzero], axis=1),
                                      jnp.concatenate([vzero, vs[1]], axis=1)], axis=0)
            r = _dot(jnp.concatenate(es, axis=1), values)
            halves = [r[:, sub * 2 * NA_HEAD_DIM:(sub + 1) * 2 * NA_HEAD_DIM] for sub in range(2)]
            halves = [h / pltpu.roll(h, NA_HEAD_DIM, axis=1) for h in halves]
            outs.append(jnp.where(first_head, halves[0], halves[1]))
            if hp == NA_HEADS // 2 - 1:
                o_ref[0, pl.ds(qrow, NA_PAIR_TOKENS), :] = jnp.concatenate(outs, axis=1).astype(BF16)
                outs = []
        return carry

    lax.fori_loop(0, pairs_per_step // NA_PAIRS_PER_ITER, body, 0)


def _na_attention(q, kT4, v, bias, pairs_per_step):
    B, S, _ = q.shape
    n_pairs = S // NA_PAIR_TOKENS
    steps = n_pairs // pairs_per_step
    tq = pairs_per_step * NA_PAIR_TOKENS
    return pl.pallas_call(
        functools.partial(_na_kernel, pairs_per_step=pairs_per_step, n_pairs=n_pairs),
        grid=(B, steps),
        in_specs=[
            pl.BlockSpec((1, tq, NA_WIDTH), lambda b, j: (b, j, 0)),
            pl.BlockSpec((1, n_pairs, NA_WIDTH, V7X_LANES), lambda b, j: (b, 0, 0, 0)),
            pl.BlockSpec((1, S, NA_HEADS * V7X_LANES), lambda b, j: (b, 0, 0)),
            _const_spec((NA_BIAS_CASES, NA_HEADS, NA_PAIR_TOKENS, NA_KEY_TOKENS)),
        ],
        out_specs=pl.BlockSpec((1, tq, NA_WIDTH), lambda b, j: (b, j, 0)),
        out_shape=jax.ShapeDtypeStruct((B, S, NA_WIDTH), BF16),
        compiler_params=_params(2),
        name="na_attn",
    )(q, kT4, v, bias)


DFT_CHUNK = 16


def _dft_tables():
    n, m = GRID_W, DFT_CHUNK
    a = np.arange(n)
    ang1 = 2.0 * np.pi * np.outer(a, a) / n
    w_ri = np.stack([np.cos(ang1), -np.sin(ang1)], axis=1) / np.sqrt(n)
    eye = np.eye(m)
    w_rows = w_ri.reshape(2 * n, n)
    perm = np.einsum('dx,cy->dcyx', eye, eye).reshape(m * m, m * m)
    c = np.arange(n)[:, None, None]
    d = np.arange(n)[None, :, None]
    b = np.arange(n)[None, None, :]
    ang2 = 2.0 * np.pi * ((c + n * d) * b % (n * n)) / (n * n)
    gr, gi = np.cos(ang2) / np.sqrt(n), -np.sin(ang2) / np.sqrt(n)
    w_cols = np.concatenate([np.concatenate([gr, -gi], axis=2),
                             np.concatenate([gi, gr], axis=2)], axis=1)
    ch = np.arange(F_GROUP_DIM)
    ang3 = 2.0 * np.pi * np.outer(ch, ch) / F_GROUP_DIM
    w_chan = np.concatenate([np.cos(ang3), np.sin(ang3)], axis=0) / np.sqrt(F_GROUP_DIM)
    return tuple(jnp.asarray(t, dtype=F32).astype(BF16) for t in (w_rows, w_cols, perm, w_chan))


def _dft_kernel(wr_ref, wc_ref, perm_ref, u_ref, y_ref, z_ref):
    n, m = GRID_W, DFT_CHUNK

    def rows_stage(j, carry):
        col0 = pl.multiple_of(j * m, m)
        x = u_ref[0, :, pl.ds(col0, m), :].reshape(n * m, F_WIDTH)
        xp = [_dot(perm_ref[...], x[g * m * m:(g + 1) * m * m]).astype(BF16) for g in range(n // m)]
        zb = []
        for b in range(m):
            xb = jnp.concatenate([xp[g][b * m:(b + 1) * m] for g in range(n // m)], axis=0)
            zb.append(_dot(wr_ref[...], xb).astype(BF16))
        for t in range(2 * n // m):
            st = jnp.concatenate([zb[b][t * m:(t + 1) * m] for b in range(m)], axis=0)
            z = _dot(perm_ref[...], st).astype(BF16)
            z_ref[t * 8:(t + 1) * 8, :, pl.ds(col0, m), :] = z.reshape(8, 2, m, F_WIDTH)
        return carry

    def cols_stage(j, carry):
        col0 = pl.multiple_of(j * m, m)
        res = []
        for c_lo in range(m):
            c = j * m + c_lo
            data = z_ref[c].reshape(2 * n, F_WIDTH)
            res.append(_dot(wc_ref[c], data).astype(BF16))
        for ri in range(2):
            for g in range(n // m):
                lo = ri * n + g * m
                stacked = jnp.concatenate([r[lo:lo + m] for r in res], axis=0)
                y = _dot(perm_ref[...], stacked).astype(BF16)
                y_ref[ri, 0, g * m:(g + 1) * m, pl.ds(col0, m), :] = y.reshape(m, m, F_WIDTH)
        return carry

    lax.fori_loop(0, n // m, rows_stage, 0)
    lax.fori_loop(0, n // m, cols_stage, 0)


def _position_dft(u4, w_rows, w_cols, perm):
    B, n, m = u4.shape[0], GRID_W, DFT_CHUNK
    return pl.pallas_call(
        _dft_kernel,
        grid=(B,),
        in_specs=[
            _const_spec((2 * n, n)),
            _const_spec((n, 2 * n, 2 * n)),
            _const_spec((m * m, m * m)),
            pl.BlockSpec((1, n, n, F_WIDTH), lambda b: (b, 0, 0, 0)),
        ],
        out_specs=pl.BlockSpec((2, 1, n, n, F_WIDTH), lambda b: (0, b, 0, 0, 0)),
        out_shape=jax.ShapeDtypeStruct((2, B, n, n, F_WIDTH), BF16),
        scratch_shapes=[pltpu.VMEM((n, 2, n, F_WIDTH), BF16)],
        compiler_params=_params(1),
        name="position_dft",
    )(w_rows, w_cols, perm, u4)


def _out_ffn_kernel(x_ref, ona_ref, y_ref, om_ref, gate_ref, wch_ref, wna_ref, wf_ref, wmo_ref,
                    wout_ref, g2_ref, w1_ref, w2_ref, o_ref, *, ff_chunk):
    yr, yi = y_ref[0], y_ref[1]
    groups = []
    for g in range(F_GROUPS):
        sl = slice(g * F_GROUP_DIM, (g + 1) * F_GROUP_DIM)
        groups.append(_dot(jnp.concatenate([yr[:, sl], yi[:, sl]], axis=1), wch_ref[...]))
    yf = jnp.concatenate(groups, axis=1).astype(BF16)

    gate = lambda br: gate_ref[:, br * D_MODEL:(br + 1) * D_MODEL].astype(F32)
    merged = gate(0) * _dot(ona_ref[...], wna_ref[...])
    merged = merged + gate(1) * _dot(yf, wf_ref[...])
    merged = merged + gate(2) * _dot(om_ref[...], wmo_ref[...])
    x1 = x_ref[...] + _dot(merged.astype(BF16), wout_ref[...])

    ms = jnp.mean(x1 * x1, axis=-1, keepdims=True)
    h2 = (x1 * lax.rsqrt(ms + EPS) * g2_ref[...]).astype(BF16)
    acc = x1
    for j in range(D_FF // ff_chunk):
        sl = slice(j * ff_chunk, (j + 1) * ff_chunk)
        a = jnp.maximum(_dot(h2, w1_ref[:, sl]), 0.0)
        acc = acc + _dot((a * a).astype(BF16), w2_ref[sl, :])
    o_ref[...] = acc


def _out_ffn(x2, ona, y3, om, gates, w_chan, w_na_o, w_f, w_mem_o, w_out, norm2_g, w_ff1, w_ff2,
             tm, ff_chunk):
    T = x2.shape[0]
    tok = lambda w: pl.BlockSpec((tm, w), lambda i: (i, 0))
    return pl.pallas_call(
        functools.partial(_out_ffn_kernel, ff_chunk=ff_chunk),
        grid=(T // tm,),
        in_specs=[
            tok(D_MODEL),
            tok(NA_WIDTH),
            pl.BlockSpec((2, tm, F_WIDTH), lambda i: (0, i, 0)),
            tok(MEM_WIDTH),
            tok(N_BRANCHES * D_MODEL),
            _const_spec((2 * F_GROUP_DIM, F_GROUP_DIM)),
            _const_spec((NA_WIDTH, D_MODEL)),
            _const_spec((F_WIDTH, D_MODEL)),
            _const_spec((MEM_WIDTH, D_MODEL)),
            _const_spec((D_MODEL, D_MODEL)),
            _const_spec((1, D_MODEL)),
            _const_spec((D_MODEL, D_FF)),
            _const_spec((D_FF, D_MODEL)),
        ],
        out_specs=tok(D_MODEL),
        out_shape=jax.ShapeDtypeStruct((T, D_MODEL), F32),
        compiler_params=_params(1),
        name="out_ffn",
    )(x2, ona, y3, om, gates, w_chan, w_na_o, w_f, w_mem_o, w_out, norm2_g.reshape(1, D_MODEL),
      w_ff1, w_ff2)


def kernel(x, mem, norm1_g, w_in, b_gate, na_q_g, na_k_g, na_rpb, w_na_o, w_f, mem_norm_g,
           w_mem_kv, mem_q_g, mem_k_g, w_mem_o, w_out, norm2_g, w_ff1, w_ff2):
    B, S, _ = x.shape
    T = B * S
    rows = S // GRID_W
    assert rows == GRID_W and S % NA_PAIR_TOKENS == 0
    x2 = x.reshape(T, D_MODEL)
    gmean_mem = _group_mean_matrix(MEM_WIDTH, MEM_HEAD_DIM)
    w_rows, w_cols, perm, w_chan = _dft_tables()

    memkT, memv, w_in_bf16, w_kT = _mem_kv(mem, mem_norm_g, w_mem_kv, gmean_mem, mem_k_g, w_in)
    later_weights = (w_na_o, w_f, w_mem_o, w_out, w_ff1, w_ff2)
    q, kT4, v, fu, om, gates, bias, *narrow = _in_proj(
        x2, norm1_g, w_in_bf16, w_kT, b_gate, na_q_g, na_k_g, mem_q_g, memkT, memv,
        _na_bias_blocks(na_rpb, rows), later_weights, S, tm=IN_PROJ_TOKENS)
    w_na_o, w_f, w_mem_o, w_out, w_ff1, w_ff2 = narrow

    ona = _na_attention(q.reshape(B, S, NA_WIDTH), kT4, v.reshape(B, S, NA_HEADS * V7X_LANES), bias,
                        pairs_per_step=NA_PAIRS_PER_STEP)

    y = _position_dft(fu.reshape(B, GRID_W, GRID_W, F_WIDTH), w_rows, w_cols, perm)
    y3 = y.reshape(2, T, F_WIDTH)

    out = _out_ffn(x2, ona.reshape(T, NA_WIDTH), y3, om, gates, w_chan, w_na_o, w_f, w_mem_o,
                   w_out, norm2_g, w_ff1, w_ff2, tm=OUT_FFN_TOKENS, ff_chunk=FF_CHUNK)
    return out.reshape(B, S, D_MODEL)
```

```python
import functools

import numpy as np
import jax
import jax.numpy as jnp
from jax import lax
from jax.experimental import pallas as pl
from jax.experimental.pallas import tpu as pltpu

D_MODEL = 1024
GRID_W = 64
NA_HEADS = 8
NA_HEAD_DIM = 64
NA_WIDTH = NA_HEADS * NA_HEAD_DIM
NA_WIN_ROWS = 8
NA_WIN_COLS = 16
F_GROUPS = 4
F_GROUP_DIM = 128
F_WIDTH = F_GROUPS * F_GROUP_DIM
MEM_HEADS = 4
MEM_HEAD_DIM = 128
MEM_WIDTH = MEM_HEADS * MEM_HEAD_DIM
N_BRANCHES = 3
D_FF = 4 * D_MODEL
IN_Q0 = 0
IN_K0 = IN_Q0 + NA_WIDTH
IN_V0 = IN_K0 + NA_WIDTH
IN_F0 = IN_V0 + NA_WIDTH
IN_M0 = IN_F0 + F_WIDTH
IN_G0 = IN_M0 + MEM_WIDTH
IN_WIDTH = IN_G0 + N_BRANCHES * D_MODEL
EPS = 1e-6
NEG_INF = -1e30

V7X_LANES = 128
V7X_VMEM_LIMIT_BYTES = 60 * 1024 * 1024

NA_PAIR_TOKENS = 2 * GRID_W
NA_KEY_PAIRS = 5
NA_KEY_TOKENS = NA_KEY_PAIRS * NA_PAIR_TOKENS
NA_BIAS_CASES = 5
NA_PAIRS_PER_ITER = 16
NA_SCORES_AHEAD = 2

IN_PROJ_TOKENS = 512
OUT_FFN_TOKENS = 512
FF_CHUNK = 1024
NA_PAIRS_PER_STEP = 16

BF16 = jnp.bfloat16
F32 = jnp.float32


def _dot(a, b):
    return jnp.dot(a, b, preferred_element_type=F32)


def _dot_nt(a, b):
    return lax.dot_general(a, b, (((1,), (1,)), ((), ())), preferred_element_type=F32)


def _const_spec(shape):
    return pl.BlockSpec(shape, lambda *_: (0,) * len(shape), pipeline_mode=pl.Buffered(1))


def _params(n_axes):
    return pltpu.CompilerParams(
        dimension_semantics=("arbitrary",) * n_axes,
        vmem_limit_bytes=V7X_VMEM_LIMIT_BYTES,
    )


def _group_mean_matrix(width, group):
    idx = np.arange(width) // group
    return jnp.asarray((idx[:, None] == idx[None, :]).astype(np.float32) / group, dtype=BF16)


PREP_STEPS = 4


def _mem_kv_kernel(mem_ref, g_ref, w_ref, gm_ref, kg_ref, win_ref, kT_ref, v_ref, wb_ref, wkT_ref,
                   *, n_batches):
    slab = win_ref[...]
    wb_ref[...] = slab.astype(BF16)
    wkT_ref[...] = slab[:, IN_K0:IN_K0 + NA_WIDTH].T.astype(BF16)

    @pl.when(pl.program_id(0) < n_batches)
    def _():
        m = mem_ref[0]
        ms = jnp.mean(m * m, axis=-1, keepdims=True)
        mn = (m * lax.rsqrt(ms + EPS) * g_ref[...]).astype(BF16)
        kv = _dot(mn, w_ref[...].astype(BF16))
        k = kv[:, :MEM_WIDTH]
        msk = _dot((k * k).astype(BF16), gm_ref[...])
        kn = k * lax.rsqrt(msk + EPS) * kg_ref[...]
        kT_ref[0] = kn.T.astype(BF16)
        v_ref[0] = kv[:, MEM_WIDTH:].astype(BF16)


def _mem_kv(mem, mem_norm_g, w_mem_kv, gmean_mem, mem_k_g, w_in):
    B, M, _ = mem.shape
    assert B <= PREP_STEPS
    rows = D_MODEL // PREP_STEPS
    batch_of = lambda i: jnp.minimum(i, B - 1)
    return pl.pallas_call(
        functools.partial(_mem_kv_kernel, n_batches=B),
        grid=(PREP_STEPS,),
        in_specs=[
            pl.BlockSpec((1, M, D_MODEL), lambda i: (batch_of(i), 0, 0)),
            _const_spec((1, D_MODEL)),
            _const_spec((D_MODEL, 2 * MEM_WIDTH)),
            _const_spec((MEM_WIDTH, MEM_WIDTH)),
            _const_spec((1, MEM_WIDTH)),
            pl.BlockSpec((rows, IN_WIDTH), lambda i: (i, 0)),
        ],
        out_specs=[
            pl.BlockSpec((1, MEM_WIDTH, M), lambda i: (batch_of(i), 0, 0)),
            pl.BlockSpec((1, M, MEM_WIDTH), lambda i: (batch_of(i), 0, 0)),
            pl.BlockSpec((rows, IN_WIDTH), lambda i: (i, 0)),
            pl.BlockSpec((NA_WIDTH, rows), lambda i: (0, i)),
        ],
        out_shape=[
            jax.ShapeDtypeStruct((B, MEM_WIDTH, M), BF16),
            jax.ShapeDtypeStruct((B, M, MEM_WIDTH), BF16),
            jax.ShapeDtypeStruct((D_MODEL, IN_WIDTH), BF16),
            jax.ShapeDtypeStruct((NA_WIDTH, D_MODEL), BF16),
        ],
        compiler_params=_params(1),
        name="mem_kv",
    )(mem, mem_norm_g.reshape(1, D_MODEL), w_mem_kv, gmean_mem,
      jnp.tile(mem_k_g, MEM_HEADS).reshape(1, MEM_WIDTH), w_in)


def _in_proj_kernel(x_ref, g1_ref, w_ref, wkT_ref, bg_ref,
                    qg_ref, kg_ref, mqg_ref, memkT_ref, memv_ref, brow_ref, *rest,
                    n_narrow, bias_dr):
    wide_refs = rest[:n_narrow]
    q_ref, kT_ref, v_ref, fu_ref, om_ref, gate_ref, bias_ref = rest[n_narrow:n_narrow + 7]

    x = x_ref[...]
    ms = jnp.mean(x * x, axis=-1, keepdims=True)
    h = (x * lax.rsqrt(ms + EPS) * g1_ref[...]).astype(BF16)

    def gate_dot(br):
        sl = slice(br * D_MODEL, (br + 1) * D_MODEL)
        zg = _dot(h, w_ref[:, IN_G0 + br * D_MODEL:IN_G0 + (br + 1) * D_MODEL]) + bg_ref[:, sl]
        gate_ref[:, sl] = jax.nn.sigmoid(zg).astype(BF16)

    zm = _dot(h, w_ref[:, IN_M0:IN_M0 + MEM_WIDTH])
    mem_scores = []
    for hd in range(MEM_HEADS):
        sl = slice(hd * MEM_HEAD_DIM, (hd + 1) * MEM_HEAD_DIM)
        zh = zm[:, sl]
        msm = jnp.mean(zh * zh, axis=-1, keepdims=True)
        qm = (zh * lax.rsqrt(msm + EPS) * mqg_ref[:, sl]).astype(BF16)
        if hd == 0:
            gate_dot(0)
        mem_scores.append(_dot(qm, memkT_ref[0, sl, :]))
    gate_dot(1)

    zq = _dot(h, w_ref[:, IN_Q0:IN_Q0 + NA_WIDTH])
    low_half = lax.broadcasted_iota(jnp.int32, (zq.shape[0], V7X_LANES), 1) < NA_HEAD_DIM
    for hp in range(NA_HEADS // 2):
        sl = slice(hp * V7X_LANES, (hp + 1) * V7X_LANES)
        zz = zq[:, sl] * zq[:, sl]
        ms_lo = jnp.sum(jnp.where(low_half, zz, 0.0), axis=-1, keepdims=True)
        ms_hi = jnp.sum(jnp.where(low_half, 0.0, zz), axis=-1, keepdims=True)
        ms = jnp.where(low_half, ms_lo, ms_hi) * (1.0 / NA_HEAD_DIM)
        q_ref[:, sl] = (zq[:, sl] * lax.rsqrt(ms + EPS) * qg_ref[:, sl]).astype(BF16)

    zkT = _dot_nt(wkT_ref[...], h)
    zk3 = zkT.reshape(NA_HEADS, NA_HEAD_DIM, zkT.shape[1])
    msk = jnp.mean(zk3 * zk3, axis=1, keepdims=True)
    kT = ((zk3 * lax.rsqrt(msk + EPS)).reshape(zkT.shape) * kg_ref[...]).astype(BF16)
    for j in range(kT_ref.shape[1]):
        kT_ref[0, j] = kT[:, j * V7X_LANES:(j + 1) * V7X_LANES]

    zv = _dot(h, w_ref[:, IN_V0:IN_V0 + NA_WIDTH])
    low_half = lax.broadcasted_iota(jnp.int32, (zv.shape[0], V7X_LANES), 1) < NA_HEAD_DIM
    for hp in range(NA_HEADS // 2):
        pair = zv[:, hp * V7X_LANES:(hp + 1) * V7X_LANES]
        v_ref[:, (2 * hp) * V7X_LANES:(2 * hp + 1) * V7X_LANES] = jnp.where(low_half, pair, 1.0).astype(BF16)
        v_ref[:, (2 * hp + 1) * V7X_LANES:(2 * hp + 2) * V7X_LANES] = jnp.where(low_half, 1.0, pair).astype(BF16)
    fu_ref[...] = _dot(h, w_ref[:, IN_F0:IN_F0 + F_WIDTH]).astype(BF16)
    gate_dot(2)

    outs = []
    for hd in range(MEM_HEADS):
        sl = slice(hd * MEM_HEAD_DIM, (hd + 1) * MEM_HEAD_DIM)
        s = mem_scores[hd]
        e = jnp.exp(s - jnp.max(s, axis=-1, keepdims=True))
        l = jnp.sum(e, axis=-1, keepdims=True)
        outs.append(_dot(e.astype(BF16), memv_ref[0, :, sl]) / l)
    om_ref[...] = jnp.concatenate(outs, axis=1).astype(BF16)

    for wide_ref, narrow_ref in zip(wide_refs, rest[n_narrow + 7:]):
        narrow_ref[...] = wide_ref[...].astype(BF16)

    @pl.when(pl.program_id(0) < NA_HEADS)
    def _():
        _assemble_na_bias(brow_ref, bias_ref, bias_dr)


def _in_proj(x2, norm1_g, w_in_bf16, w_kT, b_gate, na_q_g, na_k_g, mem_q_g,
             memkT, memv, bias_blocks, later_weights, tokens_per_batch, tm):
    T = x2.shape[0]
    bias_rows, bias_dr = bias_blocks
    n_cases = bias_dr.shape[0]
    M = memv.shape[1]
    tiles_per_batch = tokens_per_batch // tm
    slabs = tm // V7X_LANES
    B = T // tokens_per_batch
    na_scale = np.float32(1.0 / np.sqrt(NA_HEAD_DIM))
    mem_scale = np.float32(1.0 / np.sqrt(MEM_HEAD_DIM))
    qg = (jnp.tile(na_q_g, NA_HEADS) * na_scale).reshape(1, NA_WIDTH)
    kg = jnp.tile(na_k_g, NA_HEADS).reshape(NA_WIDTH, 1)
    mqg = (jnp.tile(mem_q_g, MEM_HEADS) * mem_scale).reshape(1, MEM_WIDTH)
    tok = lambda w: pl.BlockSpec((tm, w), lambda i: (i, 0))
    batch_of = lambda i: i // tiles_per_batch
    steps = T // tm
    assert steps >= NA_HEADS
    slab_specs = [pl.BlockSpec((w.shape[0] // steps, w.shape[1]), lambda i: (i, 0))
                  for w in later_weights]
    head_of = lambda i: jnp.minimum(i, NA_HEADS - 1)
    bias_rows_spec = pl.BlockSpec((1,) + bias_rows.shape[1:], lambda i: (head_of(i), 0, 0))
    return pl.pallas_call(
        functools.partial(_in_proj_kernel, n_narrow=len(later_weights), bias_dr=bias_dr),
        grid=(steps,),
        in_specs=[
            tok(D_MODEL),
            _const_spec((1, D_MODEL)),
            _const_spec((D_MODEL, IN_WIDTH)),
            _const_spec((NA_WIDTH, D_MODEL)),
            _const_spec((1, N_BRANCHES * D_MODEL)),
            _const_spec((1, NA_WIDTH)),
            _const_spec((NA_WIDTH, 1)),
            _const_spec((1, MEM_WIDTH)),
            pl.BlockSpec((1, MEM_WIDTH, M), lambda i: (batch_of(i), 0, 0)),
            pl.BlockSpec((1, M, MEM_WIDTH), lambda i: (batch_of(i), 0, 0)),
            bias_rows_spec,
        ] + slab_specs,
        out_specs=[
            tok(NA_WIDTH),
            pl.BlockSpec((1, slabs, NA_WIDTH, V7X_LANES),
                         lambda i: (batch_of(i), i % tiles_per_batch, 0, 0)),
            tok(NA_HEADS * V7X_LANES),
            tok(F_WIDTH),
            tok(MEM_WIDTH),
            tok(N_BRANCHES * D_MODEL),
            pl.BlockSpec((n_cases, 1, NA_PAIR_TOKENS, NA_KEY_TOKENS), lambda i: (0, head_of(i), 0, 0)),
        ] + slab_specs,
        out_shape=[
            jax.ShapeDtypeStruct((T, NA_WIDTH), BF16),
            jax.ShapeDtypeStruct((B, tokens_per_batch // V7X_LANES, NA_WIDTH, V7X_LANES), BF16),
            jax.ShapeDtypeStruct((T, NA_HEADS * V7X_LANES), BF16),
            jax.ShapeDtypeStruct((T, F_WIDTH), BF16),
            jax.ShapeDtypeStruct((T, MEM_WIDTH), BF16),
            jax.ShapeDtypeStruct((T, N_BRANCHES * D_MODEL), BF16),
            jax.ShapeDtypeStruct((n_cases, NA_HEADS, NA_PAIR_TOKENS, NA_KEY_TOKENS), F32),
        ] + [jax.ShapeDtypeStruct(w.shape, BF16) for w in later_weights],
        compiler_params=_params(1),
        name="in_proj",
    )(x2, norm1_g.reshape(1, D_MODEL), w_in_bf16, w_kT,
      b_gate.reshape(1, N_BRANCHES * D_MODEL), qg, kg, mqg, memkT, memv, bias_rows,
      *later_weights)


def _na_bias_blocks(rpb, rows):
    n_pairs = rows // 2
    wr = min(NA_WIN_ROWS, rows)
    reps = np.array([0, 1, 2, n_pairs - 2, n_pairs - 1])
    starts = np.clip(reps - 2, 0, n_pairs - NA_KEY_PAIRS)
    n_dr, n_dc = 2 * NA_WIN_ROWS - 1, 2 * NA_WIN_COLS - 1
    offsets = np.arange(n_dc) - (NA_WIN_COLS - 1)
    place = (np.arange(2 * GRID_W)[None, :] % GRID_W == (offsets % GRID_W)[:, None]).astype(np.float32)
    rows128 = jnp.dot(rpb.astype(F32).reshape(NA_HEADS * n_dr, n_dc), place,
                      precision=lax.Precision.HIGHEST).reshape(NA_HEADS, n_dr, 2 * GRID_W)
    r = (2 * reps[:, None] + np.arange(2)[None, :])[:, :, None]
    kr = (2 * starts[:, None] + np.arange(2 * NA_KEY_PAIRS)[None, :])[:, None, :]
    rs = np.clip(r - wr // 2, 0, rows - wr)
    row_valid = (kr >= rs) & (kr < rs + wr)
    dr = np.where(row_valid, kr - r + (NA_WIN_ROWS - 1), n_dr)
    assert dr.min() >= 0 and dr.max() <= n_dr
    return rows128, dr


def _assemble_na_bias(rows_ref, o_ref, dr):
    n_dr = rows_ref.shape[1]
    c = lax.broadcasted_iota(jnp.int32, (GRID_W, 2 * GRID_W), 0)
    lane = lax.broadcasted_iota(jnp.int32, (GRID_W, 2 * GRID_W), 1)
    kc = lane & (GRID_W - 1)
    cs = jnp.clip(c - NA_WIN_COLS // 2, 0, GRID_W - NA_WIN_COLS)
    col_valid = (kc >= cs) & (kc < cs + NA_WIN_COLS)
    masked = jnp.full((GRID_W, 2 * GRID_W), NEG_INF, F32)
    blocks = []
    for i in range(n_dr):
        row = jnp.broadcast_to(rows_ref[0, i:i + 1, :], (GRID_W, 2 * GRID_W))
        toeplitz = pltpu.roll(row, 0, axis=1, stride=1, stride_axis=0)
        blocks.append(jnp.where(col_valid, toeplitz, masked))
    blocks.append(masked)
    even_row = lane < GRID_W
    for case in range(dr.shape[0]):
        for rr in range(2):
            for j in range(NA_KEY_PAIRS):
                tile = jnp.where(even_row, blocks[int(dr[case, rr, 2 * j])],
                                 blocks[int(dr[case, rr, 2 * j + 1])])
                o_ref[case, 0, rr * GRID_W:(rr + 1) * GRID_W,
                      j * V7X_LANES:(j + 1) * V7X_LANES] = tile


def _na_kernel(q_ref, kT_ref, v_ref, bias_ref, o_ref, *, pairs_per_step, n_pairs):
    lane = lax.broadcasted_iota(jnp.int32, (NA_PAIR_TOKENS, 2 * NA_HEAD_DIM), 1)
    first_head = lane < NA_HEAD_DIM

    def pair_coords(i):
        p = pl.program_id(1) * pairs_per_step + i
        start = jnp.clip(p - 2, 0, n_pairs - NA_KEY_PAIRS)
        case = jnp.where(p < 2, p, jnp.where(p >= n_pairs - 2, p - (n_pairs - NA_BIAS_CASES), 2))
        qrow = pl.multiple_of(i * NA_PAIR_TOKENS, NA_PAIR_TOKENS)
        krow = pl.multiple_of(start * NA_PAIR_TOKENS, NA_PAIR_TOKENS)
        return start, case, qrow, krow

    kzero = jnp.zeros((NA_HEAD_DIM, NA_KEY_TOKENS), BF16)
    vzero = jnp.zeros((NA_KEY_TOKENS, 2 * NA_HEAD_DIM), BF16)

    def scores(coords, hp):
        start, case, qrow, _ = coords
        psl = slice(hp * 2 * NA_HEAD_DIM, (hp + 1) * 2 * NA_HEAD_DIM)
        qp = q_ref[0, pl.ds(qrow, NA_PAIR_TOKENS), psl]
        kslabs = kT_ref[0, pl.ds(start, NA_KEY_PAIRS), psl, :]
        kT = jnp.concatenate([kslabs[j] for j in range(NA_KEY_PAIRS)], axis=1)
        keys = jnp.concatenate([jnp.concatenate([kT[:NA_HEAD_DIM], kzero], axis=1),
                                jnp.concatenate([kzero, kT[NA_HEAD_DIM:]], axis=1)], axis=0)
        bias = jnp.concatenate([bias_ref[case, 2 * hp], bias_ref[case, 2 * hp + 1]], axis=1)
        return _dot(qp, keys) + bias

    def body(it, carry):
        coords = [pair_coords(it * NA_PAIRS_PER_ITER + j) for j in range(NA_PAIRS_PER_ITER)]
        items = [(j, hp) for j in range(NA_PAIRS_PER_ITER) for hp in range(NA_HEADS // 2)]
        pending = [scores(coords[j], hp) for j, hp in items[:NA_SCORES_AHEAD]]
        outs = []
        for n, (j, hp) in enumerate(items):
            s = pending.pop(0)
            if n + NA_SCORES_AHEAD < len(items):
                jn, hn = items[n + NA_SCORES_AHEAD]
                pending.append(scores(coords[jn], hn))
            _, _, qrow, krow = coords[j]
            es, vs = [], []
            for sub in range(2):
                hd = 2 * hp + sub
                sh = s[:, sub * NA_KEY_TOKENS:(sub + 1) * NA_KEY_TOKENS]
                es.append(jnp.exp((sh - jnp.max(sh, axis=-1, keepdims=True)).astype(BF16)))
                vs.append(v_ref[0, pl.ds(krow, NA_KEY_TOKENS), hd * 2 * NA_HEAD_DIM:(hd + 1) * 2 * NA_HEAD_DIM])
            values = jnp.concatenate([jnp.concatenate([vs[0], vzero], axis=1),
                                      jnp.concatenate([vzero, vs[1]], axis=1)], axis=0)
            r = _dot(jnp.concatenate(es, axis=1), values)
            halves = [r[:, sub * 2 * NA_HEAD_DIM:(sub + 1) * 2 * NA_HEAD_DIM] for sub in range(2)]
            halves = [h / pltpu.roll(h, NA_HEAD_DIM, axis=1) for h in halves]
            outs.append(jnp.where(first_head, halves[0], halves[1]))
            if hp == NA_HEADS // 2 - 1:
                o_ref[0, pl.ds(qrow, NA_PAIR_TOKENS), :] = jnp.concatenate(outs, axis=1).astype(BF16)
                outs = []
        return carry

    lax.fori_loop(0, pairs_per_step // NA_PAIRS_PER_ITER, body, 0)


def _na_attention(q, kT4, v, bias, pairs_per_step):
    B, S, _ = q.shape
    n_pairs = S // NA_PAIR_TOKENS
    steps = n_pairs // pairs_per_step
    tq = pairs_per_step * NA_PAIR_TOKENS
    return pl.pallas_call(
        functools.partial(_na_kernel, pairs_per_step=pairs_per_step, n_pairs=n_pairs),
        grid=(B, steps),
        in_specs=[
            pl.BlockSpec((1, tq, NA_WIDTH), lambda b, j: (b, j, 0)),
            pl.BlockSpec((1, n_pairs, NA_WIDTH, V7X_LANES), lambda b, j: (b, 0, 0, 0)),
            pl.BlockSpec((1, S, NA_HEADS * V7X_LANES), lambda b, j: (b, 0, 0)),
            _const_spec((NA_BIAS_CASES, NA_HEADS, NA_PAIR_TOKENS, NA_KEY_TOKENS)),
        ],
        out_specs=pl.BlockSpec((1, tq, NA_WIDTH), lambda b, j: (b, j, 0)),
        out_shape=jax.ShapeDtypeStruct((B, S, NA_WIDTH), BF16),
        compiler_params=_params(2),
        name="na_attn",
    )(q, kT4, v, bias)


DFT_CHUNK = 16


def _dft_tables():
    n, m = GRID_W, DFT_CHUNK
    a = np.arange(n)
    ang1 = 2.0 * np.pi * np.outer(a, a) / n
    w_ri = np.stack([np.cos(ang1), -np.sin(ang1)], axis=1) / np.sqrt(n)
    eye = np.eye(m)
    w_rows = w_ri.reshape(2 * n, n)
    perm = np.einsum('dx,cy->dcyx', eye, eye).reshape(m * m, m * m)
    c = np.arange(n)[:, None, None]
    d = np.arange(n)[None, :, None]
    b = np.arange(n)[None, None, :]
    ang2 = 2.0 * np.pi * ((c + n * d) * b % (n * n)) / (n * n)
    gr, gi = np.cos(ang2) / np.sqrt(n), -np.sin(ang2) / np.sqrt(n)
    w_cols = np.concatenate([np.concatenate([gr, -gi], axis=2),
                             np.concatenate([gi, gr], axis=2)], axis=1)
    ch = np.arange(F_GROUP_DIM)
    ang3 = 2.0 * np.pi * np.outer(ch, ch) / F_GROUP_DIM
    w_chan = np.concatenate([np.cos(ang3), np.sin(ang3)], axis=0) / np.sqrt(F_GROUP_DIM)
    return tuple(jnp.asarray(t, dtype=F32).astype(BF16) for t in (w_rows, w_cols, perm, w_chan))


def _dft_kernel(wr_ref, wc_ref, perm_ref, u_ref, y_ref, z_ref):
    n, m = GRID_W, DFT_CHUNK

    def rows_stage(j, carry):
        col0 = pl.multiple_of(j * m, m)
        x = u_ref[0, :, pl.ds(col0, m), :].reshape(n * m, F_WIDTH)
        xp = [_dot(perm_ref[...], x[g * m * m:(g + 1) * m * m]).astype(BF16) for g in range(n // m)]
        zb = []
        for b in range(m):
            xb = jnp.concatenate([xp[g][b * m:(b + 1) * m] for g in range(n // m)], axis=0)
            zb.append(_dot(wr_ref[...], xb).astype(BF16))
        for t in range(2 * n // m):
            st = jnp.concatenate([zb[b][t * m:(t + 1) * m] for b in range(m)], axis=0)
            z = _dot(perm_ref[...], st).astype(BF16)
            z_ref[t * 8:(t + 1) * 8, :, pl.ds(col0, m), :] = z.reshape(8, 2, m, F_WIDTH)
        return carry

    def cols_stage(j, carry):
        col0 = pl.multiple_of(j * m, m)
        res = []
        for c_lo in range(m):
            c = j * m + c_lo
            data = z_ref[c].reshape(2 * n, F_WIDTH)
            res.append(_dot(wc_ref[c], data).astype(BF16))
        for ri in range(2):
            for g in range(n // m):
                lo = ri * n + g * m
                stacked = jnp.concatenate([r[lo:lo + m] for r in res], axis=0)
                y = _dot(perm_ref[...], stacked).astype(BF16)
                y_ref[ri, 0, g * m:(g + 1) * m, pl.ds(col0, m), :] = y.reshape(m, m, F_WIDTH)
        return carry

    lax.fori_loop(0, n // m, rows_stage, 0)
    lax.fori_loop(0, n // m, cols_stage, 0)


def _position_dft(u4, w_rows, w_cols, perm):
    B, n, m = u4.shape[0], GRID_W, DFT_CHUNK
    return pl.pallas_call(
        _dft_kernel,
        grid=(B,),
        in_specs=[
            _const_spec((2 * n, n)),
            _const_spec((n, 2 * n, 2 * n)),
            _const_spec((m * m, m * m)),
            pl.BlockSpec((1, n, n, F_WIDTH), lambda b: (b, 0, 0, 0)),
        ],
        out_specs=pl.BlockSpec((2, 1, n, n, F_WIDTH), lambda b: (0, b, 0, 0, 0)),
        out_shape=jax.ShapeDtypeStruct((2, B, n, n, F_WIDTH), BF16),
        scratch_shapes=[pltpu.VMEM((n, 2, n, F_WIDTH), BF16)],
        compiler_params=_params(1),
        name="position_dft",
    )(w_rows, w_cols, perm, u4)


def _out_ffn_kernel(x_ref, ona_ref, y_ref, om_ref, gate_ref, wch_ref, wna_ref, wf_ref, wmo_ref,
                    wout_ref, g2_ref, w1_ref, w2_ref, o_ref, *, ff_chunk):
    yr, yi = y_ref[0], y_ref[1]
    groups = []
    for g in range(F_GROUPS):
        sl = slice(g * F_GROUP_DIM, (g + 1) * F_GROUP_DIM)
        groups.append(_dot(jnp.concatenate([yr[:, sl], yi[:, sl]], axis=1), wch_ref[...]))
    yf = jnp.concatenate(groups, axis=1).astype(BF16)

    gate = lambda br: gate_ref[:, br * D_MODEL:(br + 1) * D_MODEL].astype(F32)
    merged = gate(0) * _dot(ona_ref[...], wna_ref[...])
    merged = merged + gate(1) * _dot(yf, wf_ref[...])
    merged = merged + gate(2) * _dot(om_ref[...], wmo_ref[...])
    x1 = x_ref[...] + _dot(merged.astype(BF16), wout_ref[...])

    ms = jnp.mean(x1 * x1, axis=-1, keepdims=True)
    h2 = (x1 * lax.rsqrt(ms + EPS) * g2_ref[...]).astype(BF16)
    acc = x1
    for j in range(D_FF // ff_chunk):
        sl = slice(j * ff_chunk, (j + 1) * ff_chunk)
        a = jnp.maximum(_dot(h2, w1_ref[:, sl]), 0.0)
        acc = acc + _dot((a * a).astype(BF16), w2_ref[sl, :])
    o_ref[...] = acc


def _out_ffn(x2, ona, y3, om, gates, w_chan, w_na_o, w_f, w_mem_o, w_out, norm2_g, w_ff1, w_ff2,
             tm, ff_chunk):
    T = x2.shape[0]
    tok = lambda w: pl.BlockSpec((tm, w), lambda i: (i, 0))
    return pl.pallas_call(
        functools.partial(_out_ffn_kernel, ff_chunk=ff_chunk),
        grid=(T // tm,),
        in_specs=[
            tok(D_MODEL),
            tok(NA_WIDTH),
            pl.BlockSpec((2, tm, F_WIDTH), lambda i: (0, i, 0)),
            tok(MEM_WIDTH),
            tok(N_BRANCHES * D_MODEL),
            _const_spec((2 * F_GROUP_DIM, F_GROUP_DIM)),
            _const_spec((NA_WIDTH, D_MODEL)),
            _const_spec((F_WIDTH, D_MODEL)),
            _const_spec((MEM_WIDTH, D_MODEL)),
            _const_spec((D_MODEL, D_MODEL)),
            _const_spec((1, D_MODEL)),
            _const_spec((D_MODEL, D_FF)),
            _const_spec((D_FF, D_MODEL)),
        ],
        out_specs=tok(D_MODEL),
        out_shape=jax.ShapeDtypeStruct((T, D_MODEL), F32),
        compiler_params=_params(1),
        name="out_ffn",
    )(x2, ona, y3, om, gates, w_chan, w_na_o, w_f, w_mem_o, w_out, norm2_g.reshape(1, D_MODEL),
      w_ff1, w_ff2)


def kernel(x, mem, norm1_g, w_in, b_gate, na_q_g, na_k_g, na_rpb, w_na_o, w_f, mem_norm_g,
           w_mem_kv, mem_q_g, mem_k_g, w_mem_o, w_out, norm2_g, w_ff1, w_ff2):
    B, S, _ = x.shape
    T = B * S
    rows = S // GRID_W
    assert rows == GRID_W and S % NA_PAIR_TOKENS == 0
    x2 = x.reshape(T, D_MODEL)
    gmean_mem = _group_mean_matrix(MEM_WIDTH, MEM_HEAD_DIM)
    w_rows, w_cols, perm, w_chan = _dft_tables()

    memkT, memv, w_in_bf16, w_kT = _mem_kv(mem, mem_norm_g, w_mem_kv, gmean_mem, mem_k_g, w_in)
    later_weights = (w_na_o, w_f, w_mem_o, w_out, w_ff1, w_ff2)
    q, kT4, v, fu, om, gates, bias, *narrow = _in_proj(
        x2, norm1_g, w_in_bf16, w_kT, b_gate, na_q_g, na_k_g, mem_q_g, memkT, memv,
        _na_bias_blocks(na_rpb, rows), later_weights, S, tm=IN_PROJ_TOKENS)
    w_na_o, w_f, w_mem_o, w_out, w_ff1, w_ff2 = narrow

    ona = _na_attention(q.reshape(B, S, NA_WIDTH), kT4, v.reshape(B, S, NA_HEADS * V7X_LANES), bias,
                        pairs_per_step=NA_PAIRS_PER_STEP)

    y = _position_dft(fu.reshape(B, GRID_W, GRID_W, F_WIDTH), w_rows, w_cols, perm)
    y3 = y.reshape(2, T, F_WIDTH)

    out = _out_ffn(x2, ona.reshape(T, NA_WIDTH), y3, om, gates, w_chan, w_na_o, w_f, w_mem_o,
                   w_out, norm2_g, w_ff1, w_ff2, tm=OUT_FFN_TOKENS, ff_chunk=FF_CHUNK)
    return out.reshape(B, S, D_MODEL)
```

```python
import functools

import numpy as np
import jax
import jax.numpy as jnp
from jax import lax
from jax.experimental import pallas as pl
from jax.experimental.pallas import tpu as pltpu

D_MODEL = 1024
GRID_W = 64
NA_HEADS = 8
NA_HEAD_DIM = 64
NA_WIDTH = NA_HEADS * NA_HEAD_DIM
NA_WIN_ROWS = 8
NA_WIN_COLS = 16
F_GROUPS = 4
F_GROUP_DIM = 128
F_WIDTH = F_GROUPS * F_GROUP_DIM
MEM_HEADS = 4
MEM_HEAD_DIM = 128
MEM_WIDTH = MEM_HEADS * MEM_HEAD_DIM
N_BRANCHES = 3
D_FF = 4 * D_MODEL
IN_Q0 = 0
IN_K0 = IN_Q0 + NA_WIDTH
IN_V0 = IN_K0 + NA_WIDTH
IN_F0 = IN_V0 + NA_WIDTH
IN_M0 = IN_F0 + F_WIDTH
IN_G0 = IN_M0 + MEM_WIDTH
IN_WIDTH = IN_G0 + N_BRANCHES * D_MODEL
EPS = 1e-6
NEG_INF = -1e30

V7X_LANES = 128
V7X_VMEM_LIMIT_BYTES = 60 * 1024 * 1024

NA_PAIR_TOKENS = 2 * GRID_W
NA_KEY_PAIRS = 5
NA_KEY_TOKENS = NA_KEY_PAIRS * NA_PAIR_TOKENS
NA_BIAS_CASES = 5
NA_PAIRS_PER_ITER = 16
NA_SCORES_AHEAD = 2

IN_PROJ_TOKENS = 512
OUT_FFN_TOKENS = 512
FF_CHUNK = 1024
NA_PAIRS_PER_STEP = 16

BF16 = jnp.bfloat16
F32 = jnp.float32


def _dot(a, b):
    return jnp.dot(a, b, preferred_element_type=F32)


def _dot_nt(a, b):
    return lax.dot_general(a, b, (((1,), (1,)), ((), ())), preferred_element_type=F32)


def _const_spec(shape):
    return pl.BlockSpec(shape, lambda *_: (0,) * len(shape), pipeline_mode=pl.Buffered(1))


def _params(n_axes):
    return pltpu.CompilerParams(
        dimension_semantics=("arbitrary",) * n_axes,
        vmem_limit_bytes=V7X_VMEM_LIMIT_BYTES,
    )


def _group_mean_matrix(width, group):
    idx = np.arange(width) // group
    return jnp.asarray((idx[:, None] == idx[None, :]).astype(np.float32) / group, dtype=BF16)


PREP_STEPS = 4


def _mem_kv_kernel(mem_ref, g_ref, w_ref, gm_ref, kg_ref, win_ref, kT_ref, v_ref, wb_ref, wkT_ref,
                   *, n_batches):
    slab = win_ref[...]
    wb_ref[...] = slab.astype(BF16)
    wkT_ref[...] = slab[:, IN_K0:IN_K0 + NA_WIDTH].T.astype(BF16)

    @pl.when(pl.program_id(0) < n_batches)
    def _():
        m = mem_ref[0]
        ms = jnp.mean(m * m, axis=-1, keepdims=True)
        mn = (m * lax.rsqrt(ms + EPS) * g_ref[...]).astype(BF16)
        kv = _dot(mn, w_ref[...].astype(BF16))
        k = kv[:, :MEM_WIDTH]
        msk = _dot((k * k).astype(BF16), gm_ref[...])
        kn = k * lax.rsqrt(msk + EPS) * kg_ref[...]
        kT_ref[0] = kn.T.astype(BF16)
        v_ref[0] = kv[:, MEM_WIDTH:].astype(BF16)


def _mem_kv(mem, mem_norm_g, w_mem_kv, gmean_mem, mem_k_g, w_in):
    B, M, _ = mem.shape
    assert B <= PREP_STEPS
    rows = D_MODEL // PREP_STEPS
    batch_of = lambda i: jnp.minimum(i, B - 1)
    return pl.pallas_call(
        functools.partial(_mem_kv_kernel, n_batches=B),
        grid=(PREP_STEPS,),
        in_specs=[
            pl.BlockSpec((1, M, D_MODEL), lambda i: (batch_of(i), 0, 0)),
            _const_spec((1, D_MODEL)),
            _const_spec((D_MODEL, 2 * MEM_WIDTH)),
            _const_spec((MEM_WIDTH, MEM_WIDTH)),
            _const_spec((1, MEM_WIDTH)),
            pl.BlockSpec((rows, IN_WIDTH), lambda i: (i, 0)),
        ],
        out_specs=[
            pl.BlockSpec((1, MEM_WIDTH, M), lambda i: (batch_of(i), 0, 0)),
            pl.BlockSpec((1, M, MEM_WIDTH), lambda i: (batch_of(i), 0, 0)),
            pl.BlockSpec((rows, IN_WIDTH), lambda i: (i, 0)),
            pl.BlockSpec((NA_WIDTH, rows), lambda i: (0, i)),
        ],
        out_shape=[
            jax.ShapeDtypeStruct((B, MEM_WIDTH, M), BF16),
            jax.ShapeDtypeStruct((B, M, MEM_WIDTH), BF16),
            jax.ShapeDtypeStruct((D_MODEL, IN_WIDTH), BF16),
            jax.ShapeDtypeStruct((NA_WIDTH, D_MODEL), BF16),
        ],
        compiler_params=_params(1),
        name="mem_kv",
    )(mem, mem_norm_g.reshape(1, D_MODEL), w_mem_kv, gmean_mem,
      jnp.tile(mem_k_g, MEM_HEADS).reshape(1, MEM_WIDTH), w_in)


def _in_proj_kernel(x_ref, g1_ref, w_ref, wkT_ref, bg_ref,
                    qg_ref, kg_ref, mqg_ref, memkT_ref, memv_ref, brow_ref, *rest,
                    n_narrow, bias_dr):
    wide_refs = rest[:n_narrow]
    q_ref, kT_ref, v_ref, fu_ref, om_ref, gate_ref, bias_ref = rest[n_narrow:n_narrow + 7]

    x = x_ref[...]
    ms = jnp.mean(x * x, axis=-1, keepdims=True)
    h = (x * lax.rsqrt(ms + EPS) * g1_ref[...]).astype(BF16)

    def gate_dot(br):
        sl = slice(br * D_MODEL, (br + 1) * D_MODEL)
        zg = _dot(h, w_ref[:, IN_G0 + br * D_MODEL:IN_G0 + (br + 1) * D_MODEL]) + bg_ref[:, sl]
        gate_ref[:, sl] = jax.nn.sigmoid(zg).astype(BF16)

    zm = _dot(h, w_ref[:, IN_M0:IN_M0 + MEM_WIDTH])
    mem_scores = []
    for hd in range(MEM_HEADS):
        sl = slice(hd * MEM_HEAD_DIM, (hd + 1) * MEM_HEAD_DIM)
        zh = zm[:, sl]
        msm = jnp.mean(zh * zh, axis=-1, keepdims=True)
        qm = (zh * lax.rsqrt(msm + EPS) * mqg_ref[:, sl]).astype(BF16)
        if hd == 0:
            gate_dot(0)
        mem_scores.append(_dot(qm, memkT_ref[0, sl, :]))
    gate_dot(1)

    zq = _dot(h, w_ref[:, IN_Q0:IN_Q0 + NA_WIDTH])
    low_half = lax.broadcasted_iota(jnp.int32, (zq.shape[0], V7X_LANES), 1) < NA_HEAD_DIM
    for hp in range(NA_HEADS // 2):
        sl = slice(hp * V7X_LANES, (hp + 1) * V7X_LANES)
        zz = zq[:, sl] * zq[:, sl]
        ms_lo = jnp.sum(jnp.where(low_half, zz, 0.0), axis=-1, keepdims=True)
        ms_hi = jnp.sum(jnp.where(low_half, 0.0, zz), axis=-1, keepdims=True)
        ms = jnp.where(low_half, ms_lo, ms_hi) * (1.0 / NA_HEAD_DIM)
        q_ref[:, sl] = (zq[:, sl] * lax.rsqrt(ms + EPS) * qg_ref[:, sl]).astype(BF16)

    zkT = _dot_nt(wkT_ref[...], h)
    zk3 = zkT.reshape(NA_HEADS, NA_HEAD_DIM, zkT.shape[1])
    msk = jnp.mean(zk3 * zk3, axis=1, keepdims=True)
    kT = ((zk3 * lax.rsqrt(msk + EPS)).reshape(zkT.shape) * kg_ref[...]).astype(BF16)
    for j in range(kT_ref.shape[1]):
        kT_ref[0, j] = kT[:, j * V7X_LANES:(j + 1) * V7X_LANES]

    zv = _dot(h, w_ref[:, IN_V0:IN_V0 + NA_WIDTH])
    low_half = lax.broadcasted_iota(jnp.int32, (zv.shape[0], V7X_LANES), 1) < NA_HEAD_DIM
    for hp in range(NA_HEADS // 2):
        pair = zv[:, hp * V7X_LANES:(hp + 1) * V7X_LANES]
        v_ref[:, (2 * hp) * V7X_LANES:(2 * hp + 1) * V7X_LANES] = jnp.where(low_half, pair, 1.0).astype(BF16)
        v_ref[:, (2 * hp + 1) * V7X_LANES:(2 * hp + 2) * V7X_LANES] = jnp.where(low_half, 1.0, pair).astype(BF16)
    fu_ref[...] = _dot(h, w_ref[:, IN_F0:IN_F0 + F_WIDTH]).astype(BF16)
    gate_dot(2)

    outs = []
    for hd in range(MEM_HEADS):
        sl = slice(hd * MEM_HEAD_DIM, (hd + 1) * MEM_HEAD_DIM)
        s = mem_scores[hd]
        e = jnp.exp(s - jnp.max(s, axis=-1, keepdims=True))
        l = jnp.sum(e, axis=-1, keepdims=True)
        outs.append(_dot(e.astype(BF16), memv_ref[0, :, sl]) / l)
    om_ref[...] = jnp.concatenate(outs, axis=1).astype(BF16)

    for wide_ref, narrow_ref in zip(wide_refs, rest[n_narrow + 7:]):
        narrow_ref[...] = wide_ref[...].astype(BF16)

    @pl.when(pl.program_id(0) < NA_HEADS)
    def _():
        _assemble_na_bias(brow_ref, bias_ref, bias_dr)


def _in_proj(x2, norm1_g, w_in_bf16, w_kT, b_gate, na_q_g, na_k_g, mem_q_g,
             memkT, memv, bias_blocks, later_weights, tokens_per_batch, tm):
    T = x2.shape[0]
    bias_rows, bias_dr = bias_blocks
    n_cases = bias_dr.shape[0]
    M = memv.shape[1]
    tiles_per_batch = tokens_per_batch // tm
    slabs = tm // V7X_LANES
    B = T // tokens_per_batch
    na_scale = np.float32(1.0 / np.sqrt(NA_HEAD_DIM))
    mem_scale = np.float32(1.0 / np.sqrt(MEM_HEAD_DIM))
    qg = (jnp.tile(na_q_g, NA_HEADS) * na_scale).reshape(1, NA_WIDTH)
    kg = jnp.tile(na_k_g, NA_HEADS).reshape(NA_WIDTH, 1)
    mqg = (jnp.tile(mem_q_g, MEM_HEADS) * mem_scale).reshape(1, MEM_WIDTH)
    tok = lambda w: pl.BlockSpec((tm, w), lambda i: (i, 0))
    batch_of = lambda i: i // tiles_per_batch
    steps = T // tm
    assert steps >= NA_HEADS
    slab_specs = [pl.BlockSpec((w.shape[0] // steps, w.shape[1]), lambda i: (i, 0))
                  for w in later_weights]
    head_of = lambda i: jnp.minimum(i, NA_HEADS - 1)
    bias_rows_spec = pl.BlockSpec((1,) + bias_rows.shape[1:], lambda i: (head_of(i), 0, 0))
    return pl.pallas_call(
        functools.partial(_in_proj_kernel, n_narrow=len(later_weights), bias_dr=bias_dr),
        grid=(steps,),
        in_specs=[
            tok(D_MODEL),
            _const_spec((1, D_MODEL)),
            _const_spec((D_MODEL, IN_WIDTH)),
            _const_spec((NA_WIDTH, D_MODEL)),
            _const_spec((1, N_BRANCHES * D_MODEL)),
            _const_spec((1, NA_WIDTH)),
            _const_spec((NA_WIDTH, 1)),
            _const_spec((1, MEM_WIDTH)),
            pl.BlockSpec((1, MEM_WIDTH, M), lambda i: (batch_of(i), 0, 0)),
            pl.BlockSpec((1, M, MEM_WIDTH), lambda i: (batch_of(i), 0, 0)),
            bias_rows_spec,
        ] + slab_specs,
        out_specs=[
            tok(NA_WIDTH),
            pl.BlockSpec((1, slabs, NA_WIDTH, V7X_LANES),
                         lambda i: (batch_of(i), i % tiles_per_batch, 0, 0)),
            tok(NA_HEADS * V7X_LANES),
            tok(F_WIDTH),
            tok(MEM_WIDTH),
            tok(N_BRANCHES * D_MODEL),
            pl.BlockSpec((n_cases, 1, NA_PAIR_TOKENS, NA_KEY_TOKENS), lambda i: (0, head_of(i), 0, 0)),
        ] + slab_specs,
        out_shape=[
            jax.ShapeDtypeStruct((T, NA_WIDTH), BF16),
            jax.ShapeDtypeStruct((B, tokens_per_batch // V7X_LANES, NA_WIDTH, V7X_LANES), BF16),
            jax.ShapeDtypeStruct((T, NA_HEADS * V7X_LANES), BF16),
            jax.ShapeDtypeStruct((T, F_WIDTH), BF16),
            jax.ShapeDtypeStruct((T, MEM_WIDTH), BF16),
            jax.ShapeDtypeStruct((T, N_BRANCHES * D_MODEL), BF16),
            jax.ShapeDtypeStruct((n_cases, NA_HEADS, NA_PAIR_TOKENS, NA_KEY_TOKENS), F32),
        ] + [jax.ShapeDtypeStruct(w.shape, BF16) for w in later_weights],
        compiler_params=_params(1),
        name="in_proj",
    )(x2, norm1_g.reshape(1, D_MODEL), w_in_bf16, w_kT,
      b_gate.reshape(1, N_BRANCHES * D_MODEL), qg, kg, mqg, memkT, memv, bias_rows,
      *later_weights)


def _na_bias_blocks(rpb, rows):
    n_pairs = rows // 2
    wr = min(NA_WIN_ROWS, rows)
    reps = np.array([0, 1, 2, n_pairs - 2, n_pairs - 1])
    starts = np.clip(reps - 2, 0, n_pairs - NA_KEY_PAIRS)
    n_dr, n_dc = 2 * NA_WIN_ROWS - 1, 2 * NA_WIN_COLS - 1
    offsets = np.arange(n_dc) - (NA_WIN_COLS - 1)
    place = (np.arange(2 * GRID_W)[None, :] % GRID_W == (offsets % GRID_W)[:, None]).astype(np.float32)
    rows128 = jnp.dot(rpb.astype(F32).reshape(NA_HEADS * n_dr, n_dc), place,
                      precision=lax.Precision.HIGHEST).reshape(NA_HEADS, n_dr, 2 * GRID_W)
    r = (2 * reps[:, None] + np.arange(2)[None, :])[:, :, None]
    kr = (2 * starts[:, None] + np.arange(2 * NA_KEY_PAIRS)[None, :])[:, None, :]
    rs = np.clip(r - wr // 2, 0, rows - wr)
    row_valid = (kr >= rs) & (kr < rs + wr)
    dr = np.where(row_valid, kr - r + (NA_WIN_ROWS - 1), n_dr)
    assert dr.min() >= 0 and dr.max() <= n_dr
    return rows128, dr


def _assemble_na_bias(rows_ref, o_ref, dr):
    n_dr = rows_ref.shape[1]
    c = lax.broadcasted_iota(jnp.int32, (GRID_W, 2 * GRID_W), 0)
    lane = lax.broadcasted_iota(jnp.int32, (GRID_W, 2 * GRID_W), 1)
    kc = lane & (GRID_W - 1)
    cs = jnp.clip(c - NA_WIN_COLS // 2, 0, GRID_W - NA_WIN_COLS)
    col_valid = (kc >= cs) & (kc < cs + NA_WIN_COLS)
    masked = jnp.full((GRID_W, 2 * GRID_W), NEG_INF, F32)
    blocks = []
    for i in range(n_dr):
        row = jnp.broadcast_to(rows_ref[0, i:i + 1, :], (GRID_W, 2 * GRID_W))
        toeplitz = pltpu.roll(row, 0, axis=1, stride=1, stride_axis=0)
        blocks.append(jnp.where(col_valid, toeplitz, masked))
    blocks.append(masked)
    even_row = lane < GRID_W
    for case in range(dr.shape[0]):
        for rr in range(2):
            for j in range(NA_KEY_PAIRS):
                tile = jnp.where(even_row, blocks[int(dr[case, rr, 2 * j])],
                                 blocks[int(dr[case, rr, 2 * j + 1])])
                o_ref[case, 0, rr * GRID_W:(rr + 1) * GRID_W,
                      j * V7X_LANES:(j + 1) * V7X_LANES] = tile


def _na_kernel(q_ref, kT_ref, v_ref, bias_ref, o_ref, *, pairs_per_step, n_pairs):
    lane = lax.broadcasted_iota(jnp.int32, (NA_PAIR_TOKENS, 2 * NA_HEAD_DIM), 1)
    first_head = lane < NA_HEAD_DIM

    def pair_coords(i):
        p = pl.program_id(1) * pairs_per_step + i
        start = jnp.clip(p - 2, 0, n_pairs - NA_KEY_PAIRS)
        case = jnp.where(p < 2, p, jnp.where(p >= n_pairs - 2, p - (n_pairs - NA_BIAS_CASES), 2))
        qrow = pl.multiple_of(i * NA_PAIR_TOKENS, NA_PAIR_TOKENS)
        krow = pl.multiple_of(start * NA_PAIR_TOKENS, NA_PAIR_TOKENS)
        return start, case, qrow, krow

    kzero = jnp.zeros((NA_HEAD_DIM, NA_KEY_TOKENS), BF16)
    vzero = jnp.zeros((NA_KEY_TOKENS, 2 * NA_HEAD_DIM), BF16)

    def scores(coords, hp):
        start, case, qrow, _ = coords
        psl = slice(hp * 2 * NA_HEAD_DIM, (hp + 1) * 2 * NA_HEAD_DIM)
        qp = q_ref[0, pl.ds(qrow, NA_PAIR_TOKENS), psl]
        kslabs = kT_ref[0, pl.ds(start, NA_KEY_PAIRS), psl, :]
        kT = jnp.concatenate([kslabs[j] for j in range(NA_KEY_PAIRS)], axis=1)
        keys = jnp.concatenate([jnp.concatenate([kT[:NA_HEAD_DIM], kzero], axis=1),
                                jnp.concatenate([kzero, kT[NA_HEAD_DIM:]], axis=1)], axis=0)
        bias = jnp.concatenate([bias_ref[case, 2 * hp], bias_ref[case, 2 * hp + 1]], axis=1)
        return _dot(qp, keys) + bias

    def body(it, carry):
        coords = [pair_coords(it * NA_PAIRS_PER_ITER + j) for j in range(NA_PAIRS_PER_ITER)]
        items = [(j, hp) for j in range(NA_PAIRS_PER_ITER) for hp in range(NA_HEADS // 2)]
        pending = [scores(coords[j], hp) for j, hp in items[:NA_SCORES_AHEAD]]
        outs = []
        for n, (j, hp) in enumerate(items):
            s = pending.pop(0)
            if n + NA_SCORES_AHEAD < len(items):
                jn, hn = items[n + NA_SCORES_AHEAD]
                pending.append(scores(coords[jn], hn))
            _, _, qrow, krow = coords[j]
            es, vs = [], []
            for sub in range(2):
                hd = 2 * hp + sub
                sh = s[:, sub * NA_KEY_TOKENS:(sub + 1) * NA_KEY_TOKENS]
                es.append(jnp.exp((sh - jnp.max(sh, axis=-1, keepdims=True)).astype(BF16)))
                vs.append(v_ref[0, pl.ds(krow, NA_KEY_TOKENS), hd * 2 * NA_HEAD_DIM:(hd + 1) * 2 * NA_HEAD_DIM])
            values = jnp.concatenate([jnp.concatenate([vs[0], vzero], axis=1),
                                      jnp.concatenate([vzero, vs[1]], axis=1)], axis=0)
            r = _dot(jnp.concatenate(es, axis=1), values)
            halves = [r[:, sub * 2 * NA_HEAD_DIM:(sub + 1) * 2 * NA_HEAD_DIM] for sub in range(2)]
            halves = [h / pltpu.roll(h, NA_HEAD_DIM, axis=1) for h in halves]
            outs.append(jnp.where(first_head, halves[0], halves[1]))
            if hp == NA_HEADS // 2 - 1:
                o_ref[0, pl.ds(qrow, NA_PAIR_TOKENS), :] = jnp.concatenate(outs, axis=1).astype(BF16)
                outs = []
        return carry

    lax.fori_loop(0, pairs_per_step // NA_PAIRS_PER_ITER, body, 0)


def _na_attention(q, kT4, v, bias, pairs_per_step):
    B, S, _ = q.shape
    n_pairs = S // NA_PAIR_TOKENS
    steps = n_pairs // pairs_per_step
    tq = pairs_per_step * NA_PAIR_TOKENS
    return pl.pallas_call(
        functools.partial(_na_kernel, pairs_per_step=pairs_per_step, n_pairs=n_pairs),
        grid=(B, steps),
        in_specs=[
            pl.BlockSpec((1, tq, NA_WIDTH), lambda b, j: (b, j, 0)),
            pl.BlockSpec((1, n_pairs, NA_WIDTH, V7X_LANES), lambda b, j: (b, 0, 0, 0)),
            pl.BlockSpec((1, S, NA_HEADS * V7X_LANES), lambda b, j: (b, 0, 0)),
            _const_spec((NA_BIAS_CASES, NA_HEADS, NA_PAIR_TOKENS, NA_KEY_TOKENS)),
        ],
        out_specs=pl.BlockSpec((1, tq, NA_WIDTH), lambda b, j: (b, j, 0)),
        out_shape=jax.ShapeDtypeStruct((B, S, NA_WIDTH), BF16),
        compiler_params=_params(2),
        name="na_attn",
    )(q, kT4, v, bias)


DFT_CHUNK = 16


def _dft_tables():
    n, m = GRID_W, DFT_CHUNK
    a = np.arange(n)
    ang1 = 2.0 * np.pi * np.outer(a, a) / n
    w_ri = np.stack([np.cos(ang1), -np.sin(ang1)], axis=1) / np.sqrt(n)
    eye = np.eye(m)
    w_rows = w_ri.reshape(2 * n, n)
    perm = np.einsum('dx,cy->dcyx', eye, eye).reshape(m * m, m * m)
    c = np.arange(n)[:, None, None]
    d = np.arange(n)[None, :, None]
    b = np.arange(n)[None, None, :]
    ang2 = 2.0 * np.pi * ((c + n * d) * b % (n * n)) / (n * n)
    gr, gi = np.cos(ang2) / np.sqrt(n), -np.sin(ang2) / np.sqrt(n)
    w_cols = np.concatenate([np.concatenate([gr, -gi], axis=2),
                             np.concatenate([gi, gr], axis=2)], axis=1)
    ch = np.arange(F_GROUP_DIM)
    ang3 = 2.0 * np.pi * np.outer(ch, ch) / F_GROUP_DIM
    w_chan = np.concatenate([np.cos(ang3), np.sin(ang3)], axis=0) / np.sqrt(F_GROUP_DIM)
    return tuple(jnp.asarray(t, dtype=F32).astype(BF16) for t in (w_rows, w_cols, perm, w_chan))


def _dft_kernel(wr_ref, wc_ref, perm_ref, u_ref, y_ref, z_ref):
    n, m = GRID_W, DFT_CHUNK

    def rows_stage(j, carry):
        col0 = j * m
        x = u_ref[0, :, pl.ds(col0, m), :].reshape(n * m, F_WIDTH)
        xp = [_dot(perm_ref[...], x[g * m * m:(g + 1) * m * m]).astype(BF16) for g in range(n // m)]
        zb = []
        for b in range(m):
            xb = jnp.concatenate([xp[g][b * m:(b + 1) * m] for g in range(n // m)], axis=0)
            zb.append(_dot(wr_ref[...], xb).astype(BF16))
        for t in range(2 * n // m):
            st = jnp.concatenate([zb[b][t * m:(t + 1) * m] for b in range(m)], axis=0)
            z = _dot(perm_ref[...], st).astype(BF16)
            z_ref[t * 8:(t + 1) * 8, :, pl.ds(col0, m), :] = z.reshape(8, 2, m, F_WIDTH)
        return carry

    def cols_stage(j, carry):
        col0 = j * m
        res = []
        for c_lo in range(m):
            c = j * m + c_lo
            data = z_ref[c].reshape(2 * n, F_WIDTH)
            res.append(_dot(wc_ref[c], data).astype(BF16))
        for ri in range(2):
            for g in range(n // m):
                lo = ri * n + g * m
                stacked = jnp.concatenate([r[lo:lo + m] for r in res], axis=0)
                y = _dot(perm_ref[...], stacked).astype(BF16)
                y_ref[ri, 0, g * m:(g + 1) * m, pl.ds(col0, m), :] = y.reshape(m, m, F_WIDTH)
        return carry

    for j in range(n // m):
        rows_stage(j, 0)
    for j in range(n // m):
        cols_stage(j, 0)


def _position_dft(u4, w_rows, w_cols, perm):
    B, n, m = u4.shape[0], GRID_W, DFT_CHUNK
    return pl.pallas_call(
        _dft_kernel,
        grid=(B,),
        in_specs=[
            _const_spec((2 * n, n)),
            _const_spec((n, 2 * n, 2 * n)),
            _const_spec((m * m, m * m)),
            pl.BlockSpec((1, n, n, F_WIDTH), lambda b: (b, 0, 0, 0)),
        ],
        out_specs=pl.BlockSpec((2, 1, n, n, F_WIDTH), lambda b: (0, b, 0, 0, 0)),
        out_shape=jax.ShapeDtypeStruct((2, B, n, n, F_WIDTH), BF16),
        scratch_shapes=[pltpu.VMEM((n, 2, n, F_WIDTH), BF16)],
        compiler_params=_params(1),
        name="position_dft",
    )(w_rows, w_cols, perm, u4)


def _out_ffn_kernel(x_ref, ona_ref, y_ref, om_ref, gate_ref, wch_ref, wna_ref, wf_ref, wmo_ref,
                    wout_ref, g2_ref, w1_ref, w2_ref, o_ref, *, ff_chunk):
    yr, yi = y_ref[0], y_ref[1]
    groups = []
    for g in range(F_GROUPS):
        sl = slice(g * F_GROUP_DIM, (g + 1) * F_GROUP_DIM)
        groups.append(_dot(jnp.concatenate([yr[:, sl], yi[:, sl]], axis=1), wch_ref[...]))
    yf = jnp.concatenate(groups, axis=1).astype(BF16)

    gate = lambda br: gate_ref[:, br * D_MODEL:(br + 1) * D_MODEL].astype(F32)
    merged = gate(0) * _dot(ona_ref[...], wna_ref[...])
    merged = merged + gate(1) * _dot(yf, wf_ref[...])
    merged = merged + gate(2) * _dot(om_ref[...], wmo_ref[...])
    x1 = x_ref[...] + _dot(merged.astype(BF16), wout_ref[...])

    ms = jnp.mean(x1 * x1, axis=-1, keepdims=True)
    h2 = (x1 * lax.rsqrt(ms + EPS) * g2_ref[...]).astype(BF16)
    acc = x1
    for j in range(D_FF // ff_chunk):
        sl = slice(j * ff_chunk, (j + 1) * ff_chunk)
        a = jnp.maximum(_dot(h2, w1_ref[:, sl]), 0.0)
        acc = acc + _dot((a * a).astype(BF16), w2_ref[sl, :])
    o_ref[...] = acc


def _out_ffn(x2, ona, y3, om, gates, w_chan, w_na_o, w_f, w_mem_o, w_out, norm2_g, w_ff1, w_ff2,
             tm, ff_chunk):
    T = x2.shape[0]
    tok = lambda w: pl.BlockSpec((tm, w), lambda i: (i, 0))
    return pl.pallas_call(
        functools.partial(_out_ffn_kernel, ff_chunk=ff_chunk),
        grid=(T // tm,),
        in_specs=[
            tok(D_MODEL),
            tok(NA_WIDTH),
            pl.BlockSpec((2, tm, F_WIDTH), lambda i: (0, i, 0)),
            tok(MEM_WIDTH),
            tok(N_BRANCHES * D_MODEL),
            _const_spec((2 * F_GROUP_DIM, F_GROUP_DIM)),
            _const_spec((NA_WIDTH, D_MODEL)),
            _const_spec((F_WIDTH, D_MODEL)),
            _const_spec((MEM_WIDTH, D_MODEL)),
            _const_spec((D_MODEL, D_MODEL)),
            _const_spec((1, D_MODEL)),
            _const_spec((D_MODEL, D_FF)),
            _const_spec((D_FF, D_MODEL)),
        ],
        out_specs=tok(D_MODEL),
        out_shape=jax.ShapeDtypeStruct((T, D_MODEL), F32),
        compiler_params=_params(1),
        name="out_ffn",
    )(x2, ona, y3, om, gates, w_chan, w_na_o, w_f, w_mem_o, w_out, norm2_g.reshape(1, D_MODEL),
      w_ff1, w_ff2)


def kernel(x, mem, norm1_g, w_in, b_gate, na_q_g, na_k_g, na_rpb, w_na_o, w_f, mem_norm_g,
           w_mem_kv, mem_q_g, mem_k_g, w_mem_o, w_out, norm2_g, w_ff1, w_ff2):
    B, S, _ = x.shape
    T = B * S
    rows = S // GRID_W
    assert rows == GRID_W and S % NA_PAIR_TOKENS == 0
    x2 = x.reshape(T, D_MODEL)
    gmean_mem = _group_mean_matrix(MEM_WIDTH, MEM_HEAD_DIM)
    w_rows, w_cols, perm, w_chan = _dft_tables()

    memkT, memv, w_in_bf16, w_kT = _mem_kv(mem, mem_norm_g, w_mem_kv, gmean_mem, mem_k_g, w_in)
    later_weights = (w_na_o, w_f, w_mem_o, w_out, w_ff1, w_ff2)
    q, kT4, v, fu, om, gates, bias, *narrow = _in_proj(
        x2, norm1_g, w_in_bf16, w_kT, b_gate, na_q_g, na_k_g, mem_q_g, memkT, memv,
        _na_bias_blocks(na_rpb, rows), later_weights, S, tm=IN_PROJ_TOKENS)
    w_na_o, w_f, w_mem_o, w_out, w_ff1, w_ff2 = narrow

    ona = _na_attention(q.reshape(B, S, NA_WIDTH), kT4, v.reshape(B, S, NA_HEADS * V7X_LANES), bias,
                        pairs_per_step=NA_PAIRS_PER_STEP)

    y = _position_dft(fu.reshape(B, GRID_W, GRID_W, F_WIDTH), w_rows, w_cols, perm)
    y3 = y.reshape(2, T, F_WIDTH)

    out = _out_ffn(x2, ona.reshape(T, NA_WIDTH), y3, om, gates, w_chan, w_na_o, w_f, w_mem_o,
                   w_out, norm2_g, w_ff1, w_ff2, tm=OUT_FFN_TOKENS, ff_chunk=FF_CHUNK)
    return out.reshape(B, S, D_MODEL)
```

```python
import functools

import numpy as np
import jax
import jax.numpy as jnp
from jax import lax
from jax.experimental import pallas as pl
from jax.experimental.pallas import tpu as pltpu

D_MODEL = 1024
GRID_W = 64
NA_HEADS = 8
NA_HEAD_DIM = 64
NA_WIDTH = NA_HEADS * NA_HEAD_DIM
NA_WIN_ROWS = 8
NA_WIN_COLS = 16
F_GROUPS = 4
F_GROUP_DIM = 128
F_WIDTH = F_GROUPS * F_GROUP_DIM
MEM_HEADS = 4
MEM_HEAD_DIM = 128
MEM_WIDTH = MEM_HEADS * MEM_HEAD_DIM
N_BRANCHES = 3
D_FF = 4 * D_MODEL
IN_Q0 = 0
IN_K0 = IN_Q0 + NA_WIDTH
IN_V0 = IN_K0 + NA_WIDTH
IN_F0 = IN_V0 + NA_WIDTH
IN_M0 = IN_F0 + F_WIDTH
IN_G0 = IN_M0 + MEM_WIDTH
IN_WIDTH = IN_G0 + N_BRANCHES * D_MODEL
EPS = 1e-6
NEG_INF = -1e30

V7X_LANES = 128
V7X_VMEM_LIMIT_BYTES = 60 * 1024 * 1024

NA_PAIR_TOKENS = 2 * GRID_W
NA_KEY_PAIRS = 5
NA_KEY_TOKENS = NA_KEY_PAIRS * NA_PAIR_TOKENS
NA_BIAS_CASES = 5
NA_PAIRS_PER_ITER = 16
NA_SCORES_AHEAD = 2

IN_PROJ_TOKENS = 512
OUT_FFN_TOKENS = 512
FF_CHUNK = 4096
NA_PAIRS_PER_STEP = 16

BF16 = jnp.bfloat16
F32 = jnp.float32


def _dot(a, b):
    return jnp.dot(a, b, preferred_element_type=F32)


def _dot_nt(a, b):
    return lax.dot_general(a, b, (((1,), (1,)), ((), ())), preferred_element_type=F32)


def _const_spec(shape):
    return pl.BlockSpec(shape, lambda *_: (0,) * len(shape), pipeline_mode=pl.Buffered(1))


def _params(n_axes):
    return pltpu.CompilerParams(
        dimension_semantics=("arbitrary",) * n_axes,
        vmem_limit_bytes=V7X_VMEM_LIMIT_BYTES,
    )


def _group_mean_matrix(width, group):
    idx = np.arange(width) // group
    return jnp.asarray((idx[:, None] == idx[None, :]).astype(np.float32) / group, dtype=BF16)


PREP_STEPS = 4


def _mem_kv_kernel(mem_ref, g_ref, w_ref, gm_ref, kg_ref, win_ref, kT_ref, v_ref, wb_ref, wkT_ref,
                   *, n_batches):
    slab = win_ref[...]
    wb_ref[...] = slab.astype(BF16)
    wkT_ref[...] = slab[:, IN_K0:IN_K0 + NA_WIDTH].T.astype(BF16)

    @pl.when(pl.program_id(0) < n_batches)
    def _():
        m = mem_ref[0]
        ms = jnp.mean(m * m, axis=-1, keepdims=True)
        mn = (m * lax.rsqrt(ms + EPS) * g_ref[...]).astype(BF16)
        kv = _dot(mn, w_ref[...].astype(BF16))
        k = kv[:, :MEM_WIDTH]
        msk = _dot((k * k).astype(BF16), gm_ref[...])
        kn = k * lax.rsqrt(msk + EPS) * kg_ref[...]
        kT_ref[0] = kn.T.astype(BF16)
        v_ref[0] = kv[:, MEM_WIDTH:].astype(BF16)


def _mem_kv(mem, mem_norm_g, w_mem_kv, gmean_mem, mem_k_g, w_in):
    B, M, _ = mem.shape
    assert B <= PREP_STEPS
    rows = D_MODEL // PREP_STEPS
    batch_of = lambda i: jnp.minimum(i, B - 1)
    return pl.pallas_call(
        functools.partial(_mem_kv_kernel, n_batches=B),
        grid=(PREP_STEPS,),
        in_specs=[
            pl.BlockSpec((1, M, D_MODEL), lambda i: (batch_of(i), 0, 0)),
            _const_spec((1, D_MODEL)),
            _const_spec((D_MODEL, 2 * MEM_WIDTH)),
            _const_spec((MEM_WIDTH, MEM_WIDTH)),
            _const_spec((1, MEM_WIDTH)),
            pl.BlockSpec((rows, IN_WIDTH), lambda i: (i, 0)),
        ],
        out_specs=[
            pl.BlockSpec((1, MEM_WIDTH, M), lambda i: (batch_of(i), 0, 0)),
            pl.BlockSpec((1, M, MEM_WIDTH), lambda i: (batch_of(i), 0, 0)),
            pl.BlockSpec((rows, IN_WIDTH), lambda i: (i, 0)),
            pl.BlockSpec((NA_WIDTH, rows), lambda i: (0, i)),
        ],
        out_shape=[
            jax.ShapeDtypeStruct((B, MEM_WIDTH, M), BF16),
            jax.ShapeDtypeStruct((B, M, MEM_WIDTH), BF16),
            jax.ShapeDtypeStruct((D_MODEL, IN_WIDTH), BF16),
            jax.ShapeDtypeStruct((NA_WIDTH, D_MODEL), BF16),
        ],
        compiler_params=_params(1),
        name="mem_kv",
    )(mem, mem_norm_g.reshape(1, D_MODEL), w_mem_kv, gmean_mem,
      jnp.tile(mem_k_g, MEM_HEADS).reshape(1, MEM_WIDTH), w_in)


def _in_proj_kernel(x_ref, g1_ref, w_ref, wkT_ref, bg_ref,
                    qg_ref, kg_ref, mqg_ref, memkT_ref, memv_ref, brow_ref, *rest,
                    n_narrow, bias_dr):
    wide_refs = rest[:n_narrow]
    q_ref, kT_ref, v_ref, fu_ref, om_ref, gate_ref, bias_ref = rest[n_narrow:n_narrow + 7]

    x = x_ref[...]
    ms = jnp.mean(x * x, axis=-1, keepdims=True)
    h = (x * lax.rsqrt(ms + EPS) * g1_ref[...]).astype(BF16)

    def gate_dot(br):
        sl = slice(br * D_MODEL, (br + 1) * D_MODEL)
        zg = _dot(h, w_ref[:, IN_G0 + br * D_MODEL:IN_G0 + (br + 1) * D_MODEL]) + bg_ref[:, sl]
        gate_ref[:, sl] = jax.nn.sigmoid(zg).astype(BF16)

    zm = _dot(h, w_ref[:, IN_M0:IN_M0 + MEM_WIDTH])
    mem_scores = []
    for hd in range(MEM_HEADS):
        sl = slice(hd * MEM_HEAD_DIM, (hd + 1) * MEM_HEAD_DIM)
        zh = zm[:, sl]
        msm = jnp.mean(zh * zh, axis=-1, keepdims=True)
        qm = (zh * lax.rsqrt(msm + EPS) * mqg_ref[:, sl]).astype(BF16)
        if hd == 0:
            gate_dot(0)
        mem_scores.append(_dot(qm, memkT_ref[0, sl, :]))
    gate_dot(1)

    zq = _dot(h, w_ref[:, IN_Q0:IN_Q0 + NA_WIDTH])
    low_half = lax.broadcasted_iota(jnp.int32, (zq.shape[0], V7X_LANES), 1) < NA_HEAD_DIM
    for hp in range(NA_HEADS // 2):
        sl = slice(hp * V7X_LANES, (hp + 1) * V7X_LANES)
        zz = zq[:, sl] * zq[:, sl]
        ms_lo = jnp.sum(jnp.where(low_half, zz, 0.0), axis=-1, keepdims=True)
        ms_hi = jnp.sum(jnp.where(low_half, 0.0, zz), axis=-1, keepdims=True)
        ms = jnp.where(low_half, ms_lo, ms_hi) * (1.0 / NA_HEAD_DIM)
        q_ref[:, sl] = (zq[:, sl] * lax.rsqrt(ms + EPS) * qg_ref[:, sl]).astype(BF16)

    zkT = _dot_nt(wkT_ref[...], h)
    zk3 = zkT.reshape(NA_HEADS, NA_HEAD_DIM, zkT.shape[1])
    msk = jnp.mean(zk3 * zk3, axis=1, keepdims=True)
    kT = ((zk3 * lax.rsqrt(msk + EPS)).reshape(zkT.shape) * kg_ref[...]).astype(BF16)
    for j in range(kT_ref.shape[1]):
        kT_ref[0, j] = kT[:, j * V7X_LANES:(j + 1) * V7X_LANES]

    zv = _dot(h, w_ref[:, IN_V0:IN_V0 + NA_WIDTH])
    low_half = lax.broadcasted_iota(jnp.int32, (zv.shape[0], V7X_LANES), 1) < NA_HEAD_DIM
    for hp in range(NA_HEADS // 2):
        pair = zv[:, hp * V7X_LANES:(hp + 1) * V7X_LANES]
        v_ref[:, (2 * hp) * V7X_LANES:(2 * hp + 1) * V7X_LANES] = jnp.where(low_half, pair, 1.0).astype(BF16)
        v_ref[:, (2 * hp + 1) * V7X_LANES:(2 * hp + 2) * V7X_LANES] = jnp.where(low_half, 1.0, pair).astype(BF16)
    fu_ref[...] = _dot(h, w_ref[:, IN_F0:IN_F0 + F_WIDTH]).astype(BF16)
    gate_dot(2)

    outs = []
    for hd in range(MEM_HEADS):
        sl = slice(hd * MEM_HEAD_DIM, (hd + 1) * MEM_HEAD_DIM)
        s = mem_scores[hd]
        e = jnp.exp(s - jnp.max(s, axis=-1, keepdims=True))
        l = jnp.sum(e, axis=-1, keepdims=True)
        outs.append(_dot(e.astype(BF16), memv_ref[0, :, sl]) / l)
    om_ref[...] = jnp.concatenate(outs, axis=1).astype(BF16)

    for wide_ref, narrow_ref in zip(wide_refs, rest[n_narrow + 7:]):
        narrow_ref[...] = wide_ref[...].astype(BF16)

    @pl.when(pl.program_id(0) < NA_HEADS)
    def _():
        _assemble_na_bias(brow_ref, bias_ref, bias_dr)


def _in_proj(x2, norm1_g, w_in_bf16, w_kT, b_gate, na_q_g, na_k_g, mem_q_g,
             memkT, memv, bias_blocks, later_weights, tokens_per_batch, tm):
    T = x2.shape[0]
    bias_rows, bias_dr = bias_blocks
    n_cases = bias_dr.shape[0]
    M = memv.shape[1]
    tiles_per_batch = tokens_per_batch // tm
    slabs = tm // V7X_LANES
    B = T // tokens_per_batch
    na_scale = np.float32(1.0 / np.sqrt(NA_HEAD_DIM))
    mem_scale = np.float32(1.0 / np.sqrt(MEM_HEAD_DIM))
    qg = (jnp.tile(na_q_g, NA_HEADS) * na_scale).reshape(1, NA_WIDTH)
    kg = jnp.tile(na_k_g, NA_HEADS).reshape(NA_WIDTH, 1)
    mqg = (jnp.tile(mem_q_g, MEM_HEADS) * mem_scale).reshape(1, MEM_WIDTH)
    tok = lambda w: pl.BlockSpec((tm, w), lambda i: (i, 0))
    batch_of = lambda i: i // tiles_per_batch
    steps = T // tm
    assert steps >= NA_HEADS
    slab_specs = [pl.BlockSpec((w.shape[0] // steps, w.shape[1]), lambda i: (i, 0))
                  for w in later_weights]
    head_of = lambda i: jnp.minimum(i, NA_HEADS - 1)
    bias_rows_spec = pl.BlockSpec((1,) + bias_rows.shape[1:], lambda i: (head_of(i), 0, 0))
    return pl.pallas_call(
        functools.partial(_in_proj_kernel, n_narrow=len(later_weights), bias_dr=bias_dr),
        grid=(steps,),
        in_specs=[
            tok(D_MODEL),
            _const_spec((1, D_MODEL)),
            _const_spec((D_MODEL, IN_WIDTH)),
            _const_spec((NA_WIDTH, D_MODEL)),
            _const_spec((1, N_BRANCHES * D_MODEL)),
            _const_spec((1, NA_WIDTH)),
            _const_spec((NA_WIDTH, 1)),
            _const_spec((1, MEM_WIDTH)),
            pl.BlockSpec((1, MEM_WIDTH, M), lambda i: (batch_of(i), 0, 0)),
            pl.BlockSpec((1, M, MEM_WIDTH), lambda i: (batch_of(i), 0, 0)),
            bias_rows_spec,
        ] + slab_specs,
        out_specs=[
            tok(NA_WIDTH),
            pl.BlockSpec((1, slabs, NA_WIDTH, V7X_LANES),
                         lambda i: (batch_of(i), i % tiles_per_batch, 0, 0)),
            tok(NA_HEADS * V7X_LANES),
            tok(F_WIDTH),
            tok(MEM_WIDTH),
            tok(N_BRANCHES * D_MODEL),
            pl.BlockSpec((n_cases, 1, NA_PAIR_TOKENS, NA_KEY_TOKENS), lambda i: (0, head_of(i), 0, 0)),
        ] + slab_specs,
        out_shape=[
            jax.ShapeDtypeStruct((T, NA_WIDTH), BF16),
            jax.ShapeDtypeStruct((B, tokens_per_batch // V7X_LANES, NA_WIDTH, V7X_LANES), BF16),
            jax.ShapeDtypeStruct((T, NA_HEADS * V7X_LANES), BF16),
            jax.ShapeDtypeStruct((T, F_WIDTH), BF16),
            jax.ShapeDtypeStruct((T, MEM_WIDTH), BF16),
            jax.ShapeDtypeStruct((T, N_BRANCHES * D_MODEL), BF16),
            jax.ShapeDtypeStruct((n_cases, NA_HEADS, NA_PAIR_TOKENS, NA_KEY_TOKENS), F32),
        ] + [jax.ShapeDtypeStruct(w.shape, BF16) for w in later_weights],
        compiler_params=_params(1),
        name="in_proj",
    )(x2, norm1_g.reshape(1, D_MODEL), w_in_bf16, w_kT,
      b_gate.reshape(1, N_BRANCHES * D_MODEL), qg, kg, mqg, memkT, memv, bias_rows,
      *later_weights)


def _na_bias_blocks(rpb, rows):
    n_pairs = rows // 2
    wr = min(NA_WIN_ROWS, rows)
    reps = np.array([0, 1, 2, n_pairs - 2, n_pairs - 1])
    starts = np.clip(reps - 2, 0, n_pairs - NA_KEY_PAIRS)
    n_dr, n_dc = 2 * NA_WIN_ROWS - 1, 2 * NA_WIN_COLS - 1
    offsets = np.arange(n_dc) - (NA_WIN_COLS - 1)
    place = (np.arange(2 * GRID_W)[None, :] % GRID_W == (offsets % GRID_W)[:, None]).astype(np.float32)
    rows128 = jnp.dot(rpb.astype(F32).reshape(NA_HEADS * n_dr, n_dc), place,
                      precision=lax.Precision.HIGHEST).reshape(NA_HEADS, n_dr, 2 * GRID_W)
    r = (2 * reps[:, None] + np.arange(2)[None, :])[:, :, None]
    kr = (2 * starts[:, None] + np.arange(2 * NA_KEY_PAIRS)[None, :])[:, None, :]
    rs = np.clip(r - wr // 2, 0, rows - wr)
    row_valid = (kr >= rs) & (kr < rs + wr)
    dr = np.where(row_valid, kr - r + (NA_WIN_ROWS - 1), n_dr)
    assert dr.min() >= 0 and dr.max() <= n_dr
    return rows128, dr


def _assemble_na_bias(rows_ref, o_ref, dr):
    n_dr = rows_ref.shape[1]
    c = lax.broadcasted_iota(jnp.int32, (GRID_W, 2 * GRID_W), 0)
    lane = lax.broadcasted_iota(jnp.int32, (GRID_W, 2 * GRID_W), 1)
    kc = lane & (GRID_W - 1)
    cs = jnp.clip(c - NA_WIN_COLS // 2, 0, GRID_W - NA_WIN_COLS)
    col_valid = (kc >= cs) & (kc < cs + NA_WIN_COLS)
    masked = jnp.full((GRID_W, 2 * GRID_W), NEG_INF, F32)
    blocks = []
    for i in range(n_dr):
        row = jnp.broadcast_to(rows_ref[0, i:i + 1, :], (GRID_W, 2 * GRID_W))
        toeplitz = pltpu.roll(row, 0, axis=1, stride=1, stride_axis=0)
        blocks.append(jnp.where(col_valid, toeplitz, masked))
    blocks.append(masked)
    even_row = lane < GRID_W
    for case in range(dr.shape[0]):
        for rr in range(2):
            for j in range(NA_KEY_PAIRS):
                tile = jnp.where(even_row, blocks[int(dr[case, rr, 2 * j])],
                                 blocks[int(dr[case, rr, 2 * j + 1])])
                o_ref[case, 0, rr * GRID_W:(rr + 1) * GRID_W,
                      j * V7X_LANES:(j + 1) * V7X_LANES] = tile


def _na_kernel(q_ref, kT_ref, v_ref, bias_ref, o_ref, *, pairs_per_step, n_pairs):
    lane = lax.broadcasted_iota(jnp.int32, (NA_PAIR_TOKENS, 2 * NA_HEAD_DIM), 1)
    first_head = lane < NA_HEAD_DIM

    def pair_coords(i):
        p = pl.program_id(1) * pairs_per_step + i
        start = jnp.clip(p - 2, 0, n_pairs - NA_KEY_PAIRS)
        case = jnp.where(p < 2, p, jnp.where(p >= n_pairs - 2, p - (n_pairs - NA_BIAS_CASES), 2))
        qrow = pl.multiple_of(i * NA_PAIR_TOKENS, NA_PAIR_TOKENS)
        krow = pl.multiple_of(start * NA_PAIR_TOKENS, NA_PAIR_TOKENS)
        return start, case, qrow, krow

    kzero = jnp.zeros((NA_HEAD_DIM, NA_KEY_TOKENS), BF16)
    vzero = jnp.zeros((NA_KEY_TOKENS, 2 * NA_HEAD_DIM), BF16)

    def scores(coords, hp):
        start, case, qrow, _ = coords
        psl = slice(hp * 2 * NA_HEAD_DIM, (hp + 1) * 2 * NA_HEAD_DIM)
        qp = q_ref[0, pl.ds(qrow, NA_PAIR_TOKENS), psl]
        kslabs = kT_ref[0, pl.ds(start, NA_KEY_PAIRS), psl, :]
        kT = jnp.concatenate([kslabs[j] for j in range(NA_KEY_PAIRS)], axis=1)
        keys = jnp.concatenate([jnp.concatenate([kT[:NA_HEAD_DIM], kzero], axis=1),
                                jnp.concatenate([kzero, kT[NA_HEAD_DIM:]], axis=1)], axis=0)
        bias = jnp.concatenate([bias_ref[case, 2 * hp], bias_ref[case, 2 * hp + 1]], axis=1)
        return _dot(qp, keys) + bias

    def body(it, carry):
        coords = [pair_coords(it * NA_PAIRS_PER_ITER + j) for j in range(NA_PAIRS_PER_ITER)]
        items = [(j, hp) for j in range(NA_PAIRS_PER_ITER) for hp in range(NA_HEADS // 2)]
        pending = [scores(coords[j], hp) for j, hp in items[:NA_SCORES_AHEAD]]
        outs = []
        for n, (j, hp) in enumerate(items):
            s = pending.pop(0)
            if n + NA_SCORES_AHEAD < len(items):
                jn, hn = items[n + NA_SCORES_AHEAD]
                pending.append(scores(coords[jn], hn))
            _, _, qrow, krow = coords[j]
            es, vs = [], []
            for sub in range(2):
                hd = 2 * hp + sub
                sh = s[:, sub * NA_KEY_TOKENS:(sub + 1) * NA_KEY_TOKENS]
                es.append(jnp.exp((sh - jnp.max(sh, axis=-1, keepdims=True)).astype(BF16)))
                vs.append(v_ref[0, pl.ds(krow, NA_KEY_TOKENS), hd * 2 * NA_HEAD_DIM:(hd + 1) * 2 * NA_HEAD_DIM])
            values = jnp.concatenate([jnp.concatenate([vs[0], vzero], axis=1),
                                      jnp.concatenate([vzero, vs[1]], axis=1)], axis=0)
            r = _dot(jnp.concatenate(es, axis=1), values)
            halves = [r[:, sub * 2 * NA_HEAD_DIM:(sub + 1) * 2 * NA_HEAD_DIM] for sub in range(2)]
            halves = [h / pltpu.roll(h, NA_HEAD_DIM, axis=1) for h in halves]
            outs.append(jnp.where(first_head, halves[0], halves[1]))
            if hp == NA_HEADS // 2 - 1:
                o_ref[0, pl.ds(qrow, NA_PAIR_TOKENS), :] = jnp.concatenate(outs, axis=1).astype(BF16)
                outs = []
        return carry

    lax.fori_loop(0, pairs_per_step // NA_PAIRS_PER_ITER, body, 0)


def _na_attention(q, kT4, v, bias, pairs_per_step):
    B, S, _ = q.shape
    n_pairs = S // NA_PAIR_TOKENS
    steps = n_pairs // pairs_per_step
    tq = pairs_per_step * NA_PAIR_TOKENS
    return pl.pallas_call(
        functools.partial(_na_kernel, pairs_per_step=pairs_per_step, n_pairs=n_pairs),
        grid=(B, steps),
        in_specs=[
            pl.BlockSpec((1, tq, NA_WIDTH), lambda b, j: (b, j, 0)),
            pl.BlockSpec((1, n_pairs, NA_WIDTH, V7X_LANES), lambda b, j: (b, 0, 0, 0)),
            pl.BlockSpec((1, S, NA_HEADS * V7X_LANES), lambda b, j: (b, 0, 0)),
            _const_spec((NA_BIAS_CASES, NA_HEADS, NA_PAIR_TOKENS, NA_KEY_TOKENS)),
        ],
        out_specs=pl.BlockSpec((1, tq, NA_WIDTH), lambda b, j: (b, j, 0)),
        out_shape=jax.ShapeDtypeStruct((B, S, NA_WIDTH), BF16),
        compiler_params=_params(2),
        name="na_attn",
    )(q, kT4, v, bias)


DFT_CHUNK = 16


def _dft_tables():
    n, m = GRID_W, DFT_CHUNK
    a = np.arange(n)
    ang1 = 2.0 * np.pi * np.outer(a, a) / n
    w_ri = np.stack([np.cos(ang1), -np.sin(ang1)], axis=1) / np.sqrt(n)
    eye = np.eye(m)
    w_rows = w_ri.reshape(2 * n, n)
    perm = np.einsum('dx,cy->dcyx', eye, eye).reshape(m * m, m * m)
    c = np.arange(n)[:, None, None]
    d = np.arange(n)[None, :, None]
    b = np.arange(n)[None, None, :]
    ang2 = 2.0 * np.pi * ((c + n * d) * b % (n * n)) / (n * n)
    gr, gi = np.cos(ang2) / np.sqrt(n), -np.sin(ang2) / np.sqrt(n)
    w_cols = np.concatenate([np.concatenate([gr, -gi], axis=2),
                             np.concatenate([gi, gr], axis=2)], axis=1)
    ch = np.arange(F_GROUP_DIM)
    ang3 = 2.0 * np.pi * np.outer(ch, ch) / F_GROUP_DIM
    w_chan = np.concatenate([np.cos(ang3), np.sin(ang3)], axis=0) / np.sqrt(F_GROUP_DIM)
    return tuple(jnp.asarray(t, dtype=F32).astype(BF16) for t in (w_rows, w_cols, perm, w_chan))


def _dft_kernel(wr_ref, wc_ref, perm_ref, u_ref, y_ref, z_ref):
    n, m = GRID_W, DFT_CHUNK

    def rows_stage(j, carry):
        col0 = j * m
        x = u_ref[0, :, pl.ds(col0, m), :].reshape(n * m, F_WIDTH)
        xp = [_dot(perm_ref[...], x[g * m * m:(g + 1) * m * m]).astype(BF16) for g in range(n // m)]
        zb = []
        for b in range(m):
            xb = jnp.concatenate([xp[g][b * m:(b + 1) * m] for g in range(n // m)], axis=0)
            zb.append(_dot(wr_ref[...], xb).astype(BF16))
        for t in range(2 * n // m):
            st = jnp.concatenate([zb[b][t * m:(t + 1) * m] for b in range(m)], axis=0)
            z = _dot(perm_ref[...], st).astype(BF16)
            z_ref[t * 8:(t + 1) * 8, :, pl.ds(col0, m), :] = z.reshape(8, 2, m, F_WIDTH)
        return carry

    def cols_stage(j, carry):
        col0 = j * m
        res = []
        for c_lo in range(m):
            c = j * m + c_lo
            data = z_ref[c].reshape(2 * n, F_WIDTH)
            res.append(_dot(wc_ref[c], data).astype(BF16))
        for ri in range(2):
            for g in range(n // m):
                lo = ri * n + g * m
                stacked = jnp.concatenate([r[lo:lo + m] for r in res], axis=0)
                y = _dot(perm_ref[...], stacked).astype(BF16)
                y_ref[ri, 0, g * m:(g + 1) * m, pl.ds(col0, m), :] = y.reshape(m, m, F_WIDTH)
        return carry

    for j in range(n // m):
        rows_stage(j, 0)
    for j in range(n // m):
        cols_stage(j, 0)


def _position_dft(u4, w_rows, w_cols, perm):
    B, n, m = u4.shape[0], GRID_W, DFT_CHUNK
    return pl.pallas_call(
        _dft_kernel,
        grid=(B,),
        in_specs=[
            _const_spec((2 * n, n)),
            _const_spec((n, 2 * n, 2 * n)),
            _const_spec((m * m, m * m)),
            pl.BlockSpec((1, n, n, F_WIDTH), lambda b: (b, 0, 0, 0)),
        ],
        out_specs=pl.BlockSpec((2, 1, n, n, F_WIDTH), lambda b: (0, b, 0, 0, 0)),
        out_shape=jax.ShapeDtypeStruct((2, B, n, n, F_WIDTH), BF16),
        scratch_shapes=[pltpu.VMEM((n, 2, n, F_WIDTH), BF16)],
        compiler_params=_params(1),
        name="position_dft",
    )(w_rows, w_cols, perm, u4)


def _out_ffn_kernel(x_ref, ona_ref, y_ref, om_ref, gate_ref, wch_ref, wna_ref, wf_ref, wmo_ref,
                    wout_ref, g2_ref, w1_ref, w2_ref, o_ref, *, ff_chunk):
    yr, yi = y_ref[0], y_ref[1]
    groups = []
    for g in range(F_GROUPS):
        sl = slice(g * F_GROUP_DIM, (g + 1) * F_GROUP_DIM)
        groups.append(_dot(jnp.concatenate([yr[:, sl], yi[:, sl]], axis=1), wch_ref[...]))
    yf = jnp.concatenate(groups, axis=1).astype(BF16)

    gate = lambda br: gate_ref[:, br * D_MODEL:(br + 1) * D_MODEL].astype(F32)
    merged = gate(0) * _dot(ona_ref[...], wna_ref[...])
    merged = merged + gate(1) * _dot(yf, wf_ref[...])
    merged = merged + gate(2) * _dot(om_ref[...], wmo_ref[...])
    x1 = x_ref[...] + _dot(merged.astype(BF16), wout_ref[...])

    ms = jnp.mean(x1 * x1, axis=-1, keepdims=True)
    h2 = (x1 * lax.rsqrt(ms + EPS) * g2_ref[...]).astype(BF16)
    acc = x1
    for j in range(D_FF // ff_chunk):
        sl = slice(j * ff_chunk, (j + 1) * ff_chunk)
        a = jnp.maximum(_dot(h2, w1_ref[:, sl]), 0.0)
        acc = acc + _dot((a * a).astype(BF16), w2_ref[sl, :])
    o_ref[...] = acc


def _out_ffn(x2, ona, y3, om, gates, w_chan, w_na_o, w_f, w_mem_o, w_out, norm2_g, w_ff1, w_ff2,
             tm, ff_chunk):
    T = x2.shape[0]
    tok = lambda w: pl.BlockSpec((tm, w), lambda i: (i, 0))
    return pl.pallas_call(
        functools.partial(_out_ffn_kernel, ff_chunk=ff_chunk),
        grid=(T // tm,),
        in_specs=[
            tok(D_MODEL),
            tok(NA_WIDTH),
            pl.BlockSpec((2, tm, F_WIDTH), lambda i: (0, i, 0)),
            tok(MEM_WIDTH),
            tok(N_BRANCHES * D_MODEL),
            _const_spec((2 * F_GROUP_DIM, F_GROUP_DIM)),
            _const_spec((NA_WIDTH, D_MODEL)),
            _const_spec((F_WIDTH, D_MODEL)),
            _const_spec((MEM_WIDTH, D_MODEL)),
            _const_spec((D_MODEL, D_MODEL)),
            _const_spec((1, D_MODEL)),
            _const_spec((D_MODEL, D_FF)),
            _const_spec((D_FF, D_MODEL)),
        ],
        out_specs=tok(D_MODEL),
        out_shape=jax.ShapeDtypeStruct((T, D_MODEL), F32),
        compiler_params=_params(1),
        name="out_ffn",
    )(x2, ona, y3, om, gates, w_chan, w_na_o, w_f, w_mem_o, w_out, norm2_g.reshape(1, D_MODEL),
      w_ff1, w_ff2)


def kernel(x, mem, norm1_g, w_in, b_gate, na_q_g, na_k_g, na_rpb, w_na_o, w_f, mem_norm_g,
           w_mem_kv, mem_q_g, mem_k_g, w_mem_o, w_out, norm2_g, w_ff1, w_ff2):
    B, S, _ = x.shape
    T = B * S
    rows = S // GRID_W
    assert rows == GRID_W and S % NA_PAIR_TOKENS == 0
    x2 = x.reshape(T, D_MODEL)
    gmean_mem = _group_mean_matrix(MEM_WIDTH, MEM_HEAD_DIM)
    w_rows, w_cols, perm, w_chan = _dft_tables()

    memkT, memv, w_in_bf16, w_kT = _mem_kv(mem, mem_norm_g, w_mem_kv, gmean_mem, mem_k_g, w_in)
    later_weights = (w_na_o, w_f, w_mem_o, w_out, w_ff1, w_ff2)
    q, kT4, v, fu, om, gates, bias, *narrow = _in_proj(
        x2, norm1_g, w_in_bf16, w_kT, b_gate, na_q_g, na_k_g, mem_q_g, memkT, memv,
        _na_bias_blocks(na_rpb, rows), later_weights, S, tm=IN_PROJ_TOKENS)
    w_na_o, w_f, w_mem_o, w_out, w_ff1, w_ff2 = narrow

    ona = _na_attention(q.reshape(B, S, NA_WIDTH), kT4, v.reshape(B, S, NA_HEADS * V7X_LANES), bias,
                        pairs_per_step=NA_PAIRS_PER_STEP)

    y = _position_dft(fu.reshape(B, GRID_W, GRID_W, F_WIDTH), w_rows, w_cols, perm)
    y3 = y.reshape(2, T, F_WIDTH)

    out = _out_ffn(x2, ona.reshape(T, NA_WIDTH), y3, om, gates, w_chan, w_na_o, w_f, w_mem_o,
                   w_out, norm2_g, w_ff1, w_ff2, tm=OUT_FFN_TOKENS, ff_chunk=FF_CHUNK)
    return out.reshape(B, S, D_MODEL)
```

```python
import functools

import numpy as np
import jax
import jax.numpy as jnp
from jax import lax
from jax.experimental import pallas as pl
from jax.experimental.pallas import tpu as pltpu

D_MODEL = 1024
GRID_W = 64
NA_HEADS = 8
NA_HEAD_DIM = 64
NA_WIDTH = NA_HEADS * NA_HEAD_DIM
NA_WIN_ROWS = 8
NA_WIN_COLS = 16
F_GROUPS = 4
F_GROUP_DIM = 128
F_WIDTH = F_GROUPS * F_GROUP_DIM
MEM_HEADS = 4
MEM_HEAD_DIM = 128
MEM_WIDTH = MEM_HEADS * MEM_HEAD_DIM
N_BRANCHES = 3
D_FF = 4 * D_MODEL
IN_Q0 = 0
IN_K0 = IN_Q0 + NA_WIDTH
IN_V0 = IN_K0 + NA_WIDTH
IN_F0 = IN_V0 + NA_WIDTH
IN_M0 = IN_F0 + F_WIDTH
IN_G0 = IN_M0 + MEM_WIDTH
IN_WIDTH = IN_G0 + N_BRANCHES * D_MODEL
EPS = 1e-6
NEG_INF = -1e30

V7X_LANES = 128
V7X_VMEM_LIMIT_BYTES = 60 * 1024 * 1024

NA_PAIR_TOKENS = 2 * GRID_W
NA_KEY_PAIRS = 5
NA_KEY_TOKENS = NA_KEY_PAIRS * NA_PAIR_TOKENS
NA_BIAS_CASES = 5
NA_PAIRS_PER_ITER = 16
NA_SCORES_AHEAD = 2

IN_PROJ_TOKENS = 512
OUT_FFN_TOKENS = 512
NA_PAIRS_PER_STEP = 16

BF16 = jnp.bfloat16
F32 = jnp.float32


def _dot(a, b):
    return jnp.dot(a, b, preferred_element_type=F32)


def _dot_nt(a, b):
    return lax.dot_general(a, b, (((1,), (1,)), ((), ())), preferred_element_type=F32)


def _const_spec(shape):
    return pl.BlockSpec(shape, lambda *_: (0,) * len(shape), pipeline_mode=pl.Buffered(1))


def _params(n_axes):
    return pltpu.CompilerParams(
        dimension_semantics=("arbitrary",) * n_axes,
        vmem_limit_bytes=V7X_VMEM_LIMIT_BYTES,
    )


def _group_mean_matrix(width, group):
    idx = np.arange(width) // group
    return jnp.asarray((idx[:, None] == idx[None, :]).astype(np.float32) / group, dtype=BF16)


PREP_STEPS = 4


def _mem_kv_kernel(mem_ref, g_ref, w_ref, gm_ref, kg_ref, win_ref, kT_ref, v_ref, wb_ref, wkT_ref,
                   *, n_batches):
    slab = win_ref[...]
    wb_ref[...] = slab.astype(BF16)
    wkT_ref[...] = slab[:, IN_K0:IN_K0 + NA_WIDTH].T.astype(BF16)

    @pl.when(pl.program_id(0) < n_batches)
    def _():
        m = mem_ref[0]
        ms = jnp.mean(m * m, axis=-1, keepdims=True)
        mn = (m * lax.rsqrt(ms + EPS) * g_ref[...]).astype(BF16)
        kv = _dot(mn, w_ref[...].astype(BF16))
        k = kv[:, :MEM_WIDTH]
        msk = _dot((k * k).astype(BF16), gm_ref[...])
        kn = k * lax.rsqrt(msk + EPS) * kg_ref[...]
        kT_ref[0] = kn.T.astype(BF16)
        v_ref[0] = kv[:, MEM_WIDTH:].astype(BF16)


def _mem_kv(mem, mem_norm_g, w_mem_kv, gmean_mem, mem_k_g, w_in):
    B, M, _ = mem.shape
    assert B <= PREP_STEPS
    rows = D_MODEL // PREP_STEPS
    batch_of = lambda i: jnp.minimum(i, B - 1)
    return pl.pallas_call(
        functools.partial(_mem_kv_kernel, n_batches=B),
        grid=(PREP_STEPS,),
        in_specs=[
            pl.BlockSpec((1, M, D_MODEL), lambda i: (batch_of(i), 0, 0)),
            _const_spec((1, D_MODEL)),
            _const_spec((D_MODEL, 2 * MEM_WIDTH)),
            _const_spec((MEM_WIDTH, MEM_WIDTH)),
            _const_spec((1, MEM_WIDTH)),
            pl.BlockSpec((rows, IN_WIDTH), lambda i: (i, 0)),
        ],
        out_specs=[
            pl.BlockSpec((1, MEM_WIDTH, M), lambda i: (batch_of(i), 0, 0)),
            pl.BlockSpec((1, M, MEM_WIDTH), lambda i: (batch_of(i), 0, 0)),
            pl.BlockSpec((rows, IN_WIDTH), lambda i: (i, 0)),
            pl.BlockSpec((NA_WIDTH, rows), lambda i: (0, i)),
        ],
        out_shape=[
            jax.ShapeDtypeStruct((B, MEM_WIDTH, M), BF16),
            jax.ShapeDtypeStruct((B, M, MEM_WIDTH), BF16),
            jax.ShapeDtypeStruct((D_MODEL, IN_WIDTH), BF16),
            jax.ShapeDtypeStruct((NA_WIDTH, D_MODEL), BF16),
        ],
        compiler_params=_params(1),
        name="mem_kv",
    )(mem, mem_norm_g.reshape(1, D_MODEL), w_mem_kv, gmean_mem,
      jnp.tile(mem_k_g, MEM_HEADS).reshape(1, MEM_WIDTH), w_in)


def _in_proj_kernel(x_ref, g1_ref, w_ref, wkT_ref, bg_ref,
                    qg_ref, kg_ref, mqg_ref, memkT_ref, memv_ref, brow_ref, *rest,
                    n_narrow, bias_dr):
    wide_refs = rest[:n_narrow]
    q_ref, kT_ref, v_ref, fu_ref, om_ref, gate_ref, bias_ref = rest[n_narrow:n_narrow + 7]

    x = x_ref[...]
    ms = jnp.mean(x * x, axis=-1, keepdims=True)
    h = (x * lax.rsqrt(ms + EPS) * g1_ref[...]).astype(BF16)

    zvfm = _dot(h, w_ref[:, IN_V0:IN_G0])
    zv = zvfm[:, :NA_WIDTH]
    fu_ref[...] = zvfm[:, IN_F0 - IN_V0:IN_M0 - IN_V0].astype(BF16)
    zm = zvfm[:, IN_M0 - IN_V0:]

    mem_scores = []
    for hd in range(MEM_HEADS):
        sl = slice(hd * MEM_HEAD_DIM, (hd + 1) * MEM_HEAD_DIM)
        zh = zm[:, sl]
        msm = jnp.mean(zh * zh, axis=-1, keepdims=True)
        qm = (zh * lax.rsqrt(msm + EPS) * mqg_ref[:, sl]).astype(BF16)
        mem_scores.append(_dot(qm, memkT_ref[0, sl, :]))

    zg = _dot(h, w_ref[:, IN_G0:]) + bg_ref[...]
    gate_ref[...] = jax.nn.sigmoid(zg).astype(BF16)

    zq = _dot(h, w_ref[:, IN_Q0:IN_Q0 + NA_WIDTH])
    low_half = lax.broadcasted_iota(jnp.int32, (zq.shape[0], V7X_LANES), 1) < NA_HEAD_DIM
    for hp in range(NA_HEADS // 2):
        sl = slice(hp * V7X_LANES, (hp + 1) * V7X_LANES)
        zz = zq[:, sl] * zq[:, sl]
        ms_lo = jnp.sum(jnp.where(low_half, zz, 0.0), axis=-1, keepdims=True)
        ms_hi = jnp.sum(jnp.where(low_half, 0.0, zz), axis=-1, keepdims=True)
        ms = jnp.where(low_half, ms_lo, ms_hi) * (1.0 / NA_HEAD_DIM)
        q_ref[:, sl] = (zq[:, sl] * lax.rsqrt(ms + EPS) * qg_ref[:, sl]).astype(BF16)

    zkT = _dot_nt(wkT_ref[...], h)
    zk3 = zkT.reshape(NA_HEADS, NA_HEAD_DIM, zkT.shape[1])
    msk = jnp.mean(zk3 * zk3, axis=1, keepdims=True)
    kT = ((zk3 * lax.rsqrt(msk + EPS)).reshape(zkT.shape) * kg_ref[...]).astype(BF16)
    for j in range(kT_ref.shape[1]):
        kT_ref[0, j] = kT[:, j * V7X_LANES:(j + 1) * V7X_LANES]

    for hp in range(NA_HEADS // 2):
        pair = zv[:, hp * V7X_LANES:(hp + 1) * V7X_LANES]
        v_ref[:, (2 * hp) * V7X_LANES:(2 * hp + 1) * V7X_LANES] = jnp.where(low_half, pair, 1.0).astype(BF16)
        v_ref[:, (2 * hp + 1) * V7X_LANES:(2 * hp + 2) * V7X_LANES] = jnp.where(low_half, 1.0, pair).astype(BF16)

    outs = []
    for hd in range(MEM_HEADS):
        sl = slice(hd * MEM_HEAD_DIM, (hd + 1) * MEM_HEAD_DIM)
        s = mem_scores[hd]
        e = jnp.exp(s - jnp.max(s, axis=-1, keepdims=True))
        l = jnp.sum(e, axis=-1, keepdims=True)
        outs.append(_dot(e.astype(BF16), memv_ref[0, :, sl]) / l)
    om_ref[...] = jnp.concatenate(outs, axis=1).astype(BF16)

    for wide_ref, narrow_ref in zip(wide_refs, rest[n_narrow + 7:]):
        narrow_ref[...] = wide_ref[...].astype(BF16)

    @pl.when(pl.program_id(0) < NA_HEADS)
    def _():
        _assemble_na_bias(brow_ref, bias_ref, bias_dr)


def _in_proj(x2, norm1_g, w_in_bf16, w_kT, b_gate, na_q_g, na_k_g, mem_q_g,
             memkT, memv, bias_blocks, later_weights, tokens_per_batch, tm):
    T = x2.shape[0]
    bias_rows, bias_dr = bias_blocks
    n_cases = bias_dr.shape[0]
    M = memv.shape[1]
    tiles_per_batch = tokens_per_batch // tm
    slabs = tm // V7X_LANES
    B = T // tokens_per_batch
    na_scale = np.float32(1.0 / np.sqrt(NA_HEAD_DIM))
    mem_scale = np.float32(1.0 / np.sqrt(MEM_HEAD_DIM))
    qg = (jnp.tile(na_q_g, NA_HEADS) * na_scale).reshape(1, NA_WIDTH)
    kg = jnp.tile(na_k_g, NA_HEADS).reshape(NA_WIDTH, 1)
    mqg = (jnp.tile(mem_q_g, MEM_HEADS) * mem_scale).reshape(1, MEM_WIDTH)
    tok = lambda w: pl.BlockSpec((tm, w), lambda i: (i, 0))
    batch_of = lambda i: i // tiles_per_batch
    steps = T // tm
    assert steps >= NA_HEADS
    slab_specs = [pl.BlockSpec((w.shape[0] // steps, w.shape[1]), lambda i: (i, 0))
                  for w in later_weights]
    head_of = lambda i: jnp.minimum(i, NA_HEADS - 1)
    bias_rows_spec = pl.BlockSpec((1,) + bias_rows.shape[1:], lambda i: (head_of(i), 0, 0))
    return pl.pallas_call(
        functools.partial(_in_proj_kernel, n_narrow=len(later_weights), bias_dr=bias_dr),
        grid=(steps,),
        in_specs=[
            tok(D_MODEL),
            _const_spec((1, D_MODEL)),
            _const_spec((D_MODEL, IN_WIDTH)),
            _const_spec((NA_WIDTH, D_MODEL)),
            _const_spec((1, N_BRANCHES * D_MODEL)),
            _const_spec((1, NA_WIDTH)),
            _const_spec((NA_WIDTH, 1)),
            _const_spec((1, MEM_WIDTH)),
            pl.BlockSpec((1, MEM_WIDTH, M), lambda i: (batch_of(i), 0, 0)),
            pl.BlockSpec((1, M, MEM_WIDTH), lambda i: (batch_of(i), 0, 0)),
            bias_rows_spec,
        ] + slab_specs,
        out_specs=[
            tok(NA_WIDTH),
            pl.BlockSpec((1, slabs, NA_WIDTH, V7X_LANES),
                         lambda i: (batch_of(i), i % tiles_per_batch, 0, 0)),
            tok(NA_HEADS * V7X_LANES),
            tok(F_WIDTH),
            tok(MEM_WIDTH),
            tok(N_BRANCHES * D_MODEL),
            pl.BlockSpec((n_cases, 1, NA_PAIR_TOKENS, NA_KEY_TOKENS), lambda i: (0, head_of(i), 0, 0)),
        ] + slab_specs,
        out_shape=[
            jax.ShapeDtypeStruct((T, NA_WIDTH), BF16),
            jax.ShapeDtypeStruct((B, tokens_per_batch // V7X_LANES, NA_WIDTH, V7X_LANES), BF16),
            jax.ShapeDtypeStruct((T, NA_HEADS * V7X_LANES), BF16),
            jax.ShapeDtypeStruct((T, F_WIDTH), BF16),
            jax.ShapeDtypeStruct((T, MEM_WIDTH), BF16),
            jax.ShapeDtypeStruct((T, N_BRANCHES * D_MODEL), BF16),
            jax.ShapeDtypeStruct((n_cases, NA_HEADS, NA_PAIR_TOKENS, NA_KEY_TOKENS), F32),
        ] + [jax.ShapeDtypeStruct(w.shape, BF16) for w in later_weights],
        compiler_params=_params(1),
        name="in_proj",
    )(x2, norm1_g.reshape(1, D_MODEL), w_in_bf16, w_kT,
      b_gate.reshape(1, N_BRANCHES * D_MODEL), qg, kg, mqg, memkT, memv, bias_rows,
      *later_weights)


def _na_bias_blocks(rpb, rows):
    n_pairs = rows // 2
    wr = min(NA_WIN_ROWS, rows)
    reps = np.array([0, 1, 2, n_pairs - 2, n_pairs - 1])
    starts = np.clip(reps - 2, 0, n_pairs - NA_KEY_PAIRS)
    n_dr, n_dc = 2 * NA_WIN_ROWS - 1, 2 * NA_WIN_COLS - 1
    offsets = np.arange(n_dc) - (NA_WIN_COLS - 1)
    place = (np.arange(2 * GRID_W)[None, :] % GRID_W == (offsets % GRID_W)[:, None]).astype(np.float32)
    rows128 = jnp.dot(rpb.astype(F32).reshape(NA_HEADS * n_dr, n_dc), place,
                      precision=lax.Precision.HIGHEST).reshape(NA_HEADS, n_dr, 2 * GRID_W)
    r = (2 * reps[:, None] + np.arange(2)[None, :])[:, :, None]
    kr = (2 * starts[:, None] + np.arange(2 * NA_KEY_PAIRS)[None, :])[:, None, :]
    rs = np.clip(r - wr // 2, 0, rows - wr)
    row_valid = (kr >= rs) & (kr < rs + wr)
    dr = np.where(row_valid, kr - r + (NA_WIN_ROWS - 1), n_dr)
    assert dr.min() >= 0 and dr.max() <= n_dr
    return rows128, dr


def _assemble_na_bias(rows_ref, o_ref, dr):
    n_dr = rows_ref.shape[1]
    c = lax.broadcasted_iota(jnp.int32, (GRID_W, 2 * GRID_W), 0)
    lane = lax.broadcasted_iota(jnp.int32, (GRID_W, 2 * GRID_W), 1)
    kc = lane & (GRID_W - 1)
    cs = jnp.clip(c - NA_WIN_COLS // 2, 0, GRID_W - NA_WIN_COLS)
    col_valid = (kc >= cs) & (kc < cs + NA_WIN_COLS)
    masked = jnp.full((GRID_W, 2 * GRID_W), NEG_INF, F32)
    blocks = []
    for i in range(n_dr):
        row = jnp.broadcast_to(rows_ref[0, i:i + 1, :], (GRID_W, 2 * GRID_W))
        toeplitz = pltpu.roll(row, 0, axis=1, stride=1, stride_axis=0)
        blocks.append(jnp.where(col_valid, toeplitz, masked))
    blocks.append(masked)
    even_row = lane < GRID_W
    for case in range(dr.shape[0]):
        for rr in range(2):
            for j in range(NA_KEY_PAIRS):
                tile = jnp.where(even_row, blocks[int(dr[case, rr, 2 * j])],
                                 blocks[int(dr[case, rr, 2 * j + 1])])
                o_ref[case, 0, rr * GRID_W:(rr + 1) * GRID_W,
                      j * V7X_LANES:(j + 1) * V7X_LANES] = tile


def _na_kernel(q_ref, kT_ref, v_ref, bias_ref, o_ref, *, pairs_per_step, n_pairs):
    lane = lax.broadcasted_iota(jnp.int32, (NA_PAIR_TOKENS, 2 * NA_HEAD_DIM), 1)
    first_head = lane < NA_HEAD_DIM

    def pair_coords(i):
        p = pl.program_id(1) * pairs_per_step + i
        start = jnp.clip(p - 2, 0, n_pairs - NA_KEY_PAIRS)
        case = jnp.where(p < 2, p, jnp.where(p >= n_pairs - 2, p - (n_pairs - NA_BIAS_CASES), 2))
        qrow = pl.multiple_of(i * NA_PAIR_TOKENS, NA_PAIR_TOKENS)
        krow = pl.multiple_of(start * NA_PAIR_TOKENS, NA_PAIR_TOKENS)
        return start, case, qrow, krow

    kzero = jnp.zeros((NA_HEAD_DIM, NA_KEY_TOKENS), BF16)
    vzero = jnp.zeros((NA_KEY_TOKENS, 2 * NA_HEAD_DIM), BF16)

    def scores(coords, hp):
        start, case, qrow, _ = coords
        psl = slice(hp * 2 * NA_HEAD_DIM, (hp + 1) * 2 * NA_HEAD_DIM)
        qp = q_ref[0, pl.ds(qrow, NA_PAIR_TOKENS), psl]
        kslabs = kT_ref[0, pl.ds(start, NA_KEY_PAIRS), psl, :]
        kT = jnp.concatenate([kslabs[j] for j in range(NA_KEY_PAIRS)], axis=1)
        keys = jnp.concatenate([jnp.concatenate([kT[:NA_HEAD_DIM], kzero], axis=1),
                                jnp.concatenate([kzero, kT[NA_HEAD_DIM:]], axis=1)], axis=0)
        bias = jnp.concatenate([bias_ref[case, 2 * hp], bias_ref[case, 2 * hp + 1]], axis=1)
        return _dot(qp, keys) + bias

    def body(it, carry):
        coords = [pair_coords(it * NA_PAIRS_PER_ITER + j) for j in range(NA_PAIRS_PER_ITER)]
        items = [(j, hp) for j in range(NA_PAIRS_PER_ITER) for hp in range(NA_HEADS // 2)]
        pending = [scores(coords[j], hp) for j, hp in items[:NA_SCORES_AHEAD]]
        outs = []
        for n, (j, hp) in enumerate(items):
            s = pending.pop(0)
            if n + NA_SCORES_AHEAD < len(items):
                jn, hn = items[n + NA_SCORES_AHEAD]
                pending.append(scores(coords[jn], hn))
            _, _, qrow, krow = coords[j]
            es, vs = [], []
            for sub in range(2):
                hd = 2 * hp + sub
                sh = s[:, sub * NA_KEY_TOKENS:(sub + 1) * NA_KEY_TOKENS]
                es.append(jnp.exp((sh - jnp.max(sh, axis=-1, keepdims=True)).astype(BF16)))
                vs.append(v_ref[0, pl.ds(krow, NA_KEY_TOKENS), hd * 2 * NA_HEAD_DIM:(hd + 1) * 2 * NA_HEAD_DIM])
            values = jnp.concatenate([jnp.concatenate([vs[0], vzero], axis=1),
                                      jnp.concatenate([vzero, vs[1]], axis=1)], axis=0)
            r = _dot(jnp.concatenate(es, axis=1), values)
            halves = [r[:, sub * 2 * NA_HEAD_DIM:(sub + 1) * 2 * NA_HEAD_DIM] for sub in range(2)]
            halves = [h / pltpu.roll(h, NA_HEAD_DIM, axis=1) for h in halves]
            outs.append(jnp.where(first_head, halves[0], halves[1]))
            if hp == NA_HEADS // 2 - 1:
                o_ref[0, pl.ds(qrow, NA_PAIR_TOKENS), :] = jnp.concatenate(outs, axis=1).astype(BF16)
                outs = []
        return carry

    lax.fori_loop(0, pairs_per_step // NA_PAIRS_PER_ITER, body, 0)


def _na_attention(q, kT4, v, bias, pairs_per_step):
    B, S, _ = q.shape
    n_pairs = S // NA_PAIR_TOKENS
    steps = n_pairs // pairs_per_step
    tq = pairs_per_step * NA_PAIR_TOKENS
    return pl.pallas_call(
        functools.partial(_na_kernel, pairs_per_step=pairs_per_step, n_pairs=n_pairs),
        grid=(B, steps),
        in_specs=[
            pl.BlockSpec((1, tq, NA_WIDTH), lambda b, j: (b, j, 0)),
            pl.BlockSpec((1, n_pairs, NA_WIDTH, V7X_LANES), lambda b, j: (b, 0, 0, 0)),
            pl.BlockSpec((1, S, NA_HEADS * V7X_LANES), lambda b, j: (b, 0, 0)),
            _const_spec((NA_BIAS_CASES, NA_HEADS, NA_PAIR_TOKENS, NA_KEY_TOKENS)),
        ],
        out_specs=pl.BlockSpec((1, tq, NA_WIDTH), lambda b, j: (b, j, 0)),
        out_shape=jax.ShapeDtypeStruct((B, S, NA_WIDTH), BF16),
        compiler_params=_params(2),
        name="na_attn",
    )(q, kT4, v, bias)


DFT_CHUNK = 16


def _dft_tables():
    n, m = GRID_W, DFT_CHUNK
    a = np.arange(n)
    ang1 = 2.0 * np.pi * np.outer(a, a) / n
    w_ri = np.stack([np.cos(ang1), -np.sin(ang1)], axis=1) / np.sqrt(n)
    eye = np.eye(m)
    w_rows = w_ri.reshape(2 * n, n)
    perm = np.einsum('dx,cy->dcyx', eye, eye).reshape(m * m, m * m)
    c = np.arange(n)[:, None, None]
    d = np.arange(n)[None, :, None]
    b = np.arange(n)[None, None, :]
    ang2 = 2.0 * np.pi * ((c + n * d) * b % (n * n)) / (n * n)
    gr, gi = np.cos(ang2) / np.sqrt(n), -np.sin(ang2) / np.sqrt(n)
    w_cols = np.concatenate([np.concatenate([gr, -gi], axis=2),
                             np.concatenate([gi, gr], axis=2)], axis=1)
    ch = np.arange(F_GROUP_DIM)
    ang3 = 2.0 * np.pi * np.outer(ch, ch) / F_GROUP_DIM
    w_chan = np.concatenate([np.cos(ang3), np.sin(ang3)], axis=0) / np.sqrt(F_GROUP_DIM)
    return tuple(jnp.asarray(t, dtype=F32).astype(BF16) for t in (w_rows, w_cols, perm, w_chan))


def _dft_kernel(wr_ref, wc_ref, perm_ref, u_ref, y_ref, z_ref):
    n, m = GRID_W, DFT_CHUNK

    def rows_stage(j, carry):
        col0 = j * m
        x = u_ref[0, :, pl.ds(col0, m), :].reshape(n * m, F_WIDTH)
        xp = [_dot(perm_ref[...], x[g * m * m:(g + 1) * m * m]).astype(BF16) for g in range(n // m)]
        zb = []
        for b in range(m):
            xb = jnp.concatenate([xp[g][b * m:(b + 1) * m] for g in range(n // m)], axis=0)
            zb.append(_dot(wr_ref[...], xb).astype(BF16))
        for t in range(2 * n // m):
            st = jnp.concatenate([zb[b][t * m:(t + 1) * m] for b in range(m)], axis=0)
            z = _dot(perm_ref[...], st).astype(BF16)
            z_ref[t * 8:(t + 1) * 8, :, pl.ds(col0, m), :] = z.reshape(8, 2, m, F_WIDTH)
        return carry

    def cols_stage(j, carry):
        col0 = j * m
        res = []
        for c_lo in range(m):
            c = j * m + c_lo
            data = z_ref[c].reshape(2 * n, F_WIDTH)
            res.append(_dot(wc_ref[c], data).astype(BF16))
        for ri in range(2):
            for g in range(n // m):
                lo = ri * n + g * m
                stacked = jnp.concatenate([r[lo:lo + m] for r in res], axis=0)
                y = _dot(perm_ref[...], stacked).astype(BF16)
                y_ref[ri, 0, g * m:(g + 1) * m, pl.ds(col0, m), :] = y.reshape(m, m, F_WIDTH)
        return carry

    for j in range(n // m):
        rows_stage(j, 0)
    for j in range(n // m):
        cols_stage(j, 0)


def _position_dft(u4, w_rows, w_cols, perm):
    B, n, m = u4.shape[0], GRID_W, DFT_CHUNK
    return pl.pallas_call(
        _dft_kernel,
        grid=(B,),
        in_specs=[
            _const_spec((2 * n, n)),
            _const_spec((n, 2 * n, 2 * n)),
            _const_spec((m * m, m * m)),
            pl.BlockSpec((1, n, n, F_WIDTH), lambda b: (b, 0, 0, 0)),
        ],
        out_specs=pl.BlockSpec((2, 1, n, n, F_WIDTH), lambda b: (0, b, 0, 0, 0)),
        out_shape=jax.ShapeDtypeStruct((2, B, n, n, F_WIDTH), BF16),
        scratch_shapes=[pltpu.VMEM((n, 2, n, F_WIDTH), BF16)],
        compiler_params=_params(1),
        name="position_dft",
    )(w_rows, w_cols, perm, u4)


def _out_ffn_kernel(x_ref, ona_ref, y_ref, om_ref, gate_ref, wch_ref, wna_ref, wf_ref, wmo_ref,
                    wout_ref, g2_ref, w1_ref, w2_ref, o_ref):
    yr, yi = y_ref[0], y_ref[1]
    groups = []
    for g in range(F_GROUPS):
        sl = slice(g * F_GROUP_DIM, (g + 1) * F_GROUP_DIM)
        groups.append(_dot(jnp.concatenate([yr[:, sl], yi[:, sl]], axis=1), wch_ref[...]))
    yf = jnp.concatenate(groups, axis=1).astype(BF16)

    gate = lambda br: gate_ref[:, br * D_MODEL:(br + 1) * D_MODEL].astype(F32)
    merged = gate(0) * _dot(ona_ref[...], wna_ref[...])
    merged = merged + gate(1) * _dot(yf, wf_ref[...])
    merged = merged + gate(2) * _dot(om_ref[...], wmo_ref[...])
    x1 = x_ref[...] + _dot(merged.astype(BF16), wout_ref[...])

    ms = jnp.mean(x1 * x1, axis=-1, keepdims=True)
    h2 = (x1 * lax.rsqrt(ms + EPS) * g2_ref[...]).astype(BF16)
    a = jnp.maximum(_dot(h2, w1_ref[...]), 0.0)
    o_ref[...] = x1 + _dot((a * a).astype(BF16), w2_ref[...])


def _out_ffn(x2, ona, y3, om, gates, w_chan, w_na_o, w_f, w_mem_o, w_out, norm2_g, w_ff1, w_ff2,
             tm):
    T = x2.shape[0]
    tok = lambda w: pl.BlockSpec((tm, w), lambda i: (i, 0))
    return pl.pallas_call(
        _out_ffn_kernel,
        grid=(T // tm,),
        in_specs=[
            tok(D_MODEL),
            tok(NA_WIDTH),
            pl.BlockSpec((2, tm, F_WIDTH), lambda i: (0, i, 0)),
            tok(MEM_WIDTH),
            tok(N_BRANCHES * D_MODEL),
            _const_spec((2 * F_GROUP_DIM, F_GROUP_DIM)),
            _const_spec((NA_WIDTH, D_MODEL)),
            _const_spec((F_WIDTH, D_MODEL)),
            _const_spec((MEM_WIDTH, D_MODEL)),
            _const_spec((D_MODEL, D_MODEL)),
            _const_spec((1, D_MODEL)),
            _const_spec((D_MODEL, D_FF)),
            _const_spec((D_FF, D_MODEL)),
        ],
        out_specs=tok(D_MODEL),
        out_shape=jax.ShapeDtypeStruct((T, D_MODEL), F32),
        compiler_params=_params(1),
        name="out_ffn",
    )(x2, ona, y3, om, gates, w_chan, w_na_o, w_f, w_mem_o, w_out, norm2_g.reshape(1, D_MODEL),
      w_ff1, w_ff2)


def kernel(x, mem, norm1_g, w_in, b_gate, na_q_g, na_k_g, na_rpb, w_na_o, w_f, mem_norm_g,
           w_mem_kv, mem_q_g, mem_k_g, w_mem_o, w_out, norm2_g, w_ff1, w_ff2):
    B, S, _ = x.shape
    T = B * S
    rows = S // GRID_W
    assert rows == GRID_W and S % NA_PAIR_TOKENS == 0
    x2 = x.reshape(T, D_MODEL)
    gmean_mem = _group_mean_matrix(MEM_WIDTH, MEM_HEAD_DIM)
    w_rows, w_cols, perm, w_chan = _dft_tables()

    memkT, memv, w_in_bf16, w_kT = _mem_kv(mem, mem_norm_g, w_mem_kv, gmean_mem, mem_k_g, w_in)
    later_weights = (w_na_o, w_f, w_mem_o, w_out, w_ff1, w_ff2)
    q, kT4, v, fu, om, gates, bias, *narrow = _in_proj(
        x2, norm1_g, w_in_bf16, w_kT, b_gate, na_q_g, na_k_g, mem_q_g, memkT, memv,
        _na_bias_blocks(na_rpb, rows), later_weights, S, tm=IN_PROJ_TOKENS)
    w_na_o, w_f, w_mem_o, w_out, w_ff1, w_ff2 = narrow

    ona = _na_attention(q.reshape(B, S, NA_WIDTH), kT4, v.reshape(B, S, NA_HEADS * V7X_LANES), bias,
                        pairs_per_step=NA_PAIRS_PER_STEP)

    y = _position_dft(fu.reshape(B, GRID_W, GRID_W, F_WIDTH), w_rows, w_cols, perm)
    y3 = y.reshape(2, T, F_WIDTH)

    out = _out_ffn(x2, ona.reshape(T, NA_WIDTH), y3, om, gates, w_chan, w_na_o, w_f, w_mem_o,
                   w_out, norm2_g, w_ff1, w_ff2, tm=OUT_FFN_TOKENS)
    return out.reshape(B, S, D_MODEL)
```

```python
import functools

import numpy as np
import jax
import jax.numpy as jnp
from jax import lax
from jax.experimental import pallas as pl
from jax.experimental.pallas import tpu as pltpu

D_MODEL = 1024
GRID_W = 64
NA_HEADS = 8
NA_HEAD_DIM = 64
NA_WIDTH = NA_HEADS * NA_HEAD_DIM
NA_WIN_ROWS = 8
NA_WIN_COLS = 16
F_GROUPS = 4
F_GROUP_DIM = 128
F_WIDTH = F_GROUPS * F_GROUP_DIM
MEM_HEADS = 4
MEM_HEAD_DIM = 128
MEM_WIDTH = MEM_HEADS * MEM_HEAD_DIM
N_BRANCHES = 3
D_FF = 4 * D_MODEL
IN_Q0 = 0
IN_K0 = IN_Q0 + NA_WIDTH
IN_V0 = IN_K0 + NA_WIDTH
IN_F0 = IN_V0 + NA_WIDTH
IN_M0 = IN_F0 + F_WIDTH
IN_G0 = IN_M0 + MEM_WIDTH
IN_WIDTH = IN_G0 + N_BRANCHES * D_MODEL
EPS = 1e-6
NEG_INF = -1e30

V7X_LANES = 128
V7X_VMEM_LIMIT_BYTES = 60 * 1024 * 1024

NA_PAIR_TOKENS = 2 * GRID_W
NA_KEY_PAIRS = 5
NA_KEY_TOKENS = NA_KEY_PAIRS * NA_PAIR_TOKENS
NA_BIAS_CASES = 5
NA_PAIRS_PER_ITER = 16
NA_SCORES_AHEAD = 2

IN_PROJ_TOKENS = 512
OUT_FFN_TOKENS = 512
NA_PAIRS_PER_STEP = 16

BF16 = jnp.bfloat16
F32 = jnp.float32


def _dot(a, b):
    return jnp.dot(a, b, preferred_element_type=F32)


def _dot_nt(a, b):
    return lax.dot_general(a, b, (((1,), (1,)), ((), ())), preferred_element_type=F32)


def _const_spec(shape):
    return pl.BlockSpec(shape, lambda *_: (0,) * len(shape), pipeline_mode=pl.Buffered(1))


def _params(n_axes):
    return pltpu.CompilerParams(
        dimension_semantics=("arbitrary",) * n_axes,
        vmem_limit_bytes=V7X_VMEM_LIMIT_BYTES,
    )


def _group_mean_matrix(width, group):
    idx = np.arange(width) // group
    return jnp.asarray((idx[:, None] == idx[None, :]).astype(np.float32) / group, dtype=BF16)


PREP_STEPS = 4


def _mem_kv_kernel(mem_ref, g_ref, w_ref, gm_ref, kg_ref, win_ref, kT_ref, v_ref, wb_ref, wkT_ref,
                   *, n_batches):
    slab = win_ref[...]
    wb_ref[...] = slab.astype(BF16)
    wkT_ref[...] = slab[:, IN_K0:IN_K0 + NA_WIDTH].T.astype(BF16)

    @pl.when(pl.program_id(0) < n_batches)
    def _():
        m = mem_ref[0]
        ms = jnp.mean(m * m, axis=-1, keepdims=True)
        mn = (m * lax.rsqrt(ms + EPS) * g_ref[...]).astype(BF16)
        kv = _dot(mn, w_ref[...].astype(BF16))
        k = kv[:, :MEM_WIDTH]
        msk = _dot((k * k).astype(BF16), gm_ref[...])
        kn = k * lax.rsqrt(msk + EPS) * kg_ref[...]
        kT_ref[0] = kn.T.astype(BF16)
        v_ref[0] = kv[:, MEM_WIDTH:].astype(BF16)


def _mem_kv(mem, mem_norm_g, w_mem_kv, gmean_mem, mem_k_g, w_in):
    B, M, _ = mem.shape
    assert B <= PREP_STEPS
    rows = D_MODEL // PREP_STEPS
    batch_of = lambda i: jnp.minimum(i, B - 1)
    return pl.pallas_call(
        functools.partial(_mem_kv_kernel, n_batches=B),
        grid=(PREP_STEPS,),
        in_specs=[
            pl.BlockSpec((1, M, D_MODEL), lambda i: (batch_of(i), 0, 0)),
            _const_spec((1, D_MODEL)),
            _const_spec((D_MODEL, 2 * MEM_WIDTH)),
            _const_spec((MEM_WIDTH, MEM_WIDTH)),
            _const_spec((1, MEM_WIDTH)),
            pl.BlockSpec((rows, IN_WIDTH), lambda i: (i, 0)),
        ],
        out_specs=[
            pl.BlockSpec((1, MEM_WIDTH, M), lambda i: (batch_of(i), 0, 0)),
            pl.BlockSpec((1, M, MEM_WIDTH), lambda i: (batch_of(i), 0, 0)),
            pl.BlockSpec((rows, IN_WIDTH), lambda i: (i, 0)),
            pl.BlockSpec((NA_WIDTH, rows), lambda i: (0, i)),
        ],
        out_shape=[
            jax.ShapeDtypeStruct((B, MEM_WIDTH, M), BF16),
            jax.ShapeDtypeStruct((B, M, MEM_WIDTH), BF16),
            jax.ShapeDtypeStruct((D_MODEL, IN_WIDTH), BF16),
            jax.ShapeDtypeStruct((NA_WIDTH, D_MODEL), BF16),
        ],
        compiler_params=_params(1),
        name="mem_kv",
    )(mem, mem_norm_g.reshape(1, D_MODEL), w_mem_kv, gmean_mem,
      jnp.tile(mem_k_g, MEM_HEADS).reshape(1, MEM_WIDTH), w_in)


def _in_proj_kernel(x_ref, g1_ref, w_ref, wkT_ref, bg_ref,
                    qg_ref, kg_ref, mqg_ref, memkT_ref, memv_ref, brow_ref, *rest,
                    n_narrow, bias_dr):
    wide_refs = rest[:n_narrow]
    q_ref, kT_ref, v_ref, fu_ref, om_ref, gate_ref, bias_ref = rest[n_narrow:n_narrow + 7]

    x = x_ref[...]
    ms = jnp.mean(x * x, axis=-1, keepdims=True)
    h = (x * lax.rsqrt(ms + EPS) * g1_ref[...]).astype(BF16)

    def gate_dot(br):
        sl = slice(br * D_MODEL, (br + 1) * D_MODEL)
        zg = _dot(h, w_ref[:, IN_G0 + br * D_MODEL:IN_G0 + (br + 1) * D_MODEL]) + bg_ref[:, sl]
        gate_ref[:, sl] = jax.nn.sigmoid(zg).astype(BF16)

    zm = _dot(h, w_ref[:, IN_M0:IN_M0 + MEM_WIDTH])
    mem_scores = []
    for hd in range(MEM_HEADS):
        sl = slice(hd * MEM_HEAD_DIM, (hd + 1) * MEM_HEAD_DIM)
        zh = zm[:, sl]
        msm = jnp.mean(zh * zh, axis=-1, keepdims=True)
        qm = (zh * lax.rsqrt(msm + EPS) * mqg_ref[:, sl]).astype(BF16)
        if hd == 0:
            gate_dot(0)
        mem_scores.append(_dot(qm, memkT_ref[0, sl, :]))
    gate_dot(1)

    zq = _dot(h, w_ref[:, IN_Q0:IN_Q0 + NA_WIDTH])
    low_half = lax.broadcasted_iota(jnp.int32, (zq.shape[0], V7X_LANES), 1) < NA_HEAD_DIM
    for hp in range(NA_HEADS // 2):
        sl = slice(hp * V7X_LANES, (hp + 1) * V7X_LANES)
        zz = zq[:, sl] * zq[:, sl]
        ms_lo = jnp.sum(jnp.where(low_half, zz, 0.0), axis=-1, keepdims=True)
        ms_hi = jnp.sum(jnp.where(low_half, 0.0, zz), axis=-1, keepdims=True)
        ms = jnp.where(low_half, ms_lo, ms_hi) * (1.0 / NA_HEAD_DIM)
        q_ref[:, sl] = (zq[:, sl] * lax.rsqrt(ms + EPS) * qg_ref[:, sl]).astype(BF16)

    zkT = _dot_nt(wkT_ref[...], h)
    zk3 = zkT.reshape(NA_HEADS, NA_HEAD_DIM, zkT.shape[1])
    msk = jnp.mean(zk3 * zk3, axis=1, keepdims=True)
    kT = ((zk3 * lax.rsqrt(msk + EPS)).reshape(zkT.shape) * kg_ref[...]).astype(BF16)
    for j in range(kT_ref.shape[1]):
        kT_ref[0, j] = kT[:, j * V7X_LANES:(j + 1) * V7X_LANES]

    zv = _dot(h, w_ref[:, IN_V0:IN_V0 + NA_WIDTH])
    low_half = lax.broadcasted_iota(jnp.int32, (zv.shape[0], V7X_LANES), 1) < NA_HEAD_DIM
    for hp in range(NA_HEADS // 2):
        pair = zv[:, hp * V7X_LANES:(hp + 1) * V7X_LANES]
        v_ref[:, (2 * hp) * V7X_LANES:(2 * hp + 1) * V7X_LANES] = jnp.where(low_half, pair, 1.0).astype(BF16)
        v_ref[:, (2 * hp + 1) * V7X_LANES:(2 * hp + 2) * V7X_LANES] = jnp.where(low_half, 1.0, pair).astype(BF16)
    fu_ref[...] = _dot(h, w_ref[:, IN_F0:IN_F0 + F_WIDTH]).astype(BF16)
    gate_dot(2)

    outs = []
    for hd in range(MEM_HEADS):
        sl = slice(hd * MEM_HEAD_DIM, (hd + 1) * MEM_HEAD_DIM)
        s = mem_scores[hd]
        e = jnp.exp(s - jnp.max(s, axis=-1, keepdims=True))
        l = jnp.sum(e, axis=-1, keepdims=True)
        outs.append(_dot(e.astype(BF16), memv_ref[0, :, sl]) / l)
    om_ref[...] = jnp.concatenate(outs, axis=1).astype(BF16)

    for wide_ref, narrow_ref in zip(wide_refs, rest[n_narrow + 7:]):
        narrow_ref[...] = wide_ref[...].astype(BF16)

    @pl.when(pl.program_id(0) < NA_HEADS)
    def _():
        _assemble_na_bias(brow_ref, bias_ref, bias_dr)


def _in_proj(x2, norm1_g, w_in_bf16, w_kT, b_gate, na_q_g, na_k_g, mem_q_g,
             memkT, memv, bias_blocks, later_weights, tokens_per_batch, tm):
    T = x2.shape[0]
    bias_rows, bias_dr = bias_blocks
    n_cases = bias_dr.shape[0]
    M = memv.shape[1]
    tiles_per_batch = tokens_per_batch // tm
    slabs = tm // V7X_LANES
    B = T // tokens_per_batch
    na_scale = np.float32(1.0 / np.sqrt(NA_HEAD_DIM))
    mem_scale = np.float32(1.0 / np.sqrt(MEM_HEAD_DIM))
    qg = (jnp.tile(na_q_g, NA_HEADS) * na_scale).reshape(1, NA_WIDTH)
    kg = jnp.tile(na_k_g, NA_HEADS).reshape(NA_WIDTH, 1)
    mqg = (jnp.tile(mem_q_g, MEM_HEADS) * mem_scale).reshape(1, MEM_WIDTH)
    tok = lambda w: pl.BlockSpec((tm, w), lambda i: (i, 0))
    batch_of = lambda i: i // tiles_per_batch
    steps = T // tm
    assert steps >= NA_HEADS
    slab_specs = [pl.BlockSpec((w.shape[0] // steps, w.shape[1]), lambda i: (i, 0))
                  for w in later_weights]
    head_of = lambda i: jnp.minimum(i, NA_HEADS - 1)
    bias_rows_spec = pl.BlockSpec((1,) + bias_rows.shape[1:], lambda i: (head_of(i), 0, 0))
    return pl.pallas_call(
        functools.partial(_in_proj_kernel, n_narrow=len(later_weights), bias_dr=bias_dr),
        grid=(steps,),
        in_specs=[
            tok(D_MODEL),
            _const_spec((1, D_MODEL)),
            _const_spec((D_MODEL, IN_WIDTH)),
            _const_spec((NA_WIDTH, D_MODEL)),
            _const_spec((1, N_BRANCHES * D_MODEL)),
            _const_spec((1, NA_WIDTH)),
            _const_spec((NA_WIDTH, 1)),
            _const_spec((1, MEM_WIDTH)),
            pl.BlockSpec((1, MEM_WIDTH, M), lambda i: (batch_of(i), 0, 0)),
            pl.BlockSpec((1, M, MEM_WIDTH), lambda i: (batch_of(i), 0, 0)),
            bias_rows_spec,
        ] + slab_specs,
        out_specs=[
            tok(NA_WIDTH),
            pl.BlockSpec((1, slabs, NA_WIDTH, V7X_LANES),
                         lambda i: (batch_of(i), i % tiles_per_batch, 0, 0)),
            tok(NA_HEADS * V7X_LANES),
            tok(F_WIDTH),
            tok(MEM_WIDTH),
            tok(N_BRANCHES * D_MODEL),
            pl.BlockSpec((n_cases, 1, NA_PAIR_TOKENS, NA_KEY_TOKENS), lambda i: (0, head_of(i), 0, 0)),
        ] + slab_specs,
        out_shape=[
            jax.ShapeDtypeStruct((T, NA_WIDTH), BF16),
            jax.ShapeDtypeStruct((B, tokens_per_batch // V7X_LANES, NA_WIDTH, V7X_LANES), BF16),
            jax.ShapeDtypeStruct((T, NA_HEADS * V7X_LANES), BF16),
            jax.ShapeDtypeStruct((T, F_WIDTH), BF16),
            jax.ShapeDtypeStruct((T, MEM_WIDTH), BF16),
            jax.ShapeDtypeStruct((T, N_BRANCHES * D_MODEL), BF16),
            jax.ShapeDtypeStruct((n_cases, NA_HEADS, NA_PAIR_TOKENS, NA_KEY_TOKENS), F32),
        ] + [jax.ShapeDtypeStruct(w.shape, BF16) for w in later_weights],
        compiler_params=_params(1),
        name="in_proj",
    )(x2, norm1_g.reshape(1, D_MODEL), w_in_bf16, w_kT,
      b_gate.reshape(1, N_BRANCHES * D_MODEL), qg, kg, mqg, memkT, memv, bias_rows,
      *later_weights)


def _na_bias_blocks(rpb, rows):
    n_pairs = rows // 2
    wr = min(NA_WIN_ROWS, rows)
    reps = np.array([0, 1, 2, n_pairs - 2, n_pairs - 1])
    starts = np.clip(reps - 2, 0, n_pairs - NA_KEY_PAIRS)
    n_dr, n_dc = 2 * NA_WIN_ROWS - 1, 2 * NA_WIN_COLS - 1
    offsets = np.arange(n_dc) - (NA_WIN_COLS - 1)
    place = (np.arange(2 * GRID_W)[None, :] % GRID_W == (offsets % GRID_W)[:, None]).astype(np.float32)
    rows128 = jnp.dot(rpb.astype(F32).reshape(NA_HEADS * n_dr, n_dc), place,
                      precision=lax.Precision.HIGHEST).reshape(NA_HEADS, n_dr, 2 * GRID_W)
    r = (2 * reps[:, None] + np.arange(2)[None, :])[:, :, None]
    kr = (2 * starts[:, None] + np.arange(2 * NA_KEY_PAIRS)[None, :])[:, None, :]
    rs = np.clip(r - wr // 2, 0, rows - wr)
    row_valid = (kr >= rs) & (kr < rs + wr)
    dr = np.where(row_valid, kr - r + (NA_WIN_ROWS - 1), n_dr)
    assert dr.min() >= 0 and dr.max() <= n_dr
    return rows128, dr


def _assemble_na_bias(rows_ref, o_ref, dr):
    n_dr = rows_ref.shape[1]
    c = lax.broadcasted_iota(jnp.int32, (GRID_W, 2 * GRID_W), 0)
    lane = lax.broadcasted_iota(jnp.int32, (GRID_W, 2 * GRID_W), 1)
    kc = lane & (GRID_W - 1)
    cs = jnp.clip(c - NA_WIN_COLS // 2, 0, GRID_W - NA_WIN_COLS)
    col_valid = (kc >= cs) & (kc < cs + NA_WIN_COLS)
    masked = jnp.full((GRID_W, 2 * GRID_W), NEG_INF, F32)
    blocks = []
    for i in range(n_dr):
        row = jnp.broadcast_to(rows_ref[0, i:i + 1, :], (GRID_W, 2 * GRID_W))
        toeplitz = pltpu.roll(row, 0, axis=1, stride=1, stride_axis=0)
        blocks.append(jnp.where(col_valid, toeplitz, masked))
    blocks.append(masked)
    even_row = lane < GRID_W
    for case in range(dr.shape[0]):
        for rr in range(2):
            for j in range(NA_KEY_PAIRS):
                tile = jnp.where(even_row, blocks[int(dr[case, rr, 2 * j])],
                                 blocks[int(dr[case, rr, 2 * j + 1])])
                o_ref[case, 0, rr * GRID_W:(rr + 1) * GRID_W,
                      j * V7X_LANES:(j + 1) * V7X_LANES] = tile


def _na_kernel(q_ref, kT_ref, v_ref, bias_ref, o_ref, *, pairs_per_step, n_pairs):
    lane = lax.broadcasted_iota(jnp.int32, (NA_PAIR_TOKENS, 2 * NA_HEAD_DIM), 1)
    first_head = lane < NA_HEAD_DIM

    def pair_coords(i):
        p = pl.program_id(1) * pairs_per_step + i
        start = jnp.clip(p - 2, 0, n_pairs - NA_KEY_PAIRS)
        case = jnp.where(p < 2, p, jnp.where(p >= n_pairs - 2, p - (n_pairs - NA_BIAS_CASES), 2))
        qrow = pl.multiple_of(i * NA_PAIR_TOKENS, NA_PAIR_TOKENS)
        krow = pl.multiple_of(start * NA_PAIR_TOKENS, NA_PAIR_TOKENS)
        return start, case, qrow, krow

    kzero = jnp.zeros((NA_HEAD_DIM, NA_KEY_TOKENS), BF16)
    vzero = jnp.zeros((NA_KEY_TOKENS, 2 * NA_HEAD_DIM), BF16)

    def scores(coords, hp):
        start, case, qrow, _ = coords
        psl = slice(hp * 2 * NA_HEAD_DIM, (hp + 1) * 2 * NA_HEAD_DIM)
        qp = q_ref[0, pl.ds(qrow, NA_PAIR_TOKENS), psl]
        kslabs = kT_ref[0, pl.ds(start, NA_KEY_PAIRS), psl, :]
        kT = jnp.concatenate([kslabs[j] for j in range(NA_KEY_PAIRS)], axis=1)
        keys = jnp.concatenate([jnp.concatenate([kT[:NA_HEAD_DIM], kzero], axis=1),
                                jnp.concatenate([kzero, kT[NA_HEAD_DIM:]], axis=1)], axis=0)
        bias = jnp.concatenate([bias_ref[case, 2 * hp], bias_ref[case, 2 * hp + 1]], axis=1)
        return _dot(qp, keys) + bias

    def body(it, carry):
        coords = [pair_coords(it * NA_PAIRS_PER_ITER + j) for j in range(NA_PAIRS_PER_ITER)]
        items = [(j, hp) for j in range(NA_PAIRS_PER_ITER) for hp in range(NA_HEADS // 2)]
        pending = [scores(coords[j], hp) for j, hp in items[:NA_SCORES_AHEAD]]
        outs = []
        for n, (j, hp) in enumerate(items):
            s = pending.pop(0)
            if n + NA_SCORES_AHEAD < len(items):
                jn, hn = items[n + NA_SCORES_AHEAD]
                pending.append(scores(coords[jn], hn))
            _, _, qrow, krow = coords[j]
            es, vs = [], []
            for sub in range(2):
                hd = 2 * hp + sub
                sh = s[:, sub * NA_KEY_TOKENS:(sub + 1) * NA_KEY_TOKENS]
                es.append(jnp.exp((sh - jnp.max(sh, axis=-1, keepdims=True)).astype(BF16)))
                vs.append(v_ref[0, pl.ds(krow, NA_KEY_TOKENS), hd * 2 * NA_HEAD_DIM:(hd + 1) * 2 * NA_HEAD_DIM])
            values = jnp.concatenate([jnp.concatenate([vs[0], vzero], axis=1),
                                      jnp.concatenate([vzero, vs[1]], axis=1)], axis=0)
            r = _dot(jnp.concatenate(es, axis=1), values)
            halves = [r[:, sub * 2 * NA_HEAD_DIM:(sub + 1) * 2 * NA_HEAD_DIM] for sub in range(2)]
            halves = [h / pltpu.roll(h, NA_HEAD_DIM, axis=1) for h in halves]
            outs.append(jnp.where(first_head, halves[0], halves[1]))
            if hp == NA_HEADS // 2 - 1:
                o_ref[0, pl.ds(qrow, NA_PAIR_TOKENS), :] = jnp.concatenate(outs, axis=1).astype(BF16)
                outs = []
        return carry

    lax.fori_loop(0, pairs_per_step // NA_PAIRS_PER_ITER, body, 0)


def _na_attention(q, kT4, v, bias, pairs_per_step):
    B, S, _ = q.shape
    n_pairs = S // NA_PAIR_TOKENS
    steps = n_pairs // pairs_per_step
    tq = pairs_per_step * NA_PAIR_TOKENS
    return pl.pallas_call(
        functools.partial(_na_kernel, pairs_per_step=pairs_per_step, n_pairs=n_pairs),
        grid=(B, steps),
        in_specs=[
            pl.BlockSpec((1, tq, NA_WIDTH), lambda b, j: (b, j, 0)),
            pl.BlockSpec((1, n_pairs, NA_WIDTH, V7X_LANES), lambda b, j: (b, 0, 0, 0)),
            pl.BlockSpec((1, S, NA_HEADS * V7X_LANES), lambda b, j: (b, 0, 0)),
            _const_spec((NA_BIAS_CASES, NA_HEADS, NA_PAIR_TOKENS, NA_KEY_TOKENS)),
        ],
        out_specs=pl.BlockSpec((1, tq, NA_WIDTH), lambda b, j: (b, j, 0)),
        out_shape=jax.ShapeDtypeStruct((B, S, NA_WIDTH), BF16),
        compiler_params=_params(2),
        name="na_attn",
    )(q, kT4, v, bias)


DFT_CHUNK = 16


def _dft_tables():
    n, m = GRID_W, DFT_CHUNK
    a = np.arange(n)
    ang1 = 2.0 * np.pi * np.outer(a, a) / n
    w_ri = np.stack([np.cos(ang1), -np.sin(ang1)], axis=1) / np.sqrt(n)
    eye = np.eye(m)
    w_rows = w_ri.reshape(2 * n, n)
    perm = np.einsum('dx,cy->dcyx', eye, eye).reshape(m * m, m * m)
    c = np.arange(n)[:, None, None]
    d = np.arange(n)[None, :, None]
    b = np.arange(n)[None, None, :]
    ang2 = 2.0 * np.pi * ((c + n * d) * b % (n * n)) / (n * n)
    gr, gi = np.cos(ang2) / np.sqrt(n), -np.sin(ang2) / np.sqrt(n)
    w_cols = np.concatenate([np.concatenate([gr, -gi], axis=2),
                             np.concatenate([gi, gr], axis=2)], axis=1)
    ch = np.arange(F_GROUP_DIM)
    ang3 = 2.0 * np.pi * np.outer(ch, ch) / F_GROUP_DIM
    w_chan = np.concatenate([np.cos(ang3), np.sin(ang3)], axis=0) / np.sqrt(F_GROUP_DIM)
    return tuple(jnp.asarray(t, dtype=F32).astype(BF16) for t in (w_rows, w_cols, perm, w_chan))


def _dft_kernel(wr_ref, wc_ref, perm_ref, u_ref, y_ref, z_ref):
    n, m = GRID_W, DFT_CHUNK

    def rows_stage(j, carry):
        col0 = j * m
        x = u_ref[0, :, pl.ds(col0, m), :].reshape(n * m, F_WIDTH)
        xp = [_dot(perm_ref[...], x[g * m * m:(g + 1) * m * m]).astype(BF16) for g in range(n // m)]
        zb = []
        for b in range(m):
            xb = jnp.concatenate([xp[g][b * m:(b + 1) * m] for g in range(n // m)], axis=0)
            zb.append(_dot(wr_ref[...], xb).astype(BF16))
        for t in range(2 * n // m):
            st = jnp.concatenate([zb[b][t * m:(t + 1) * m] for b in range(m)], axis=0)
            z = _dot(perm_ref[...], st).astype(BF16)
            z_ref[t * 8:(t + 1) * 8, :, pl.ds(col0, m), :] = z.reshape(8, 2, m, F_WIDTH)
        return carry

    def cols_stage(j, carry):
        col0 = j * m
        res = []
        for c_lo in range(m):
            c = j * m + c_lo
            data = z_ref[c].reshape(2 * n, F_WIDTH)
            res.append(_dot(wc_ref[c], data).astype(BF16))
        for ri in range(2):
            for g in range(n // m):
                lo = ri * n + g * m
                stacked = jnp.concatenate([r[lo:lo + m] for r in res], axis=0)
                y = _dot(perm_ref[...], stacked).astype(BF16)
                y_ref[ri, 0, g * m:(g + 1) * m, pl.ds(col0, m), :] = y.reshape(m, m, F_WIDTH)
        return carry

    for j in range(n // m):
        rows_stage(j, 0)
    for j in range(n // m):
        cols_stage(j, 0)


def _position_dft(u4, w_rows, w_cols, perm):
    B, n, m = u4.shape[0], GRID_W, DFT_CHUNK
    return pl.pallas_call(
        _dft_kernel,
        grid=(B,),
        in_specs=[
            _const_spec((2 * n, n)),
            _const_spec((n, 2 * n, 2 * n)),
            _const_spec((m * m, m * m)),
            pl.BlockSpec((1, n, n, F_WIDTH), lambda b: (b, 0, 0, 0)),
        ],
        out_specs=pl.BlockSpec((2, 1, n, n, F_WIDTH), lambda b: (0, b, 0, 0, 0)),
        out_shape=jax.ShapeDtypeStruct((2, B, n, n, F_WIDTH), BF16),
        scratch_shapes=[pltpu.VMEM((n, 2, n, F_WIDTH), BF16)],
        compiler_params=_params(1),
        name="position_dft",
    )(w_rows, w_cols, perm, u4)


def _out_ffn_kernel(x_ref, ona_ref, y_ref, om_ref, gate_ref, wch_ref, wna_ref, wf_ref, wmo_ref,
                    wout_ref, g2_ref, w1_ref, w2_ref, o_ref):
    yr, yi = y_ref[0], y_ref[1]
    groups = []
    for g in range(F_GROUPS):
        sl = slice(g * F_GROUP_DIM, (g + 1) * F_GROUP_DIM)
        groups.append(_dot(jnp.concatenate([yr[:, sl], yi[:, sl]], axis=1), wch_ref[...]))
    yf = jnp.concatenate(groups, axis=1).astype(BF16)

    gate = lambda br: gate_ref[:, br * D_MODEL:(br + 1) * D_MODEL].astype(F32)
    merged = gate(0) * _dot(ona_ref[...], wna_ref[...])
    merged = merged + gate(1) * _dot(yf, wf_ref[...])
    merged = merged + gate(2) * _dot(om_ref[...], wmo_ref[...])
    x1 = x_ref[...] + _dot(merged.astype(BF16), wout_ref[...])

    ms = jnp.mean(x1 * x1, axis=-1, keepdims=True)
    h2 = (x1 * lax.rsqrt(ms + EPS) * g2_ref[...]).astype(BF16)
    a = jnp.maximum(_dot(h2, w1_ref[...]), 0.0)
    o_ref[...] = x1 + _dot((a * a).astype(BF16), w2_ref[...])


def _out_ffn(x2, ona, y3, om, gates, w_chan, w_na_o, w_f, w_mem_o, w_out, norm2_g, w_ff1, w_ff2,
             tm):
    T = x2.shape[0]
    tok = lambda w: pl.BlockSpec((tm, w), lambda i: (i, 0))
    return pl.pallas_call(
        _out_ffn_kernel,
        grid=(T // tm,),
        in_specs=[
            tok(D_MODEL),
            tok(NA_WIDTH),
            pl.BlockSpec((2, tm, F_WIDTH), lambda i: (0, i, 0)),
            tok(MEM_WIDTH),
            tok(N_BRANCHES * D_MODEL),
            _const_spec((2 * F_GROUP_DIM, F_GROUP_DIM)),
            _const_spec((NA_WIDTH, D_MODEL)),
            _const_spec((F_WIDTH, D_MODEL)),
            _const_spec((MEM_WIDTH, D_MODEL)),
            _const_spec((D_MODEL, D_MODEL)),
            _const_spec((1, D_MODEL)),
            _const_spec((D_MODEL, D_FF)),
            _const_spec((D_FF, D_MODEL)),
        ],
        out_specs=tok(D_MODEL),
        out_shape=jax.ShapeDtypeStruct((T, D_MODEL), F32),
        compiler_params=_params(1),
        name="out_ffn",
    )(x2, ona, y3, om, gates, w_chan, w_na_o, w_f, w_mem_o, w_out, norm2_g.reshape(1, D_MODEL),
      w_ff1, w_ff2)


def kernel(x, mem, norm1_g, w_in, b_gate, na_q_g, na_k_g, na_rpb, w_na_o, w_f, mem_norm_g,
           w_mem_kv, mem_q_g, mem_k_g, w_mem_o, w_out, norm2_g, w_ff1, w_ff2):
    B, S, _ = x.shape
    T = B * S
    rows = S // GRID_W
    assert rows == GRID_W and S % NA_PAIR_TOKENS == 0
    x2 = x.reshape(T, D_MODEL)
    gmean_mem = _group_mean_matrix(MEM_WIDTH, MEM_HEAD_DIM)
    w_rows, w_cols, perm, w_chan = _dft_tables()

    memkT, memv, w_in_bf16, w_kT = _mem_kv(mem, mem_norm_g, w_mem_kv, gmean_mem, mem_k_g, w_in)
    later_weights = (w_na_o, w_f, w_mem_o, w_out, w_ff1, w_ff2)
    q, kT4, v, fu, om, gates, bias, *narrow = _in_proj(
        x2, norm1_g, w_in_bf16, w_kT, b_gate, na_q_g, na_k_g, mem_q_g, memkT, memv,
        _na_bias_blocks(na_rpb, rows), later_weights, S, tm=IN_PROJ_TOKENS)
    w_na_o, w_f, w_mem_o, w_out, w_ff1, w_ff2 = narrow

    ona = _na_attention(q.reshape(B, S, NA_WIDTH), kT4, v.reshape(B, S, NA_HEADS * V7X_LANES), bias,
                        pairs_per_step=NA_PAIRS_PER_STEP)

    y = _position_dft(fu.reshape(B, GRID_W, GRID_W, F_WIDTH), w_rows, w_cols, perm)
    y3 = y.reshape(2, T, F_WIDTH)

    out = _out_ffn(x2, ona.reshape(T, NA_WIDTH), y3, om, gates, w_chan, w_na_o, w_f, w_mem_o,
                   w_out, norm2_g, w_ff1, w_ff2, tm=OUT_FFN_TOKENS)
    return out.reshape(B, S, D_MODEL)
```

```python
import functools

import numpy as np
import jax
import jax.numpy as jnp
from jax import lax
from jax.experimental import pallas as pl
from jax.experimental.pallas import tpu as pltpu

D_MODEL = 1024
GRID_W = 64
NA_HEADS = 8
NA_HEAD_DIM = 64
NA_WIDTH = NA_HEADS * NA_HEAD_DIM
NA_WIN_ROWS = 8
NA_WIN_COLS = 16
F_GROUPS = 4
F_GROUP_DIM = 128
F_WIDTH = F_GROUPS * F_GROUP_DIM
MEM_HEADS = 4
MEM_HEAD_DIM = 128
MEM_WIDTH = MEM_HEADS * MEM_HEAD_DIM
N_BRANCHES = 3
D_FF = 4 * D_MODEL
IN_Q0 = 0
IN_K0 = IN_Q0 + NA_WIDTH
IN_V0 = IN_K0 + NA_WIDTH
IN_F0 = IN_V0 + NA_WIDTH
IN_M0 = IN_F0 + F_WIDTH
IN_G0 = IN_M0 + MEM_WIDTH
IN_WIDTH = IN_G0 + N_BRANCHES * D_MODEL
EPS = 1e-6
NEG_INF = -1e30

V7X_LANES = 128
V7X_VMEM_LIMIT_BYTES = 60 * 1024 * 1024

NA_PAIR_TOKENS = 2 * GRID_W
NA_KEY_PAIRS = 5
NA_KEY_TOKENS = NA_KEY_PAIRS * NA_PAIR_TOKENS
NA_BIAS_CASES = 5
NA_PAIRS_PER_ITER = 16
NA_SCORES_AHEAD = 2

IN_PROJ_TOKENS = 512
OUT_FFN_TOKENS = 512
NA_PAIRS_PER_STEP = 16

BF16 = jnp.bfloat16
F32 = jnp.float32


def _dot(a, b):
    return jnp.dot(a, b, preferred_element_type=F32)


def _dot_nt(a, b):
    return lax.dot_general(a, b, (((1,), (1,)), ((), ())), preferred_element_type=F32)


def _const_spec(shape):
    return pl.BlockSpec(shape, lambda *_: (0,) * len(shape), pipeline_mode=pl.Buffered(1))


def _params(n_axes):
    return pltpu.CompilerParams(
        dimension_semantics=("arbitrary",) * n_axes,
        vmem_limit_bytes=V7X_VMEM_LIMIT_BYTES,
    )


def _group_mean_matrix(width, group):
    idx = np.arange(width) // group
    return jnp.asarray((idx[:, None] == idx[None, :]).astype(np.float32) / group, dtype=BF16)


PREP_STEPS = 4


def _mem_kv_kernel(mem_ref, g_ref, w_ref, gm_ref, kg_ref, win_ref, kT_ref, v_ref, wb_ref, wkT_ref,
                   *, n_batches):
    slab = win_ref[...]
    wb_ref[...] = slab.astype(BF16)
    wkT_ref[...] = slab[:, IN_K0:IN_K0 + NA_WIDTH].T.astype(BF16)

    @pl.when(pl.program_id(0) < n_batches)
    def _():
        m = mem_ref[0]
        ms = jnp.mean(m * m, axis=-1, keepdims=True)
        mn = (m * lax.rsqrt(ms + EPS) * g_ref[...]).astype(BF16)
        kv = _dot(mn, w_ref[...].astype(BF16))
        k = kv[:, :MEM_WIDTH]
        msk = _dot((k * k).astype(BF16), gm_ref[...])
        kn = k * lax.rsqrt(msk + EPS) * kg_ref[...]
        kT_ref[0] = kn.T.astype(BF16)
        v_ref[0] = kv[:, MEM_WIDTH:].astype(BF16)


def _mem_kv(mem, mem_norm_g, w_mem_kv, gmean_mem, mem_k_g, w_in):
    B, M, _ = mem.shape
    assert B <= PREP_STEPS
    rows = D_MODEL // PREP_STEPS
    batch_of = lambda i: jnp.minimum(i, B - 1)
    return pl.pallas_call(
        functools.partial(_mem_kv_kernel, n_batches=B),
        grid=(PREP_STEPS,),
        in_specs=[
            pl.BlockSpec((1, M, D_MODEL), lambda i: (batch_of(i), 0, 0)),
            _const_spec((1, D_MODEL)),
            _const_spec((D_MODEL, 2 * MEM_WIDTH)),
            _const_spec((MEM_WIDTH, MEM_WIDTH)),
            _const_spec((1, MEM_WIDTH)),
            pl.BlockSpec((rows, IN_WIDTH), lambda i: (i, 0)),
        ],
        out_specs=[
            pl.BlockSpec((1, MEM_WIDTH, M), lambda i: (batch_of(i), 0, 0)),
            pl.BlockSpec((1, M, MEM_WIDTH), lambda i: (batch_of(i), 0, 0)),
            pl.BlockSpec((rows, IN_WIDTH), lambda i: (i, 0)),
            pl.BlockSpec((NA_WIDTH, rows), lambda i: (0, i)),
        ],
        out_shape=[
            jax.ShapeDtypeStruct((B, MEM_WIDTH, M), BF16),
            jax.ShapeDtypeStruct((B, M, MEM_WIDTH), BF16),
            jax.ShapeDtypeStruct((D_MODEL, IN_WIDTH), BF16),
            jax.ShapeDtypeStruct((NA_WIDTH, D_MODEL), BF16),
        ],
        compiler_params=_params(1),
        name="mem_kv",
    )(mem, mem_norm_g.reshape(1, D_MODEL), w_mem_kv, gmean_mem,
      jnp.tile(mem_k_g, MEM_HEADS).reshape(1, MEM_WIDTH), w_in)


def _in_proj_kernel(x_ref, g1_ref, w_ref, wkT_ref, bg_ref,
                    qg_ref, kg_ref, mqg_ref, memkT_ref, memv_ref, brow_ref, *rest,
                    n_narrow, bias_dr):
    wide_refs = rest[:n_narrow]
    q_ref, kT_ref, v_ref, fu_ref, om_ref, gate_ref, bias_ref = rest[n_narrow:n_narrow + 7]

    x = x_ref[...]
    ms = jnp.mean(x * x, axis=-1, keepdims=True)
    h = (x * lax.rsqrt(ms + EPS) * g1_ref[...]).astype(BF16)

    def gate_dot(br):
        for half in range(2):
            lo = br * D_MODEL + half * (D_MODEL // 2)
            sl = slice(lo, lo + D_MODEL // 2)
            zg = _dot(h, w_ref[:, IN_G0 + lo:IN_G0 + lo + D_MODEL // 2]) + bg_ref[:, sl]
            gate_ref[:, sl] = jax.nn.sigmoid(zg).astype(BF16)

    zm = _dot(h, w_ref[:, IN_M0:IN_M0 + MEM_WIDTH])
    mem_scores = []
    for hd in range(MEM_HEADS):
        sl = slice(hd * MEM_HEAD_DIM, (hd + 1) * MEM_HEAD_DIM)
        zh = zm[:, sl]
        msm = jnp.mean(zh * zh, axis=-1, keepdims=True)
        qm = (zh * lax.rsqrt(msm + EPS) * mqg_ref[:, sl]).astype(BF16)
        if hd == 0:
            gate_dot(0)
        mem_scores.append(_dot(qm, memkT_ref[0, sl, :]))
    gate_dot(1)

    zq = _dot(h, w_ref[:, IN_Q0:IN_Q0 + NA_WIDTH])
    low_half = lax.broadcasted_iota(jnp.int32, (zq.shape[0], V7X_LANES), 1) < NA_HEAD_DIM
    for hp in range(NA_HEADS // 2):
        sl = slice(hp * V7X_LANES, (hp + 1) * V7X_LANES)
        zz = zq[:, sl] * zq[:, sl]
        ms_lo = jnp.sum(jnp.where(low_half, zz, 0.0), axis=-1, keepdims=True)
        ms_hi = jnp.sum(jnp.where(low_half, 0.0, zz), axis=-1, keepdims=True)
        ms = jnp.where(low_half, ms_lo, ms_hi) * (1.0 / NA_HEAD_DIM)
        q_ref[:, sl] = (zq[:, sl] * lax.rsqrt(ms + EPS) * qg_ref[:, sl]).astype(BF16)

    zkT = _dot_nt(wkT_ref[...], h)
    zk3 = zkT.reshape(NA_HEADS, NA_HEAD_DIM, zkT.shape[1])
    msk = jnp.mean(zk3 * zk3, axis=1, keepdims=True)
    kT = ((zk3 * lax.rsqrt(msk + EPS)).reshape(zkT.shape) * kg_ref[...]).astype(BF16)
    for j in range(kT_ref.shape[1]):
        kT_ref[0, j] = kT[:, j * V7X_LANES:(j + 1) * V7X_LANES]

    zv = _dot(h, w_ref[:, IN_V0:IN_V0 + NA_WIDTH])
    low_half = lax.broadcasted_iota(jnp.int32, (zv.shape[0], V7X_LANES), 1) < NA_HEAD_DIM
    for hp in range(NA_HEADS // 2):
        pair = zv[:, hp * V7X_LANES:(hp + 1) * V7X_LANES]
        v_ref[:, (2 * hp) * V7X_LANES:(2 * hp + 1) * V7X_LANES] = jnp.where(low_half, pair, 1.0).astype(BF16)
        v_ref[:, (2 * hp + 1) * V7X_LANES:(2 * hp + 2) * V7X_LANES] = jnp.where(low_half, 1.0, pair).astype(BF16)
    fu_ref[...] = _dot(h, w_ref[:, IN_F0:IN_F0 + F_WIDTH]).astype(BF16)
    gate_dot(2)

    outs = []
    for hd in range(MEM_HEADS):
        sl = slice(hd * MEM_HEAD_DIM, (hd + 1) * MEM_HEAD_DIM)
        s = mem_scores[hd]
        e = jnp.exp(s - jnp.max(s, axis=-1, keepdims=True))
        l = jnp.sum(e, axis=-1, keepdims=True)
        outs.append(_dot(e.astype(BF16), memv_ref[0, :, sl]) / l)
    om_ref[...] = jnp.concatenate(outs, axis=1).astype(BF16)

    for wide_ref, narrow_ref in zip(wide_refs, rest[n_narrow + 7:]):
        narrow_ref[...] = wide_ref[...].astype(BF16)

    @pl.when(pl.program_id(0) < NA_HEADS)
    def _():
        _assemble_na_bias(brow_ref, bias_ref, bias_dr)


def _in_proj(x2, norm1_g, w_in_bf16, w_kT, b_gate, na_q_g, na_k_g, mem_q_g,
             memkT, memv, bias_blocks, later_weights, tokens_per_batch, tm):
    T = x2.shape[0]
    bias_rows, bias_dr = bias_blocks
    n_cases = bias_dr.shape[0]
    M = memv.shape[1]
    tiles_per_batch = tokens_per_batch // tm
    slabs = tm // V7X_LANES
    B = T // tokens_per_batch
    na_scale = np.float32(1.0 / np.sqrt(NA_HEAD_DIM))
    mem_scale = np.float32(1.0 / np.sqrt(MEM_HEAD_DIM))
    qg = (jnp.tile(na_q_g, NA_HEADS) * na_scale).reshape(1, NA_WIDTH)
    kg = jnp.tile(na_k_g, NA_HEADS).reshape(NA_WIDTH, 1)
    mqg = (jnp.tile(mem_q_g, MEM_HEADS) * mem_scale).reshape(1, MEM_WIDTH)
    tok = lambda w: pl.BlockSpec((tm, w), lambda i: (i, 0))
    batch_of = lambda i: i // tiles_per_batch
    steps = T // tm
    assert steps >= NA_HEADS
    slab_specs = [pl.BlockSpec((w.shape[0] // steps, w.shape[1]), lambda i: (i, 0))
                  for w in later_weights]
    head_of = lambda i: jnp.minimum(i, NA_HEADS - 1)
    bias_rows_spec = pl.BlockSpec((1,) + bias_rows.shape[1:], lambda i: (head_of(i), 0, 0))
    return pl.pallas_call(
        functools.partial(_in_proj_kernel, n_narrow=len(later_weights), bias_dr=bias_dr),
        grid=(steps,),
        in_specs=[
            tok(D_MODEL),
            _const_spec((1, D_MODEL)),
            _const_spec((D_MODEL, IN_WIDTH)),
            _const_spec((NA_WIDTH, D_MODEL)),
            _const_spec((1, N_BRANCHES * D_MODEL)),
            _const_spec((1, NA_WIDTH)),
            _const_spec((NA_WIDTH, 1)),
            _const_spec((1, MEM_WIDTH)),
            pl.BlockSpec((1, MEM_WIDTH, M), lambda i: (batch_of(i), 0, 0)),
            pl.BlockSpec((1, M, MEM_WIDTH), lambda i: (batch_of(i), 0, 0)),
            bias_rows_spec,
        ] + slab_specs,
        out_specs=[
            tok(NA_WIDTH),
            pl.BlockSpec((1, slabs, NA_WIDTH, V7X_LANES),
                         lambda i: (batch_of(i), i % tiles_per_batch, 0, 0)),
            tok(NA_HEADS * V7X_LANES),
            tok(F_WIDTH),
            tok(MEM_WIDTH),
            tok(N_BRANCHES * D_MODEL),
            pl.BlockSpec((n_cases, 1, NA_PAIR_TOKENS, NA_KEY_TOKENS), lambda i: (0, head_of(i), 0, 0)),
        ] + slab_specs,
        out_shape=[
            jax.ShapeDtypeStruct((T, NA_WIDTH), BF16),
            jax.ShapeDtypeStruct((B, tokens_per_batch // V7X_LANES, NA_WIDTH, V7X_LANES), BF16),
            jax.ShapeDtypeStruct((T, NA_HEADS * V7X_LANES), BF16),
            jax.ShapeDtypeStruct((T, F_WIDTH), BF16),
            jax.ShapeDtypeStruct((T, MEM_WIDTH), BF16),
            jax.ShapeDtypeStruct((T, N_BRANCHES * D_MODEL), BF16),
            jax.ShapeDtypeStruct((n_cases, NA_HEADS, NA_PAIR_TOKENS, NA_KEY_TOKENS), F32),
        ] + [jax.ShapeDtypeStruct(w.shape, BF16) for w in later_weights],
        compiler_params=_params(1),
        name="in_proj",
    )(x2, norm1_g.reshape(1, D_MODEL), w_in_bf16, w_kT,
      b_gate.reshape(1, N_BRANCHES * D_MODEL), qg, kg, mqg, memkT, memv, bias_rows,
      *later_weights)


def _na_bias_blocks(rpb, rows):
    n_pairs = rows // 2
    wr = min(NA_WIN_ROWS, rows)
    reps = np.array([0, 1, 2, n_pairs - 2, n_pairs - 1])
    starts = np.clip(reps - 2, 0, n_pairs - NA_KEY_PAIRS)
    n_dr, n_dc = 2 * NA_WIN_ROWS - 1, 2 * NA_WIN_COLS - 1
    offsets = np.arange(n_dc) - (NA_WIN_COLS - 1)
    place = (np.arange(2 * GRID_W)[None, :] % GRID_W == (offsets % GRID_W)[:, None]).astype(np.float32)
    rows128 = jnp.dot(rpb.astype(F32).reshape(NA_HEADS * n_dr, n_dc), place,
                      precision=lax.Precision.HIGHEST).reshape(NA_HEADS, n_dr, 2 * GRID_W)
    r = (2 * reps[:, None] + np.arange(2)[None, :])[:, :, None]
    kr = (2 * starts[:, None] + np.arange(2 * NA_KEY_PAIRS)[None, :])[:, None, :]
    rs = np.clip(r - wr // 2, 0, rows - wr)
    row_valid = (kr >= rs) & (kr < rs + wr)
    dr = np.where(row_valid, kr - r + (NA_WIN_ROWS - 1), n_dr)
    assert dr.min() >= 0 and dr.max() <= n_dr
    return rows128, dr


def _assemble_na_bias(rows_ref, o_ref, dr):
    n_dr = rows_ref.shape[1]
    c = lax.broadcasted_iota(jnp.int32, (GRID_W, 2 * GRID_W), 0)
    lane = lax.broadcasted_iota(jnp.int32, (GRID_W, 2 * GRID_W), 1)
    kc = lane & (GRID_W - 1)
    cs = jnp.clip(c - NA_WIN_COLS // 2, 0, GRID_W - NA_WIN_COLS)
    col_valid = (kc >= cs) & (kc < cs + NA_WIN_COLS)
    masked = jnp.full((GRID_W, 2 * GRID_W), NEG_INF, F32)
    blocks = []
    for i in range(n_dr):
        row = jnp.broadcast_to(rows_ref[0, i:i + 1, :], (GRID_W, 2 * GRID_W))
        toeplitz = pltpu.roll(row, 0, axis=1, stride=1, stride_axis=0)
        blocks.append(jnp.where(col_valid, toeplitz, masked))
    blocks.append(masked)
    even_row = lane < GRID_W
    for case in range(dr.shape[0]):
        for rr in range(2):
            for j in range(NA_KEY_PAIRS):
                tile = jnp.where(even_row, blocks[int(dr[case, rr, 2 * j])],
                                 blocks[int(dr[case, rr, 2 * j + 1])])
                o_ref[case, 0, rr * GRID_W:(rr + 1) * GRID_W,
                      j * V7X_LANES:(j + 1) * V7X_LANES] = tile


def _na_kernel(q_ref, kT_ref, v_ref, bias_ref, o_ref, *, pairs_per_step, n_pairs):
    lane = lax.broadcasted_iota(jnp.int32, (NA_PAIR_TOKENS, 2 * NA_HEAD_DIM), 1)
    first_head = lane < NA_HEAD_DIM

    def pair_coords(i):
        p = pl.program_id(1) * pairs_per_step + i
        start = jnp.clip(p - 2, 0, n_pairs - NA_KEY_PAIRS)
        case = jnp.where(p < 2, p, jnp.where(p >= n_pairs - 2, p - (n_pairs - NA_BIAS_CASES), 2))
        qrow = pl.multiple_of(i * NA_PAIR_TOKENS, NA_PAIR_TOKENS)
        krow = pl.multiple_of(start * NA_PAIR_TOKENS, NA_PAIR_TOKENS)
        return start, case, qrow, krow

    kzero = jnp.zeros((NA_HEAD_DIM, NA_KEY_TOKENS), BF16)
    vzero = jnp.zeros((NA_KEY_TOKENS, 2 * NA_HEAD_DIM), BF16)

    def scores(coords, hp):
        start, case, qrow, _ = coords
        psl = slice(hp * 2 * NA_HEAD_DIM, (hp + 1) * 2 * NA_HEAD_DIM)
        qp = q_ref[0, pl.ds(qrow, NA_PAIR_TOKENS), psl]
        kslabs = kT_ref[0, pl.ds(start, NA_KEY_PAIRS), psl, :]
        kT = jnp.concatenate([kslabs[j] for j in range(NA_KEY_PAIRS)], axis=1)
        keys = jnp.concatenate([jnp.concatenate([kT[:NA_HEAD_DIM], kzero], axis=1),
                                jnp.concatenate([kzero, kT[NA_HEAD_DIM:]], axis=1)], axis=0)
        bias = jnp.concatenate([bias_ref[case, 2 * hp], bias_ref[case, 2 * hp + 1]], axis=1)
        return _dot(qp, keys) + bias

    def body(it, carry):
        coords = [pair_coords(it * NA_PAIRS_PER_ITER + j) for j in range(NA_PAIRS_PER_ITER)]
        items = [(j, hp) for j in range(NA_PAIRS_PER_ITER) for hp in range(NA_HEADS // 2)]
        pending = [scores(coords[j], hp) for j, hp in items[:NA_SCORES_AHEAD]]
        outs = []
        for n, (j, hp) in enumerate(items):
            s = pending.pop(0)
            if n + NA_SCORES_AHEAD < len(items):
                jn, hn = items[n + NA_SCORES_AHEAD]
                pending.append(scores(coords[jn], hn))
            _, _, qrow, krow = coords[j]
            es, vs = [], []
            for sub in range(2):
                hd = 2 * hp + sub
                sh = s[:, sub * NA_KEY_TOKENS:(sub + 1) * NA_KEY_TOKENS]
                es.append(jnp.exp((sh - jnp.max(sh, axis=-1, keepdims=True)).astype(BF16)))
                vs.append(v_ref[0, pl.ds(krow, NA_KEY_TOKENS), hd * 2 * NA_HEAD_DIM:(hd + 1) * 2 * NA_HEAD_DIM])
            values = jnp.concatenate([jnp.concatenate([vs[0], vzero], axis=1),
                                      jnp.concatenate([vzero, vs[1]], axis=1)], axis=0)
            r = _dot(jnp.concatenate(es, axis=1), values)
            halves = [r[:, sub * 2 * NA_HEAD_DIM:(sub + 1) * 2 * NA_HEAD_DIM] for sub in range(2)]
            halves = [h / pltpu.roll(h, NA_HEAD_DIM, axis=1) for h in halves]
            outs.append(jnp.where(first_head, halves[0], halves[1]))
            if hp == NA_HEADS // 2 - 1:
                o_ref[0, pl.ds(qrow, NA_PAIR_TOKENS), :] = jnp.concatenate(outs, axis=1).astype(BF16)
                outs = []
        return carry

    lax.fori_loop(0, pairs_per_step // NA_PAIRS_PER_ITER, body, 0)


def _na_attention(q, kT4, v, bias, pairs_per_step):
    B, S, _ = q.shape
    n_pairs = S // NA_PAIR_TOKENS
    steps = n_pairs // pairs_per_step
    tq = pairs_per_step * NA_PAIR_TOKENS
    return pl.pallas_call(
        functools.partial(_na_kernel, pairs_per_step=pairs_per_step, n_pairs=n_pairs),
        grid=(B, steps),
        in_specs=[
            pl.BlockSpec((1, tq, NA_WIDTH), lambda b, j: (b, j, 0)),
            pl.BlockSpec((1, n_pairs, NA_WIDTH, V7X_LANES), lambda b, j: (b, 0, 0, 0)),
            pl.BlockSpec((1, S, NA_HEADS * V7X_LANES), lambda b, j: (b, 0, 0)),
            _const_spec((NA_BIAS_CASES, NA_HEADS, NA_PAIR_TOKENS, NA_KEY_TOKENS)),
        ],
        out_specs=pl.BlockSpec((1, tq, NA_WIDTH), lambda b, j: (b, j, 0)),
        out_shape=jax.ShapeDtypeStruct((B, S, NA_WIDTH), BF16),
        compiler_params=_params(2),
        name="na_attn",
    )(q, kT4, v, bias)


DFT_CHUNK = 16


def _dft_tables():
    n, m = GRID_W, DFT_CHUNK
    a = np.arange(n)
    ang1 = 2.0 * np.pi * np.outer(a, a) / n
    w_ri = np.stack([np.cos(ang1), -np.sin(ang1)], axis=1) / np.sqrt(n)
    eye = np.eye(m)
    w_rows = w_ri.reshape(2 * n, n)
    perm = np.einsum('dx,cy->dcyx', eye, eye).reshape(m * m, m * m)
    c = np.arange(n)[:, None, None]
    d = np.arange(n)[None, :, None]
    b = np.arange(n)[None, None, :]
    ang2 = 2.0 * np.pi * ((c + n * d) * b % (n * n)) / (n * n)
    gr, gi = np.cos(ang2) / np.sqrt(n), -np.sin(ang2) / np.sqrt(n)
    w_cols = np.concatenate([np.concatenate([gr, -gi], axis=2),
                             np.concatenate([gi, gr], axis=2)], axis=1)
    ch = np.arange(F_GROUP_DIM)
    ang3 = 2.0 * np.pi * np.outer(ch, ch) / F_GROUP_DIM
    w_chan = np.concatenate([np.cos(ang3), np.sin(ang3)], axis=0) / np.sqrt(F_GROUP_DIM)
    return tuple(jnp.asarray(t, dtype=F32).astype(BF16) for t in (w_rows, w_cols, perm, w_chan))


def _dft_kernel(wr_ref, wc_ref, perm_ref, u_ref, y_ref, z_ref):
    n, m = GRID_W, DFT_CHUNK

    def rows_stage(j, carry):
        col0 = j * m
        x = u_ref[0, :, pl.ds(col0, m), :].reshape(n * m, F_WIDTH)
        xp = [_dot(perm_ref[...], x[g * m * m:(g + 1) * m * m]).astype(BF16) for g in range(n // m)]
        zb = []
        for b in range(m):
            xb = jnp.concatenate([xp[g][b * m:(b + 1) * m] for g in range(n // m)], axis=0)
            zb.append(_dot(wr_ref[...], xb).astype(BF16))
        for t in range(2 * n // m):
            st = jnp.concatenate([zb[b][t * m:(t + 1) * m] for b in range(m)], axis=0)
            z = _dot(perm_ref[...], st).astype(BF16)
            z_ref[t * 8:(t + 1) * 8, :, pl.ds(col0, m), :] = z.reshape(8, 2, m, F_WIDTH)
        return carry

    def cols_stage(j, carry):
        col0 = j * m
        res = []
        for c_lo in range(m):
            c = j * m + c_lo
            data = z_ref[c].reshape(2 * n, F_WIDTH)
            res.append(_dot(wc_ref[c], data).astype(BF16))
        for ri in range(2):
            for g in range(n // m):
                lo = ri * n + g * m
                stacked = jnp.concatenate([r[lo:lo + m] for r in res], axis=0)
                y = _dot(perm_ref[...], stacked).astype(BF16)
                y_ref[ri, 0, g * m:(g + 1) * m, pl.ds(col0, m), :] = y.reshape(m, m, F_WIDTH)
        return carry

    for j in range(n // m):
        rows_stage(j, 0)
    for j in range(n // m):
        cols_stage(j, 0)


def _position_dft(u4, w_rows, w_cols, perm):
    B, n, m = u4.shape[0], GRID_W, DFT_CHUNK
    return pl.pallas_call(
        _dft_kernel,
        grid=(B,),
        in_specs=[
            _const_spec((2 * n, n)),
            _const_spec((n, 2 * n, 2 * n)),
            _const_spec((m * m, m * m)),
            pl.BlockSpec((1, n, n, F_WIDTH), lambda b: (b, 0, 0, 0)),
        ],
        out_specs=pl.BlockSpec((2, 1, n, n, F_WIDTH), lambda b: (0, b, 0, 0, 0)),
        out_shape=jax.ShapeDtypeStruct((2, B, n, n, F_WIDTH), BF16),
        scratch_shapes=[pltpu.VMEM((n, 2, n, F_WIDTH), BF16)],
        compiler_params=_params(1),
        name="position_dft",
    )(w_rows, w_cols, perm, u4)


def _out_ffn_kernel(x_ref, ona_ref, y_ref, om_ref, gate_ref, wch_ref, wna_ref, wf_ref, wmo_ref,
                    wout_ref, g2_ref, w1_ref, w2_ref, o_ref):
    yr, yi = y_ref[0], y_ref[1]
    groups = []
    for g in range(F_GROUPS):
        sl = slice(g * F_GROUP_DIM, (g + 1) * F_GROUP_DIM)
        groups.append(_dot(jnp.concatenate([yr[:, sl], yi[:, sl]], axis=1), wch_ref[...]))
    yf = jnp.concatenate(groups, axis=1).astype(BF16)

    gate = lambda br: gate_ref[:, br * D_MODEL:(br + 1) * D_MODEL].astype(F32)
    merged = gate(0) * _dot(ona_ref[...], wna_ref[...])
    merged = merged + gate(1) * _dot(yf, wf_ref[...])
    merged = merged + gate(2) * _dot(om_ref[...], wmo_ref[...])
    x1 = x_ref[...] + _dot(merged.astype(BF16), wout_ref[...])

    ms = jnp.mean(x1 * x1, axis=-1, keepdims=True)
    h2 = (x1 * lax.rsqrt(ms + EPS) * g2_ref[...]).astype(BF16)
    a = jnp.maximum(_dot(h2, w1_ref[...]), 0.0)
    o_ref[...] = x1 + _dot((a * a).astype(BF16), w2_ref[...])


def _out_ffn(x2, ona, y3, om, gates, w_chan, w_na_o, w_f, w_mem_o, w_out, norm2_g, w_ff1, w_ff2,
             tm):
    T = x2.shape[0]
    tok = lambda w: pl.BlockSpec((tm, w), lambda i: (i, 0))
    return pl.pallas_call(
        _out_ffn_kernel,
        grid=(T // tm,),
        in_specs=[
            tok(D_MODEL),
            tok(NA_WIDTH),
            pl.BlockSpec((2, tm, F_WIDTH), lambda i: (0, i, 0)),
            tok(MEM_WIDTH),
            tok(N_BRANCHES * D_MODEL),
            _const_spec((2 * F_GROUP_DIM, F_GROUP_DIM)),
            _const_spec((NA_WIDTH, D_MODEL)),
            _const_spec((F_WIDTH, D_MODEL)),
            _const_spec((MEM_WIDTH, D_MODEL)),
            _const_spec((D_MODEL, D_MODEL)),
            _const_spec((1, D_MODEL)),
            _const_spec((D_MODEL, D_FF)),
            _const_spec((D_FF, D_MODEL)),
        ],
        out_specs=tok(D_MODEL),
        out_shape=jax.ShapeDtypeStruct((T, D_MODEL), F32),
        compiler_params=_params(1),
        name="out_ffn",
    )(x2, ona, y3, om, gates, w_chan, w_na_o, w_f, w_mem_o, w_out, norm2_g.reshape(1, D_MODEL),
      w_ff1, w_ff2)


def kernel(x, mem, norm1_g, w_in, b_gate, na_q_g, na_k_g, na_rpb, w_na_o, w_f, mem_norm_g,
           w_mem_kv, mem_q_g, mem_k_g, w_mem_o, w_out, norm2_g, w_ff1, w_ff2):
    B, S, _ = x.shape
    T = B * S
    rows = S // GRID_W
    assert rows == GRID_W and S % NA_PAIR_TOKENS == 0
    x2 = x.reshape(T, D_MODEL)
    gmean_mem = _group_mean_matrix(MEM_WIDTH, MEM_HEAD_DIM)
    w_rows, w_cols, perm, w_chan = _dft_tables()

    memkT, memv, w_in_bf16, w_kT = _mem_kv(mem, mem_norm_g, w_mem_kv, gmean_mem, mem_k_g, w_in)
    later_weights = (w_na_o, w_f, w_mem_o, w_out, w_ff1, w_ff2)
    q, kT4, v, fu, om, gates, bias, *narrow = _in_proj(
        x2, norm1_g, w_in_bf16, w_kT, b_gate, na_q_g, na_k_g, mem_q_g, memkT, memv,
        _na_bias_blocks(na_rpb, rows), later_weights, S, tm=IN_PROJ_TOKENS)
    w_na_o, w_f, w_mem_o, w_out, w_ff1, w_ff2 = narrow

    ona = _na_attention(q.reshape(B, S, NA_WIDTH), kT4, v.reshape(B, S, NA_HEADS * V7X_LANES), bias,
                        pairs_per_step=NA_PAIRS_PER_STEP)

    y = _position_dft(fu.reshape(B, GRID_W, GRID_W, F_WIDTH), w_rows, w_cols, perm)
    y3 = y.reshape(2, T, F_WIDTH)

    out = _out_ffn(x2, ona.reshape(T, NA_WIDTH), y3, om, gates, w_chan, w_na_o, w_f, w_mem_o,
                   w_out, norm2_g, w_ff1, w_ff2, tm=OUT_FFN_TOKENS)
    return out.reshape(B, S, D_MODEL)
```

```python
import functools

import numpy as np
import jax
import jax.numpy as jnp
from jax import lax
from jax.experimental import pallas as pl
from jax.experimental.pallas import tpu as pltpu

D_MODEL = 1024
GRID_W = 64
NA_HEADS = 8
NA_HEAD_DIM = 64
NA_WIDTH = NA_HEADS * NA_HEAD_DIM
NA_WIN_ROWS = 8
NA_WIN_COLS = 16
F_GROUPS = 4
F_GROUP_DIM = 128
F_WIDTH = F_GROUPS * F_GROUP_DIM
MEM_HEADS = 4
MEM_HEAD_DIM = 128
MEM_WIDTH = MEM_HEADS * MEM_HEAD_DIM
N_BRANCHES = 3
D_FF = 4 * D_MODEL
IN_Q0 = 0
IN_K0 = IN_Q0 + NA_WIDTH
IN_V0 = IN_K0 + NA_WIDTH
IN_F0 = IN_V0 + NA_WIDTH
IN_M0 = IN_F0 + F_WIDTH
IN_G0 = IN_M0 + MEM_WIDTH
IN_WIDTH = IN_G0 + N_BRANCHES * D_MODEL
EPS = 1e-6
NEG_INF = -1e30

V7X_LANES = 128
V7X_VMEM_LIMIT_BYTES = 60 * 1024 * 1024

NA_PAIR_TOKENS = 2 * GRID_W
NA_KEY_PAIRS = 5
NA_KEY_TOKENS = NA_KEY_PAIRS * NA_PAIR_TOKENS
NA_BIAS_CASES = 5
NA_PAIRS_PER_ITER = 16
NA_SCORES_AHEAD = 2

IN_PROJ_TOKENS = 512
OUT_FFN_TOKENS = 512
NA_PAIRS_PER_STEP = 16

BF16 = jnp.bfloat16
F32 = jnp.float32


def _dot(a, b):
    return jnp.dot(a, b, preferred_element_type=F32)


def _dot_nt(a, b):
    return lax.dot_general(a, b, (((1,), (1,)), ((), ())), preferred_element_type=F32)


def _const_spec(shape):
    return pl.BlockSpec(shape, lambda *_: (0,) * len(shape), pipeline_mode=pl.Buffered(1))


def _params(n_axes):
    return pltpu.CompilerParams(
        dimension_semantics=("arbitrary",) * n_axes,
        vmem_limit_bytes=V7X_VMEM_LIMIT_BYTES,
    )


def _group_mean_matrix(width, group):
    idx = np.arange(width) // group
    return jnp.asarray((idx[:, None] == idx[None, :]).astype(np.float32) / group, dtype=BF16)


PREP_STEPS = 4


def _mem_kv_kernel(mem_ref, g_ref, w_ref, gm_ref, kg_ref, win_ref, kT_ref, v_ref, wb_ref, wkT_ref,
                   *, n_batches):
    slab = win_ref[...]
    wb_ref[...] = slab.astype(BF16)
    wkT_ref[...] = slab[:, IN_K0:IN_K0 + NA_WIDTH].T.astype(BF16)

    @pl.when(pl.program_id(0) < n_batches)
    def _():
        m = mem_ref[0]
        ms = jnp.mean(m * m, axis=-1, keepdims=True)
        mn = (m * lax.rsqrt(ms + EPS) * g_ref[...]).astype(BF16)
        kv = _dot(mn, w_ref[...].astype(BF16))
        k = kv[:, :MEM_WIDTH]
        msk = _dot((k * k).astype(BF16), gm_ref[...])
        kn = k * lax.rsqrt(msk + EPS) * kg_ref[...]
        kT_ref[0] = kn.T.astype(BF16)
        v_ref[0] = kv[:, MEM_WIDTH:].astype(BF16)


def _mem_kv(mem, mem_norm_g, w_mem_kv, gmean_mem, mem_k_g, w_in):
    B, M, _ = mem.shape
    assert B <= PREP_STEPS
    rows = D_MODEL // PREP_STEPS
    batch_of = lambda i: jnp.minimum(i, B - 1)
    return pl.pallas_call(
        functools.partial(_mem_kv_kernel, n_batches=B),
        grid=(PREP_STEPS,),
        in_specs=[
            pl.BlockSpec((1, M, D_MODEL), lambda i: (batch_of(i), 0, 0)),
            _const_spec((1, D_MODEL)),
            _const_spec((D_MODEL, 2 * MEM_WIDTH)),
            _const_spec((MEM_WIDTH, MEM_WIDTH)),
            _const_spec((1, MEM_WIDTH)),
            pl.BlockSpec((rows, IN_WIDTH), lambda i: (i, 0)),
        ],
        out_specs=[
            pl.BlockSpec((1, MEM_WIDTH, M), lambda i: (batch_of(i), 0, 0)),
            pl.BlockSpec((1, M, MEM_WIDTH), lambda i: (batch_of(i), 0, 0)),
            pl.BlockSpec((rows, IN_WIDTH), lambda i: (i, 0)),
            pl.BlockSpec((NA_WIDTH, rows), lambda i: (0, i)),
        ],
        out_shape=[
            jax.ShapeDtypeStruct((B, MEM_WIDTH, M), BF16),
            jax.ShapeDtypeStruct((B, M, MEM_WIDTH), BF16),
            jax.ShapeDtypeStruct((D_MODEL, IN_WIDTH), BF16),
            jax.ShapeDtypeStruct((NA_WIDTH, D_MODEL), BF16),
        ],
        compiler_params=_params(1),
        name="mem_kv",
    )(mem, mem_norm_g.reshape(1, D_MODEL), w_mem_kv, gmean_mem,
      jnp.tile(mem_k_g, MEM_HEADS).reshape(1, MEM_WIDTH), w_in)


def _in_proj_kernel(x_ref, g1_ref, w_ref, wkT_ref, bg_ref,
                    qg_ref, kg_ref, mqg_ref, memkT_ref, memv_ref, brow_ref, *rest,
                    n_narrow, bias_dr):
    wide_refs = rest[:n_narrow]
    q_ref, kT_ref, v_ref, fu_ref, om_ref, gate_ref, bias_ref = rest[n_narrow:n_narrow + 7]

    x = x_ref[...]
    ms = jnp.mean(x * x, axis=-1, keepdims=True)
    h = (x * lax.rsqrt(ms + EPS) * g1_ref[...]).astype(BF16)

    def gate_dot(br):
        sl = slice(br * D_MODEL, (br + 1) * D_MODEL)
        zg = _dot(h, w_ref[:, IN_G0 + br * D_MODEL:IN_G0 + (br + 1) * D_MODEL]) + bg_ref[:, sl]
        gate_ref[:, sl] = jax.nn.sigmoid(zg).astype(BF16)

    zm = _dot(h, w_ref[:, IN_M0:IN_M0 + MEM_WIDTH])
    mem_scores = []
    for hd in range(MEM_HEADS):
        sl = slice(hd * MEM_HEAD_DIM, (hd + 1) * MEM_HEAD_DIM)
        zh = zm[:, sl]
        msm = jnp.mean(zh * zh, axis=-1, keepdims=True)
        qm = (zh * lax.rsqrt(msm + EPS) * mqg_ref[:, sl]).astype(BF16)
        if hd == 0:
            gate_dot(0)
        mem_scores.append(_dot(qm, memkT_ref[0, sl, :]))
    gate_dot(1)

    zq = _dot(h, w_ref[:, IN_Q0:IN_Q0 + NA_WIDTH])
    low_half = lax.broadcasted_iota(jnp.int32, (zq.shape[0], V7X_LANES), 1) < NA_HEAD_DIM
    for hp in range(NA_HEADS // 2):
        sl = slice(hp * V7X_LANES, (hp + 1) * V7X_LANES)
        zz = zq[:, sl] * zq[:, sl]
        ms_lo = jnp.sum(jnp.where(low_half, zz, 0.0), axis=-1, keepdims=True)
        ms_hi = jnp.sum(jnp.where(low_half, 0.0, zz), axis=-1, keepdims=True)
        ms = jnp.where(low_half, ms_lo, ms_hi) * (1.0 / NA_HEAD_DIM)
        q_ref[:, sl] = (zq[:, sl] * lax.rsqrt(ms + EPS) * qg_ref[:, sl]).astype(BF16)

    zkT = _dot_nt(wkT_ref[...], h)
    zk3 = zkT.reshape(NA_HEADS, NA_HEAD_DIM, zkT.shape[1])
    msk = jnp.mean(zk3 * zk3, axis=1, keepdims=True)
    kT = ((zk3 * lax.rsqrt(msk + EPS)).reshape(zkT.shape) * kg_ref[...]).astype(BF16)
    for j in range(kT_ref.shape[1]):
        kT_ref[0, j] = kT[:, j * V7X_LANES:(j + 1) * V7X_LANES]

    zv = _dot(h, w_ref[:, IN_V0:IN_V0 + NA_WIDTH])
    low_half = lax.broadcasted_iota(jnp.int32, (zv.shape[0], V7X_LANES), 1) < NA_HEAD_DIM
    for hp in range(NA_HEADS // 2):
        pair = zv[:, hp * V7X_LANES:(hp + 1) * V7X_LANES]
        v_ref[:, (2 * hp) * V7X_LANES:(2 * hp + 1) * V7X_LANES] = jnp.where(low_half, pair, 1.0).astype(BF16)
        v_ref[:, (2 * hp + 1) * V7X_LANES:(2 * hp + 2) * V7X_LANES] = jnp.where(low_half, 1.0, pair).astype(BF16)
    fu_ref[...] = _dot(h, w_ref[:, IN_F0:IN_F0 + F_WIDTH]).astype(BF16)
    gate_dot(2)

    outs = []
    for hd in range(MEM_HEADS):
        sl = slice(hd * MEM_HEAD_DIM, (hd + 1) * MEM_HEAD_DIM)
        s = mem_scores[hd]
        e = jnp.exp(s - jnp.max(s, axis=-1, keepdims=True))
        l = jnp.sum(e, axis=-1, keepdims=True)
        outs.append(_dot(e.astype(BF16), memv_ref[0, :, sl]) / l)
    om_ref[...] = jnp.concatenate(outs, axis=1).astype(BF16)

    for wide_ref, narrow_ref in zip(wide_refs, rest[n_narrow + 7:]):
        narrow_ref[...] = wide_ref[...].astype(BF16)

    @pl.when(pl.program_id(0) < NA_HEADS)
    def _():
        _assemble_na_bias(brow_ref, bias_ref, bias_dr)


def _in_proj(x2, norm1_g, w_in_bf16, w_kT, b_gate, na_q_g, na_k_g, mem_q_g,
             memkT, memv, bias_blocks, later_weights, tokens_per_batch, tm):
    T = x2.shape[0]
    bias_rows, bias_dr = bias_blocks
    n_cases = bias_dr.shape[0]
    M = memv.shape[1]
    tiles_per_batch = tokens_per_batch // tm
    slabs = tm // V7X_LANES
    B = T // tokens_per_batch
    na_scale = np.float32(1.0 / np.sqrt(NA_HEAD_DIM))
    mem_scale = np.float32(1.0 / np.sqrt(MEM_HEAD_DIM))
    qg = (jnp.tile(na_q_g, NA_HEADS) * na_scale).reshape(1, NA_WIDTH)
    kg = jnp.tile(na_k_g, NA_HEADS).reshape(NA_WIDTH, 1)
    mqg = (jnp.tile(mem_q_g, MEM_HEADS) * mem_scale).reshape(1, MEM_WIDTH)
    tok = lambda w: pl.BlockSpec((tm, w), lambda i: (i, 0))
    batch_of = lambda i: i // tiles_per_batch
    steps = T // tm
    assert steps >= NA_HEADS
    slab_specs = [pl.BlockSpec((w.shape[0] // steps, w.shape[1]), lambda i: (i, 0))
                  for w in later_weights]
    head_of = lambda i: jnp.minimum(i, NA_HEADS - 1)
    bias_rows_spec = pl.BlockSpec((1,) + bias_rows.shape[1:], lambda i: (head_of(i), 0, 0))
    return pl.pallas_call(
        functools.partial(_in_proj_kernel, n_narrow=len(later_weights), bias_dr=bias_dr),
        grid=(steps,),
        in_specs=[
            tok(D_MODEL),
            _const_spec((1, D_MODEL)),
            _const_spec((D_MODEL, IN_WIDTH)),
            _const_spec((NA_WIDTH, D_MODEL)),
            _const_spec((1, N_BRANCHES * D_MODEL)),
            _const_spec((1, NA_WIDTH)),
            _const_spec((NA_WIDTH, 1)),
            _const_spec((1, MEM_WIDTH)),
            pl.BlockSpec((1, MEM_WIDTH, M), lambda i: (batch_of(i), 0, 0)),
            pl.BlockSpec((1, M, MEM_WIDTH), lambda i: (batch_of(i), 0, 0)),
            bias_rows_spec,
        ] + slab_specs,
        out_specs=[
            tok(NA_WIDTH),
            pl.BlockSpec((1, slabs, NA_WIDTH, V7X_LANES),
                         lambda i: (batch_of(i), i % tiles_per_batch, 0, 0)),
            tok(NA_HEADS * V7X_LANES),
            tok(F_WIDTH),
            tok(MEM_WIDTH),
            tok(N_BRANCHES * D_MODEL),
            pl.BlockSpec((n_cases, 1, NA_PAIR_TOKENS, NA_KEY_TOKENS), lambda i: (0, head_of(i), 0, 0)),
        ] + slab_specs,
        out_shape=[
            jax.ShapeDtypeStruct((T, NA_WIDTH), BF16),
            jax.ShapeDtypeStruct((B, tokens_per_batch // V7X_LANES, NA_WIDTH, V7X_LANES), BF16),
            jax.ShapeDtypeStruct((T, NA_HEADS * V7X_LANES), BF16),
            jax.ShapeDtypeStruct((T, F_WIDTH), BF16),
            jax.ShapeDtypeStruct((T, MEM_WIDTH), BF16),
            jax.ShapeDtypeStruct((T, N_BRANCHES * D_MODEL), BF16),
            jax.ShapeDtypeStruct((n_cases, NA_HEADS, NA_PAIR_TOKENS, NA_KEY_TOKENS), F32),
        ] + [jax.ShapeDtypeStruct(w.shape, BF16) for w in later_weights],
        compiler_params=_params(1),
        name="in_proj",
    )(x2, norm1_g.reshape(1, D_MODEL), w_in_bf16, w_kT,
      b_gate.reshape(1, N_BRANCHES * D_MODEL), qg, kg, mqg, memkT, memv, bias_rows,
      *later_weights)


def _na_bias_blocks(rpb, rows):
    n_pairs = rows // 2
    wr = min(NA_WIN_ROWS, rows)
    reps = np.array([0, 1, 2, n_pairs - 2, n_pairs - 1])
    starts = np.clip(reps - 2, 0, n_pairs - NA_KEY_PAIRS)
    n_dr, n_dc = 2 * NA_WIN_ROWS - 1, 2 * NA_WIN_COLS - 1
    offsets = np.arange(n_dc) - (NA_WIN_COLS - 1)
    place = (np.arange(2 * GRID_W)[None, :] % GRID_W == (offsets % GRID_W)[:, None]).astype(np.float32)
    rows128 = jnp.dot(rpb.astype(F32).reshape(NA_HEADS * n_dr, n_dc), place,
                      precision=lax.Precision.HIGHEST).reshape(NA_HEADS, n_dr, 2 * GRID_W)
    r = (2 * reps[:, None] + np.arange(2)[None, :])[:, :, None]
    kr = (2 * starts[:, None] + np.arange(2 * NA_KEY_PAIRS)[None, :])[:, None, :]
    rs = np.clip(r - wr // 2, 0, rows - wr)
    row_valid = (kr >= rs) & (kr < rs + wr)
    dr = np.where(row_valid, kr - r + (NA_WIN_ROWS - 1), n_dr)
    assert dr.min() >= 0 and dr.max() <= n_dr
    return rows128, dr


def _assemble_na_bias(rows_ref, o_ref, dr):
    n_dr = rows_ref.shape[1]
    c = lax.broadcasted_iota(jnp.int32, (GRID_W, 2 * GRID_W), 0)
    lane = lax.broadcasted_iota(jnp.int32, (GRID_W, 2 * GRID_W), 1)
    kc = lane & (GRID_W - 1)
    cs = jnp.clip(c - NA_WIN_COLS // 2, 0, GRID_W - NA_WIN_COLS)
    col_valid = (kc >= cs) & (kc < cs + NA_WIN_COLS)
    masked = jnp.full((GRID_W, 2 * GRID_W), NEG_INF, F32)
    blocks = []
    for i in range(n_dr):
        row = jnp.broadcast_to(rows_ref[0, i:i + 1, :], (GRID_W, 2 * GRID_W))
        toeplitz = pltpu.roll(row, 0, axis=1, stride=1, stride_axis=0)
        blocks.append(jnp.where(col_valid, toeplitz, masked))
    blocks.append(masked)
    even_row = lane < GRID_W
    for case in range(dr.shape[0]):
        for rr in range(2):
            for j in range(NA_KEY_PAIRS):
                tile = jnp.where(even_row, blocks[int(dr[case, rr, 2 * j])],
                                 blocks[int(dr[case, rr, 2 * j + 1])])
                o_ref[case, 0, rr * GRID_W:(rr + 1) * GRID_W,
                      j * V7X_LANES:(j + 1) * V7X_LANES] = tile


def _na_kernel(q_ref, kT_ref, v_ref, bias_ref, o_ref, *, pairs_per_step, n_pairs):
    lane = lax.broadcasted_iota(jnp.int32, (NA_PAIR_TOKENS, 2 * NA_HEAD_DIM), 1)
    first_head = lane < NA_HEAD_DIM

    def pair_coords(i):
        p = pl.program_id(1) * pairs_per_step + i
        start = jnp.clip(p - 2, 0, n_pairs - NA_KEY_PAIRS)
        case = jnp.where(p < 2, p, jnp.where(p >= n_pairs - 2, p - (n_pairs - NA_BIAS_CASES), 2))
        qrow = pl.multiple_of(i * NA_PAIR_TOKENS, NA_PAIR_TOKENS)
        krow = pl.multiple_of(start * NA_PAIR_TOKENS, NA_PAIR_TOKENS)
        return start, case, qrow, krow

    kzero = jnp.zeros((NA_HEAD_DIM, NA_KEY_TOKENS), BF16)
    vzero = jnp.zeros((NA_KEY_TOKENS, 2 * NA_HEAD_DIM), BF16)

    def scores(coords, hp):
        start, case, qrow, _ = coords
        psl = slice(hp * 2 * NA_HEAD_DIM, (hp + 1) * 2 * NA_HEAD_DIM)
        qp = q_ref[0, pl.ds(qrow, NA_PAIR_TOKENS), psl]
        kslabs = kT_ref[0, pl.ds(start, NA_KEY_PAIRS), psl, :]
        kT = jnp.concatenate([kslabs[j] for j in range(NA_KEY_PAIRS)], axis=1)
        keys = jnp.concatenate([jnp.concatenate([kT[:NA_HEAD_DIM], kzero], axis=1),
                                jnp.concatenate([kzero, kT[NA_HEAD_DIM:]], axis=1)], axis=0)
        bias = jnp.concatenate([bias_ref[case, 2 * hp], bias_ref[case, 2 * hp + 1]], axis=1)
        return _dot(qp, keys) + bias

    def body(it, carry):
        coords = [pair_coords(it * NA_PAIRS_PER_ITER + j) for j in range(NA_PAIRS_PER_ITER)]
        items = [(j, hp) for j in range(NA_PAIRS_PER_ITER) for hp in range(NA_HEADS // 2)]
        pending = [scores(coords[j], hp) for j, hp in items[:NA_SCORES_AHEAD]]
        outs = []
        for n, (j, hp) in enumerate(items):
            s = pending.pop(0)
            if n + NA_SCORES_AHEAD < len(items):
                jn, hn = items[n + NA_SCORES_AHEAD]
                pending.append(scores(coords[jn], hn))
            _, _, qrow, krow = coords[j]
            es, vs = [], []
            for sub in range(2):
                hd = 2 * hp + sub
                sh = s[:, sub * NA_KEY_TOKENS:(sub + 1) * NA_KEY_TOKENS]
                es.append(jnp.exp((sh - jnp.max(sh, axis=-1, keepdims=True)).astype(BF16)))
                vs.append(v_ref[0, pl.ds(krow, NA_KEY_TOKENS), hd * 2 * NA_HEAD_DIM:(hd + 1) * 2 * NA_HEAD_DIM])
            values = jnp.concatenate([jnp.concatenate([vs[0], vzero], axis=1),
                                      jnp.concatenate([vzero, vs[1]], axis=1)], axis=0)
            r = _dot(jnp.concatenate(es, axis=1), values)
            halves = [r[:, sub * 2 * NA_HEAD_DIM:(sub + 1) * 2 * NA_HEAD_DIM] for sub in range(2)]
            halves = [h / pltpu.roll(h, NA_HEAD_DIM, axis=1) for h in halves]
            outs.append(jnp.where(first_head, halves[0], halves[1]))
            if hp == NA_HEADS // 2 - 1:
                o_ref[0, pl.ds(qrow, NA_PAIR_TOKENS), :] = jnp.concatenate(outs, axis=1).astype(BF16)
                outs = []
        return carry

    lax.fori_loop(0, pairs_per_step // NA_PAIRS_PER_ITER, body, 0)


def _na_attention(q, kT4, v, bias, pairs_per_step):
    B, S, _ = q.shape
    n_pairs = S // NA_PAIR_TOKENS
    steps = n_pairs // pairs_per_step
    tq = pairs_per_step * NA_PAIR_TOKENS
    return pl.pallas_call(
        functools.partial(_na_kernel, pairs_per_step=pairs_per_step, n_pairs=n_pairs),
        grid=(B, steps),
        in_specs=[
            pl.BlockSpec((1, tq, NA_WIDTH), lambda b, j: (b, j, 0)),
            pl.BlockSpec((1, n_pairs, NA_WIDTH, V7X_LANES), lambda b, j: (b, 0, 0, 0)),
            pl.BlockSpec((1, S, NA_HEADS * V7X_LANES), lambda b, j: (b, 0, 0)),
            _const_spec((NA_BIAS_CASES, NA_HEADS, NA_PAIR_TOKENS, NA_KEY_TOKENS)),
        ],
        out_specs=pl.BlockSpec((1, tq, NA_WIDTH), lambda b, j: (b, j, 0)),
        out_shape=jax.ShapeDtypeStruct((B, S, NA_WIDTH), BF16),
        compiler_params=_params(2),
        name="na_attn",
    )(q, kT4, v, bias)


DFT_CHUNK = 16


def _dft_tables():
    n, m = GRID_W, DFT_CHUNK
    a = np.arange(n)
    ang1 = 2.0 * np.pi * np.outer(a, a) / n
    w_ri = np.stack([np.cos(ang1), -np.sin(ang1)], axis=1) / np.sqrt(n)
    eye = np.eye(m)
    w_rows = w_ri.reshape(2 * n, n)
    perm = np.einsum('dx,cy->dcyx', eye, eye).reshape(m * m, m * m)
    c = np.arange(n)[:, None, None]
    d = np.arange(n)[None, :, None]
    b = np.arange(n)[None, None, :]
    ang2 = 2.0 * np.pi * ((c + n * d) * b % (n * n)) / (n * n)
    gr, gi = np.cos(ang2) / np.sqrt(n), -np.sin(ang2) / np.sqrt(n)
    w_cols = np.concatenate([np.concatenate([gr, -gi], axis=2),
                             np.concatenate([gi, gr], axis=2)], axis=1)
    ch = np.arange(F_GROUP_DIM)
    ang3 = 2.0 * np.pi * np.outer(ch, ch) / F_GROUP_DIM
    w_chan = np.concatenate([np.cos(ang3), np.sin(ang3)], axis=0) / np.sqrt(F_GROUP_DIM)
    return tuple(jnp.asarray(t, dtype=F32).astype(BF16) for t in (w_rows, w_cols, perm, w_chan))


def _dft_kernel(wr_ref, wc_ref, perm_ref, u_ref, y_ref, z_ref):
    n, m = GRID_W, DFT_CHUNK

    def rows_stage(j, carry):
        col0 = j * m
        x = u_ref[0, :, pl.ds(col0, m), :].reshape(n * m, F_WIDTH)
        xp = [_dot(perm_ref[...], x[g * m * m:(g + 1) * m * m]).astype(BF16) for g in range(n // m)]
        zb = []
        for b in range(m):
            xb = jnp.concatenate([xp[g][b * m:(b + 1) * m] for g in range(n // m)], axis=0)
            zb.append(_dot(wr_ref[...], xb).astype(BF16))
        for t in range(2 * n // m):
            st = jnp.concatenate([zb[b][t * m:(t + 1) * m] for b in range(m)], axis=0)
            z = _dot(perm_ref[...], st).astype(BF16)
            z_ref[t * 8:(t + 1) * 8, :, pl.ds(col0, m), :] = z.reshape(8, 2, m, F_WIDTH)
        return carry

    def cols_stage(j, carry):
        col0 = j * m
        res = []
        for c_lo in range(m):
            c = j * m + c_lo
            data = z_ref[c].reshape(2 * n, F_WIDTH)
            res.append(_dot(wc_ref[c], data).astype(BF16))
        for ri in range(2):
            for g in range(n // m):
                lo = ri * n + g * m
                stacked = jnp.concatenate([r[lo:lo + m] for r in res], axis=0)
                y = _dot(perm_ref[...], stacked).astype(BF16)
                y_ref[ri, 0, g * m:(g + 1) * m, pl.ds(col0, m), :] = y.reshape(m, m, F_WIDTH)
        return carry

    for j in range(n // m):
        rows_stage(j, 0)
    for j in range(n // m):
        cols_stage(j, 0)


def _position_dft(u4, w_rows, w_cols, perm):
    B, n, m = u4.shape[0], GRID_W, DFT_CHUNK
    return pl.pallas_call(
        _dft_kernel,
        grid=(B,),
        in_specs=[
            _const_spec((2 * n, n)),
            _const_spec((n, 2 * n, 2 * n)),
            _const_spec((m * m, m * m)),
            pl.BlockSpec((1, n, n, F_WIDTH), lambda b: (b, 0, 0, 0)),
        ],
        out_specs=pl.BlockSpec((2, 1, n, n, F_WIDTH), lambda b: (0, b, 0, 0, 0)),
        out_shape=jax.ShapeDtypeStruct((2, B, n, n, F_WIDTH), BF16),
        scratch_shapes=[pltpu.VMEM((n, 2, n, F_WIDTH), BF16)],
        compiler_params=_params(1),
        name="position_dft",
    )(w_rows, w_cols, perm, u4)


def _out_ffn_kernel(x_ref, ona_ref, y_ref, om_ref, gate_ref, wch_ref, wna_ref, wf_ref, wmo_ref,
                    wout_ref, g2_ref, w1_ref, w2_ref, o_ref):
    yr, yi = y_ref[0], y_ref[1]
    groups = []
    for g in range(F_GROUPS):
        sl = slice(g * F_GROUP_DIM, (g + 1) * F_GROUP_DIM)
        groups.append(_dot(jnp.concatenate([yr[:, sl], yi[:, sl]], axis=1), wch_ref[...]))
    yf = jnp.concatenate(groups, axis=1).astype(BF16)

    gate = lambda br: gate_ref[:, br * D_MODEL:(br + 1) * D_MODEL].astype(F32)
    merged = gate(0) * _dot(ona_ref[...], wna_ref[...])
    merged = merged + gate(1) * _dot(yf, wf_ref[...])
    merged = merged + gate(2) * _dot(om_ref[...], wmo_ref[...])
    x1 = x_ref[...] + _dot(merged.astype(BF16), wout_ref[...])

    ms = jnp.mean(x1 * x1, axis=-1, keepdims=True)
    h2 = (x1 * lax.rsqrt(ms + EPS) * g2_ref[...]).astype(BF16)
    a = jnp.maximum(_dot(h2, w1_ref[...]), 0.0)
    o_ref[...] = x1 + _dot((a * a).astype(BF16), w2_ref[...])


def _out_ffn(x2, ona, y3, om, gates, w_chan, w_na_o, w_f, w_mem_o, w_out, norm2_g, w_ff1, w_ff2,
             tm):
    T = x2.shape[0]
    tok = lambda w: pl.BlockSpec((tm, w), lambda i: (i, 0))
    return pl.pallas_call(
        _out_ffn_kernel,
        grid=(T // tm,),
        in_specs=[
            tok(D_MODEL),
            tok(NA_WIDTH),
            pl.BlockSpec((2, tm, F_WIDTH), lambda i: (0, i, 0)),
            tok(MEM_WIDTH),
            tok(N_BRANCHES * D_MODEL),
            _const_spec((2 * F_GROUP_DIM, F_GROUP_DIM)),
            _const_spec((NA_WIDTH, D_MODEL)),
            _const_spec((F_WIDTH, D_MODEL)),
            _const_spec((MEM_WIDTH, D_MODEL)),
            _const_spec((D_MODEL, D_MODEL)),
            _const_spec((1, D_MODEL)),
            _const_spec((D_MODEL, D_FF)),
            _const_spec((D_FF, D_MODEL)),
        ],
        out_specs=tok(D_MODEL),
        out_shape=jax.ShapeDtypeStruct((T, D_MODEL), F32),
        compiler_params=_params(1),
        name="out_ffn",
    )(x2, ona, y3, om, gates, w_chan, w_na_o, w_f, w_mem_o, w_out, norm2_g.reshape(1, D_MODEL),
      w_ff1, w_ff2)


def kernel(x, mem, norm1_g, w_in, b_gate, na_q_g, na_k_g, na_rpb, w_na_o, w_f, mem_norm_g,
           w_mem_kv, mem_q_g, mem_k_g, w_mem_o, w_out, norm2_g, w_ff1, w_ff2):
    B, S, _ = x.shape
    T = B * S
    rows = S // GRID_W
    assert rows == GRID_W and S % NA_PAIR_TOKENS == 0
    x2 = x.reshape(T, D_MODEL)
    gmean_mem = _group_mean_matrix(MEM_WIDTH, MEM_HEAD_DIM)
    w_rows, w_cols, perm, w_chan = _dft_tables()

    memkT, memv, w_in_bf16, w_kT = _mem_kv(mem, mem_norm_g, w_mem_kv, gmean_mem, mem_k_g, w_in)
    later_weights = (w_na_o, w_f, w_mem_o, w_out, w_ff1, w_ff2)
    q, kT4, v, fu, om, gates, bias, *narrow = _in_proj(
        x2, norm1_g, w_in_bf16, w_kT, b_gate, na_q_g, na_k_g, mem_q_g, memkT, memv,
        _na_bias_blocks(na_rpb, rows), later_weights, S, tm=IN_PROJ_TOKENS)
    w_na_o, w_f, w_mem_o, w_out, w_ff1, w_ff2 = narrow

    ona = _na_attention(q.reshape(B, S, NA_WIDTH), kT4, v.reshape(B, S, NA_HEADS * V7X_LANES), bias,
                        pairs_per_step=NA_PAIRS_PER_STEP)

    y = _position_dft(fu.reshape(B, GRID_W, GRID_W, F_WIDTH), w_rows, w_cols, perm)
    y3 = y.reshape(2, T, F_WIDTH)

    out = _out_ffn(x2, ona.reshape(T, NA_WIDTH), y3, om, gates, w_chan, w_na_o, w_f, w_mem_o,
                   w_out, norm2_g, w_ff1, w_ff2, tm=OUT_FFN_TOKENS)
    return out.reshape(B, S, D_MODEL)
```

```python
import functools

import numpy as np
import jax
import jax.numpy as jnp
from jax import lax
from jax.experimental import pallas as pl
from jax.experimental.pallas import tpu as pltpu

D_MODEL = 1024
GRID_W = 64
NA_HEADS = 8
NA_HEAD_DIM = 64
NA_WIDTH = NA_HEADS * NA_HEAD_DIM
NA_WIN_ROWS = 8
NA_WIN_COLS = 16
F_GROUPS = 4
F_GROUP_DIM = 128
F_WIDTH = F_GROUPS * F_GROUP_DIM
MEM_HEADS = 4
MEM_HEAD_DIM = 128
MEM_WIDTH = MEM_HEADS * MEM_HEAD_DIM
N_BRANCHES = 3
D_FF = 4 * D_MODEL
IN_Q0 = 0
IN_K0 = IN_Q0 + NA_WIDTH
IN_V0 = IN_K0 + NA_WIDTH
IN_F0 = IN_V0 + NA_WIDTH
IN_M0 = IN_F0 + F_WIDTH
IN_G0 = IN_M0 + MEM_WIDTH
IN_WIDTH = IN_G0 + N_BRANCHES * D_MODEL
EPS = 1e-6
NEG_INF = -1e30
NA_SCORE_SCALE = float(np.float32(1.0 / np.sqrt(NA_HEAD_DIM)))
MEM_SCORE_SCALE = float(np.float32(1.0 / np.sqrt(MEM_HEAD_DIM)))

V7X_LANES = 128
V7X_VMEM_LIMIT_BYTES = 60 * 1024 * 1024

NA_PAIR_TOKENS = 2 * GRID_W
NA_KEY_PAIRS = 5
NA_KEY_TOKENS = NA_KEY_PAIRS * NA_PAIR_TOKENS
NA_BIAS_CASES = 5
NA_PAIRS_PER_ITER = 16
NA_SCORES_AHEAD = 2

IN_PROJ_TOKENS = 512
OUT_FFN_TOKENS = 512
NA_PAIRS_PER_STEP = 16

BF16 = jnp.bfloat16
F32 = jnp.float32


def _dot(a, b):
    return jnp.dot(a, b, preferred_element_type=F32)


def _dot_nt(a, b):
    return lax.dot_general(a, b, (((1,), (1,)), ((), ())), preferred_element_type=F32)


def _const_spec(shape):
    return pl.BlockSpec(shape, lambda *_: (0,) * len(shape), pipeline_mode=pl.Buffered(1))


def _params(n_axes):
    return pltpu.CompilerParams(
        dimension_semantics=("arbitrary",) * n_axes,
        vmem_limit_bytes=V7X_VMEM_LIMIT_BYTES,
    )


def _group_mean_matrix(width, group):
    idx = np.arange(width) // group
    return jnp.asarray((idx[:, None] == idx[None, :]).astype(np.float32) / group, dtype=BF16)


PREP_STEPS = 4


def _mem_kv_kernel(mem_ref, g_ref, w_ref, gm_ref, kg_ref, win_ref, kT_ref, v_ref, wb_ref, wkT_ref,
                   *, n_batches):
    slab = win_ref[...]
    wb_ref[...] = slab.astype(BF16)
    wkT_ref[...] = slab[:, IN_K0:IN_K0 + NA_WIDTH].T.astype(BF16)

    @pl.when(pl.program_id(0) < n_batches)
    def _():
        m = mem_ref[0]
        ms = jnp.mean(m * m, axis=-1, keepdims=True)
        mn = (m * lax.rsqrt(ms + EPS) * g_ref[...]).astype(BF16)
        kv = _dot(mn, w_ref[...].astype(BF16))
        k = kv[:, :MEM_WIDTH]
        msk = _dot((k * k).astype(BF16), gm_ref[...])
        kn = k * lax.rsqrt(msk + EPS) * jnp.concatenate([kg_ref[...]] * MEM_HEADS, axis=1)
        kT_ref[0] = kn.T.astype(BF16)
        v_ref[0] = kv[:, MEM_WIDTH:].astype(BF16)


def _mem_kv(mem, mem_norm_g, w_mem_kv, gmean_mem, mem_k_g, w_in):
    B, M, _ = mem.shape
    assert B <= PREP_STEPS
    rows = D_MODEL // PREP_STEPS
    batch_of = lambda i: jnp.minimum(i, B - 1)
    return pl.pallas_call(
        functools.partial(_mem_kv_kernel, n_batches=B),
        grid=(PREP_STEPS,),
        in_specs=[
            pl.BlockSpec((1, M, D_MODEL), lambda i: (batch_of(i), 0, 0)),
            _const_spec((1, D_MODEL)),
            _const_spec((D_MODEL, 2 * MEM_WIDTH)),
            _const_spec((MEM_WIDTH, MEM_WIDTH)),
            _const_spec((1, MEM_HEAD_DIM)),
            pl.BlockSpec((rows, IN_WIDTH), lambda i: (i, 0)),
        ],
        out_specs=[
            pl.BlockSpec((1, MEM_WIDTH, M), lambda i: (batch_of(i), 0, 0)),
            pl.BlockSpec((1, M, MEM_WIDTH), lambda i: (batch_of(i), 0, 0)),
            pl.BlockSpec((rows, IN_WIDTH), lambda i: (i, 0)),
            pl.BlockSpec((NA_WIDTH, rows), lambda i: (0, i)),
        ],
        out_shape=[
            jax.ShapeDtypeStruct((B, MEM_WIDTH, M), BF16),
            jax.ShapeDtypeStruct((B, M, MEM_WIDTH), BF16),
            jax.ShapeDtypeStruct((D_MODEL, IN_WIDTH), BF16),
            jax.ShapeDtypeStruct((NA_WIDTH, D_MODEL), BF16),
        ],
        compiler_params=_params(1),
        name="mem_kv",
    )(mem, mem_norm_g.reshape(1, D_MODEL), w_mem_kv, gmean_mem,
      mem_k_g.reshape(1, MEM_HEAD_DIM), w_in)


def _in_proj_kernel(x_ref, g1_ref, w_ref, wkT_ref, bg_ref,
                    qg_ref, kg_ref, mqg_ref, memkT_ref, memv_ref, brow_ref, *rest,
                    n_narrow, bias_dr):
    wide_refs = rest[:n_narrow]
    q_ref, kT_ref, v_ref, fu_ref, om_ref, gate_ref, bias_ref = rest[n_narrow:n_narrow + 7]

    x = x_ref[...]
    ms = jnp.mean(x * x, axis=-1, keepdims=True)
    h = (x * lax.rsqrt(ms + EPS) * g1_ref[...]).astype(BF16)

    def gate_dot(br):
        sl = slice(br * D_MODEL, (br + 1) * D_MODEL)
        zg = _dot(h, w_ref[:, IN_G0 + br * D_MODEL:IN_G0 + (br + 1) * D_MODEL]) + bg_ref[:, sl]
        gate_ref[:, sl] = jax.nn.sigmoid(zg).astype(BF16)

    zm = _dot(h, w_ref[:, IN_M0:IN_M0 + MEM_WIDTH])
    mem_scores = []
    for hd in range(MEM_HEADS):
        sl = slice(hd * MEM_HEAD_DIM, (hd + 1) * MEM_HEAD_DIM)
        zh = zm[:, sl]
        msm = jnp.mean(zh * zh, axis=-1, keepdims=True)
        qm = (zh * lax.rsqrt(msm + EPS) * (mqg_ref[...] * MEM_SCORE_SCALE)).astype(BF16)
        if hd == 0:
            gate_dot(0)
        mem_scores.append(_dot(qm, memkT_ref[0, sl, :]))
    gate_dot(1)

    zq = _dot(h, w_ref[:, IN_Q0:IN_Q0 + NA_WIDTH])
    low_half = lax.broadcasted_iota(jnp.int32, (zq.shape[0], V7X_LANES), 1) < NA_HEAD_DIM
    q_gain = jnp.concatenate([qg_ref[...]] * 2, axis=1) * NA_SCORE_SCALE
    for hp in range(NA_HEADS // 2):
        sl = slice(hp * V7X_LANES, (hp + 1) * V7X_LANES)
        zz = zq[:, sl] * zq[:, sl]
        ms_lo = jnp.sum(jnp.where(low_half, zz, 0.0), axis=-1, keepdims=True)
        ms_hi = jnp.sum(jnp.where(low_half, 0.0, zz), axis=-1, keepdims=True)
        ms = jnp.where(low_half, ms_lo, ms_hi) * (1.0 / NA_HEAD_DIM)
        q_ref[:, sl] = (zq[:, sl] * lax.rsqrt(ms + EPS) * q_gain).astype(BF16)

    zkT = _dot_nt(wkT_ref[...], h)
    zk3 = zkT.reshape(NA_HEADS, NA_HEAD_DIM, zkT.shape[1])
    msk = jnp.mean(zk3 * zk3, axis=1, keepdims=True)
    kT = (zk3 * lax.rsqrt(msk + EPS) * kg_ref[...][None]).reshape(zkT.shape).astype(BF16)
    for j in range(kT_ref.shape[1]):
        kT_ref[0, j] = kT[:, j * V7X_LANES:(j + 1) * V7X_LANES]

    zv = _dot(h, w_ref[:, IN_V0:IN_V0 + NA_WIDTH])
    low_half = lax.broadcasted_iota(jnp.int32, (zv.shape[0], V7X_LANES), 1) < NA_HEAD_DIM
    for hp in range(NA_HEADS // 2):
        pair = zv[:, hp * V7X_LANES:(hp + 1) * V7X_LANES]
        v_ref[:, (2 * hp) * V7X_LANES:(2 * hp + 1) * V7X_LANES] = jnp.where(low_half, pair, 1.0).astype(BF16)
        v_ref[:, (2 * hp + 1) * V7X_LANES:(2 * hp + 2) * V7X_LANES] = jnp.where(low_half, 1.0, pair).astype(BF16)
    fu_ref[...] = _dot(h, w_ref[:, IN_F0:IN_F0 + F_WIDTH]).astype(BF16)
    gate_dot(2)

    outs = []
    for hd in range(MEM_HEADS):
        sl = slice(hd * MEM_HEAD_DIM, (hd + 1) * MEM_HEAD_DIM)
        s = mem_scores[hd]
        e = jnp.exp(s - jnp.max(s, axis=-1, keepdims=True))
        l = jnp.sum(e, axis=-1, keepdims=True)
        outs.append(_dot(e.astype(BF16), memv_ref[0, :, sl]) / l)
    om_ref[...] = jnp.concatenate(outs, axis=1).astype(BF16)

    for wide_ref, narrow_ref in zip(wide_refs, rest[n_narrow + 7:]):
        narrow_ref[...] = wide_ref[...].astype(BF16)

    @pl.when(pl.program_id(0) < NA_HEADS)
    def _():
        _assemble_na_bias(brow_ref, bias_ref, bias_dr)


def _in_proj(x2, norm1_g, w_in_bf16, w_kT, b_gate, na_q_g, na_k_g, mem_q_g,
             memkT, memv, bias_blocks, later_weights, tokens_per_batch, tm):
    T = x2.shape[0]
    bias_rows, bias_dr = bias_blocks
    n_cases = bias_dr.shape[0]
    M = memv.shape[1]
    tiles_per_batch = tokens_per_batch // tm
    slabs = tm // V7X_LANES
    B = T // tokens_per_batch
    qg = na_q_g.reshape(1, NA_HEAD_DIM)
    kg = na_k_g.reshape(NA_HEAD_DIM, 1)
    mqg = mem_q_g.reshape(1, MEM_HEAD_DIM)
    tok =lambda w: pl.BlockSpec((tm, w), lambda i: (i, 0))
    batch_of = lambda i: i // tiles_per_batch
    steps = T // tm
    assert steps >= NA_HEADS
    slab_specs = [pl.BlockSpec((w.shape[0] // steps, w.shape[1]), lambda i: (i, 0))
                  for w in later_weights]
    head_of = lambda i: jnp.minimum(i, NA_HEADS - 1)
    bias_rows_spec = pl.BlockSpec((1,) + bias_rows.shape[1:], lambda i: (head_of(i), 0, 0))
    return pl.pallas_call(
        functools.partial(_in_proj_kernel, n_narrow=len(later_weights), bias_dr=bias_dr),
        grid=(steps,),
        in_specs=[
            tok(D_MODEL),
            _const_spec((1, D_MODEL)),
            _const_spec((D_MODEL, IN_WIDTH)),
            _const_spec((NA_WIDTH, D_MODEL)),
            _const_spec((1, N_BRANCHES * D_MODEL)),
            _const_spec((1, NA_HEAD_DIM)),
            _const_spec((NA_HEAD_DIM, 1)),
            _const_spec((1, MEM_HEAD_DIM)),
            pl.BlockSpec((1, MEM_WIDTH, M), lambda i: (batch_of(i), 0, 0)),
            pl.BlockSpec((1, M, MEM_WIDTH), lambda i: (batch_of(i), 0, 0)),
            bias_rows_spec,
        ] + slab_specs,
        out_specs=[
            tok(NA_WIDTH),
            pl.BlockSpec((1, slabs, NA_WIDTH, V7X_LANES),
                         lambda i: (batch_of(i), i % tiles_per_batch, 0, 0)),
            tok(NA_HEADS * V7X_LANES),
            tok(F_WIDTH),
            tok(MEM_WIDTH),
            tok(N_BRANCHES * D_MODEL),
            pl.BlockSpec((n_cases, 1, NA_PAIR_TOKENS, NA_KEY_TOKENS), lambda i: (0, head_of(i), 0, 0)),
        ] + slab_specs,
        out_shape=[
            jax.ShapeDtypeStruct((T, NA_WIDTH), BF16),
            jax.ShapeDtypeStruct((B, tokens_per_batch // V7X_LANES, NA_WIDTH, V7X_LANES), BF16),
            jax.ShapeDtypeStruct((T, NA_HEADS * V7X_LANES), BF16),
            jax.ShapeDtypeStruct((T, F_WIDTH), BF16),
            jax.ShapeDtypeStruct((T, MEM_WIDTH), BF16),
            jax.ShapeDtypeStruct((T, N_BRANCHES * D_MODEL), BF16),
            jax.ShapeDtypeStruct((n_cases, NA_HEADS, NA_PAIR_TOKENS, NA_KEY_TOKENS), F32),
        ] + [jax.ShapeDtypeStruct(w.shape, BF16) for w in later_weights],
        compiler_params=_params(1),
        name="in_proj",
    )(x2, norm1_g.reshape(1, D_MODEL), w_in_bf16, w_kT,
      b_gate.reshape(1, N_BRANCHES * D_MODEL), qg, kg, mqg, memkT, memv, bias_rows,
      *later_weights)


def _na_bias_blocks(rpb, rows):
    n_pairs = rows // 2
    wr = min(NA_WIN_ROWS, rows)
    reps = np.array([0, 1, 2, n_pairs - 2, n_pairs - 1])
    starts = np.clip(reps - 2, 0, n_pairs - NA_KEY_PAIRS)
    n_dr, n_dc = 2 * NA_WIN_ROWS - 1, 2 * NA_WIN_COLS - 1
    offsets = np.arange(n_dc) - (NA_WIN_COLS - 1)
    place = (np.arange(2 * GRID_W)[None, :] % GRID_W == (offsets % GRID_W)[:, None]).astype(np.float32)
    rows128 = jnp.dot(rpb.astype(F32).reshape(NA_HEADS * n_dr, n_dc), place,
                      precision=lax.Precision.HIGHEST).reshape(NA_HEADS, n_dr, 2 * GRID_W)
    r = (2 * reps[:, None] + np.arange(2)[None, :])[:, :, None]
    kr = (2 * starts[:, None] + np.arange(2 * NA_KEY_PAIRS)[None, :])[:, None, :]
    rs = np.clip(r - wr // 2, 0, rows - wr)
    row_valid = (kr >= rs) & (kr < rs + wr)
    dr = np.where(row_valid, kr - r + (NA_WIN_ROWS - 1), n_dr)
    assert dr.min() >= 0 and dr.max() <= n_dr
    return rows128, dr


def _assemble_na_bias(rows_ref, o_ref, dr):
    n_dr = rows_ref.shape[1]
    c = lax.broadcasted_iota(jnp.int32, (GRID_W, 2 * GRID_W), 0)
    lane = lax.broadcasted_iota(jnp.int32, (GRID_W, 2 * GRID_W), 1)
    kc = lane & (GRID_W - 1)
    cs = jnp.clip(c - NA_WIN_COLS // 2, 0, GRID_W - NA_WIN_COLS)
    col_valid = (kc >= cs) & (kc < cs + NA_WIN_COLS)
    masked = jnp.full((GRID_W, 2 * GRID_W), NEG_INF, F32)
    blocks = []
    for i in range(n_dr):
        row = jnp.broadcast_to(rows_ref[0, i:i + 1, :], (GRID_W, 2 * GRID_W))
        toeplitz = pltpu.roll(row, 0, axis=1, stride=1, stride_axis=0)
        blocks.append(jnp.where(col_valid, toeplitz, masked))
    blocks.append(masked)
    even_row = lane < GRID_W
    for case in range(dr.shape[0]):
        for rr in range(2):
            for j in range(NA_KEY_PAIRS):
                tile = jnp.where(even_row, blocks[int(dr[case, rr, 2 * j])],
                                 blocks[int(dr[case, rr, 2 * j + 1])])
                o_ref[case, 0, rr * GRID_W:(rr + 1) * GRID_W,
                      j * V7X_LANES:(j + 1) * V7X_LANES] = tile


def _na_kernel(q_ref, kT_ref, v_ref, bias_ref, o_ref, *, pairs_per_step, n_pairs):
    lane = lax.broadcasted_iota(jnp.int32, (NA_PAIR_TOKENS, 2 * NA_HEAD_DIM), 1)
    first_head = lane < NA_HEAD_DIM

    def pair_coords(i):
        p = pl.program_id(1) * pairs_per_step + i
        start = jnp.clip(p - 2, 0, n_pairs - NA_KEY_PAIRS)
        case = jnp.where(p < 2, p, jnp.where(p >= n_pairs - 2, p - (n_pairs - NA_BIAS_CASES), 2))
        qrow = pl.multiple_of(i * NA_PAIR_TOKENS, NA_PAIR_TOKENS)
        krow = pl.multiple_of(start * NA_PAIR_TOKENS, NA_PAIR_TOKENS)
        return start, case, qrow, krow

    kzero = jnp.zeros((NA_HEAD_DIM, NA_KEY_TOKENS), BF16)
    vzero = jnp.zeros((NA_KEY_TOKENS, 2 * NA_HEAD_DIM), BF16)

    def scores(coords, hp):
        start, case, qrow, _ = coords
        psl = slice(hp * 2 * NA_HEAD_DIM, (hp + 1) * 2 * NA_HEAD_DIM)
        qp = q_ref[0, pl.ds(qrow, NA_PAIR_TOKENS), psl]
        kslabs = kT_ref[0, pl.ds(start, NA_KEY_PAIRS), psl, :]
        kT = jnp.concatenate([kslabs[j] for j in range(NA_KEY_PAIRS)], axis=1)
        keys = jnp.concatenate([jnp.concatenate([kT[:NA_HEAD_DIM], kzero], axis=1),
                                jnp.concatenate([kzero, kT[NA_HEAD_DIM:]], axis=1)], axis=0)
        bias = jnp.concatenate([bias_ref[case, 2 * hp], bias_ref[case, 2 * hp + 1]], axis=1)
        return _dot(qp, keys) + bias

    def body(it, carry):
        coords = [pair_coords(it * NA_PAIRS_PER_ITER + j) for j in range(NA_PAIRS_PER_ITER)]
        items = [(j, hp) for j in range(NA_PAIRS_PER_ITER) for hp in range(NA_HEADS // 2)]
        pending = [scores(coords[j], hp) for j, hp in items[:NA_SCORES_AHEAD]]
        outs = []
        for n, (j, hp) in enumerate(items):
            s = pending.pop(0)
            if n + NA_SCORES_AHEAD < len(items):
                jn, hn = items[n + NA_SCORES_AHEAD]
                pending.append(scores(coords[jn], hn))
            _, _, qrow, krow = coords[j]
            es, vs = [], []
            for sub in range(2):
                hd = 2 * hp + sub
                sh = s[:, sub * NA_KEY_TOKENS:(sub + 1) * NA_KEY_TOKENS]
                es.append(jnp.exp((sh - jnp.max(sh, axis=-1, keepdims=True)).astype(BF16)))
                vs.append(v_ref[0, pl.ds(krow, NA_KEY_TOKENS), hd * 2 * NA_HEAD_DIM:(hd + 1) * 2 * NA_HEAD_DIM])
            values = jnp.concatenate([jnp.concatenate([vs[0], vzero], axis=1),
                                      jnp.concatenate([vzero, vs[1]], axis=1)], axis=0)
            r = _dot(jnp.concatenate(es, axis=1), values)
            halves = [r[:, sub * 2 * NA_HEAD_DIM:(sub + 1) * 2 * NA_HEAD_DIM] for sub in range(2)]
            halves = [h / pltpu.roll(h, NA_HEAD_DIM, axis=1) for h in halves]
            outs.append(jnp.where(first_head, halves[0], halves[1]))
            if hp == NA_HEADS // 2 - 1:
                o_ref[0, pl.ds(qrow, NA_PAIR_TOKENS), :] = jnp.concatenate(outs, axis=1).astype(BF16)
                outs = []
        return carry

    lax.fori_loop(0, pairs_per_step // NA_PAIRS_PER_ITER, body, 0)


def _na_attention(q, kT4, v, bias, pairs_per_step):
    B, S, _ = q.shape
    n_pairs = S // NA_PAIR_TOKENS
    steps = n_pairs // pairs_per_step
    tq = pairs_per_step * NA_PAIR_TOKENS
    return pl.pallas_call(
        functools.partial(_na_kernel, pairs_per_step=pairs_per_step, n_pairs=n_pairs),
        grid=(B, steps),
        in_specs=[
            pl.BlockSpec((1, tq, NA_WIDTH), lambda b, j: (b, j, 0)),
            pl.BlockSpec((1, n_pairs, NA_WIDTH, V7X_LANES), lambda b, j: (b, 0, 0, 0)),
            pl.BlockSpec((1, S, NA_HEADS * V7X_LANES), lambda b, j: (b, 0, 0)),
            _const_spec((NA_BIAS_CASES, NA_HEADS, NA_PAIR_TOKENS, NA_KEY_TOKENS)),
        ],
        out_specs=pl.BlockSpec((1, tq, NA_WIDTH), lambda b, j: (b, j, 0)),
        out_shape=jax.ShapeDtypeStruct((B, S, NA_WIDTH), BF16),
        compiler_params=_params(2),
        name="na_attn",
    )(q, kT4, v, bias)


DFT_CHUNK = 16


def _dft_tables():
    n, m = GRID_W, DFT_CHUNK
    a = np.arange(n)
    ang1 = 2.0 * np.pi * np.outer(a, a) / n
    w_ri = np.stack([np.cos(ang1), -np.sin(ang1)], axis=1) / np.sqrt(n)
    eye = np.eye(m)
    w_rows = w_ri.reshape(2 * n, n)
    perm = np.einsum('dx,cy->dcyx', eye, eye).reshape(m * m, m * m)
    c = np.arange(n)[:, None, None]
    d = np.arange(n)[None, :, None]
    b = np.arange(n)[None, None, :]
    ang2 = 2.0 * np.pi * ((c + n * d) * b % (n * n)) / (n * n)
    gr, gi = np.cos(ang2) / np.sqrt(n), -np.sin(ang2) / np.sqrt(n)
    w_cols = np.concatenate([np.concatenate([gr, -gi], axis=2),
                             np.concatenate([gi, gr], axis=2)], axis=1)
    ch = np.arange(F_GROUP_DIM)
    ang3 = 2.0 * np.pi * np.outer(ch, ch) / F_GROUP_DIM
    w_chan = np.concatenate([np.cos(ang3), np.sin(ang3)], axis=0) / np.sqrt(F_GROUP_DIM)
    return tuple(jnp.asarray(t, dtype=F32).astype(BF16) for t in (w_rows, w_cols, perm, w_chan))


def _dft_kernel(wr_ref, wc_ref, perm_ref, u_ref, y_ref, z_ref):
    n, m = GRID_W, DFT_CHUNK

    def rows_stage(j, carry):
        col0 = j * m
        x = u_ref[0, :, pl.ds(col0, m), :].reshape(n * m, F_WIDTH)
        xp = [_dot(perm_ref[...], x[g * m * m:(g + 1) * m * m]).astype(BF16) for g in range(n // m)]
        zb = []
        for b in range(m):
            xb = jnp.concatenate([xp[g][b * m:(b + 1) * m] for g in range(n // m)], axis=0)
            zb.append(_dot(wr_ref[...], xb).astype(BF16))
        for t in range(2 * n // m):
            st = jnp.concatenate([zb[b][t * m:(t + 1) * m] for b in range(m)], axis=0)
            z = _dot(perm_ref[...], st).astype(BF16)
            z_ref[t * 8:(t + 1) * 8, :, pl.ds(col0, m), :] = z.reshape(8, 2, m, F_WIDTH)
        return carry

    def cols_stage(j, carry):
        col0 = j * m
        res = []
        for c_lo in range(m):
            c = j * m + c_lo
            data = z_ref[c].reshape(2 * n, F_WIDTH)
            res.append(_dot(wc_ref[c], data).astype(BF16))
        for ri in range(2):
            for g in range(n // m):
                lo = ri * n + g * m
                stacked = jnp.concatenate([r[lo:lo + m] for r in res], axis=0)
                y = _dot(perm_ref[...], stacked).astype(BF16)
                y_ref[ri, 0, g * m:(g + 1) * m, pl.ds(col0, m), :] = y.reshape(m, m, F_WIDTH)
        return carry

    for j in range(n // m):
        rows_stage(j, 0)
    for j in range(n // m):
        cols_stage(j, 0)


def _position_dft(u4, w_rows, w_cols, perm):
    B, n, m = u4.shape[0], GRID_W, DFT_CHUNK
    return pl.pallas_call(
        _dft_kernel,
        grid=(B,),
        in_specs=[
            _const_spec((2 * n, n)),
            _const_spec((n, 2 * n, 2 * n)),
            _const_spec((m * m, m * m)),
            pl.BlockSpec((1, n, n, F_WIDTH), lambda b: (b, 0, 0, 0)),
        ],
        out_specs=pl.BlockSpec((2, 1, n, n, F_WIDTH), lambda b: (0, b, 0, 0, 0)),
        out_shape=jax.ShapeDtypeStruct((2, B, n, n, F_WIDTH), BF16),
        scratch_shapes=[pltpu.VMEM((n, 2, n, F_WIDTH), BF16)],
        compiler_params=_params(1),
        name="position_dft",
    )(w_rows, w_cols, perm, u4)


def _out_ffn_kernel(x_ref, ona_ref, y_ref, om_ref, gate_ref, wch_ref, wna_ref, wf_ref, wmo_ref,
                    wout_ref, g2_ref, w1_ref, w2_ref, o_ref):
    yr, yi = y_ref[0], y_ref[1]
    groups = []
    for g in range(F_GROUPS):
        sl = slice(g * F_GROUP_DIM, (g + 1) * F_GROUP_DIM)
        groups.append(_dot(jnp.concatenate([yr[:, sl], yi[:, sl]], axis=1), wch_ref[...]))
    yf = jnp.concatenate(groups, axis=1).astype(BF16)

    gate = lambda br: gate_ref[:, br * D_MODEL:(br + 1) * D_MODEL].astype(F32)
    merged = gate(0) * _dot(ona_ref[...], wna_ref[...])
    merged = merged + gate(1) * _dot(yf, wf_ref[...])
    merged = merged + gate(2) * _dot(om_ref[...], wmo_ref[...])
    x1 = x_ref[...] + _dot(merged.astype(BF16), wout_ref[...])

    ms = jnp.mean(x1 * x1, axis=-1, keepdims=True)
    h2 = (x1 * lax.rsqrt(ms + EPS) * g2_ref[...]).astype(BF16)
    a = jnp.maximum(_dot(h2, w1_ref[...]), 0.0)
    o_ref[...] = x1 + _dot((a * a).astype(BF16), w2_ref[...])


def _out_ffn(x2, ona, y3, om, gates, w_chan, w_na_o, w_f, w_mem_o, w_out, norm2_g, w_ff1, w_ff2,
             tm):
    T = x2.shape[0]
    tok = lambda w: pl.BlockSpec((tm, w), lambda i: (i, 0))
    return pl.pallas_call(
        _out_ffn_kernel,
        grid=(T // tm,),
        in_specs=[
            tok(D_MODEL),
            tok(NA_WIDTH),
            pl.BlockSpec((2, tm, F_WIDTH), lambda i: (0, i, 0)),
            tok(MEM_WIDTH),
            tok(N_BRANCHES * D_MODEL),
            _const_spec((2 * F_GROUP_DIM, F_GROUP_DIM)),
            _const_spec((NA_WIDTH, D_MODEL)),
            _const_spec((F_WIDTH, D_MODEL)),
            _const_spec((MEM_WIDTH, D_MODEL)),
            _const_spec((D_MODEL, D_MODEL)),
            _const_spec((1, D_MODEL)),
            _const_spec((D_MODEL, D_FF)),
            _const_spec((D_FF, D_MODEL)),
        ],
        out_specs=tok(D_MODEL),
        out_shape=jax.ShapeDtypeStruct((T, D_MODEL), F32),
        compiler_params=_params(1),
        name="out_ffn",
    )(x2, ona, y3, om, gates, w_chan, w_na_o, w_f, w_mem_o, w_out, norm2_g.reshape(1, D_MODEL),
      w_ff1, w_ff2)


def kernel(x, mem, norm1_g, w_in, b_gate, na_q_g, na_k_g, na_rpb, w_na_o, w_f, mem_norm_g,
           w_mem_kv, mem_q_g, mem_k_g, w_mem_o, w_out, norm2_g, w_ff1, w_ff2):
    B, S, _ = x.shape
    T = B * S
    rows = S // GRID_W
    assert rows == GRID_W and S % NA_PAIR_TOKENS == 0
    x2 = x.reshape(T, D_MODEL)
    gmean_mem = _group_mean_matrix(MEM_WIDTH, MEM_HEAD_DIM)
    w_rows, w_cols, perm, w_chan = _dft_tables()

    memkT, memv, w_in_bf16, w_kT = _mem_kv(mem, mem_norm_g, w_mem_kv, gmean_mem, mem_k_g, w_in)
    later_weights = (w_na_o, w_f, w_mem_o, w_out, w_ff1, w_ff2)
    q, kT4, v, fu, om, gates, bias, *narrow = _in_proj(
        x2, norm1_g, w_in_bf16, w_kT, b_gate, na_q_g, na_k_g, mem_q_g, memkT, memv,
        _na_bias_blocks(na_rpb, rows), later_weights, S, tm=IN_PROJ_TOKENS)
    w_na_o, w_f, w_mem_o, w_out, w_ff1, w_ff2 = narrow

    ona = _na_attention(q.reshape(B, S, NA_WIDTH), kT4, v.reshape(B, S, NA_HEADS * V7X_LANES), bias,
                        pairs_per_step=NA_PAIRS_PER_STEP)

    y = _position_dft(fu.reshape(B, GRID_W, GRID_W, F_WIDTH), w_rows, w_cols, perm)
    y3 = y.reshape(2, T, F_WIDTH)

    out = _out_ffn(x2, ona.reshape(T, NA_WIDTH), y3, om, gates, w_chan, w_na_o, w_f, w_mem_o,
                   w_out, norm2_g, w_ff1, w_ff2, tm=OUT_FFN_TOKENS)
    return out.reshape(B, S, D_MODEL)
```

```python
import functools

import numpy as np
import jax
import jax.numpy as jnp
from jax import lax
from jax.experimental import pallas as pl
from jax.experimental.pallas import tpu as pltpu

D_MODEL = 1024
GRID_W = 64
NA_HEADS = 8
NA_HEAD_DIM = 64
NA_WIDTH = NA_HEADS * NA_HEAD_DIM
NA_WIN_ROWS = 8
NA_WIN_COLS = 16
F_GROUPS = 4
F_GROUP_DIM = 128
F_WIDTH = F_GROUPS * F_GROUP_DIM
MEM_HEADS = 4
MEM_HEAD_DIM = 128
MEM_WIDTH = MEM_HEADS * MEM_HEAD_DIM
N_BRANCHES = 3
D_FF = 4 * D_MODEL
IN_Q0 = 0
IN_K0 = IN_Q0 + NA_WIDTH
IN_V0 = IN_K0 + NA_WIDTH
IN_F0 = IN_V0 + NA_WIDTH
IN_M0 = IN_F0 + F_WIDTH
IN_G0 = IN_M0 + MEM_WIDTH
IN_WIDTH = IN_G0 + N_BRANCHES * D_MODEL
EPS = 1e-6
NEG_INF = -1e30
NA_SCORE_SCALE = float(np.float32(1.0 / np.sqrt(NA_HEAD_DIM)))
MEM_SCORE_SCALE = float(np.float32(1.0 / np.sqrt(MEM_HEAD_DIM)))

V7X_LANES = 128
V7X_VMEM_LIMIT_BYTES = 60 * 1024 * 1024

NA_PAIR_TOKENS = 2 * GRID_W
NA_KEY_PAIRS = 5
NA_KEY_TOKENS = NA_KEY_PAIRS * NA_PAIR_TOKENS
NA_BIAS_CASES = 5
NA_PAIRS_PER_ITER = 16
NA_SCORES_AHEAD = 2

IN_PROJ_TOKENS = 512
OUT_FFN_TOKENS = 512
NA_PAIRS_PER_STEP = 16

BF16 = jnp.bfloat16
F32 = jnp.float32


def _dot(a, b):
    return jnp.dot(a, b, preferred_element_type=F32)


def _dot_nt(a, b):
    return lax.dot_general(a, b, (((1,), (1,)), ((), ())), preferred_element_type=F32)


def _const_spec(shape):
    return pl.BlockSpec(shape, lambda *_: (0,) * len(shape), pipeline_mode=pl.Buffered(1))


def _params(n_axes):
    return pltpu.CompilerParams(
        dimension_semantics=("arbitrary",) * n_axes,
        vmem_limit_bytes=V7X_VMEM_LIMIT_BYTES,
    )


def _group_mean_matrix(width, group):
    idx = np.arange(width) // group
    return jnp.asarray((idx[:, None] == idx[None, :]).astype(np.float32) / group, dtype=BF16)


PREP_STEPS = 4


def _mem_kv_kernel(mem_ref, g_ref, w_ref, gm_ref, kg_ref, win_ref, kT_ref, v_ref, wb_ref, wkT_ref,
                   *, n_batches):
    slab = win_ref[...]
    wb_ref[...] = slab.astype(BF16)
    wkT_ref[...] = slab[:, IN_K0:IN_K0 + NA_WIDTH].T.astype(BF16)

    @pl.when(pl.program_id(0) < n_batches)
    def _():
        m = mem_ref[0]
        ms = jnp.mean(m * m, axis=-1, keepdims=True)
        mn = (m * lax.rsqrt(ms + EPS) * g_ref[...]).astype(BF16)
        kv = _dot(mn, w_ref[...].astype(BF16))
        k = kv[:, :MEM_WIDTH]
        msk = _dot((k * k).astype(BF16), gm_ref[...])
        kn = k * lax.rsqrt(msk + EPS) * jnp.concatenate([kg_ref[...]] * MEM_HEADS, axis=1)
        kT_ref[0] = kn.T.astype(BF16)
        v_ref[0] = kv[:, MEM_WIDTH:].astype(BF16)


def _mem_kv(mem, mem_norm_g, w_mem_kv, gmean_mem, mem_k_g, w_in):
    B, M, _ = mem.shape
    assert B <= PREP_STEPS
    rows = D_MODEL // PREP_STEPS
    batch_of = lambda i: jnp.minimum(i, B - 1)
    return pl.pallas_call(
        functools.partial(_mem_kv_kernel, n_batches=B),
        grid=(PREP_STEPS,),
        in_specs=[
            pl.BlockSpec((1, M, D_MODEL), lambda i: (batch_of(i), 0, 0)),
            _const_spec((1, D_MODEL)),
            _const_spec((D_MODEL, 2 * MEM_WIDTH)),
            _const_spec((MEM_WIDTH, MEM_WIDTH)),
            _const_spec((1, MEM_HEAD_DIM)),
            pl.BlockSpec((rows, IN_WIDTH), lambda i: (i, 0)),
        ],
        out_specs=[
            pl.BlockSpec((1, MEM_WIDTH, M), lambda i: (batch_of(i), 0, 0)),
            pl.BlockSpec((1, M, MEM_WIDTH), lambda i: (batch_of(i), 0, 0)),
            pl.BlockSpec((rows, IN_WIDTH), lambda i: (i, 0)),
            pl.BlockSpec((NA_WIDTH, rows), lambda i: (0, i)),
        ],
        out_shape=[
            jax.ShapeDtypeStruct((B, MEM_WIDTH, M), BF16),
            jax.ShapeDtypeStruct((B, M, MEM_WIDTH), BF16),
            jax.ShapeDtypeStruct((D_MODEL, IN_WIDTH), BF16),
            jax.ShapeDtypeStruct((NA_WIDTH, D_MODEL), BF16),
        ],
        compiler_params=_params(1),
        name="mem_kv",
    )(mem, mem_norm_g.reshape(1, D_MODEL), w_mem_kv, gmean_mem,
      mem_k_g.reshape(1, MEM_HEAD_DIM), w_in)


def _in_proj_kernel(x_ref, g1_ref, w_ref, wkT_ref, bg_ref,
                    qg_ref, kg_ref, mqg_ref, memkT_ref, memv_ref, brow_ref, *rest,
                    n_narrow, bias_dr):
    wide_refs = rest[:n_narrow]
    q_ref, kT_ref, v_ref, fu_ref, om_ref, gate_ref, bias_ref = rest[n_narrow:n_narrow + 7]

    x = x_ref[...]
    ms = jnp.mean(x * x, axis=-1, keepdims=True)
    h = (x * lax.rsqrt(ms + EPS) * g1_ref[...]).astype(BF16)

    def gate_dot(br):
        sl = slice(br * D_MODEL, (br + 1) * D_MODEL)
        zg = _dot(h, w_ref[:, IN_G0 + br * D_MODEL:IN_G0 + (br + 1) * D_MODEL]) + bg_ref[:, sl]
        gate_ref[:, sl] = jax.nn.sigmoid(zg).astype(BF16)

    zm = _dot(h, w_ref[:, IN_M0:IN_M0 + MEM_WIDTH])
    mem_scores = []
    for hd in range(MEM_HEADS):
        sl = slice(hd * MEM_HEAD_DIM, (hd + 1) * MEM_HEAD_DIM)
        zh = zm[:, sl]
        msm = jnp.mean(zh * zh, axis=-1, keepdims=True)
        qm = (zh * lax.rsqrt(msm + EPS) * (mqg_ref[...] * MEM_SCORE_SCALE)).astype(BF16)
        if hd == 0:
            gate_dot(0)
        mem_scores.append(_dot(qm, memkT_ref[0, sl, :]))
    gate_dot(1)

    zq = _dot(h, w_ref[:, IN_Q0:IN_Q0 + NA_WIDTH])
    low_half = lax.broadcasted_iota(jnp.int32, (zq.shape[0], V7X_LANES), 1) < NA_HEAD_DIM
    q_gain = jnp.concatenate([qg_ref[...]] * 2, axis=1) * NA_SCORE_SCALE
    for hp in range(NA_HEADS // 2):
        sl = slice(hp * V7X_LANES, (hp + 1) * V7X_LANES)
        zz = zq[:, sl] * zq[:, sl]
        ms_lo = jnp.sum(jnp.where(low_half, zz, 0.0), axis=-1, keepdims=True)
        ms_hi = jnp.sum(jnp.where(low_half, 0.0, zz), axis=-1, keepdims=True)
        ms = jnp.where(low_half, ms_lo, ms_hi) * (1.0 / NA_HEAD_DIM)
        q_ref[:, sl] = (zq[:, sl] * lax.rsqrt(ms + EPS) * q_gain).astype(BF16)

    zkT = _dot_nt(wkT_ref[...], h)
    zk3 = zkT.reshape(NA_HEADS, NA_HEAD_DIM, zkT.shape[1])
    msk = jnp.mean(zk3 * zk3, axis=1, keepdims=True)
    kT = (zk3 * lax.rsqrt(msk + EPS) * kg_ref[...][None]).reshape(zkT.shape).astype(BF16)
    for j in range(kT_ref.shape[1]):
        kT_ref[0, j] = kT[:, j * V7X_LANES:(j + 1) * V7X_LANES]

    zv = _dot(h, w_ref[:, IN_V0:IN_V0 + NA_WIDTH])
    low_half = lax.broadcasted_iota(jnp.int32, (zv.shape[0], V7X_LANES), 1) < NA_HEAD_DIM
    for hp in range(NA_HEADS // 2):
        pair = zv[:, hp * V7X_LANES:(hp + 1) * V7X_LANES]
        v_ref[:, (2 * hp) * V7X_LANES:(2 * hp + 1) * V7X_LANES] = jnp.where(low_half, pair, 1.0).astype(BF16)
        v_ref[:, (2 * hp + 1) * V7X_LANES:(2 * hp + 2) * V7X_LANES] = jnp.where(low_half, 1.0, pair).astype(BF16)
    fu_ref[...] = _dot(h, w_ref[:, IN_F0:IN_F0 + F_WIDTH]).astype(BF16)
    gate_dot(2)

    outs = []
    for hd in range(MEM_HEADS):
        sl = slice(hd * MEM_HEAD_DIM, (hd + 1) * MEM_HEAD_DIM)
        s = mem_scores[hd]
        e = jnp.exp(s - jnp.max(s, axis=-1, keepdims=True))
        l = jnp.sum(e, axis=-1, keepdims=True)
        outs.append(_dot(e.astype(BF16), memv_ref[0, :, sl]) / l)
    om_ref[...] = jnp.concatenate(outs, axis=1).astype(BF16)

    for wide_ref, narrow_ref in zip(wide_refs, rest[n_narrow + 7:]):
        narrow_ref[...] = wide_ref[...].astype(BF16)

    @pl.when(pl.program_id(0) < NA_HEADS)
    def _():
        _assemble_na_bias(brow_ref, bias_ref, bias_dr)


def _in_proj(x2, norm1_g, w_in_bf16, w_kT, b_gate, na_q_g, na_k_g, mem_q_g,
             memkT, memv, bias_blocks, later_weights, tokens_per_batch, tm):
    T = x2.shape[0]
    bias_rows, bias_dr = bias_blocks
    n_cases = bias_dr.shape[0]
    M = memv.shape[1]
    tiles_per_batch = tokens_per_batch // tm
    slabs = tm // V7X_LANES
    B = T // tokens_per_batch
    qg = na_q_g.reshape(1, NA_HEAD_DIM)
    kg = na_k_g.reshape(NA_HEAD_DIM, 1)
    mqg = mem_q_g.reshape(1, MEM_HEAD_DIM)
    tok =lambda w: pl.BlockSpec((tm, w), lambda i: (i, 0))
    batch_of = lambda i: i // tiles_per_batch
    steps = T // tm
    assert steps >= NA_HEADS
    slab_specs = [pl.BlockSpec((w.shape[0] // steps, w.shape[1]), lambda i: (i, 0))
                  for w in later_weights]
    head_of = lambda i: jnp.minimum(i, NA_HEADS - 1)
    bias_rows_spec = pl.BlockSpec((1,) + bias_rows.shape[1:], lambda i: (head_of(i), 0, 0))
    return pl.pallas_call(
        functools.partial(_in_proj_kernel, n_narrow=len(later_weights), bias_dr=bias_dr),
        grid=(steps,),
        in_specs=[
            tok(D_MODEL),
            _const_spec((1, D_MODEL)),
            _const_spec((D_MODEL, IN_WIDTH)),
            _const_spec((NA_WIDTH, D_MODEL)),
            _const_spec((1, N_BRANCHES * D_MODEL)),
            _const_spec((1, NA_HEAD_DIM)),
            _const_spec((NA_HEAD_DIM, 1)),
            _const_spec((1, MEM_HEAD_DIM)),
            pl.BlockSpec((1, MEM_WIDTH, M), lambda i: (batch_of(i), 0, 0)),
            pl.BlockSpec((1, M, MEM_WIDTH), lambda i: (batch_of(i), 0, 0)),
            bias_rows_spec,
        ] + slab_specs,
        out_specs=[
            tok(NA_WIDTH),
            pl.BlockSpec((1, slabs, NA_WIDTH, V7X_LANES),
                         lambda i: (batch_of(i), i % tiles_per_batch, 0, 0)),
            tok(NA_HEADS * V7X_LANES),
            tok(F_WIDTH),
            tok(MEM_WIDTH),
            tok(N_BRANCHES * D_MODEL),
            pl.BlockSpec((n_cases, 1, NA_PAIR_TOKENS, NA_KEY_TOKENS), lambda i: (0, head_of(i), 0, 0)),
        ] + slab_specs,
        out_shape=[
            jax.ShapeDtypeStruct((T, NA_WIDTH), BF16),
            jax.ShapeDtypeStruct((B, tokens_per_batch // V7X_LANES, NA_WIDTH, V7X_LANES), BF16),
            jax.ShapeDtypeStruct((T, NA_HEADS * V7X_LANES), BF16),
            jax.ShapeDtypeStruct((T, F_WIDTH), BF16),
            jax.ShapeDtypeStruct((T, MEM_WIDTH), BF16),
            jax.ShapeDtypeStruct((T, N_BRANCHES * D_MODEL), BF16),
            jax.ShapeDtypeStruct((n_cases, NA_HEADS, NA_PAIR_TOKENS, NA_KEY_TOKENS), F32),
        ] + [jax.ShapeDtypeStruct(w.shape, BF16) for w in later_weights],
        compiler_params=_params(1),
        name="in_proj",
    )(x2, norm1_g.reshape(1, D_MODEL), w_in_bf16, w_kT,
      b_gate.reshape(1, N_BRANCHES * D_MODEL), qg, kg, mqg, memkT, memv, bias_rows,
      *later_weights)


def _na_bias_blocks(rpb, rows):
    n_pairs = rows // 2
    wr = min(NA_WIN_ROWS, rows)
    reps = np.array([0, 1, 2, n_pairs - 2, n_pairs - 1])
    starts = np.clip(reps - 2, 0, n_pairs - NA_KEY_PAIRS)
    n_dr, n_dc = 2 * NA_WIN_ROWS - 1, 2 * NA_WIN_COLS - 1
    offsets = np.arange(n_dc) - (NA_WIN_COLS - 1)
    place = (np.arange(2 * GRID_W)[None, :] % GRID_W == (offsets % GRID_W)[:, None]).astype(np.float32)
    rows128 = jnp.dot(rpb.astype(F32).reshape(NA_HEADS * n_dr, n_dc), place,
                      precision=lax.Precision.HIGHEST).reshape(NA_HEADS, n_dr, 2 * GRID_W)
    r = (2 * reps[:, None] + np.arange(2)[None, :])[:, :, None]
    kr = (2 * starts[:, None] + np.arange(2 * NA_KEY_PAIRS)[None, :])[:, None, :]
    rs = np.clip(r - wr // 2, 0, rows - wr)
    row_valid = (kr >= rs) & (kr < rs + wr)
    dr = np.where(row_valid, kr - r + (NA_WIN_ROWS - 1), n_dr)
    assert dr.min() >= 0 and dr.max() <= n_dr
    return rows128, dr


def _assemble_na_bias(rows_ref, o_ref, dr):
    n_dr = rows_ref.shape[1]
    c = lax.broadcasted_iota(jnp.int32, (GRID_W, 2 * GRID_W), 0)
    lane = lax.broadcasted_iota(jnp.int32, (GRID_W, 2 * GRID_W), 1)
    kc = lane & (GRID_W - 1)
    cs = jnp.clip(c - NA_WIN_COLS // 2, 0, GRID_W - NA_WIN_COLS)
    col_valid = (kc >= cs) & (kc < cs + NA_WIN_COLS)
    masked = jnp.full((GRID_W, 2 * GRID_W), NEG_INF, F32)
    blocks = []
    for i in range(n_dr):
        row = jnp.broadcast_to(rows_ref[0, i:i + 1, :], (GRID_W, 2 * GRID_W))
        toeplitz = pltpu.roll(row, 0, axis=1, stride=1, stride_axis=0)
        blocks.append(jnp.where(col_valid, toeplitz, masked))
    blocks.append(masked)
    even_row = lane < GRID_W
    for case in range(dr.shape[0]):
        for rr in range(2):
            for j in range(NA_KEY_PAIRS):
                tile = jnp.where(even_row, blocks[int(dr[case, rr, 2 * j])],
                                 blocks[int(dr[case, rr, 2 * j + 1])])
                o_ref[case, 0, rr * GRID_W:(rr + 1) * GRID_W,
                      j * V7X_LANES:(j + 1) * V7X_LANES] = tile


def _na_kernel(q_ref, kT_ref, v_ref, bias_ref, o_ref, *, pairs_per_step, n_pairs):
    lane = lax.broadcasted_iota(jnp.int32, (NA_PAIR_TOKENS, 2 * NA_HEAD_DIM), 1)
    first_head = lane < NA_HEAD_DIM

    def pair_coords(i):
        p = pl.program_id(1) * pairs_per_step + i
        start = jnp.clip(p - 2, 0, n_pairs - NA_KEY_PAIRS)
        case = jnp.where(p < 2, p, jnp.where(p >= n_pairs - 2, p - (n_pairs - NA_BIAS_CASES), 2))
        qrow = pl.multiple_of(i * NA_PAIR_TOKENS, NA_PAIR_TOKENS)
        krow = pl.multiple_of(start * NA_PAIR_TOKENS, NA_PAIR_TOKENS)
        return start, case, qrow, krow

    kzero = jnp.zeros((NA_HEAD_DIM, NA_KEY_TOKENS), BF16)
    vzero = jnp.zeros((NA_KEY_TOKENS, 2 * NA_HEAD_DIM), BF16)

    def scores(coords, hp):
        start, case, qrow, _ = coords
        psl = slice(hp * 2 * NA_HEAD_DIM, (hp + 1) * 2 * NA_HEAD_DIM)
        qp = q_ref[0, pl.ds(qrow, NA_PAIR_TOKENS), psl]
        kslabs = kT_ref[0, pl.ds(start, NA_KEY_PAIRS), psl, :]
        kT = jnp.concatenate([kslabs[j] for j in range(NA_KEY_PAIRS)], axis=1)
        keys = jnp.concatenate([jnp.concatenate([kT[:NA_HEAD_DIM], kzero], axis=1),
                                jnp.concatenate([kzero, kT[NA_HEAD_DIM:]], axis=1)], axis=0)
        bias = jnp.concatenate([bias_ref[case, 2 * hp], bias_ref[case, 2 * hp + 1]], axis=1)
        return _dot(qp, keys) + bias

    def body(it, carry):
        coords = [pair_coords(it * NA_PAIRS_PER_ITER + j) for j in range(NA_PAIRS_PER_ITER)]
        items = [(j, hp) for j in range(NA_PAIRS_PER_ITER) for hp in range(NA_HEADS // 2)]
        pending = [scores(coords[j], hp) for j, hp in items[:NA_SCORES_AHEAD]]
        outs = []
        for n, (j, hp) in enumerate(items):
            s = pending.pop(0)
            if n + NA_SCORES_AHEAD < len(items):
                jn, hn = items[n + NA_SCORES_AHEAD]
                pending.append(scores(coords[jn], hn))
            _, _, qrow, krow = coords[j]
            es, vs = [], []
            for sub in range(2):
                hd = 2 * hp + sub
                sh = s[:, sub * NA_KEY_TOKENS:(sub + 1) * NA_KEY_TOKENS]
                es.append(jnp.exp((sh - jnp.max(sh, axis=-1, keepdims=True)).astype(BF16)))
                vs.append(v_ref[0, pl.ds(krow, NA_KEY_TOKENS), hd * 2 * NA_HEAD_DIM:(hd + 1) * 2 * NA_HEAD_DIM])
            values = jnp.concatenate([jnp.concatenate([vs[0], vzero], axis=1),
                                      jnp.concatenate([vzero, vs[1]], axis=1)], axis=0)
            r = _dot(jnp.concatenate(es, axis=1), values)
            halves = [r[:, sub * 2 * NA_HEAD_DIM:(sub + 1) * 2 * NA_HEAD_DIM] for sub in range(2)]
            halves = [h / pltpu.roll(h, NA_HEAD_DIM, axis=1) for h in halves]
            outs.append(jnp.where(first_head, halves[0], halves[1]))
            if hp == NA_HEADS // 2 - 1:
                o_ref[0, pl.ds(qrow, NA_PAIR_TOKENS), :] = jnp.concatenate(outs, axis=1).astype(BF16)
                outs = []
        return carry

    lax.fori_loop(0, pairs_per_step // NA_PAIRS_PER_ITER, body, 0)


def _na_attention(q, kT4, v, bias, pairs_per_step):
    B, S, _ = q.shape
    n_pairs = S // NA_PAIR_TOKENS
    steps = n_pairs // pairs_per_step
    tq = pairs_per_step * NA_PAIR_TOKENS
    return pl.pallas_call(
        functools.partial(_na_kernel, pairs_per_step=pairs_per_step, n_pairs=n_pairs),
        grid=(B, steps),
        in_specs=[
            pl.BlockSpec((1, tq, NA_WIDTH), lambda b, j: (b, j, 0)),
            pl.BlockSpec((1, n_pairs, NA_WIDTH, V7X_LANES), lambda b, j: (b, 0, 0, 0)),
            pl.BlockSpec((1, S, NA_HEADS * V7X_LANES), lambda b, j: (b, 0, 0)),
            _const_spec((NA_BIAS_CASES, NA_HEADS, NA_PAIR_TOKENS, NA_KEY_TOKENS)),
        ],
        out_specs=pl.BlockSpec((1, tq, NA_WIDTH), lambda b, j: (b, j, 0)),
        out_shape=jax.ShapeDtypeStruct((B, S, NA_WIDTH), BF16),
        compiler_params=_params(2),
        name="na_attn",
    )(q, kT4, v, bias)


DFT_CHUNK = 16


def _dft_tables():
    n, m = GRID_W, DFT_CHUNK
    a = np.arange(n)
    ang1 = 2.0 * np.pi * np.outer(a, a) / n
    w_ri = np.stack([np.cos(ang1), -np.sin(ang1)], axis=1) / np.sqrt(n)
    eye = np.eye(m)
    w_rows = w_ri.reshape(2 * n, n)
    perm = np.einsum('dx,cy->dcyx', eye, eye).reshape(m * m, m * m)
    c = np.arange(n)[:, None, None]
    d = np.arange(n)[None, :, None]
    b = np.arange(n)[None, None, :]
    ang2 = 2.0 * np.pi * ((c + n * d) * b % (n * n)) / (n * n)
    gr, gi = np.cos(ang2) / np.sqrt(n), -np.sin(ang2) / np.sqrt(n)
    w_cols = np.concatenate([np.concatenate([gr, -gi], axis=2),
                             np.concatenate([gi, gr], axis=2)], axis=1)
    ch = np.arange(F_GROUP_DIM)
    ang3 = 2.0 * np.pi * np.outer(ch, ch) / F_GROUP_DIM
    w_chan = np.concatenate([np.cos(ang3), np.sin(ang3)], axis=0) / np.sqrt(F_GROUP_DIM)
    return tuple(jnp.asarray(t, dtype=F32).astype(BF16) for t in (w_rows, w_cols, perm, w_chan))


def _dft_kernel(wr_ref, wc_ref, perm_ref, u_ref, y_ref, z_ref):
    n, m = GRID_W, DFT_CHUNK

    def rows_stage(j, carry):
        col0 = j * m
        x = u_ref[0, :, pl.ds(col0, m), :].reshape(n * m, F_WIDTH)
        xp = [_dot(perm_ref[...], x[g * m * m:(g + 1) * m * m]).astype(BF16) for g in range(n // m)]
        zb = []
        for b in range(m):
            xb = jnp.concatenate([xp[g][b * m:(b + 1) * m] for g in range(n // m)], axis=0)
            zb.append(_dot(wr_ref[...], xb).astype(BF16))
        for t in range(2 * n // m):
            st = jnp.concatenate([zb[b][t * m:(t + 1) * m] for b in range(m)], axis=0)
            z = _dot(perm_ref[...], st).astype(BF16)
            z_ref[t * 8:(t + 1) * 8, :, pl.ds(col0, m), :] = z.reshape(8, 2, m, F_WIDTH)
        return carry

    def cols_stage(j, carry):
        col0 = j * m
        res = []
        for c_lo in range(m):
            c = j * m + c_lo
            data = z_ref[c].reshape(2 * n, F_WIDTH)
            res.append(_dot(wc_ref[c], data).astype(BF16))
        for ri in range(2):
            for g in range(n // m):
                lo = ri * n + g * m
                stacked = jnp.concatenate([r[lo:lo + m] for r in res], axis=0)
                y = _dot(perm_ref[...], stacked).astype(BF16)
                y_ref[ri, 0, g * m:(g + 1) * m, pl.ds(col0, m), :] = y.reshape(m, m, F_WIDTH)
        return carry

    for j in range(n // m):
        rows_stage(j, 0)
    for j in range(n // m):
        cols_stage(j, 0)


def _position_dft(u4, w_rows, w_cols, perm):
    B, n, m = u4.shape[0], GRID_W, DFT_CHUNK
    return pl.pallas_call(
        _dft_kernel,
        grid=(B,),
        in_specs=[
            _const_spec((2 * n, n)),
            _const_spec((n, 2 * n, 2 * n)),
            _const_spec((m * m, m * m)),
            pl.BlockSpec((1, n, n, F_WIDTH), lambda b: (b, 0, 0, 0)),
        ],
        out_specs=pl.BlockSpec((2, 1, n, n, F_WIDTH), lambda b: (0, b, 0, 0, 0)),
        out_shape=jax.ShapeDtypeStruct((2, B, n, n, F_WIDTH), BF16),
        scratch_shapes=[pltpu.VMEM((n, 2, n, F_WIDTH), BF16)],
        compiler_params=_params(1),
        name="position_dft",
    )(w_rows, w_cols, perm, u4)


def _out_ffn_kernel(x_ref, ona_ref, y_ref, om_ref, gate_ref, wch_ref, wna_ref, wf_ref, wmo_ref,
                    wout_ref, g2_ref, w1_hbm, w2_hbm, o_ref, w1_ref, w2_ref, sems):
    first_step = pl.program_id(0) == 0
    copies = (pltpu.make_async_copy(w1_hbm, w1_ref, sems.at[0]),
              pltpu.make_async_copy(w2_hbm, w2_ref, sems.at[1]))

    @pl.when(first_step)
    def _():
        for copy in copies:
            copy.start()

    yr, yi = y_ref[0], y_ref[1]
    groups = []
    for g in range(F_GROUPS):
        sl = slice(g * F_GROUP_DIM, (g + 1) * F_GROUP_DIM)
        groups.append(_dot(jnp.concatenate([yr[:, sl], yi[:, sl]], axis=1), wch_ref[...]))
    yf = jnp.concatenate(groups, axis=1).astype(BF16)

    gate = lambda br: gate_ref[:, br * D_MODEL:(br + 1) * D_MODEL].astype(F32)
    merged = gate(0) * _dot(ona_ref[...], wna_ref[...])
    merged = merged + gate(1) * _dot(yf, wf_ref[...])
    merged = merged + gate(2) * _dot(om_ref[...], wmo_ref[...])
    x1 = x_ref[...] + _dot(merged.astype(BF16), wout_ref[...])

    ms = jnp.mean(x1 * x1, axis=-1, keepdims=True)
    h2 = (x1 * lax.rsqrt(ms + EPS) * g2_ref[...]).astype(BF16)

    @pl.when(first_step)
    def _():
        for copy in copies:
            copy.wait()

    a = jnp.maximum(_dot(h2, w1_ref[...]), 0.0)
    o_ref[...] = x1 + _dot((a * a).astype(BF16), w2_ref[...])


def _out_ffn(x2, ona, y3, om, gates, w_chan, w_na_o, w_f, w_mem_o, w_out, norm2_g, w_ff1, w_ff2,
             tm):
    T = x2.shape[0]
    tok = lambda w: pl.BlockSpec((tm, w), lambda i: (i, 0))
    return pl.pallas_call(
        _out_ffn_kernel,
        grid=(T // tm,),
        in_specs=[
            tok(D_MODEL),
            tok(NA_WIDTH),
            pl.BlockSpec((2, tm, F_WIDTH), lambda i: (0, i, 0)),
            tok(MEM_WIDTH),
            tok(N_BRANCHES * D_MODEL),
            _const_spec((2 * F_GROUP_DIM, F_GROUP_DIM)),
            _const_spec((NA_WIDTH, D_MODEL)),
            _const_spec((F_WIDTH, D_MODEL)),
            _const_spec((MEM_WIDTH, D_MODEL)),
            _const_spec((D_MODEL, D_MODEL)),
            _const_spec((1, D_MODEL)),
            pl.BlockSpec(memory_space=pl.ANY),
            pl.BlockSpec(memory_space=pl.ANY),
        ],
        out_specs=tok(D_MODEL),
        out_shape=jax.ShapeDtypeStruct((T, D_MODEL), F32),
        scratch_shapes=[pltpu.VMEM((D_MODEL, D_FF), BF16), pltpu.VMEM((D_FF, D_MODEL), BF16),
                        pltpu.SemaphoreType.DMA((2,))],
        compiler_params=_params(1),
        name="out_ffn",
    )(x2, ona, y3, om, gates, w_chan, w_na_o, w_f, w_mem_o, w_out, norm2_g.reshape(1, D_MODEL),
      w_ff1, w_ff2)


def kernel(x, mem, norm1_g, w_in, b_gate, na_q_g, na_k_g, na_rpb, w_na_o, w_f, mem_norm_g,
           w_mem_kv, mem_q_g, mem_k_g, w_mem_o, w_out, norm2_g, w_ff1, w_ff2):
    B, S, _ = x.shape
    T = B * S
    rows = S // GRID_W
    assert rows == GRID_W and S % NA_PAIR_TOKENS == 0
    x2 = x.reshape(T, D_MODEL)
    gmean_mem = _group_mean_matrix(MEM_WIDTH, MEM_HEAD_DIM)
    w_rows, w_cols, perm, w_chan = _dft_tables()

    memkT, memv, w_in_bf16, w_kT = _mem_kv(mem, mem_norm_g, w_mem_kv, gmean_mem, mem_k_g, w_in)
    later_weights = (w_na_o, w_f, w_mem_o, w_out, w_ff1, w_ff2)
    q, kT4, v, fu, om, gates, bias, *narrow = _in_proj(
        x2, norm1_g, w_in_bf16, w_kT, b_gate, na_q_g, na_k_g, mem_q_g, memkT, memv,
        _na_bias_blocks(na_rpb, rows), later_weights, S, tm=IN_PROJ_TOKENS)
    w_na_o, w_f, w_mem_o, w_out, w_ff1, w_ff2 = narrow

    ona = _na_attention(q.reshape(B, S, NA_WIDTH), kT4, v.reshape(B, S, NA_HEADS * V7X_LANES), bias,
                        pairs_per_step=NA_PAIRS_PER_STEP)

    y = _position_dft(fu.reshape(B, GRID_W, GRID_W, F_WIDTH), w_rows, w_cols, perm)
    y3 = y.reshape(2, T, F_WIDTH)

    out = _out_ffn(x2, ona.reshape(T, NA_WIDTH), y3, om, gates, w_chan, w_na_o, w_f, w_mem_o,
                   w_out, norm2_g, w_ff1, w_ff2, tm=OUT_FFN_TOKENS)
    return out.reshape(B, S, D_MODEL)
```

```python
import functools

import numpy as np
import jax
import jax.numpy as jnp
from jax import lax
from jax.experimental import pallas as pl
from jax.experimental.pallas import tpu as pltpu

D_MODEL = 1024
GRID_W = 64
NA_HEADS = 8
NA_HEAD_DIM = 64
NA_WIDTH = NA_HEADS * NA_HEAD_DIM
NA_WIN_ROWS = 8
NA_WIN_COLS = 16
F_GROUPS = 4
F_GROUP_DIM = 128
F_WIDTH = F_GROUPS * F_GROUP_DIM
MEM_HEADS = 4
MEM_HEAD_DIM = 128
MEM_WIDTH = MEM_HEADS * MEM_HEAD_DIM
N_BRANCHES = 3
D_FF = 4 * D_MODEL
IN_Q0 = 0
IN_K0 = IN_Q0 + NA_WIDTH
IN_V0 = IN_K0 + NA_WIDTH
IN_F0 = IN_V0 + NA_WIDTH
IN_M0 = IN_F0 + F_WIDTH
IN_G0 = IN_M0 + MEM_WIDTH
IN_WIDTH = IN_G0 + N_BRANCHES * D_MODEL
EPS = 1e-6
NEG_INF = -1e30
NA_SCORE_SCALE = float(np.float32(1.0 / np.sqrt(NA_HEAD_DIM)))
MEM_SCORE_SCALE = float(np.float32(1.0 / np.sqrt(MEM_HEAD_DIM)))

V7X_LANES = 128
V7X_VMEM_LIMIT_BYTES = 60 * 1024 * 1024

NA_PAIR_TOKENS = 2 * GRID_W
NA_KEY_PAIRS = 5
NA_KEY_TOKENS = NA_KEY_PAIRS * NA_PAIR_TOKENS
NA_BIAS_CASES = 5
NA_PAIRS_PER_ITER = 16
NA_SCORES_AHEAD = 2

IN_PROJ_TOKENS = 512
OUT_FFN_TOKENS = 512
NA_PAIRS_PER_STEP = 16

BF16 = jnp.bfloat16
F32 = jnp.float32


def _dot(a, b):
    return jnp.dot(a, b, preferred_element_type=F32)


def _dot_nt(a, b):
    return lax.dot_general(a, b, (((1,), (1,)), ((), ())), preferred_element_type=F32)


def _const_spec(shape):
    return pl.BlockSpec(shape, lambda *_: (0,) * len(shape), pipeline_mode=pl.Buffered(1))


def _params(n_axes, independent_steps=False):
    return pltpu.CompilerParams(
        dimension_semantics=(("parallel" if independent_steps else "arbitrary"),) * n_axes,
        vmem_limit_bytes=V7X_VMEM_LIMIT_BYTES,
    )


def _group_mean_matrix(width, group):
    idx = np.arange(width) // group
    return jnp.asarray((idx[:, None] == idx[None, :]).astype(np.float32) / group, dtype=BF16)


PREP_STEPS = 4


def _mem_kv_kernel(mem_ref, g_ref, w_ref, gm_ref, kg_ref, win_ref, kT_ref, v_ref, wb_ref, wkT_ref,
                   *, n_batches):
    slab = win_ref[...]
    wb_ref[...] = slab.astype(BF16)
    wkT_ref[...] = slab[:, IN_K0:IN_K0 + NA_WIDTH].T.astype(BF16)

    @pl.when(pl.program_id(0) < n_batches)
    def _():
        m = mem_ref[0]
        ms = jnp.mean(m * m, axis=-1, keepdims=True)
        mn = (m * lax.rsqrt(ms + EPS) * g_ref[...]).astype(BF16)
        kv = _dot(mn, w_ref[...].astype(BF16))
        k = kv[:, :MEM_WIDTH]
        msk = _dot((k * k).astype(BF16), gm_ref[...])
        kn = k * lax.rsqrt(msk + EPS) * jnp.concatenate([kg_ref[...]] * MEM_HEADS, axis=1)
        kT_ref[0] = kn.T.astype(BF16)
        v_ref[0] = kv[:, MEM_WIDTH:].astype(BF16)


def _mem_kv(mem, mem_norm_g, w_mem_kv, gmean_mem, mem_k_g, w_in):
    B, M, _ = mem.shape
    assert B <= PREP_STEPS
    rows = D_MODEL // PREP_STEPS
    batch_of = lambda i: jnp.minimum(i, B - 1)
    return pl.pallas_call(
        functools.partial(_mem_kv_kernel, n_batches=B),
        grid=(PREP_STEPS,),
        in_specs=[
            pl.BlockSpec((1, M, D_MODEL), lambda i: (batch_of(i), 0, 0)),
            _const_spec((1, D_MODEL)),
            _const_spec((D_MODEL, 2 * MEM_WIDTH)),
            _const_spec((MEM_WIDTH, MEM_WIDTH)),
            _const_spec((1, MEM_HEAD_DIM)),
            pl.BlockSpec((rows, IN_WIDTH), lambda i: (i, 0)),
        ],
        out_specs=[
            pl.BlockSpec((1, MEM_WIDTH, M), lambda i: (batch_of(i), 0, 0)),
            pl.BlockSpec((1, M, MEM_WIDTH), lambda i: (batch_of(i), 0, 0)),
            pl.BlockSpec((rows, IN_WIDTH), lambda i: (i, 0)),
            pl.BlockSpec((NA_WIDTH, rows), lambda i: (0, i)),
        ],
        out_shape=[
            jax.ShapeDtypeStruct((B, MEM_WIDTH, M), BF16),
            jax.ShapeDtypeStruct((B, M, MEM_WIDTH), BF16),
            jax.ShapeDtypeStruct((D_MODEL, IN_WIDTH), BF16),
            jax.ShapeDtypeStruct((NA_WIDTH, D_MODEL), BF16),
        ],
        compiler_params=_params(1),
        name="mem_kv",
    )(mem, mem_norm_g.reshape(1, D_MODEL), w_mem_kv, gmean_mem,
      mem_k_g.reshape(1, MEM_HEAD_DIM), w_in)


def _in_proj_kernel(x_ref, g1_ref, w_ref, wkT_ref, bg_ref,
                    qg_ref, kg_ref, mqg_ref, memkT_ref, memv_ref, brow_ref, *rest,
                    n_narrow, bias_dr):
    wide_refs = rest[:n_narrow]
    q_ref, kT_ref, v_ref, fu_ref, om_ref, gate_ref, bias_ref = rest[n_narrow:n_narrow + 7]

    x = x_ref[...]
    ms = jnp.mean(x * x, axis=-1, keepdims=True)
    h = (x * lax.rsqrt(ms + EPS) * g1_ref[...]).astype(BF16)

    def gate_dot(br):
        sl = slice(br * D_MODEL, (br + 1) * D_MODEL)
        zg = _dot(h, w_ref[:, IN_G0 + br * D_MODEL:IN_G0 + (br + 1) * D_MODEL]) + bg_ref[:, sl]
        gate_ref[:, sl] = jax.nn.sigmoid(zg).astype(BF16)

    zm = _dot(h, w_ref[:, IN_M0:IN_M0 + MEM_WIDTH])
    mem_scores = []
    for hd in range(MEM_HEADS):
        sl = slice(hd * MEM_HEAD_DIM, (hd + 1) * MEM_HEAD_DIM)
        zh = zm[:, sl]
        msm = jnp.mean(zh * zh, axis=-1, keepdims=True)
        qm = (zh * lax.rsqrt(msm + EPS) * (mqg_ref[...] * MEM_SCORE_SCALE)).astype(BF16)
        if hd == 0:
            gate_dot(0)
        mem_scores.append(_dot(qm, memkT_ref[0, sl, :]))
    gate_dot(1)

    zq = _dot(h, w_ref[:, IN_Q0:IN_Q0 + NA_WIDTH])
    low_half = lax.broadcasted_iota(jnp.int32, (zq.shape[0], V7X_LANES), 1) < NA_HEAD_DIM
    q_gain = jnp.concatenate([qg_ref[...]] * 2, axis=1) * NA_SCORE_SCALE
    for hp in range(NA_HEADS // 2):
        sl = slice(hp * V7X_LANES, (hp + 1) * V7X_LANES)
        zz = zq[:, sl] * zq[:, sl]
        ms_lo = jnp.sum(jnp.where(low_half, zz, 0.0), axis=-1, keepdims=True)
        ms_hi = jnp.sum(jnp.where(low_half, 0.0, zz), axis=-1, keepdims=True)
        ms = jnp.where(low_half, ms_lo, ms_hi) * (1.0 / NA_HEAD_DIM)
        q_ref[:, sl] = (zq[:, sl] * lax.rsqrt(ms + EPS) * q_gain).astype(BF16)

    zkT = _dot_nt(wkT_ref[...], h)
    zk3 = zkT.reshape(NA_HEADS, NA_HEAD_DIM, zkT.shape[1])
    msk = jnp.mean(zk3 * zk3, axis=1, keepdims=True)
    kT = (zk3 * lax.rsqrt(msk + EPS) * kg_ref[...][None]).reshape(zkT.shape).astype(BF16)
    for j in range(kT_ref.shape[1]):
        kT_ref[0, j] = kT[:, j * V7X_LANES:(j + 1) * V7X_LANES]

    zv = _dot(h, w_ref[:, IN_V0:IN_V0 + NA_WIDTH])
    low_half = lax.broadcasted_iota(jnp.int32, (zv.shape[0], V7X_LANES), 1) < NA_HEAD_DIM
    for hp in range(NA_HEADS // 2):
        pair = zv[:, hp * V7X_LANES:(hp + 1) * V7X_LANES]
        v_ref[:, (2 * hp) * V7X_LANES:(2 * hp + 1) * V7X_LANES] = jnp.where(low_half, pair, 1.0).astype(BF16)
        v_ref[:, (2 * hp + 1) * V7X_LANES:(2 * hp + 2) * V7X_LANES] = jnp.where(low_half, 1.0, pair).astype(BF16)
    fu_ref[...] = _dot(h, w_ref[:, IN_F0:IN_F0 + F_WIDTH]).astype(BF16)
    gate_dot(2)

    outs = []
    for hd in range(MEM_HEADS):
        sl = slice(hd * MEM_HEAD_DIM, (hd + 1) * MEM_HEAD_DIM)
        s = mem_scores[hd]
        e = jnp.exp(s - jnp.max(s, axis=-1, keepdims=True))
        l = jnp.sum(e, axis=-1, keepdims=True)
        outs.append(_dot(e.astype(BF16), memv_ref[0, :, sl]) / l)
    om_ref[...] = jnp.concatenate(outs, axis=1).astype(BF16)

    for wide_ref, narrow_ref in zip(wide_refs, rest[n_narrow + 7:]):
        narrow_ref[...] = wide_ref[...].astype(BF16)

    @pl.when(pl.program_id(0) < NA_HEADS)
    def _():
        _assemble_na_bias(brow_ref, bias_ref, bias_dr)


def _in_proj(x2, norm1_g, w_in_bf16, w_kT, b_gate, na_q_g, na_k_g, mem_q_g,
             memkT, memv, bias_blocks, later_weights, tokens_per_batch, tm):
    T = x2.shape[0]
    bias_rows, bias_dr = bias_blocks
    n_cases = bias_dr.shape[0]
    M = memv.shape[1]
    tiles_per_batch = tokens_per_batch // tm
    slabs = tm // V7X_LANES
    B = T // tokens_per_batch
    qg = na_q_g.reshape(1, NA_HEAD_DIM)
    kg = na_k_g.reshape(NA_HEAD_DIM, 1)
    mqg = mem_q_g.reshape(1, MEM_HEAD_DIM)
    tok =lambda w: pl.BlockSpec((tm, w), lambda i: (i, 0))
    batch_of = lambda i: i // tiles_per_batch
    steps = T // tm
    assert steps >= NA_HEADS
    slab_specs = [pl.BlockSpec((w.shape[0] // steps, w.shape[1]), lambda i: (i, 0))
                  for w in later_weights]
    head_of = lambda i: jnp.minimum(i, NA_HEADS - 1)
    bias_rows_spec = pl.BlockSpec((1,) + bias_rows.shape[1:], lambda i: (head_of(i), 0, 0))
    return pl.pallas_call(
        functools.partial(_in_proj_kernel, n_narrow=len(later_weights), bias_dr=bias_dr),
        grid=(steps,),
        in_specs=[
            tok(D_MODEL),
            _const_spec((1, D_MODEL)),
            _const_spec((D_MODEL, IN_WIDTH)),
            _const_spec((NA_WIDTH, D_MODEL)),
            _const_spec((1, N_BRANCHES * D_MODEL)),
            _const_spec((1, NA_HEAD_DIM)),
            _const_spec((NA_HEAD_DIM, 1)),
            _const_spec((1, MEM_HEAD_DIM)),
            pl.BlockSpec((1, MEM_WIDTH, M), lambda i: (batch_of(i), 0, 0)),
            pl.BlockSpec((1, M, MEM_WIDTH), lambda i: (batch_of(i), 0, 0)),
            bias_rows_spec,
        ] + slab_specs,
        out_specs=[
            tok(NA_WIDTH),
            pl.BlockSpec((1, slabs, NA_WIDTH, V7X_LANES),
                         lambda i: (batch_of(i), i % tiles_per_batch, 0, 0)),
            tok(NA_HEADS * V7X_LANES),
            tok(F_WIDTH),
            tok(MEM_WIDTH),
            tok(N_BRANCHES * D_MODEL),
            pl.BlockSpec((n_cases, 1, NA_PAIR_TOKENS, NA_KEY_TOKENS), lambda i: (0, head_of(i), 0, 0)),
        ] + slab_specs,
        out_shape=[
            jax.ShapeDtypeStruct((T, NA_WIDTH), BF16),
            jax.ShapeDtypeStruct((B, tokens_per_batch // V7X_LANES, NA_WIDTH, V7X_LANES), BF16),
            jax.ShapeDtypeStruct((T, NA_HEADS * V7X_LANES), BF16),
            jax.ShapeDtypeStruct((T, F_WIDTH), BF16),
            jax.ShapeDtypeStruct((T, MEM_WIDTH), BF16),
            jax.ShapeDtypeStruct((T, N_BRANCHES * D_MODEL), BF16),
            jax.ShapeDtypeStruct((n_cases, NA_HEADS, NA_PAIR_TOKENS, NA_KEY_TOKENS), F32),
        ] + [jax.ShapeDtypeStruct(w.shape, BF16) for w in later_weights],
        compiler_params=_params(1),
        name="in_proj",
    )(x2, norm1_g.reshape(1, D_MODEL), w_in_bf16, w_kT,
      b_gate.reshape(1, N_BRANCHES * D_MODEL), qg, kg, mqg, memkT, memv, bias_rows,
      *later_weights)


def _na_bias_blocks(rpb, rows):
    n_pairs = rows // 2
    wr = min(NA_WIN_ROWS, rows)
    reps = np.array([0, 1, 2, n_pairs - 2, n_pairs - 1])
    starts = np.clip(reps - 2, 0, n_pairs - NA_KEY_PAIRS)
    n_dr, n_dc = 2 * NA_WIN_ROWS - 1, 2 * NA_WIN_COLS - 1
    offsets = np.arange(n_dc) - (NA_WIN_COLS - 1)
    place = (np.arange(2 * GRID_W)[None, :] % GRID_W == (offsets % GRID_W)[:, None]).astype(np.float32)
    rows128 = jnp.dot(rpb.astype(F32).reshape(NA_HEADS * n_dr, n_dc), place,
                      precision=lax.Precision.HIGHEST).reshape(NA_HEADS, n_dr, 2 * GRID_W)
    r = (2 * reps[:, None] + np.arange(2)[None, :])[:, :, None]
    kr = (2 * starts[:, None] + np.arange(2 * NA_KEY_PAIRS)[None, :])[:, None, :]
    rs = np.clip(r - wr // 2, 0, rows - wr)
    row_valid = (kr >= rs) & (kr < rs + wr)
    dr = np.where(row_valid, kr - r + (NA_WIN_ROWS - 1), n_dr)
    assert dr.min() >= 0 and dr.max() <= n_dr
    return rows128, dr


def _assemble_na_bias(rows_ref, o_ref, dr):
    n_dr = rows_ref.shape[1]
    c = lax.broadcasted_iota(jnp.int32, (GRID_W, 2 * GRID_W), 0)
    lane = lax.broadcasted_iota(jnp.int32, (GRID_W, 2 * GRID_W), 1)
    kc = lane & (GRID_W - 1)
    cs = jnp.clip(c - NA_WIN_COLS // 2, 0, GRID_W - NA_WIN_COLS)
    col_valid = (kc >= cs) & (kc < cs + NA_WIN_COLS)
    masked = jnp.full((GRID_W, 2 * GRID_W), NEG_INF, F32)
    blocks = []
    for i in range(n_dr):
        row = jnp.broadcast_to(rows_ref[0, i:i + 1, :], (GRID_W, 2 * GRID_W))
        toeplitz = pltpu.roll(row, 0, axis=1, stride=1, stride_axis=0)
        blocks.append(jnp.where(col_valid, toeplitz, masked))
    blocks.append(masked)
    even_row = lane < GRID_W
    for case in range(dr.shape[0]):
        for rr in range(2):
            for j in range(NA_KEY_PAIRS):
                tile = jnp.where(even_row, blocks[int(dr[case, rr, 2 * j])],
                                 blocks[int(dr[case, rr, 2 * j + 1])])
                o_ref[case, 0, rr * GRID_W:(rr + 1) * GRID_W,
                      j * V7X_LANES:(j + 1) * V7X_LANES] = tile


def _na_kernel(q_ref, kT_ref, v_ref, bias_ref, o_ref, *, pairs_per_step, n_pairs):
    lane = lax.broadcasted_iota(jnp.int32, (NA_PAIR_TOKENS, 2 * NA_HEAD_DIM), 1)
    first_head = lane < NA_HEAD_DIM

    def pair_coords(i):
        p = pl.program_id(1) * pairs_per_step + i
        start = jnp.clip(p - 2, 0, n_pairs - NA_KEY_PAIRS)
        case = jnp.where(p < 2, p, jnp.where(p >= n_pairs - 2, p - (n_pairs - NA_BIAS_CASES), 2))
        qrow = pl.multiple_of(i * NA_PAIR_TOKENS, NA_PAIR_TOKENS)
        krow = pl.multiple_of(start * NA_PAIR_TOKENS, NA_PAIR_TOKENS)
        return start, case, qrow, krow

    kzero = jnp.zeros((NA_HEAD_DIM, NA_KEY_TOKENS), BF16)
    vzero = jnp.zeros((NA_KEY_TOKENS, 2 * NA_HEAD_DIM), BF16)

    def scores(coords, hp):
        start, case, qrow, _ = coords
        psl = slice(hp * 2 * NA_HEAD_DIM, (hp + 1) * 2 * NA_HEAD_DIM)
        qp = q_ref[0, pl.ds(qrow, NA_PAIR_TOKENS), psl]
        kslabs = kT_ref[0, pl.ds(start, NA_KEY_PAIRS), psl, :]
        kT = jnp.concatenate([kslabs[j] for j in range(NA_KEY_PAIRS)], axis=1)
        keys = jnp.concatenate([jnp.concatenate([kT[:NA_HEAD_DIM], kzero], axis=1),
                                jnp.concatenate([kzero, kT[NA_HEAD_DIM:]], axis=1)], axis=0)
        bias = jnp.concatenate([bias_ref[case, 2 * hp], bias_ref[case, 2 * hp + 1]], axis=1)
        return _dot(qp, keys) + bias

    def body(it, carry):
        coords = [pair_coords(it * NA_PAIRS_PER_ITER + j) for j in range(NA_PAIRS_PER_ITER)]
        items = [(j, hp) for j in range(NA_PAIRS_PER_ITER) for hp in range(NA_HEADS // 2)]
        pending = [scores(coords[j], hp) for j, hp in items[:NA_SCORES_AHEAD]]
        outs = []
        for n, (j, hp) in enumerate(items):
            s = pending.pop(0)
            if n + NA_SCORES_AHEAD < len(items):
                jn, hn = items[n + NA_SCORES_AHEAD]
                pending.append(scores(coords[jn], hn))
            _, _, qrow, krow = coords[j]
            es, vs = [], []
            for sub in range(2):
                hd = 2 * hp + sub
                sh = s[:, sub * NA_KEY_TOKENS:(sub + 1) * NA_KEY_TOKENS]
                es.append(jnp.exp((sh - jnp.max(sh, axis=-1, keepdims=True)).astype(BF16)))
                vs.append(v_ref[0, pl.ds(krow, NA_KEY_TOKENS), hd * 2 * NA_HEAD_DIM:(hd + 1) * 2 * NA_HEAD_DIM])
            values = jnp.concatenate([jnp.concatenate([vs[0], vzero], axis=1),
                                      jnp.concatenate([vzero, vs[1]], axis=1)], axis=0)
            r = _dot(jnp.concatenate(es, axis=1), values)
            halves = [r[:, sub * 2 * NA_HEAD_DIM:(sub + 1) * 2 * NA_HEAD_DIM] for sub in range(2)]
            halves = [h / pltpu.roll(h, NA_HEAD_DIM, axis=1) for h in halves]
            outs.append(jnp.where(first_head, halves[0], halves[1]))
            if hp == NA_HEADS // 2 - 1:
                o_ref[0, pl.ds(qrow, NA_PAIR_TOKENS), :] = jnp.concatenate(outs, axis=1).astype(BF16)
                outs = []
        return carry

    lax.fori_loop(0, pairs_per_step // NA_PAIRS_PER_ITER, body, 0)


def _na_attention(q, kT4, v, bias, pairs_per_step):
    B, S, _ = q.shape
    n_pairs = S // NA_PAIR_TOKENS
    steps = n_pairs // pairs_per_step
    tq = pairs_per_step * NA_PAIR_TOKENS
    return pl.pallas_call(
        functools.partial(_na_kernel, pairs_per_step=pairs_per_step, n_pairs=n_pairs),
        grid=(B, steps),
        in_specs=[
            pl.BlockSpec((1, tq, NA_WIDTH), lambda b, j: (b, j, 0)),
            pl.BlockSpec((1, n_pairs, NA_WIDTH, V7X_LANES), lambda b, j: (b, 0, 0, 0)),
            pl.BlockSpec((1, S, NA_HEADS * V7X_LANES), lambda b, j: (b, 0, 0)),
            _const_spec((NA_BIAS_CASES, NA_HEADS, NA_PAIR_TOKENS, NA_KEY_TOKENS)),
        ],
        out_specs=pl.BlockSpec((1, tq, NA_WIDTH), lambda b, j: (b, j, 0)),
        out_shape=jax.ShapeDtypeStruct((B, S, NA_WIDTH), BF16),
        compiler_params=_params(2, independent_steps=True),
        name="na_attn",
    )(q, kT4, v, bias)


DFT_CHUNK = 16


def _dft_tables():
    n, m = GRID_W, DFT_CHUNK
    a = np.arange(n)
    ang1 = 2.0 * np.pi * np.outer(a, a) / n
    w_ri = np.stack([np.cos(ang1), -np.sin(ang1)], axis=1) / np.sqrt(n)
    eye = np.eye(m)
    w_rows = w_ri.reshape(2 * n, n)
    perm = np.einsum('dx,cy->dcyx', eye, eye).reshape(m * m, m * m)
    c = np.arange(n)[:, None, None]
    d = np.arange(n)[None, :, None]
    b = np.arange(n)[None, None, :]
    ang2 = 2.0 * np.pi * ((c + n * d) * b % (n * n)) / (n * n)
    gr, gi = np.cos(ang2) / np.sqrt(n), -np.sin(ang2) / np.sqrt(n)
    w_cols = np.concatenate([np.concatenate([gr, -gi], axis=2),
                             np.concatenate([gi, gr], axis=2)], axis=1)
    ch = np.arange(F_GROUP_DIM)
    ang3 = 2.0 * np.pi * np.outer(ch, ch) / F_GROUP_DIM
    w_chan = np.concatenate([np.cos(ang3), np.sin(ang3)], axis=0) / np.sqrt(F_GROUP_DIM)
    return tuple(jnp.asarray(t, dtype=F32).astype(BF16) for t in (w_rows, w_cols, perm, w_chan))


def _dft_kernel(wr_ref, wc_ref, perm_ref, u_ref, y_ref, z_ref):
    n, m = GRID_W, DFT_CHUNK

    def rows_stage(j, carry):
        col0 = j * m
        x = u_ref[0, :, pl.ds(col0, m), :].reshape(n * m, F_WIDTH)
        xp = [_dot(perm_ref[...], x[g * m * m:(g + 1) * m * m]).astype(BF16) for g in range(n // m)]
        zb = []
        for b in range(m):
            xb = jnp.concatenate([xp[g][b * m:(b + 1) * m] for g in range(n // m)], axis=0)
            zb.append(_dot(wr_ref[...], xb).astype(BF16))
        for t in range(2 * n // m):
            st = jnp.concatenate([zb[b][t * m:(t + 1) * m] for b in range(m)], axis=0)
            z = _dot(perm_ref[...], st).astype(BF16)
            z_ref[t * 8:(t + 1) * 8, :, pl.ds(col0, m), :] = z.reshape(8, 2, m, F_WIDTH)
        return carry

    def cols_stage(j, carry):
        col0 = j * m
        res = []
        for c_lo in range(m):
            c = j * m + c_lo
            data = z_ref[c].reshape(2 * n, F_WIDTH)
            res.append(_dot(wc_ref[c], data).astype(BF16))
        for ri in range(2):
            for g in range(n // m):
                lo = ri * n + g * m
                stacked = jnp.concatenate([r[lo:lo + m] for r in res], axis=0)
                y = _dot(perm_ref[...], stacked).astype(BF16)
                y_ref[ri, 0, g * m:(g + 1) * m, pl.ds(col0, m), :] = y.reshape(m, m, F_WIDTH)
        return carry

    for j in range(n // m):
        rows_stage(j, 0)
    for j in range(n // m):
        cols_stage(j, 0)


def _position_dft(u4, w_rows, w_cols, perm):
    B, n, m = u4.shape[0], GRID_W, DFT_CHUNK
    return pl.pallas_call(
        _dft_kernel,
        grid=(B,),
        in_specs=[
            _const_spec((2 * n, n)),
            _const_spec((n, 2 * n, 2 * n)),
            _const_spec((m * m, m * m)),
            pl.BlockSpec((1, n, n, F_WIDTH), lambda b: (b, 0, 0, 0)),
        ],
        out_specs=pl.BlockSpec((2, 1, n, n, F_WIDTH), lambda b: (0, b, 0, 0, 0)),
        out_shape=jax.ShapeDtypeStruct((2, B, n, n, F_WIDTH), BF16),
        scratch_shapes=[pltpu.VMEM((n, 2, n, F_WIDTH), BF16)],
        compiler_params=_params(1, independent_steps=True),
        name="position_dft",
    )(w_rows, w_cols, perm, u4)


def _out_ffn_kernel(x_ref, ona_ref, y_ref, om_ref, gate_ref, wch_ref, wna_ref, wf_ref, wmo_ref,
                    wout_ref, g2_ref, w1_ref, w2_ref, o_ref):
    yr, yi = y_ref[0], y_ref[1]
    groups = []
    for g in range(F_GROUPS):
        sl = slice(g * F_GROUP_DIM, (g + 1) * F_GROUP_DIM)
        groups.append(_dot(jnp.concatenate([yr[:, sl], yi[:, sl]], axis=1), wch_ref[...]))
    yf = jnp.concatenate(groups, axis=1).astype(BF16)

    gate = lambda br: gate_ref[:, br * D_MODEL:(br + 1) * D_MODEL].astype(F32)
    merged = gate(0) * _dot(ona_ref[...], wna_ref[...])
    merged = merged + gate(1) * _dot(yf, wf_ref[...])
    merged = merged + gate(2) * _dot(om_ref[...], wmo_ref[...])
    x1 = x_ref[...] + _dot(merged.astype(BF16), wout_ref[...])

    ms = jnp.mean(x1 * x1, axis=-1, keepdims=True)
    h2 = (x1 * lax.rsqrt(ms + EPS) * g2_ref[...]).astype(BF16)
    a = jnp.maximum(_dot(h2, w1_ref[...]), 0.0)
    o_ref[...] = x1 + _dot((a * a).astype(BF16), w2_ref[...])


def _out_ffn(x2, ona, y3, om, gates, w_chan, w_na_o, w_f, w_mem_o, w_out, norm2_g, w_ff1, w_ff2,
             tm):
    T = x2.shape[0]
    tok = lambda w: pl.BlockSpec((tm, w), lambda i: (i, 0))
    return pl.pallas_call(
        _out_ffn_kernel,
        grid=(T // tm,),
        in_specs=[
            tok(D_MODEL),
            tok(NA_WIDTH),
            pl.BlockSpec((2, tm, F_WIDTH), lambda i: (0, i, 0)),
            tok(MEM_WIDTH),
            tok(N_BRANCHES * D_MODEL),
            _const_spec((2 * F_GROUP_DIM, F_GROUP_DIM)),
            _const_spec((NA_WIDTH, D_MODEL)),
            _const_spec((F_WIDTH, D_MODEL)),
            _const_spec((MEM_WIDTH, D_MODEL)),
            _const_spec((D_MODEL, D_MODEL)),
            _const_spec((1, D_MODEL)),
            _const_spec((D_MODEL, D_FF)),
            _const_spec((D_FF, D_MODEL)),
        ],
        out_specs=tok(D_MODEL),
        out_shape=jax.ShapeDtypeStruct((T, D_MODEL), F32),
        compiler_params=_params(1, independent_steps=True),
        name="out_ffn",
    )(x2, ona, y3, om, gates, w_chan, w_na_o, w_f, w_mem_o, w_out, norm2_g.reshape(1, D_MODEL),
      w_ff1, w_ff2)


def kernel(x, mem, norm1_g, w_in, b_gate, na_q_g, na_k_g, na_rpb, w_na_o, w_f, mem_norm_g,
           w_mem_kv, mem_q_g, mem_k_g, w_mem_o, w_out, norm2_g, w_ff1, w_ff2):
    B, S, _ = x.shape
    T = B * S
    rows = S // GRID_W
    assert rows == GRID_W and S % NA_PAIR_TOKENS == 0
    x2 = x.reshape(T, D_MODEL)
    gmean_mem = _group_mean_matrix(MEM_WIDTH, MEM_HEAD_DIM)
    w_rows, w_cols, perm, w_chan = _dft_tables()

    memkT, memv, w_in_bf16, w_kT = _mem_kv(mem, mem_norm_g, w_mem_kv, gmean_mem, mem_k_g, w_in)
    later_weights = (w_na_o, w_f, w_mem_o, w_out, w_ff1, w_ff2)
    q, kT4, v, fu, om, gates, bias, *narrow = _in_proj(
        x2, norm1_g, w_in_bf16, w_kT, b_gate, na_q_g, na_k_g, mem_q_g, memkT, memv,
        _na_bias_blocks(na_rpb, rows), later_weights, S, tm=IN_PROJ_TOKENS)
    w_na_o, w_f, w_mem_o, w_out, w_ff1, w_ff2 = narrow

    ona = _na_attention(q.reshape(B, S, NA_WIDTH), kT4, v.reshape(B, S, NA_HEADS * V7X_LANES), bias,
                        pairs_per_step=NA_PAIRS_PER_STEP)

    y = _position_dft(fu.reshape(B, GRID_W, GRID_W, F_WIDTH), w_rows, w_cols, perm)
    y3 = y.reshape(2, T, F_WIDTH)

    out = _out_ffn(x2, ona.reshape(T, NA_WIDTH), y3, om, gates, w_chan, w_na_o, w_f, w_mem_o,
                   w_out, norm2_g, w_ff1, w_ff2, tm=OUT_FFN_TOKENS)
    return out.reshape(B, S, D_MODEL)
```
